```python
import jax, jax.numpy as jnp
from jax import lax
import numpy as np

D_MODEL = 1024
BATCH = 8
SEQ = 2048
DEPTH = 2
DEC_BATCH = 128
DEC_SEQ = 8
PAST_LEN = 8192
PAGE_SIZE = 128

N_MIXERS = 2
N_SWA_LAYERS = (DEPTH + 1) // 2
N_GLA_LAYERS = DEPTH // 2

SWA_HEAD_DIM = 64
SWA_HEADS = D_MODEL // SWA_HEAD_DIM
SWA_KV_HEADS = SWA_HEADS // 4
SWA_GROUP = SWA_HEADS // SWA_KV_HEADS
WINDOW = 128
SWA_BLOCK = 128
ROPE_THETA = 10000.0

GLA_HEADS = 4
GLA_KEY_DIM = D_MODEL // 2
GLA_VAL_DIM = D_MODEL
GLA_DK = GLA_KEY_DIM // GLA_HEADS
GLA_DV = GLA_VAL_DIM // GLA_HEADS
GLA_GATE_RANK = 16
GLA_GATE_NORMALIZER = 16.0
GLA_CHUNK = 64

N_EXPERTS = 64
TOP_K = 8
N_GROUPS = 8
TOPK_GROUPS = 4
EXPERT_FF = D_MODEL // 4
SHARED_FF = EXPERT_FF
ROUTED_SCALE = 2.5
MOE_BLOCK = 128

NORM_EPS = 1e-6

kernel_name = "hybrid_swa_gla_moe_adaln_step"


def rms_norm(x, g):
    xf = x.astype(jnp.float32)
    y = xf * lax.rsqrt(jnp.mean(xf * xf, axis=-1, keepdims=True) + NORM_EPS)
    return (y * g.astype(jnp.float32)).astype(x.dtype)


def rope(x, pos):
    half = x.shape[-1] // 2
    inv = ROPE_THETA ** (-jnp.arange(half, dtype=jnp.float32) / half)
    ang = pos[:, None] * inv[None, :]
    cos = jnp.cos(ang)[:, None, :]
    sin = jnp.sin(ang)[:, None, :]
    xf = x.astype(jnp.float32)
    x1, x2 = xf[..., :half], xf[..., half:]
    return jnp.concatenate([x1 * cos - x2 * sin, x2 * cos + x1 * sin], axis=-1).astype(x.dtype)


def sink_softmax(scores, sink):
    s_col = jnp.broadcast_to(sink[..., None, None], scores.shape[:-1] + (1,))
    p = jax.nn.softmax(jnp.concatenate([scores, s_col], axis=-1), axis=-1)
    return p[..., :-1]


def swa_qkv(h, wqkv, pos):
    B, S, _ = h.shape
    qkv = h @ wqkv
    q, k, v = jnp.split(qkv, [SWA_HEADS * SWA_HEAD_DIM, (SWA_HEADS + SWA_KV_HEADS) * SWA_HEAD_DIM], axis=-1)
    q = rope(q.reshape(B, S, SWA_HEADS, SWA_HEAD_DIM), pos)
    k = rope(k.reshape(B, S, SWA_KV_HEADS, SWA_HEAD_DIM), pos)
    v = v.reshape(B, S, SWA_KV_HEADS, SWA_HEAD_DIM)
    return q, k, v


def swa_prompt(h, wqkv, sinks, wo):
    B, S, _ = h.shape
    q, k, v = swa_qkv(h, wqkv, jnp.arange(S, dtype=jnp.float32))
    nb = S // SWA_BLOCK
    qb = q.reshape(B, nb, SWA_BLOCK, SWA_KV_HEADS, SWA_GROUP, SWA_HEAD_DIM)

    def band(t):
        tb = t.reshape(B, nb, SWA_BLOCK, SWA_KV_HEADS, SWA_HEAD_DIM)
        prev = jnp.concatenate([jnp.zeros_like(tb[:, :1]), tb[:, :-1]], axis=1)
        return jnp.concatenate([prev, tb], axis=2)

    kb, vb = band(k), band(v)
    scores = jnp.einsum('bnqkgd,bnskd->bnkgqs', qb, kb).astype(jnp.float32) * (SWA_HEAD_DIM ** -0.5)
    qi = jnp.arange(SWA_BLOCK)[:, None]
    sj = jnp.arange(2 * SWA_BLOCK)[None, :]
    rel = qi + SWA_BLOCK - sj
    key_pos = jnp.arange(nb)[:, None, None] * SWA_BLOCK + sj - SWA_BLOCK
    mask = (rel >= 0) & (rel <= WINDOW) & (key_pos >= 0)
    scores = jnp.where(mask[None, :, None, None], scores, -jnp.inf)
    p = sink_softmax(scores, sinks.reshape(SWA_KV_HEADS, SWA_GROUP).astype(jnp.float32)).astype(v.dtype)
    o = jnp.einsum('bnkgqs,bnskd->bnqkgd', p, vb).reshape(B, S, SWA_HEADS * SWA_HEAD_DIM)
    return o @ wo, k[:, S - WINDOW:], v[:, S - WINDOW:]


def swa_sample(h, k_buf, v_buf, wqkv, sinks, wo):
    B, T, _ = h.shape
    q, k, v = swa_qkv(h, wqkv, PAST_LEN + jnp.arange(T, dtype=jnp.float32))
    keys = jnp.concatenate([k_buf.astype(k.dtype), k], axis=1)
    vals = jnp.concatenate([v_buf.astype(v.dtype), v], axis=1)
    qg = q.reshape(B, T, SWA_KV_HEADS, SWA_GROUP, SWA_HEAD_DIM)
    scores = jnp.einsum('btkgd,bskd->bkgts', qg, keys).astype(jnp.float32) * (SWA_HEAD_DIM ** -0.5)
    q_pos = PAST_LEN + jnp.arange(T)[:, None]
    k_pos = PAST_LEN - WINDOW + jnp.arange(WINDOW + T)[None, :]
    rel = q_pos - k_pos
    mask = (rel >= 0) & (rel <= WINDOW) & (k_pos >= 0)
    scores = jnp.where(mask, scores, -jnp.inf)
    p = sink_softmax(scores, sinks.reshape(SWA_KV_HEADS, SWA_GROUP).astype(jnp.float32)).astype(vals.dtype)
    o = jnp.einsum('bkgts,bskd->btkgd', p, vals).reshape(B, T, SWA_HEADS * SWA_HEAD_DIM)
    return o @ wo, keys[:, -WINDOW:], vals[:, -WINDOW:]


def gla_project(h, win, wa1, wa2, ba):
    B, S, _ = h.shape
    proj = h @ win
    q, k, v, r = jnp.split(proj, [GLA_KEY_DIM, 2 * GLA_KEY_DIM, 2 * GLA_KEY_DIM + GLA_VAL_DIM], axis=-1)
    gate_logit = (h @ wa1) @ wa2 + ba
    g = jax.nn.log_sigmoid(gate_logit.astype(jnp.float32)) / GLA_GATE_NORMALIZER

    def heads(t, d):
        return jnp.moveaxis(t.astype(jnp.float32).reshape(B, S, GLA_HEADS, d), 2, 1)

    return heads(q, GLA_DK) * (GLA_DK ** -0.5), heads(k, GLA_DK), heads(v, GLA_DV), heads(g, GLA_DK), r


def gla_chunk(s0, qkvg):
    q, k, v, g = qkvg
    L = q.shape[2]
    b = jnp.cumsum(g, axis=2)
    o_inter = jnp.einsum('bhtd,bhde->bhte', q * jnp.exp(b), s0)
    causal = jnp.tril(jnp.ones((L, L), dtype=bool))[:, :, None]
    diff = b[:, :, :, None, :] - b[:, :, None, :, :]
    decay = jnp.exp(jnp.where(causal, diff, -jnp.inf))
    attn = jnp.sum(q[:, :, :, None, :] * k[:, :, None, :, :] * decay, axis=-1)
    o = o_inter + jnp.einsum('bhts,bhse->bhte', attn, v)
    b_last = b[:, :, -1]
    s_new = jnp.exp(b_last)[..., None] * s0 + jnp.einsum('bhsd,bhse->bhde', k * jnp.exp(b_last[:, :, None] - b), v)
    return s_new, o


def gla_output(o, r, norm_g, wo, dtype):
    B, H, S, _ = o.shape
    o = rms_norm(o, norm_g)
    o = jnp.moveaxis(o, 1, 2).reshape(B, S, GLA_VAL_DIM)
    return (o * jax.nn.silu(r.astype(jnp.float32))).astype(dtype) @ wo


def gla_prompt(h, win, wa1, wa2, ba, norm_g, wo):
    B, S, _ = h.shape
    q, k, v, g, r = gla_project(h, win, wa1, wa2, ba)
    nc = S // GLA_CHUNK

    def chunks(t):
        return jnp.moveaxis(t.reshape(B, GLA_HEADS, nc, GLA_CHUNK, t.shape[-1]), 2, 0)

    s0 = jnp.zeros((B, GLA_HEADS, GLA_DK, GLA_DV), jnp.float32)
    s_final, o = lax.scan(gla_chunk, s0, (chunks(q), chunks(k), chunks(v), chunks(g)))
    o = jnp.moveaxis(o, 0, 2).reshape(B, GLA_HEADS, S, GLA_DV)
    return gla_output(o, r, norm_g, wo, h.dtype), s_final.astype(h.dtype)


def gla_sample(h, state, win, wa1, wa2, ba, norm_g, wo):
    q, k, v, g, r = gla_project(h, win, wa1, wa2, ba)
    s_new, o = gla_chunk(state.astype(jnp.float32), (q, k, v, g))
    return gla_output(o, r, norm_g, wo, h.dtype), s_new.astype(state.dtype)


def routed_experts(x, eidx, gate, wg, wu, wd):
    T, D = x.shape
    n_assign = T * TOP_K
    n_blocks = -(-n_assign // MOE_BLOCK) + N_EXPERTS
    e_flat = eidx.reshape(-1)
    tok_flat = jnp.repeat(jnp.arange(T, dtype=jnp.int32), TOP_K)
    order = jnp.argsort(e_flat)
    e_sorted = e_flat[order]
    counts = jnp.bincount(e_flat, length=N_EXPERTS)
    padded = (counts + MOE_BLOCK - 1) // MOE_BLOCK * MOE_BLOCK
    pad_end = jnp.cumsum(padded)
    pad_start = pad_end - padded
    start = jnp.cumsum(counts) - counts
    dest = pad_start[e_sorted] + jnp.arange(n_assign) - start[e_sorted]
    slot_tok = jnp.full((n_blocks * MOE_BLOCK,), T, jnp.int32).at[dest].set(tok_flat[order])
    slot_w = jnp.zeros((n_blocks * MOE_BLOCK,), gate.dtype).at[dest].set(gate.reshape(-1)[order])
    block_e = jnp.minimum(jnp.searchsorted(pad_end, jnp.arange(n_blocks) * MOE_BLOCK, side='right'), N_EXPERTS - 1)
    x_pad = jnp.concatenate([x, jnp.zeros((1, D), x.dtype)], axis=0)

    def one_block(args):
        tok, w, e = args
        xb = x_pad[tok]
        hb = jax.nn.silu(xb @ wg[e]) * (xb @ wu[e])
        return (hb @ wd[e]) * w[:, None]

    yb = lax.map(one_block, (slot_tok.reshape(n_blocks, MOE_BLOCK), slot_w.reshape(n_blocks, MOE_BLOCK), block_e))
    return jnp.zeros((T + 1, D), x.dtype).at[slot_tok].add(yb.reshape(-1, D))[:T]


def moe(h, router_w, router_b, wg, wu, wd, swg, swu, swd):
    B, S, D = h.shape
    x = h.reshape(-1, D)
    T = x.shape[0]
    scores = jax.nn.sigmoid((x @ router_w).astype(jnp.float32))
    sel = scores + router_b.astype(jnp.float32)
    grp_score = jnp.sum(lax.top_k(sel.reshape(T, N_GROUPS, -1), 2)[0], axis=-1)
    _, gidx = lax.top_k(grp_score, TOPK_GROUPS)
    gmask = jnp.any(gidx[:, :, None] == jnp.arange(N_GROUPS)[None, None, :], axis=1)
    emask = jnp.repeat(gmask, N_EXPERTS // N_GROUPS, axis=1)
    _, eidx = lax.top_k(jnp.where(emask, sel, -jnp.inf), TOP_K)
    w = jnp.take_along_axis(scores, eidx, axis=1)
    w = w / jnp.sum(w, axis=-1, keepdims=True) * ROUTED_SCALE
    routed = routed_experts(x, eidx, w.astype(x.dtype), wg, wu, wd)
    shared = (jax.nn.silu(x @ swg) * (x @ swu)) @ swd
    return (routed + shared).reshape(B, S, D)


def setup_inputs(seed: int = 0) -> dict:
    key = jax.random.key(seed)
    ks = iter(jax.random.split(key, 40))
    f32 = jnp.float32

    def nrm(shape, scale):
        return jax.random.normal(next(ks), shape, f32) * scale

    D = D_MODEL
    qkv_w = (SWA_HEADS + 2 * SWA_KV_HEADS) * SWA_HEAD_DIM
    gla_in = 2 * GLA_KEY_DIM + 2 * GLA_VAL_DIM
    return {
        "x_prompt": nrm((BATCH, SEQ, D), 1.0),
        "x_sample": nrm((DEC_BATCH, DEC_SEQ, D), 1.0),
        "c_prompt": nrm((BATCH, D), 1.0),
        "c_sample": nrm((DEC_BATCH, D), 1.0),
        "cache_swa_k": nrm((N_SWA_LAYERS, DEC_BATCH, WINDOW, SWA_KV_HEADS, SWA_HEAD_DIM), 1.0),
        "cache_swa_v": nrm((N_SWA_LAYERS, DEC_BATCH, WINDOW, SWA_KV_HEADS, SWA_HEAD_DIM), 1.0),
        "state_gla": nrm((N_GLA_LAYERS, DEC_BATCH, GLA_HEADS, GLA_DK, GLA_DV), 0.5),
        "norm_mix_g": 1.0 + nrm((DEPTH, D), 0.05),
        "norm_ffn_g": 1.0 + nrm((DEPTH, D), 0.05),
        "final_g": 1.0 + nrm((D,), 0.05),
        "ada_w": nrm((DEPTH, D, 6 * D), 0.5 * D ** -0.5),
        "ada_b": nrm((DEPTH, 6 * D), 0.02),
        "swa_wqkv": nrm((N_SWA_LAYERS, D, qkv_w), D ** -0.5),
        "swa_sinks": nrm((N_SWA_LAYERS, SWA_HEADS), 0.5),
        "swa_wo": nrm((N_SWA_LAYERS, SWA_HEADS * SWA_HEAD_DIM, D), (SWA_HEADS * SWA_HEAD_DIM) ** -0.5),
        "gla_win": nrm((N_GLA_LAYERS, D, gla_in), D ** -0.5),
        "gla_wa1": nrm((N_GLA_LAYERS, D, GLA_GATE_RANK), D ** -0.5),
        "gla_wa2": nrm((N_GLA_LAYERS, GLA_GATE_RANK, GLA_KEY_DIM), GLA_GATE_RANK ** -0.5),
        "gla_ba": nrm((N_GLA_LAYERS, GLA_KEY_DIM), 0.1),
        "gla_norm_g": 1.0 + nrm((N_GLA_LAYERS, GLA_DV), 0.05),
        "gla_wo": nrm((N_GLA_LAYERS, GLA_VAL_DIM, D), GLA_VAL_DIM ** -0.5),
        "moe_router": nrm((DEPTH, D, N_EXPERTS), D ** -0.5),
        "moe_bias": nrm((DEPTH, N_EXPERTS), 0.01),
        "moe_wg": nrm((DEPTH, N_EXPERTS, D, EXPERT_FF), D ** -0.5),
        "moe_wu": nrm((DEPTH, N_EXPERTS, D, EXPERT_FF), D ** -0.5),
        "moe_wd": nrm((DEPTH, N_EXPERTS, EXPERT_FF, D), EXPERT_FF ** -0.5),
        "shared_wg": nrm((DEPTH, D, SHARED_FF), D ** -0.5),
        "shared_wu": nrm((DEPTH, D, SHARED_FF), D ** -0.5),
        "shared_wd": nrm((DEPTH, SHARED_FF, D), SHARED_FF ** -0.5),
    }


def reference(x_prompt, x_sample, c_prompt, c_sample, cache_swa_k, cache_swa_v, state_gla,
              norm_mix_g, norm_ffn_g, final_g, ada_w, ada_b,
              swa_wqkv, swa_sinks, swa_wo,
              gla_win, gla_wa1, gla_wa2, gla_ba, gla_norm_g, gla_wo,
              moe_router, moe_bias, moe_wg, moe_wu, moe_wd,
              shared_wg, shared_wu, shared_wd):

    def run(x, c, swa_k, swa_v, gla_s):
        sample = swa_k is not None
        new_k, new_v, new_s = [], [], []
        for layer in range(DEPTH):
            mod = (jax.nn.silu(c) @ ada_w[layer] + ada_b[layer])[:, None, :]
            sh1, sc1, g1, sh2, sc2, g2 = jnp.split(mod, 6, axis=-1)
            hmix = rms_norm(x, norm_mix_g[layer]) * (1.0 + sc1) + sh1
            if layer % N_MIXERS == 0:
                a = layer // N_MIXERS
                if sample:
                    out, kk, vv = swa_sample(hmix, swa_k[a], swa_v[a], swa_wqkv[a], swa_sinks[a], swa_wo[a])
                else:
                    out, kk, vv = swa_prompt(hmix, swa_wqkv[a], swa_sinks[a], swa_wo[a])
                new_k.append(kk)
                new_v.append(vv)
            else:
                m = layer // N_MIXERS
                if sample:
                    out, ss = gla_sample(hmix, gla_s[m], gla_win[m], gla_wa1[m], gla_wa2[m], gla_ba[m], gla_norm_g[m], gla_wo[m])
                else:
                    out, ss = gla_prompt(hmix, gla_win[m], gla_wa1[m], gla_wa2[m], gla_ba[m], gla_norm_g[m], gla_wo[m])
                new_s.append(ss)
            x = x + g1 * out
            hffn = rms_norm(x, norm_ffn_g[layer]) * (1.0 + sc2) + sh2
            x = x + g2 * moe(hffn, moe_router[layer], moe_bias[layer], moe_wg[layer], moe_wu[layer],
                             moe_wd[layer], shared_wg[layer], shared_wu[layer], shared_wd[layer])
        return rms_norm(x, final_g), jnp.stack(new_k), jnp.stack(new_v), jnp.stack(new_s)

    y_prompt, swa_k_prompt, swa_v_prompt, gla_state_prompt = run(x_prompt, c_prompt, None, None, None)
    y_sample, swa_k_sample, swa_v_sample, gla_state_sample = run(x_sample, c_sample, cache_swa_k, cache_swa_v, state_gla)
    return (y_prompt, y_sample, swa_k_prompt, swa_v_prompt, swa_k_sample, swa_v_sample, gla_state_prompt, gla_state_sample)
```

```python
import functools

import jax
import jax.numpy as jnp
from jax import lax
from jax.experimental import pallas as pl
from jax.experimental.pallas import tpu as pltpu

F32 = jnp.float32
BF16 = jnp.bfloat16
I32 = jnp.int32

D_MODEL = 1024
PAST_LEN = 8192
SWA_HEAD_DIM = 64
SWA_HEADS = 16
SWA_KV_HEADS = 4
SWA_GROUP = 4
WINDOW = 128
ROPE_THETA = 10000.0
GLA_HEADS = 4
GLA_DK = 128
GLA_DV = 256
GLA_KEY_DIM = 512
GLA_VAL_DIM = 1024
GLA_GATE_RANK = 16
GLA_GATE_NORMALIZER = 16.0
GLA_CHUNK = 64
N_EXPERTS = 64
TOP_K = 8
N_GROUPS = 8
TOPK_GROUPS = 4
EXPERT_FF = 256
ROUTED_SCALE = 2.5
NORM_EPS = 1e-6

TOKEN_TILE = 512
ROUTE_TILE = 256
EXPERT_BLOCK = 256
ADA_TILE = 512
SAMPLE_SEQS = 8
VMEM_LIMIT = 48 * 1024 * 1024

NT_DIMS = (((1,), (1,)), ((), ()))
TN_DIMS = (((0,), (0,)), ((), ()))


def _params(semantics):
    return pltpu.CompilerParams(dimension_semantics=semantics, vmem_limit_bytes=VMEM_LIMIT)


def _silu(x):
    return x * jax.nn.sigmoid(x)


def _norm_mod(x, g, sc, sh):
    ms = jnp.mean(x * x, axis=-1, keepdims=True)
    return (x * lax.rsqrt(ms + NORM_EPS) * g) * (1.0 + sc) + sh


def _split3(x):
    x1 = x.astype(BF16)
    r1 = x - x1.astype(F32)
    x2 = r1.astype(BF16)
    x3 = (r1 - x2.astype(F32)).astype(BF16)
    return x1, x2, x3


def _ada_kernel(c_ref, w_ref, b_ref, o_ref):
    s = _silu(c_ref[...]).astype(BF16)
    o_ref[...] = jnp.dot(s, w_ref[...].astype(BF16), preferred_element_type=F32) + b_ref[...]


def ada_modulation(c_all, ada_w, ada_b):
    depth, d, n = ada_w.shape
    rows = c_all.shape[0]
    return pl.pallas_call(
        _ada_kernel,
        out_shape=jax.ShapeDtypeStruct((depth, rows, n), F32),
        grid=(depth, n // ADA_TILE),
        in_specs=[
            pl.BlockSpec((rows, d), lambda l, j: (0, 0)),
            pl.BlockSpec((None, d, ADA_TILE), lambda l, j: (l, 0, j)),
            pl.BlockSpec((None, 1, ADA_TILE), lambda l, j: (l, 0, j)),
        ],
        out_specs=pl.BlockSpec((None, rows, ADA_TILE), lambda l, j: (l, 0, j)),
        compiler_params=_params(("parallel", "parallel")),
        name="ada_modulation",
    )(c_all, ada_w, ada_b.reshape(depth, 1, n))


class _Mod:
    def __init__(self, mod, layer, n_batch, seq_len, tile):
        depth, rows, n = mod.shape
        self.tile = tile
        self.layer = layer
        self.npt = n_batch * seq_len // tile
        self.mod_tok = mod
        self.mod_seq = mod[:, rows - n_batch:].reshape(depth, n_batch, 1, n)
        self.tiles_per_seq = seq_len // tile

    def operands(self, chunk):
        del chunk
        return [self.mod_seq, self.mod_tok]

    def specs(self, chunk):
        l, npt, tps = self.layer, self.npt, self.tiles_per_seq
        n_seq = self.mod_seq.shape[1]
        seq_spec = pl.BlockSpec((None, None, 1, D_MODEL),
                                lambda i, *_: (l, jnp.minimum(i // tps, n_seq - 1), 0, chunk))
        tok_spec = pl.BlockSpec((None, self.tile, D_MODEL),
                                lambda i, *_: (l, jnp.maximum(i - npt, 0), chunk))
        return [seq_spec, tok_spec]


def _pick(is_sample, seq_ref, tok_ref):
    return jnp.where(is_sample, tok_ref[...], seq_ref[...])


def _swa_qkv_kernel(npt, x_ref, g_ref, scs_ref, sct_ref, shs_ref, sht_ref, w_ref, cos_ref, sin_ref,
                    q_ref, k_ref, v_ref):
    is_s = pl.program_id(0) >= npt
    h = _norm_mod(x_ref[...], g_ref[...], _pick(is_s, scs_ref, sct_ref), _pick(is_s, shs_ref, sht_ref))
    qkv = jnp.dot(h.astype(BF16), w_ref[...], preferred_element_type=F32)
    cos = cos_ref[...]
    sin = sin_ref[...]
    lane = lax.broadcasted_iota(I32, cos.shape, 1)
    first_half = (lane % SWA_HEAD_DIM) < (SWA_HEAD_DIM // 2)

    def rope(xc):
        rot = jnp.where(first_half, pltpu.roll(xc, 128 - SWA_HEAD_DIM // 2, 1), pltpu.roll(xc, SWA_HEAD_DIM // 2, 1))
        return xc * cos + rot * sin

    nq = SWA_HEADS * SWA_HEAD_DIM
    nk = SWA_KV_HEADS * SWA_HEAD_DIM
    for c in range(nq // 128):
        q_ref[:, 128 * c:128 * (c + 1)] = (rope(qkv[:, 128 * c:128 * (c + 1)]) * (SWA_HEAD_DIM ** -0.5)).astype(BF16)
    for c in range(nk // 128):
        k_ref[:, 128 * c:128 * (c + 1)] = rope(qkv[:, nq + 128 * c:nq + 128 * (c + 1)])
    v_ref[...] = qkv[:, nq + nk:]


def swa_qkv(x, g, mod, w_bf, cos_tab, sin_tab):
    t = x.shape[0]
    tm = mod.tile
    nq = SWA_HEADS * SWA_HEAD_DIM
    nk = SWA_KV_HEADS * SWA_HEAD_DIM
    row = lambda i: (i, 0)
    return pl.pallas_call(
        functools.partial(_swa_qkv_kernel, mod.npt),
        out_shape=(jax.ShapeDtypeStruct((t, nq), BF16), jax.ShapeDtypeStruct((t, nk), F32),
                   jax.ShapeDtypeStruct((t, nk), F32)),
        grid=(t // tm,),
        in_specs=[pl.BlockSpec((tm, D_MODEL), row), pl.BlockSpec((1, D_MODEL), lambda i: (0, 0))]
        + mod.specs(1) + mod.specs(0)
        + [pl.BlockSpec(w_bf.shape, lambda i: (0, 0)), pl.BlockSpec((tm, 128), row), pl.BlockSpec((tm, 128), row)],
        out_specs=(pl.BlockSpec((tm, nq), row), pl.BlockSpec((tm, nk), row), pl.BlockSpec((tm, nk), row)),
        compiler_params=_params(("parallel",)),
        name="swa_qkv",
    )(x, g.reshape(1, -1), *mod.operands(1), *mod.operands(0), w_bf, cos_tab, sin_tab)


def _sink_softmax(s, sink_col):
    m = jnp.maximum(jnp.max(s, axis=-1, keepdims=True), sink_col)
    e = jnp.exp(s - m)
    den = jnp.sum(e, axis=-1, keepdims=True) + jnp.exp(sink_col - m)
    return e * (1.0 / den)


def _swa_prompt_kernel(sink_ref, q_ref, kc_ref, kp_ref, vc_ref, vp_ref, o_ref):
    j = pl.program_id(1)
    blk = q_ref.shape[0]
    qi = lax.broadcasted_iota(I32, (blk, 2 * blk), 0)
    sj = lax.broadcasted_iota(I32, (blk, 2 * blk), 1)
    rel = qi + blk - sj
    mask = (rel >= 0) & (rel <= WINDOW) & ((sj >= blk) | (j > 0))
    mask = jnp.concatenate([mask] * SWA_GROUP, axis=0)
    hd = SWA_HEAD_DIM
    outs = []
    for g in range(SWA_KV_HEADS):
        kcat = jnp.concatenate([kp_ref[:, hd * g:hd * (g + 1)], kc_ref[:, hd * g:hd * (g + 1)]], axis=0).astype(BF16)
        vcat = jnp.concatenate([vp_ref[:, hd * g:hd * (g + 1)], vc_ref[:, hd * g:hd * (g + 1)]], axis=0).astype(BF16)
        heads = [SWA_GROUP * g + i for i in range(SWA_GROUP)]
        q4 = jnp.concatenate([q_ref[:, hd * h:hd * (h + 1)] for h in heads], axis=0)
        s = lax.dot_general(q4, kcat, NT_DIMS, preferred_element_type=F32)
        s = jnp.where(mask, s, -jnp.inf)
        sink_col = jnp.concatenate([jnp.full((blk, 1), sink_ref[h], F32) for h in heads], axis=0)
        p = _sink_softmax(s, sink_col).astype(BF16)
        o = jnp.dot(p, vcat, preferred_element_type=F32)
        outs += [o[blk * i:blk * (i + 1)] for i in range(SWA_GROUP)]
    o_ref[...] = jnp.concatenate(outs, axis=1).astype(BF16)


def swa_prompt_attention(q, k, v, sinks, n_batch, seq_len):
    blk = WINDOW
    nb = seq_len // blk
    nq = q.shape[1]
    nk = k.shape[1]
    cur = lambda b, j: (b * nb + j, 0)
    prev = lambda b, j: (b * nb + jnp.maximum(j - 1, 0), 0)
    return pl.pallas_call(
        _swa_prompt_kernel,
        out_shape=jax.ShapeDtypeStruct((n_batch * seq_len, nq), BF16),
        grid=(n_batch, nb),
        in_specs=[pl.BlockSpec(memory_space=pltpu.SMEM),
                  pl.BlockSpec((blk, nq), cur),
                  pl.BlockSpec((blk, nk), cur), pl.BlockSpec((blk, nk), prev),
                  pl.BlockSpec((blk, nk), cur), pl.BlockSpec((blk, nk), prev)],
        out_specs=pl.BlockSpec((blk, nq), cur),
        compiler_params=_params(("parallel", "parallel")),
        name="swa_prompt_attention",
    )(sinks, q, k, k, v, v)


def _swa_sample_kernel(n_new, sink_ref, q_ref, kn_ref, vn_ref, ck_ref, cv_ref, o_ref, nk_ref, nv_ref):
    hd = SWA_HEAD_DIM
    win = ck_ref.shape[1]
    nkeys = win + n_new
    rows = SWA_GROUP * n_new
    ti = lax.broadcasted_iota(I32, (rows, nkeys), 0) % n_new
    si = lax.broadcasted_iota(I32, (rows, nkeys), 1)
    mask = (si >= ti) & (si <= ti + WINDOW)
    qf = q_ref[...].astype(F32)
    for sb in range(ck_ref.shape[0]):
        r0 = sb * n_new
        kc = ck_ref[sb]
        vc = cv_ref[sb]
        kn = kn_ref[r0:r0 + n_new, :]
        vn = vn_ref[r0:r0 + n_new, :]
        nk_ref[sb, 0:win - n_new, :] = kc[n_new:]
        nk_ref[sb, win - n_new:win, :] = kn
        nv_ref[sb, 0:win - n_new, :] = vc[n_new:]
        nv_ref[sb, win - n_new:win, :] = vn
        keys = jnp.concatenate([kc, kn], axis=0)
        vals = jnp.concatenate([vc, vn], axis=0)
        outs = []
        for g in range(SWA_KV_HEADS):
            heads = [SWA_GROUP * g + i for i in range(SWA_GROUP)]
            kg = keys[:, hd * g:hd * (g + 1)].astype(BF16)
            vg = vals[:, hd * g:hd * (g + 1)].astype(BF16)
            qg = jnp.concatenate([qf[r0:r0 + n_new, hd * h:hd * (h + 1)] for h in heads], axis=0).astype(BF16)
            s = lax.dot_general(qg, kg, NT_DIMS, preferred_element_type=F32)
            s = jnp.where(mask, s, -jnp.inf)
            sink_col = jnp.concatenate([jnp.full((n_new, 1), sink_ref[h], F32) for h in heads], axis=0)
            p = _sink_softmax(s, sink_col).astype(BF16)
            o = jnp.dot(p, vg, preferred_element_type=F32)
            outs += [o[n_new * i:n_new * (i + 1)] for i in range(SWA_GROUP)]
        o_ref[r0:r0 + n_new, :] = jnp.concatenate(outs, axis=1).astype(BF16)


def swa_sample_attention(q, k, v, cache_k, cache_v, sinks, n_prompt_rows, n_new):
    n_seq, win, nk = cache_k.shape
    sb = SAMPLE_SEQS
    rows = sb * n_new
    nq = q.shape[1]
    base = n_prompt_rows // rows
    tok = lambda i: (base + i, 0)
    seq = lambda i: (i, 0, 0)
    return pl.pallas_call(
        functools.partial(_swa_sample_kernel, n_new),
        out_shape=(jax.ShapeDtypeStruct((n_seq * n_new, nq), BF16),
                   jax.ShapeDtypeStruct(cache_k.shape, F32), jax.ShapeDtypeStruct(cache_v.shape, F32)),
        grid=(n_seq // sb,),
        in_specs=[pl.BlockSpec(memory_space=pltpu.SMEM),
                  pl.BlockSpec((rows, nq), tok), pl.BlockSpec((rows, nk), tok), pl.BlockSpec((rows, nk), tok),
                  pl.BlockSpec((sb, win, nk), seq), pl.BlockSpec((sb, win, nk), seq)],
        out_specs=(pl.BlockSpec((rows, nq), lambda i: (i, 0)), pl.BlockSpec((sb, win, nk), seq),
                   pl.BlockSpec((sb, win, nk), seq)),
        compiler_params=_params(("parallel",)),
        name="swa_sample_attention",
    )(sinks, q, k, v, cache_k, cache_v)


def _gla_proj_kernel(npt, x_ref, g_ref, scs_ref, sct_ref, shs_ref, sht_ref, w_ref, wa2_ref, ba_ref,
                     q_ref, k_ref, v_ref, r_ref, gate_ref):
    is_s = pl.program_id(0) >= npt
    h = _norm_mod(x_ref[...], g_ref[...], _pick(is_s, scs_ref, sct_ref), _pick(is_s, shs_ref, sht_ref))
    proj = jnp.dot(h.astype(BF16), w_ref[...], preferred_element_type=F32)
    kd = GLA_KEY_DIM
    vd = GLA_VAL_DIM
    q_ref[...] = proj[:, :kd] * (GLA_DK ** -0.5)
    k_ref[...] = proj[:, kd:2 * kd]
    v_ref[...] = proj[:, 2 * kd:2 * kd + vd].astype(BF16)
    r_ref[...] = proj[:, 2 * kd + vd:2 * kd + 2 * vd]
    low = proj[:, 2 * kd + 2 * vd:].astype(BF16)
    z = jnp.dot(low, wa2_ref[...], preferred_element_type=F32) + ba_ref[...]
    log_sig = jnp.minimum(z, 0.0) - jnp.log1p(jnp.exp(-jnp.abs(z)))
    gate_ref[...] = log_sig / GLA_GATE_NORMALIZER


def gla_project(x, g, mod, win_ext, wa2_pad, ba):
    t = x.shape[0]
    tm = mod.tile
    kd, vd = GLA_KEY_DIM, GLA_VAL_DIM
    row = lambda i: (i, 0)
    const = lambda i: (0, 0)
    return pl.pallas_call(
        functools.partial(_gla_proj_kernel, mod.npt),
        out_shape=(jax.ShapeDtypeStruct((t, kd), F32), jax.ShapeDtypeStruct((t, kd), F32),
                   jax.ShapeDtypeStruct((t, vd), BF16), jax.ShapeDtypeStruct((t, vd), F32),
                   jax.ShapeDtypeStruct((t, kd), F32)),
        grid=(t // tm,),
        in_specs=[pl.BlockSpec((tm, D_MODEL), row), pl.BlockSpec((1, D_MODEL), const)]
        + mod.specs(1) + mod.specs(0)
        + [pl.BlockSpec(win_ext.shape, const), pl.BlockSpec(wa2_pad.shape, const), pl.BlockSpec((1, kd), const)],
        out_specs=(pl.BlockSpec((tm, kd), row), pl.BlockSpec((tm, kd), row), pl.BlockSpec((tm, vd), row),
                   pl.BlockSpec((tm, vd), row), pl.BlockSpec((tm, kd), row)),
        compiler_params=_params(("parallel",)),
        name="gla_project",
    )(x, g.reshape(1, -1), *mod.operands(1), *mod.operands(0), win_ext, wa2_pad, ba.reshape(1, -1))


def _cumsum_rows(tri, g):
    g1, g2, g3 = _split3(g)
    return (jnp.dot(tri, g1, preferred_element_type=F32) + jnp.dot(tri, g2, preferred_element_type=F32)
            + jnp.dot(tri, g3, preferred_element_type=F32))


def _diag_attention(q, k, b, n):
    ng = n // 8
    dk = q.shape[1]
    q3 = q.reshape(ng, 8, dk)
    k3 = k.reshape(ng, 8, dk)
    b3 = b.reshape(ng, 8, dk)
    sub = lax.broadcasted_iota(I32, (ng, 8, dk), 1)
    ti = lax.broadcasted_iota(I32, (n, n), 0)
    si = lax.broadcasted_iota(I32, (n, n), 1)
    attn = jnp.zeros((n, n), F32)
    for j in range(8):
        bj = jnp.broadcast_to(b3[:, j:j + 1, :], b3.shape)
        kj = jnp.broadcast_to(k3[:, j:j + 1, :], k3.shape)
        e = jnp.exp(jnp.minimum(b3 - bj, 0.0))
        m = jnp.where(sub >= j, q3 * e * kj, 0.0)
        col = jnp.sum(m, axis=-1, keepdims=True).reshape(n, 1)
        attn = attn + jnp.where(si == (ti // 8) * 8 + j, col, 0.0)
    return attn


def _cross_attention(q, k, b, n):
    ti = lax.broadcasted_iota(I32, (n, n), 0)
    si = lax.broadcasted_iota(I32, (n, n), 1)
    row = lax.broadcasted_iota(I32, b.shape, 0)
    attn = jnp.zeros((n, n), F32)
    m = n // 2
    while m >= 8:
        nblk = n // m
        refq = jnp.concatenate(
            [jnp.broadcast_to(b[i * m - 1:i * m], (m, b.shape[1])) if i % 2 else b[i * m:(i + 1) * m]
             for i in range(nblk)], axis=0)
        refk = jnp.concatenate(
            [b[i * m:(i + 1) * m] if i % 2 else jnp.broadcast_to(b[(i + 1) * m - 1:(i + 1) * m], (m, b.shape[1]))
             for i in range(nblk)], axis=0)
        odd = ((row // m) % 2) == 1
        qt = jnp.where(odd, q * jnp.exp(jnp.minimum(b - refq, 0.0)), 0.0).astype(BF16)
        kt = jnp.where(odd, 0.0, k * jnp.exp(jnp.minimum(refk - b, 0.0))).astype(BF16)
        a = lax.dot_general(qt, kt, NT_DIMS, preferred_element_type=F32)
        keep = (((ti // m) % 2) == 1) & ((si // m) == (ti // m) - 1)
        attn = attn + jnp.where(keep, a, 0.0)
        m //= 2
    return attn


def _gla_epilogue(o, r, ng):
    ms = jnp.mean(o * o, axis=-1, keepdims=True)
    return (o * lax.rsqrt(ms + NORM_EPS) * ng * _silu(r)).astype(BF16)


def _gla_prompt_kernel(q_ref, k_ref, g_ref, v_ref, r_ref, ng_ref, o_ref, so_ref, st_ref):
    c = pl.program_id(1)
    n = q_ref.shape[0]

    @pl.when(c == 0)
    def _():
        st_ref[...] = jnp.zeros(st_ref.shape, F32)

    ti = lax.broadcasted_iota(I32, (n, n), 0)
    si = lax.broadcasted_iota(I32, (n, n), 1)
    tri = jnp.where(ti >= si, 1.0, 0.0).astype(BF16)
    for h in range(GLA_HEADS):
        ks = slice(GLA_DK * h, GLA_DK * (h + 1))
        vs = slice(GLA_DV * h, GLA_DV * (h + 1))
        q = q_ref[:, ks]
        k = k_ref[:, ks]
        v = v_ref[:, vs]
        b = _cumsum_rows(tri, g_ref[:, ks])
        s_t = st_ref[h]
        o = lax.dot_general((q * jnp.exp(b)).astype(BF16), s_t.astype(BF16), NT_DIMS, preferred_element_type=F32)
        attn = _cross_attention(q, k, b, n) + _diag_attention(q, k, b, n)
        o = o + jnp.dot(attn.astype(BF16), v, preferred_element_type=F32)
        bl = b[n - 1:n, :]
        kd = (k * jnp.exp(bl - b)).astype(BF16)
        s_new = s_t * jnp.exp(bl) + lax.dot_general(v, kd, TN_DIMS, preferred_element_type=F32)
        st_ref[h] = s_new
        o_ref[:, vs] = _gla_epilogue(o, r_ref[:, vs], ng_ref[...])

        @pl.when(c == pl.num_programs(1) - 1)
        def _():
            so_ref[h] = s_new.T


def gla_prompt(q, k, g, v, r, norm_g, n_batch, seq_len):
    n = GLA_CHUNK
    nc = seq_len // n
    kd, vd = GLA_KEY_DIM, GLA_VAL_DIM
    row = lambda b, c: (b * nc + c, 0)
    return pl.pallas_call(
        _gla_prompt_kernel,
        out_shape=(jax.ShapeDtypeStruct((n_batch * seq_len, vd), BF16),
                   jax.ShapeDtypeStruct((n_batch, GLA_HEADS, GLA_DK, GLA_DV), F32)),
        grid=(n_batch, nc),
        in_specs=[pl.BlockSpec((n, kd), row), pl.BlockSpec((n, kd), row), pl.BlockSpec((n, kd), row),
                  pl.BlockSpec((n, vd), row), pl.BlockSpec((n, vd), row),
                  pl.BlockSpec((1, GLA_DV), lambda b, c: (0, 0))],
        out_specs=(pl.BlockSpec((n, vd), row),
                   pl.BlockSpec((None, GLA_HEADS, GLA_DK, GLA_DV), lambda b, c: (b, 0, 0, 0))),
        scratch_shapes=[pltpu.VMEM((GLA_HEADS, GLA_DV, GLA_DK), F32)],
        compiler_params=_params(("parallel", "arbitrary")),
        name="gla_prompt",
    )(q, k, g, v, r, norm_g.reshape(1, -1))


def _gla_sample_kernel(n_new, q_ref, k_ref, g_ref, v_ref, r_ref, ng_ref, si_ref, o_ref, so_ref):
    n = q_ref.shape[0]
    ti = lax.broadcasted_iota(I32, (n, n), 0)
    si = lax.broadcasted_iota(I32, (n, n), 1)
    tri = jnp.where((ti >= si) & (ti // n_new == si // n_new), 1.0, 0.0).astype(BF16)
    for h in range(GLA_HEADS):
        ks = slice(GLA_DK * h, GLA_DK * (h + 1))
        vs = slice(GLA_DV * h, GLA_DV * (h + 1))
        q = q_ref[:, ks]
        k = k_ref[:, ks]
        v = v_ref[:, vs]
        b = _cumsum_rows(tri, g_ref[:, ks])
        attn = _diag_attention(q, k, b, n)
        o_intra = jnp.dot(attn.astype(BF16), v, preferred_element_type=F32)
        qe = (q * jnp.exp(b)).astype(BF16)
        n_sb = n // n_new
        last = [b[n_new * (sb + 1) - 1:n_new * (sb + 1), :] for sb in range(n_sb)]
        bl_rows = jnp.concatenate([jnp.broadcast_to(bl, (n_new, GLA_DK)) for bl in last], axis=0)
        kd = (k * jnp.exp(bl_rows - b)).astype(BF16)
        seq_of_row = lax.broadcasted_iota(I32, (n, GLA_DV), 0) // n_new
        o = o_intra
        for sb in range(n_sb):
            mine = seq_of_row == sb
            s_t = si_ref[sb, h].T
            o_sb = lax.dot_general(qe, s_t.astype(BF16), NT_DIMS, preferred_element_type=F32)
            o = o + jnp.where(mine, o_sb, 0.0)
            v_sb = jnp.where(mine, v, jnp.zeros_like(v))
            upd = lax.dot_general(v_sb, kd, TN_DIMS, preferred_element_type=F32)
            so_ref[sb, h] = (s_t * jnp.exp(last[sb]) + upd).T
        o_ref[:, vs] = _gla_epilogue(o, r_ref[:, vs], ng_ref[...])


def gla_sample(q, k, g, v, r, norm_g, state, n_prompt_rows, n_new):
    n_seq = state.shape[0]
    sb = SAMPLE_SEQS
    rows = sb * n_new
    kd, vd = GLA_KEY_DIM, GLA_VAL_DIM
    base = n_prompt_rows // rows
    tok = lambda i: (base + i, 0)
    seq = lambda i: (i, 0, 0, 0)
    sblock = (sb, GLA_HEADS, GLA_DK, GLA_DV)
    return pl.pallas_call(
        functools.partial(_gla_sample_kernel, n_new),
        out_shape=(jax.ShapeDtypeStruct((n_seq * n_new, vd), BF16), jax.ShapeDtypeStruct(state.shape, F32)),
        grid=(n_seq // sb,),
        in_specs=[pl.BlockSpec((rows, kd), tok), pl.BlockSpec((rows, kd), tok), pl.BlockSpec((rows, kd), tok),
                  pl.BlockSpec((rows, vd), tok), pl.BlockSpec((rows, vd), tok),
                  pl.BlockSpec((1, GLA_DV), lambda i: (0, 0)),
                  pl.BlockSpec(sblock, seq)],
        out_specs=(pl.BlockSpec((rows, vd), lambda i: (i, 0)), pl.BlockSpec(sblock, seq)),
        compiler_params=_params(("parallel",)),
        name="gla_sample",
    )(q, k, g, v, r, norm_g.reshape(1, -1), state)


def _post_mixer_kernel(npt, ap_ref, as_ref, wo_ref, x_ref, g1s_ref, g1t_ref, gf_ref, scs_ref, sct_ref, shs_ref, sht_ref,
                       rw_ref, rb_ref, tri_ref,
                       x1_ref, h_ref, eidx_ref, w_ref, rank_ref, cnt_ref, carry_ref):
    i = pl.program_id(0)
    is_s = i >= npt

    @pl.when(i == 0)
    def _():
        carry_ref[...] = jnp.zeros(carry_ref.shape, F32)

    a = jnp.where(is_s, as_ref[...], ap_ref[...])
    x1 = x_ref[...] + _pick(is_s, g1s_ref, g1t_ref) * jnp.dot(a, wo_ref[...], preferred_element_type=F32)
    x1_ref[...] = x1
    h = _norm_mod(x1, gf_ref[...], _pick(is_s, scs_ref, sct_ref), _pick(is_s, shs_ref, sht_ref))
    h_ref[...] = h

    h1, h2, _ = _split3(h)
    r1, r2, _ = _split3(rw_ref[...])
    logits = (lax.dot_general(r1, h1, NT_DIMS, preferred_element_type=F32)
              + lax.dot_general(r1, h2, NT_DIMS, preferred_element_type=F32)
              + lax.dot_general(r2, h1, NT_DIMS, preferred_element_type=F32))
    scores = jax.nn.sigmoid(logits)
    sel = scores + rb_ref[...]
    tm = sel.shape[1]
    gsz = N_EXPERTS // N_GROUPS

    sub = lax.broadcasted_iota(I32, (gsz, tm), 0)
    blocks, gscore = [], []
    for g in range(N_GROUPS):
        blk = sel[gsz * g:gsz * (g + 1)]
        m1 = jnp.max(blk, axis=0, keepdims=True)
        first = jnp.min(jnp.where(blk == m1, sub, gsz), axis=0, keepdims=True)
        m2 = jnp.max(jnp.where(sub == first, -jnp.inf, blk), axis=0, keepdims=True)
        blocks.append(blk)
        gscore.append(m1 + m2)
    masked = []
    for g in range(N_GROUPS):
        beaten = jnp.zeros((1, tm), I32)
        for o in range(N_GROUPS):
            if o == g:
                continue
            wins = (gscore[o] > gscore[g]) | ((gscore[o] == gscore[g]) & (o < g))
            beaten = beaten + wins.astype(I32)
        masked.append(jnp.where(beaten < TOPK_GROUPS, blocks[g], -jnp.inf))
    cur = jnp.concatenate(masked, axis=0)

    eid = lax.broadcasted_iota(I32, (N_EXPERTS, tm), 0)
    picked, weights = [], []
    onehot = jnp.zeros((N_EXPERTS, tm), F32)
    for _ in range(TOP_K):
        m = jnp.max(cur, axis=0, keepdims=True)
        idx = jnp.min(jnp.where(cur == m, eid, N_EXPERTS), axis=0, keepdims=True)
        hit = eid == idx
        picked.append(idx)
        weights.append(jnp.sum(jnp.where(hit, scores, 0.0), axis=0, keepdims=True))
        onehot = jnp.where(hit, 1.0, onehot)
        cur = jnp.where(hit, -jnp.inf, cur)
    wsum = weights[0]
    for wk in weights[1:]:
        wsum = wsum + wk
    scale = ROUTED_SCALE / wsum

    before = jnp.dot(onehot.astype(BF16), tri_ref[...], preferred_element_type=F32) + carry_ref[...]
    carry = carry_ref[...] + jnp.sum(onehot, axis=1, keepdims=True)
    carry_ref[...] = carry
    cnt_ref[...] = jnp.broadcast_to(carry, cnt_ref.shape)
    for kk in range(TOP_K):
        eidx_ref[kk:kk + 1, :] = picked[kk]
        w_ref[kk:kk + 1, :] = weights[kk] * scale
        rank_ref[kk:kk + 1, :] = jnp.sum(jnp.where(eid == picked[kk], before, 0.0), axis=0, keepdims=True).astype(I32)


def post_mixer(a_prompt, a_sample, wo_bf, x, mod, gffn, router_t, router_b, tri):
    t = x.shape[0]
    tm = mod.tile
    npt = mod.npt
    row = lambda i: (i, 0)
    col = lambda i: (0, i)
    const = lambda i: (0, 0)
    return pl.pallas_call(
        functools.partial(_post_mixer_kernel, mod.npt),
        out_shape=(jax.ShapeDtypeStruct((t, D_MODEL), F32), jax.ShapeDtypeStruct((t, D_MODEL), F32),
                   jax.ShapeDtypeStruct((TOP_K, t), I32), jax.ShapeDtypeStruct((TOP_K, t), F32),
                   jax.ShapeDtypeStruct((TOP_K, t), I32), jax.ShapeDtypeStruct((N_EXPERTS, 128), F32)),
        grid=(t // tm,),
        in_specs=[pl.BlockSpec((tm, D_MODEL), lambda i: (jnp.minimum(i, npt - 1), 0)),
                  pl.BlockSpec((tm, D_MODEL), lambda i: (jnp.maximum(i - npt, 0), 0)),
                  pl.BlockSpec((D_MODEL, D_MODEL), const), pl.BlockSpec((tm, D_MODEL), row)]
        + mod.specs(2) + [pl.BlockSpec((1, D_MODEL), const)] + mod.specs(4) + mod.specs(3)
        + [pl.BlockSpec((N_EXPERTS, D_MODEL), const), pl.BlockSpec((N_EXPERTS, 1), const),
           pl.BlockSpec((tm, tm), const)],
        out_specs=(pl.BlockSpec((tm, D_MODEL), row), pl.BlockSpec((tm, D_MODEL), row),
                   pl.BlockSpec((TOP_K, tm), col), pl.BlockSpec((TOP_K, tm), col), pl.BlockSpec((TOP_K, tm), col),
                   pl.BlockSpec((N_EXPERTS, 128), const)),
        scratch_shapes=[pltpu.VMEM((N_EXPERTS, 1), F32)],
        compiler_params=_params(("arbitrary",)),
        name="post_mixer",
    )(a_prompt, a_sample, wo_bf, x, *mod.operands(2), gffn.reshape(1, -1), *mod.operands(4), *mod.operands(3),
      router_t, router_b.reshape(-1, 1), tri)


def _row_copy(src, src_row, dst, dst_row, sem):
    return pltpu.make_async_copy(src.at[pl.ds(src_row, 1)], dst.at[pl.ds(dst_row, 1)], sem)


def _dispatch_kernel(zb_ref, h_ref, dest_hbm, xs_hbm, dest_smem, zero_ref, sem_idx, sem_zero, sem_rows):
    i = pl.program_id(0)
    te = h_ref.shape[0]
    idx_copy = pltpu.make_async_copy(dest_hbm.at[i], dest_smem, sem_idx)
    idx_copy.start()

    def zero_copy(e):
        return pltpu.make_async_copy(zero_ref, xs_hbm.at[pl.ds(zb_ref[e] * EXPERT_BLOCK, EXPERT_BLOCK)], sem_zero)

    @pl.when(i == 0)
    def _():
        zero_ref[...] = jnp.zeros(zero_ref.shape, F32)

        def start(e, carry):
            @pl.when(zb_ref[e] >= 0)
            def _():
                zero_copy(e).start()
            return carry

        def wait(e, carry):
            @pl.when(zb_ref[e] >= 0)
            def _():
                zero_copy(e).wait()
            return carry

        lax.fori_loop(0, N_EXPERTS, start, 0)
        lax.fori_loop(0, N_EXPERTS, wait, 0)

    idx_copy.wait()

    def issue(t, carry):
        for kk in range(TOP_K):
            _row_copy(h_ref, t, xs_hbm, dest_smem[t * TOP_K + kk], sem_rows).start()
        return carry

    def drain(t, carry):
        for kk in range(TOP_K):
            _row_copy(h_ref, t, xs_hbm, dest_smem[t * TOP_K + kk], sem_rows).wait()
        return carry

    lax.fori_loop(0, te, issue, 0)
    lax.fori_loop(0, te, drain, 0)


def moe_dispatch(h, dest_tiles, zero_blocks, n_slots):
    t = h.shape[0]
    te = ROUTE_TILE
    return pl.pallas_call(
        _dispatch_kernel,
        out_shape=jax.ShapeDtypeStruct((n_slots, D_MODEL), F32),
        grid_spec=pltpu.PrefetchScalarGridSpec(
            num_scalar_prefetch=1,
            grid=(t // te,),
            in_specs=[pl.BlockSpec((te, D_MODEL), lambda i, zb: (i, 0)), pl.BlockSpec(memory_space=pl.ANY)],
            out_specs=pl.BlockSpec(memory_space=pl.ANY),
            scratch_shapes=[pltpu.SMEM((te * TOP_K,), I32), pltpu.VMEM((EXPERT_BLOCK, D_MODEL), F32),
                            pltpu.SemaphoreType.DMA, pltpu.SemaphoreType.DMA, pltpu.SemaphoreType.DMA],
        ),
        compiler_params=_params(("arbitrary",)),
        name="moe_dispatch",
    )(zero_blocks, h, dest_tiles)


def _expert_kernel(be_ref, nu_ref, xs_ref, wg_ref, wu_ref, wd_ref, ys_ref):
    @pl.when(pl.program_id(0) < nu_ref[0])
    def _():
        x = xs_ref[...].astype(BF16)
        hg = jnp.dot(x, wg_ref[...], preferred_element_type=F32)
        hu = jnp.dot(x, wu_ref[...], preferred_element_type=F32)
        ys_ref[...] = jnp.dot((_silu(hg) * hu).astype(BF16), wd_ref[...], preferred_element_type=F32)


def moe_experts(xs, block_expert, n_used, wg_bf, wu_bf, wd_bf):
    n_slots = xs.shape[0]
    rb = EXPERT_BLOCK
    ff = wg_bf.shape[2]
    rows = lambda b, be, nu: (jnp.minimum(b, nu[0] - 1), 0)
    return pl.pallas_call(
        _expert_kernel,
        out_shape=jax.ShapeDtypeStruct((n_slots, D_MODEL), F32),
        grid_spec=pltpu.PrefetchScalarGridSpec(
            num_scalar_prefetch=2,
            grid=(n_slots // rb,),
            in_specs=[pl.BlockSpec((rb, D_MODEL), rows),
                      pl.BlockSpec((None, D_MODEL, ff), lambda b, be, nu: (be[b], 0, 0)),
                      pl.BlockSpec((None, D_MODEL, ff), lambda b, be, nu: (be[b], 0, 0)),
                      pl.BlockSpec((None, ff, D_MODEL), lambda b, be, nu: (be[b], 0, 0))],
            out_specs=pl.BlockSpec((rb, D_MODEL), rows),
        ),
        compiler_params=_params(("arbitrary",)),
        name="moe_experts",
    )(block_expert, n_used, xs, wg_bf, wu_bf, wd_bf)


def _combine_kernel(npt, final, h_ref, swg_ref, swu_ref, swd_ref, x1_ref, g2s_ref, g2t_ref, w_ref, fg_ref,
                    dest_hbm, ys_hbm, o_ref, dest_smem, ybuf_ref, sem_idx, sem_rows):
    i = pl.program_id(0)
    tg = h_ref.shape[0]
    idx_copy = pltpu.make_async_copy(dest_hbm.at[i], dest_smem, sem_idx)
    idx_copy.start()
    idx_copy.wait()

    def issue(t, carry):
        for kk in range(TOP_K):
            _row_copy(ys_hbm, dest_smem[t * TOP_K + kk], ybuf_ref.at[kk], t, sem_rows).start()
        return carry

    def drain(t, carry):
        for kk in range(TOP_K):
            _row_copy(ys_hbm, dest_smem[t * TOP_K + kk], ybuf_ref.at[kk], t, sem_rows).wait()
        return carry

    lax.fori_loop(0, tg, issue, 0)
    hb = h_ref[...].astype(BF16)
    hid = _silu(jnp.dot(hb, swg_ref[...], preferred_element_type=F32)) * jnp.dot(hb, swu_ref[...],
                                                                                 preferred_element_type=F32)
    acc = jnp.dot(hid.astype(BF16), swd_ref[...], preferred_element_type=F32)
    lax.fori_loop(0, tg, drain, 0)
    w = w_ref[...]
    routed = ybuf_ref[0] * w[:, 0:1]
    for kk in range(1, TOP_K):
        routed = routed + ybuf_ref[kk] * w[:, kk:kk + 1]
    x2 = x1_ref[...] + _pick(i >= npt, g2s_ref, g2t_ref) * (routed + acc)
    if final:
        ms = jnp.mean(x2 * x2, axis=-1, keepdims=True)
        x2 = x2 * lax.rsqrt(ms + NORM_EPS) * fg_ref[...]
    o_ref[...] = x2


def moe_combine(h, swg_bf, swu_bf, swd_bf, x1, mod, w_tok, final_g, dest_tiles, ys, final):
    t = h.shape[0]
    tg = mod.tile
    row = lambda i: (i, 0)
    const = lambda i: (0, 0)
    return pl.pallas_call(
        functools.partial(_combine_kernel, mod.npt, final),
        out_shape=jax.ShapeDtypeStruct((t, D_MODEL), F32),
        grid=(t // tg,),
        in_specs=[pl.BlockSpec((tg, D_MODEL), row), pl.BlockSpec(swg_bf.shape, const), pl.BlockSpec(swu_bf.shape, const),
                  pl.BlockSpec(swd_bf.shape, const), pl.BlockSpec((tg, D_MODEL), row)]
        + mod.specs(5)
        + [pl.BlockSpec((tg, TOP_K), row), pl.BlockSpec((1, D_MODEL), const),
           pl.BlockSpec(memory_space=pl.ANY), pl.BlockSpec(memory_space=pl.ANY)],
        out_specs=pl.BlockSpec((tg, D_MODEL), row),
        scratch_shapes=[pltpu.SMEM((tg * TOP_K,), I32), pltpu.VMEM((TOP_K, tg, D_MODEL), F32),
                        pltpu.SemaphoreType.DMA, pltpu.SemaphoreType.DMA],
        compiler_params=_params(("arbitrary",)),
        name="moe_combine",
    )(h, swg_bf, swu_bf, swd_bf, x1, *mod.operands(5), w_tok, final_g.reshape(1, -1), dest_tiles, ys)


def _routing_tables(eidx_t, rank_t, counts, n_blocks):
    rb = EXPERT_BLOCK
    counts = counts.astype(I32)
    padded = (counts + rb - 1) // rb * rb
    pad_end = jnp.cumsum(padded)
    pad_start = pad_end - padded
    dest_t = pad_start[eidx_t] + rank_t
    n_used = pad_end[-1] // rb
    blocks = jnp.arange(n_blocks, dtype=I32)
    block_expert = jnp.minimum(jnp.searchsorted(pad_end, blocks * rb, side='right'), N_EXPERTS - 1).astype(I32)
    block_expert = jnp.where(blocks < n_used, block_expert, block_expert[jnp.maximum(n_used - 1, 0)])
    zero_blocks = jnp.where(counts % rb != 0, pad_end // rb - 1, -1).astype(I32)
    return dest_t, block_expert, n_used.reshape(1).astype(I32), zero_blocks


def moe_layer(h, x1, eidx_t, w_t, rank_t, counts, mod_route, wg_bf, wu_bf, wd_bf, swg_bf, swu_bf, swd_bf,
              final_g, final):
    t = h.shape[0]
    rb = EXPERT_BLOCK
    tr = ROUTE_TILE
    n_blocks = -(-(t * TOP_K) // rb) + N_EXPERTS
    dest_t, block_expert, n_used, zero_blocks = _routing_tables(eidx_t, rank_t, counts, n_blocks)
    dest_tiles = dest_t.T.reshape(t // tr, tr * TOP_K)
    xs = moe_dispatch(h, dest_tiles, zero_blocks, n_blocks * rb)
    ys = moe_experts(xs, block_expert, n_used, wg_bf, wu_bf, wd_bf)
    return moe_combine(h, swg_bf, swu_bf, swd_bf, x1, mod_route, w_t.T, final_g, dest_tiles, ys, final)


def _rope_tables(n_batch, seq_len, n_seq, n_new):
    half = SWA_HEAD_DIM // 2
    inv = ROPE_THETA ** (-jnp.arange(half, dtype=F32) / half)
    pos = jnp.concatenate([jnp.tile(jnp.arange(seq_len, dtype=F32), n_batch),
                           jnp.tile(PAST_LEN + jnp.arange(n_new, dtype=F32), n_seq)])
    ang = pos[:, None] * inv[None, :]
    cos = jnp.tile(jnp.cos(ang), (1, 128 // half))
    sin = jnp.sin(ang)
    sin = jnp.tile(jnp.concatenate([-sin, sin], axis=1), (1, 128 // SWA_HEAD_DIM))
    return cos, sin


def kernel(x_prompt, x_sample, c_prompt, c_sample, cache_swa_k, cache_swa_v, state_gla, norm_mix_g, norm_ffn_g,
           final_g, ada_w, ada_b, swa_wqkv, swa_sinks, swa_wo, gla_win, gla_wa1, gla_wa2, gla_ba, gla_norm_g,
           gla_wo, moe_router, moe_bias, moe_wg, moe_wu, moe_wd, shared_wg, shared_wu, shared_wd):
    n_batch, seq_len, d = x_prompt.shape
    n_seq, n_new, _ = x_sample.shape
    depth = ada_w.shape[0]
    tp = n_batch * seq_len
    ts = n_seq * n_new
    t = tp + ts
    tm = TOKEN_TILE
    tr = ROUTE_TILE

    x = jnp.concatenate([x_prompt.reshape(tp, d), x_sample.reshape(ts, d)], axis=0)
    c_all = jnp.concatenate([jnp.repeat(c_sample, n_new, axis=0), c_prompt], axis=0)
    mod = ada_modulation(c_all, ada_w, ada_b)
    cos_tab, sin_tab = _rope_tables(n_batch, seq_len, n_seq, n_new)
    tri = jnp.triu(jnp.ones((tm, tm), BF16), k=1)

    new_k, new_v, new_s = [], [], []
    new_k_s, new_v_s, new_s_s = [], [], []
    for layer in range(depth):
        mod_tok = _Mod(mod, layer, n_batch, seq_len, tm)
        mod_route = _Mod(mod, layer, n_batch, seq_len, tr)
        m = layer // 2
        if layer % 2 == 0:
            q, k, v = swa_qkv(x, norm_mix_g[layer], mod_tok, swa_wqkv[m].astype(BF16), cos_tab, sin_tab)
            a_p = swa_prompt_attention(q, k, v, swa_sinks[m], n_batch, seq_len)
            nk = SWA_KV_HEADS * SWA_HEAD_DIM
            a_s, ck, cv = swa_sample_attention(q, k, v, cache_swa_k[m].reshape(n_seq, WINDOW, nk),
                                             cache_swa_v[m].reshape(n_seq, WINDOW, nk), swa_sinks[m], tp, n_new)
            kv_shape = (n_batch, WINDOW, SWA_KV_HEADS, SWA_HEAD_DIM)
            new_k.append(k[:tp].reshape(n_batch, seq_len, nk)[:, seq_len - WINDOW:].reshape(kv_shape))
            new_v.append(v[:tp].reshape(n_batch, seq_len, nk)[:, seq_len - WINDOW:].reshape(kv_shape))
            new_k_s.append(ck.reshape(n_seq, WINDOW, SWA_KV_HEADS, SWA_HEAD_DIM))
            new_v_s.append(cv.reshape(n_seq, WINDOW, SWA_KV_HEADS, SWA_HEAD_DIM))
            wo = swa_wo[m]
        else:
            pad = jnp.zeros((d, 128 - GLA_GATE_RANK), F32)
            win_ext = jnp.concatenate([gla_win[m], gla_wa1[m], pad], axis=1).astype(BF16)
            wa2_pad = jnp.concatenate([gla_wa2[m], jnp.zeros((128 - GLA_GATE_RANK, GLA_KEY_DIM), F32)],
                                      axis=0).astype(BF16)
            q, k, v, r, gate = gla_project(x, norm_mix_g[layer], mod_tok, win_ext, wa2_pad, gla_ba[m])
            a_p, s_prompt = gla_prompt(q, k, gate, v, r, gla_norm_g[m], n_batch, seq_len)
            a_s, s_sample = gla_sample(q, k, gate, v, r, gla_norm_g[m], state_gla[m], tp, n_new)
            new_s.append(s_prompt)
            new_s_s.append(s_sample)
            wo = gla_wo[m]
        x1, h, eidx_t, w_t, rank_t, cnt = post_mixer(a_p, a_s, wo.astype(BF16), x, mod_tok, norm_ffn_g[layer],
                                                     moe_router[layer].T, moe_bias[layer], tri)
        x = moe_layer(h, x1, eidx_t, w_t, rank_t, cnt[:, 0], mod_route,
                      moe_wg[layer].astype(BF16), moe_wu[layer].astype(BF16), moe_wd[layer].astype(BF16),
                      shared_wg[layer].astype(BF16), shared_wu[layer].astype(BF16), shared_wd[layer].astype(BF16),
                      final_g, layer == depth - 1)

    y_prompt = x[:tp].reshape(n_batch, seq_len, d)
    y_sample = x[tp:].reshape(n_seq, n_new, d)
    return (y_prompt, y_sample, jnp.stack(new_k), jnp.stack(new_v), jnp.stack(new_k_s), jnp.stack(new_v_s),
            jnp.stack(new_s), jnp.stack(new_s_s))
```

```python
import functools

import jax
import jax.numpy as jnp
from jax import lax
from jax.experimental import pallas as pl
from jax.experimental.pallas import tpu as pltpu

F32 = jnp.float32
BF16 = jnp.bfloat16
I32 = jnp.int32

D_MODEL = 1024
PAST_LEN = 8192
SWA_HEAD_DIM = 64
SWA_HEADS = 16
SWA_KV_HEADS = 4
SWA_GROUP = 4
WINDOW = 128
ROPE_THETA = 10000.0
GLA_HEADS = 4
GLA_DK = 128
GLA_DV = 256
GLA_KEY_DIM = 512
GLA_VAL_DIM = 1024
GLA_GATE_RANK = 16
GLA_GATE_NORMALIZER = 16.0
GLA_CHUNK = 64
N_EXPERTS = 64
TOP_K = 8
N_GROUPS = 8
TOPK_GROUPS = 4
EXPERT_FF = 256
ROUTED_SCALE = 2.5
NORM_EPS = 1e-6

TOKEN_TILE = 512
ROUTE_TILE = 256
EXPERT_BLOCK = 256
ADA_TILE = 512
SAMPLE_SEQS = 8
VMEM_LIMIT = 48 * 1024 * 1024

NT_DIMS = (((1,), (1,)), ((), ()))
TN_DIMS = (((0,), (0,)), ((), ()))


def _params(semantics):
    return pltpu.CompilerParams(dimension_semantics=semantics, vmem_limit_bytes=VMEM_LIMIT)


def _silu(x):
    return x * jax.nn.sigmoid(x)


def _norm_mod(x, g, sc, sh):
    ms = jnp.mean(x * x, axis=-1, keepdims=True)
    return (x * lax.rsqrt(ms + NORM_EPS) * g) * (1.0 + sc) + sh


def _split3(x):
    x1 = x.astype(BF16)
    r1 = x - x1.astype(F32)
    x2 = r1.astype(BF16)
    x3 = (r1 - x2.astype(F32)).astype(BF16)
    return x1, x2, x3


def _ada_kernel(c_ref, w_ref, b_ref, o_ref):
    s = _silu(c_ref[...]).astype(BF16)
    o_ref[...] = jnp.dot(s, w_ref[...].astype(BF16), preferred_element_type=F32) + b_ref[...]


def ada_modulation(c_all, ada_w, ada_b):
    depth, d, n = ada_w.shape
    rows = c_all.shape[0]
    return pl.pallas_call(
        _ada_kernel,
        out_shape=jax.ShapeDtypeStruct((depth, rows, n), F32),
        grid=(depth, n // ADA_TILE),
        in_specs=[
            pl.BlockSpec((rows, d), lambda l, j: (0, 0)),
            pl.BlockSpec((None, d, ADA_TILE), lambda l, j: (l, 0, j)),
            pl.BlockSpec((None, 1, ADA_TILE), lambda l, j: (l, 0, j)),
        ],
        out_specs=pl.BlockSpec((None, rows, ADA_TILE), lambda l, j: (l, 0, j)),
        compiler_params=_params(("parallel", "parallel")),
        name="ada_modulation",
    )(c_all, ada_w, ada_b.reshape(depth, 1, n))


class _Mod:
    def __init__(self, mod, layer, n_batch, seq_len, tile):
        depth, rows, n = mod.shape
        self.tile = tile
        self.layer = layer
        self.npt = n_batch * seq_len // tile
        self.mod_tok = mod
        self.mod_seq = mod[:, rows - n_batch:].reshape(depth, n_batch, 1, n)
        self.tiles_per_seq = seq_len // tile

    def operands(self, chunk):
        del chunk
        return [self.mod_seq, self.mod_tok]

    def specs(self, chunk):
        l, npt, tps = self.layer, self.npt, self.tiles_per_seq
        n_seq = self.mod_seq.shape[1]
        seq_spec = pl.BlockSpec((None, None, 1, D_MODEL),
                                lambda i, *_: (l, jnp.minimum(i // tps, n_seq - 1), 0, chunk))
        tok_spec = pl.BlockSpec((None, self.tile, D_MODEL),
                                lambda i, *_: (l, jnp.maximum(i - npt, 0), chunk))
        return [seq_spec, tok_spec]


def _pick(is_sample, seq_ref, tok_ref):
    return jnp.where(is_sample, tok_ref[...], seq_ref[...])


def _swa_qkv_kernel(npt, x_ref, g_ref, scs_ref, sct_ref, shs_ref, sht_ref, w_ref, cos_ref, sin_ref,
                    q_ref, k_ref, v_ref):
    is_s = pl.program_id(0) >= npt
    h = _norm_mod(x_ref[...], g_ref[...], _pick(is_s, scs_ref, sct_ref), _pick(is_s, shs_ref, sht_ref))
    qkv = jnp.dot(h.astype(BF16), w_ref[...], preferred_element_type=F32)
    cos = cos_ref[...]
    sin = sin_ref[...]
    lane = lax.broadcasted_iota(I32, cos.shape, 1)
    first_half = (lane % SWA_HEAD_DIM) < (SWA_HEAD_DIM // 2)

    def rope(xc):
        rot = jnp.where(first_half, pltpu.roll(xc, 128 - SWA_HEAD_DIM // 2, 1), pltpu.roll(xc, SWA_HEAD_DIM // 2, 1))
        return xc * cos + rot * sin

    nq = SWA_HEADS * SWA_HEAD_DIM
    nk = SWA_KV_HEADS * SWA_HEAD_DIM
    for c in range(nq // 128):
        q_ref[:, 128 * c:128 * (c + 1)] = (rope(qkv[:, 128 * c:128 * (c + 1)]) * (SWA_HEAD_DIM ** -0.5)).astype(BF16)
    for c in range(nk // 128):
        k_ref[:, 128 * c:128 * (c + 1)] = rope(qkv[:, nq + 128 * c:nq + 128 * (c + 1)])
    v_ref[...] = qkv[:, nq + nk:]


def swa_qkv(x, g, mod, w_bf, cos_tab, sin_tab):
    t = x.shape[0]
    tm = mod.tile
    nq = SWA_HEADS * SWA_HEAD_DIM
    nk = SWA_KV_HEADS * SWA_HEAD_DIM
    row = lambda i: (i, 0)
    return pl.pallas_call(
        functools.partial(_swa_qkv_kernel, mod.npt),
        out_shape=(jax.ShapeDtypeStruct((t, nq), BF16), jax.ShapeDtypeStruct((t, nk), F32),
                   jax.ShapeDtypeStruct((t, nk), F32)),
        grid=(t // tm,),
        in_specs=[pl.BlockSpec((tm, D_MODEL), row), pl.BlockSpec((1, D_MODEL), lambda i: (0, 0))]
        + mod.specs(1) + mod.specs(0)
        + [pl.BlockSpec(w_bf.shape, lambda i: (0, 0)), pl.BlockSpec((tm, 128), row), pl.BlockSpec((tm, 128), row)],
        out_specs=(pl.BlockSpec((tm, nq), row), pl.BlockSpec((tm, nk), row), pl.BlockSpec((tm, nk), row)),
        compiler_params=_params(("parallel",)),
        name="swa_qkv",
    )(x, g.reshape(1, -1), *mod.operands(1), *mod.operands(0), w_bf, cos_tab, sin_tab)


def _sink_softmax(s, sink_col):
    m = jnp.maximum(jnp.max(s, axis=-1, keepdims=True), sink_col)
    e = jnp.exp(s - m)
    den = jnp.sum(e, axis=-1, keepdims=True) + jnp.exp(sink_col - m)
    return e * (1.0 / den)


def _swa_prompt_kernel(sink_ref, q_ref, kc_ref, kp_ref, vc_ref, vp_ref, o_ref):
    j = pl.program_id(1)
    blk = q_ref.shape[0]
    qi = lax.broadcasted_iota(I32, (blk, 2 * blk), 0)
    sj = lax.broadcasted_iota(I32, (blk, 2 * blk), 1)
    rel = qi + blk - sj
    mask = (rel >= 0) & (rel <= WINDOW) & ((sj >= blk) | (j > 0))
    mask = jnp.concatenate([mask] * SWA_GROUP, axis=0)
    hd = SWA_HEAD_DIM
    outs = []
    for g in range(SWA_KV_HEADS):
        kcat = jnp.concatenate([kp_ref[:, hd * g:hd * (g + 1)], kc_ref[:, hd * g:hd * (g + 1)]], axis=0).astype(BF16)
        vcat = jnp.concatenate([vp_ref[:, hd * g:hd * (g + 1)], vc_ref[:, hd * g:hd * (g + 1)]], axis=0).astype(BF16)
        heads = [SWA_GROUP * g + i for i in range(SWA_GROUP)]
        q4 = jnp.concatenate([q_ref[:, hd * h:hd * (h + 1)] for h in heads], axis=0)
        s = lax.dot_general(q4, kcat, NT_DIMS, preferred_element_type=F32)
        s = jnp.where(mask, s, -jnp.inf)
        sink_col = jnp.concatenate([jnp.full((blk, 1), sink_ref[h], F32) for h in heads], axis=0)
        p = _sink_softmax(s, sink_col).astype(BF16)
        o = jnp.dot(p, vcat, preferred_element_type=F32)
        outs += [o[blk * i:blk * (i + 1)] for i in range(SWA_GROUP)]
    o_ref[...] = jnp.concatenate(outs, axis=1).astype(BF16)


def swa_prompt_attention(q, k, v, sinks, n_batch, seq_len):
    blk = WINDOW
    nb = seq_len // blk
    nq = q.shape[1]
    nk = k.shape[1]
    cur = lambda b, j: (b * nb + j, 0)
    prev = lambda b, j: (b * nb + jnp.maximum(j - 1, 0), 0)
    return pl.pallas_call(
        _swa_prompt_kernel,
        out_shape=jax.ShapeDtypeStruct((n_batch * seq_len, nq), BF16),
        grid=(n_batch, nb),
        in_specs=[pl.BlockSpec(memory_space=pltpu.SMEM),
                  pl.BlockSpec((blk, nq), cur),
                  pl.BlockSpec((blk, nk), cur), pl.BlockSpec((blk, nk), prev),
                  pl.BlockSpec((blk, nk), cur), pl.BlockSpec((blk, nk), prev)],
        out_specs=pl.BlockSpec((blk, nq), cur),
        compiler_params=_params(("parallel", "parallel")),
        name="swa_prompt_attention",
    )(sinks, q, k, k, v, v)


def _swa_sample_kernel(n_new, sink_ref, q_ref, kn_ref, vn_ref, ck_ref, cv_ref, o_ref, nk_ref, nv_ref):
    hd = SWA_HEAD_DIM
    win = ck_ref.shape[1]
    nkeys = win + n_new
    rows = SWA_GROUP * n_new
    ti = lax.broadcasted_iota(I32, (rows, nkeys), 0) % n_new
    si = lax.broadcasted_iota(I32, (rows, nkeys), 1)
    mask = (si >= ti) & (si <= ti + WINDOW)
    qf = q_ref[...].astype(F32)
    for sb in range(ck_ref.shape[0]):
        r0 = sb * n_new
        kc = ck_ref[sb]
        vc = cv_ref[sb]
        kn = kn_ref[r0:r0 + n_new, :]
        vn = vn_ref[r0:r0 + n_new, :]
        nk_ref[sb, 0:win - n_new, :] = kc[n_new:]
        nk_ref[sb, win - n_new:win, :] = kn
        nv_ref[sb, 0:win - n_new, :] = vc[n_new:]
        nv_ref[sb, win - n_new:win, :] = vn
        keys = jnp.concatenate([kc, kn], axis=0)
        vals = jnp.concatenate([vc, vn], axis=0)
        outs = []
        for g in range(SWA_KV_HEADS):
            heads = [SWA_GROUP * g + i for i in range(SWA_GROUP)]
            kg = keys[:, hd * g:hd * (g + 1)].astype(BF16)
            vg = vals[:, hd * g:hd * (g + 1)].astype(BF16)
            qg = jnp.concatenate([qf[r0:r0 + n_new, hd * h:hd * (h + 1)] for h in heads], axis=0).astype(BF16)
            s = lax.dot_general(qg, kg, NT_DIMS, preferred_element_type=F32)
            s = jnp.where(mask, s, -jnp.inf)
            sink_col = jnp.concatenate([jnp.full((n_new, 1), sink_ref[h], F32) for h in heads], axis=0)
            p = _sink_softmax(s, sink_col).astype(BF16)
            o = jnp.dot(p, vg, preferred_element_type=F32)
            outs += [o[n_new * i:n_new * (i + 1)] for i in range(SWA_GROUP)]
        o_ref[r0:r0 + n_new, :] = jnp.concatenate(outs, axis=1).astype(BF16)


def swa_sample_attention(q, k, v, cache_k, cache_v, sinks, n_prompt_rows, n_new):
    n_seq, win, nk = cache_k.shape
    sb = SAMPLE_SEQS
    rows = sb * n_new
    nq = q.shape[1]
    base = n_prompt_rows // rows
    tok = lambda i: (base + i, 0)
    seq = lambda i: (i, 0, 0)
    return pl.pallas_call(
        functools.partial(_swa_sample_kernel, n_new),
        out_shape=(jax.ShapeDtypeStruct((n_seq * n_new, nq), BF16),
                   jax.ShapeDtypeStruct(cache_k.shape, F32), jax.ShapeDtypeStruct(cache_v.shape, F32)),
        grid=(n_seq // sb,),
        in_specs=[pl.BlockSpec(memory_space=pltpu.SMEM),
                  pl.BlockSpec((rows, nq), tok), pl.BlockSpec((rows, nk), tok), pl.BlockSpec((rows, nk), tok),
                  pl.BlockSpec((sb, win, nk), seq), pl.BlockSpec((sb, win, nk), seq)],
        out_specs=(pl.BlockSpec((rows, nq), lambda i: (i, 0)), pl.BlockSpec((sb, win, nk), seq),
                   pl.BlockSpec((sb, win, nk), seq)),
        compiler_params=_params(("parallel",)),
        name="swa_sample_attention",
    )(sinks, q, k, v, cache_k, cache_v)


def _gla_proj_kernel(npt, x_ref, g_ref, scs_ref, sct_ref, shs_ref, sht_ref, w_ref, wa2_ref, ba_ref,
                     q_ref, k_ref, v_ref, r_ref, gate_ref):
    is_s = pl.program_id(0) >= npt
    h = _norm_mod(x_ref[...], g_ref[...], _pick(is_s, scs_ref, sct_ref), _pick(is_s, shs_ref, sht_ref))
    proj = jnp.dot(h.astype(BF16), w_ref[...], preferred_element_type=F32)
    kd = GLA_KEY_DIM
    vd = GLA_VAL_DIM
    q_ref[...] = proj[:, :kd] * (GLA_DK ** -0.5)
    k_ref[...] = proj[:, kd:2 * kd]
    v_ref[...] = proj[:, 2 * kd:2 * kd + vd].astype(BF16)
    r_ref[...] = proj[:, 2 * kd + vd:2 * kd + 2 * vd]
    low = proj[:, 2 * kd + 2 * vd:].astype(BF16)
    z = jnp.dot(low, wa2_ref[...], preferred_element_type=F32) + ba_ref[...]
    log_sig = jnp.minimum(z, 0.0) - jnp.log1p(jnp.exp(-jnp.abs(z)))
    gate_ref[...] = log_sig / GLA_GATE_NORMALIZER


def gla_project(x, g, mod, win_ext, wa2_pad, ba):
    t = x.shape[0]
    tm = mod.tile
    kd, vd = GLA_KEY_DIM, GLA_VAL_DIM
    row = lambda i: (i, 0)
    const = lambda i: (0, 0)
    return pl.pallas_call(
        functools.partial(_gla_proj_kernel, mod.npt),
        out_shape=(jax.ShapeDtypeStruct((t, kd), F32), jax.ShapeDtypeStruct((t, kd), F32),
                   jax.ShapeDtypeStruct((t, vd), BF16), jax.ShapeDtypeStruct((t, vd), F32),
                   jax.ShapeDtypeStruct((t, kd), F32)),
        grid=(t // tm,),
        in_specs=[pl.BlockSpec((tm, D_MODEL), row), pl.BlockSpec((1, D_MODEL), const)]
        + mod.specs(1) + mod.specs(0)
        + [pl.BlockSpec(win_ext.shape, const), pl.BlockSpec(wa2_pad.shape, const), pl.BlockSpec((1, kd), const)],
        out_specs=(pl.BlockSpec((tm, kd), row), pl.BlockSpec((tm, kd), row), pl.BlockSpec((tm, vd), row),
                   pl.BlockSpec((tm, vd), row), pl.BlockSpec((tm, kd), row)),
        compiler_params=_params(("parallel",)),
        name="gla_project",
    )(x, g.reshape(1, -1), *mod.operands(1), *mod.operands(0), win_ext, wa2_pad, ba.reshape(1, -1))


def _cumsum_rows(tri, g):
    g1, g2, g3 = _split3(g)
    return (jnp.dot(tri, g1, preferred_element_type=F32) + jnp.dot(tri, g2, preferred_element_type=F32)
            + jnp.dot(tri, g3, preferred_element_type=F32))


def _diag_attention(q, k, b, n):
    ng = n // 8
    dk = q.shape[1]
    q3 = q.reshape(ng, 8, dk)
    k3 = k.reshape(ng, 8, dk)
    b3 = b.reshape(ng, 8, dk)
    sub = lax.broadcasted_iota(I32, (ng, 8, dk), 1)
    ti = lax.broadcasted_iota(I32, (n, n), 0)
    si = lax.broadcasted_iota(I32, (n, n), 1)
    attn = jnp.zeros((n, n), F32)
    for j in range(8):
        bj = jnp.broadcast_to(b3[:, j:j + 1, :], b3.shape)
        kj = jnp.broadcast_to(k3[:, j:j + 1, :], k3.shape)
        e = jnp.exp(jnp.minimum(b3 - bj, 0.0))
        m = jnp.where(sub >= j, q3 * e * kj, 0.0)
        col = jnp.sum(m, axis=-1, keepdims=True).reshape(n, 1)
        attn = attn + jnp.where(si == (ti // 8) * 8 + j, col, 0.0)
    return attn


def _cross_attention(q, k, b, n):
    ti = lax.broadcasted_iota(I32, (n, n), 0)
    si = lax.broadcasted_iota(I32, (n, n), 1)
    row = lax.broadcasted_iota(I32, b.shape, 0)
    attn = jnp.zeros((n, n), F32)
    m = n // 2
    while m >= 8:
        nblk = n // m
        refq = jnp.concatenate(
            [jnp.broadcast_to(b[i * m - 1:i * m], (m, b.shape[1])) if i % 2 else b[i * m:(i + 1) * m]
             for i in range(nblk)], axis=0)
        refk = jnp.concatenate(
            [b[i * m:(i + 1) * m] if i % 2 else jnp.broadcast_to(b[(i + 1) * m - 1:(i + 1) * m], (m, b.shape[1]))
             for i in range(nblk)], axis=0)
        odd = ((row // m) % 2) == 1
        qt = jnp.where(odd, q * jnp.exp(jnp.minimum(b - refq, 0.0)), 0.0).astype(BF16)
        kt = jnp.where(odd, 0.0, k * jnp.exp(jnp.minimum(refk - b, 0.0))).astype(BF16)
        a = lax.dot_general(qt, kt, NT_DIMS, preferred_element_type=F32)
        keep = (((ti // m) % 2) == 1) & ((si // m) == (ti // m) - 1)
        attn = attn + jnp.where(keep, a, 0.0)
        m //= 2
    return attn


def _gla_epilogue(o, r, ng):
    ms = jnp.mean(o * o, axis=-1, keepdims=True)
    return (o * lax.rsqrt(ms + NORM_EPS) * ng * _silu(r)).astype(BF16)


def _gla_prompt_kernel(q_ref, k_ref, g_ref, v_ref, r_ref, ng_ref, o_ref, so_ref, st_ref):
    c = pl.program_id(1)
    n = q_ref.shape[0]

    @pl.when(c == 0)
    def _():
        st_ref[...] = jnp.zeros(st_ref.shape, F32)

    ti = lax.broadcasted_iota(I32, (n, n), 0)
    si = lax.broadcasted_iota(I32, (n, n), 1)
    tri = jnp.where(ti >= si, 1.0, 0.0).astype(BF16)
    for h in range(GLA_HEADS):
        ks = slice(GLA_DK * h, GLA_DK * (h + 1))
        vs = slice(GLA_DV * h, GLA_DV * (h + 1))
        q = q_ref[:, ks]
        k = k_ref[:, ks]
        v = v_ref[:, vs]
        b = _cumsum_rows(tri, g_ref[:, ks])
        s_t = st_ref[h]
        o = lax.dot_general((q * jnp.exp(b)).astype(BF16), s_t.astype(BF16), NT_DIMS, preferred_element_type=F32)
        attn = _cross_attention(q, k, b, n) + _diag_attention(q, k, b, n)
        o = o + jnp.dot(attn.astype(BF16), v, preferred_element_type=F32)
        bl = b[n - 1:n, :]
        kd = (k * jnp.exp(bl - b)).astype(BF16)
        s_new = s_t * jnp.exp(bl) + lax.dot_general(v, kd, TN_DIMS, preferred_element_type=F32)
        st_ref[h] = s_new
        o_ref[:, vs] = _gla_epilogue(o, r_ref[:, vs], ng_ref[...])

        @pl.when(c == pl.num_programs(1) - 1)
        def _():
            so_ref[h] = s_new.T


def gla_prompt(q, k, g, v, r, norm_g, n_batch, seq_len):
    n = GLA_CHUNK
    nc = seq_len // n
    kd, vd = GLA_KEY_DIM, GLA_VAL_DIM
    row = lambda b, c: (b * nc + c, 0)
    return pl.pallas_call(
        _gla_prompt_kernel,
        out_shape=(jax.ShapeDtypeStruct((n_batch * seq_len, vd), BF16),
                   jax.ShapeDtypeStruct((n_batch, GLA_HEADS, GLA_DK, GLA_DV), F32)),
        grid=(n_batch, nc),
        in_specs=[pl.BlockSpec((n, kd), row), pl.BlockSpec((n, kd), row), pl.BlockSpec((n, kd), row),
                  pl.BlockSpec((n, vd), row), pl.BlockSpec((n, vd), row),
                  pl.BlockSpec((1, GLA_DV), lambda b, c: (0, 0))],
        out_specs=(pl.BlockSpec((n, vd), row),
                   pl.BlockSpec((None, GLA_HEADS, GLA_DK, GLA_DV), lambda b, c: (b, 0, 0, 0))),
        scratch_shapes=[pltpu.VMEM((GLA_HEADS, GLA_DV, GLA_DK), F32)],
        compiler_params=_params(("parallel", "arbitrary")),
        name="gla_prompt",
    )(q, k, g, v, r, norm_g.reshape(1, -1))


def _gla_sample_kernel(n_new, q_ref, k_ref, g_ref, v_ref, r_ref, ng_ref, si_ref, o_ref, so_ref):
    n = q_ref.shape[0]
    ti = lax.broadcasted_iota(I32, (n, n), 0)
    si = lax.broadcasted_iota(I32, (n, n), 1)
    tri = jnp.where((ti >= si) & (ti // n_new == si // n_new), 1.0, 0.0).astype(BF16)
    for h in range(GLA_HEADS):
        ks = slice(GLA_DK * h, GLA_DK * (h + 1))
        vs = slice(GLA_DV * h, GLA_DV * (h + 1))
        q = q_ref[:, ks]
        k = k_ref[:, ks]
        v = v_ref[:, vs]
        b = _cumsum_rows(tri, g_ref[:, ks])
        attn = _diag_attention(q, k, b, n)
        o_intra = jnp.dot(attn.astype(BF16), v, preferred_element_type=F32)
        qe = (q * jnp.exp(b)).astype(BF16)
        n_sb = n // n_new
        last = [b[n_new * (sb + 1) - 1:n_new * (sb + 1), :] for sb in range(n_sb)]
        bl_rows = jnp.concatenate([jnp.broadcast_to(bl, (n_new, GLA_DK)) for bl in last], axis=0)
        kd = (k * jnp.exp(bl_rows - b)).astype(BF16)
        seq_of_row = lax.broadcasted_iota(I32, (n, GLA_DV), 0) // n_new
        o = o_intra
        for sb in range(n_sb):
            mine = seq_of_row == sb
            s_t = si_ref[sb, h].T
            o_sb = lax.dot_general(qe, s_t.astype(BF16), NT_DIMS, preferred_element_type=F32)
            o = o + jnp.where(mine, o_sb, 0.0)
            v_sb = jnp.where(mine, v, jnp.zeros_like(v))
            upd = lax.dot_general(v_sb, kd, TN_DIMS, preferred_element_type=F32)
            so_ref[sb, h] = (s_t * jnp.exp(last[sb]) + upd).T
        o_ref[:, vs] = _gla_epilogue(o, r_ref[:, vs], ng_ref[...])


def gla_sample(q, k, g, v, r, norm_g, state, n_prompt_rows, n_new):
    n_seq = state.shape[0]
    sb = SAMPLE_SEQS
    rows = sb * n_new
    kd, vd = GLA_KEY_DIM, GLA_VAL_DIM
    base = n_prompt_rows // rows
    tok = lambda i: (base + i, 0)
    seq = lambda i: (i, 0, 0, 0)
    sblock = (sb, GLA_HEADS, GLA_DK, GLA_DV)
    return pl.pallas_call(
        functools.partial(_gla_sample_kernel, n_new),
        out_shape=(jax.ShapeDtypeStruct((n_seq * n_new, vd), BF16), jax.ShapeDtypeStruct(state.shape, F32)),
        grid=(n_seq // sb,),
        in_specs=[pl.BlockSpec((rows, kd), tok), pl.BlockSpec((rows, kd), tok), pl.BlockSpec((rows, kd), tok),
                  pl.BlockSpec((rows, vd), tok), pl.BlockSpec((rows, vd), tok),
                  pl.BlockSpec((1, GLA_DV), lambda i: (0, 0)),
                  pl.BlockSpec(sblock, seq)],
        out_specs=(pl.BlockSpec((rows, vd), lambda i: (i, 0)), pl.BlockSpec(sblock, seq)),
        compiler_params=_params(("parallel",)),
        name="gla_sample",
    )(q, k, g, v, r, norm_g.reshape(1, -1), state)


def _post_mixer_kernel(npt, ap_ref, as_ref, wo_ref, x_ref, g1s_ref, g1t_ref, gf_ref, scs_ref, sct_ref, shs_ref, sht_ref,
                       rw_ref, rb_ref, tri_ref,
                       x1_ref, h_ref, eidx_ref, w_ref, rank_ref, cnt_ref, carry_ref):
    i = pl.program_id(0)
    is_s = i >= npt

    @pl.when(i == 0)
    def _():
        carry_ref[...] = jnp.zeros(carry_ref.shape, F32)

    a = jnp.where(is_s, as_ref[...], ap_ref[...])
    x1 = x_ref[...] + _pick(is_s, g1s_ref, g1t_ref) * jnp.dot(a, wo_ref[...], preferred_element_type=F32)
    x1_ref[...] = x1
    h = _norm_mod(x1, gf_ref[...], _pick(is_s, scs_ref, sct_ref), _pick(is_s, shs_ref, sht_ref))
    h_ref[...] = h

    h1, h2, _ = _split3(h)
    r1, r2, _ = _split3(rw_ref[...])
    logits = (lax.dot_general(r1, h1, NT_DIMS, preferred_element_type=F32)
              + lax.dot_general(r1, h2, NT_DIMS, preferred_element_type=F32)
              + lax.dot_general(r2, h1, NT_DIMS, preferred_element_type=F32))
    scores = jax.nn.sigmoid(logits)
    sel = scores + rb_ref[...]
    tm = sel.shape[1]
    gsz = N_EXPERTS // N_GROUPS

    sub = lax.broadcasted_iota(I32, (gsz, tm), 0)
    blocks, gscore = [], []
    for g in range(N_GROUPS):
        blk = sel[gsz * g:gsz * (g + 1)]
        m1 = jnp.max(blk, axis=0, keepdims=True)
        first = jnp.min(jnp.where(blk == m1, sub, gsz), axis=0, keepdims=True)
        m2 = jnp.max(jnp.where(sub == first, -jnp.inf, blk), axis=0, keepdims=True)
        blocks.append(blk)
        gscore.append(m1 + m2)
    masked = []
    for g in range(N_GROUPS):
        beaten = jnp.zeros((1, tm), I32)
        for o in range(N_GROUPS):
            if o == g:
                continue
            wins = (gscore[o] > gscore[g]) | ((gscore[o] == gscore[g]) & (o < g))
            beaten = beaten + wins.astype(I32)
        masked.append(jnp.where(beaten < TOPK_GROUPS, blocks[g], -jnp.inf))
    cur = jnp.concatenate(masked, axis=0)

    eid = lax.broadcasted_iota(I32, (N_EXPERTS, tm), 0)
    picked, weights = [], []
    onehot = jnp.zeros((N_EXPERTS, tm), F32)
    for _ in range(TOP_K):
        m = jnp.max(cur, axis=0, keepdims=True)
        idx = jnp.min(jnp.where(cur == m, eid, N_EXPERTS), axis=0, keepdims=True)
        hit = eid == idx
        picked.append(idx)
        weights.append(jnp.sum(jnp.where(hit, scores, 0.0), axis=0, keepdims=True))
        onehot = jnp.where(hit, 1.0, onehot)
        cur = jnp.where(hit, -jnp.inf, cur)
    wsum = weights[0]
    for wk in weights[1:]:
        wsum = wsum + wk
    scale = ROUTED_SCALE / wsum

    before = jnp.dot(onehot.astype(BF16), tri_ref[...], preferred_element_type=F32) + carry_ref[...]
    carry = carry_ref[...] + jnp.sum(onehot, axis=1, keepdims=True)
    carry_ref[...] = carry
    cnt_ref[...] = jnp.broadcast_to(carry, cnt_ref.shape)
    for kk in range(TOP_K):
        eidx_ref[kk:kk + 1, :] = picked[kk]
        w_ref[kk:kk + 1, :] = weights[kk] * scale
        rank_ref[kk:kk + 1, :] = jnp.sum(jnp.where(eid == picked[kk], before, 0.0), axis=0, keepdims=True).astype(I32)


def post_mixer(a_prompt, a_sample, wo_bf, x, mod, gffn, router_t, router_b, tri):
    t = x.shape[0]
    tm = mod.tile
    npt = mod.npt
    row = lambda i: (i, 0)
    col = lambda i: (0, i)
    const = lambda i: (0, 0)
    return pl.pallas_call(
        functools.partial(_post_mixer_kernel, mod.npt),
        out_shape=(jax.ShapeDtypeStruct((t, D_MODEL), F32), jax.ShapeDtypeStruct((t, D_MODEL), F32),
                   jax.ShapeDtypeStruct((TOP_K, t), I32), jax.ShapeDtypeStruct((TOP_K, t), F32),
                   jax.ShapeDtypeStruct((TOP_K, t), I32), jax.ShapeDtypeStruct((N_EXPERTS, 128), F32)),
        grid=(t // tm,),
        in_specs=[pl.BlockSpec((tm, D_MODEL), lambda i: (jnp.minimum(i, npt - 1), 0)),
                  pl.BlockSpec((tm, D_MODEL), lambda i: (jnp.maximum(i - npt, 0), 0)),
                  pl.BlockSpec((D_MODEL, D_MODEL), const), pl.BlockSpec((tm, D_MODEL), row)]
        + mod.specs(2) + [pl.BlockSpec((1, D_MODEL), const)] + mod.specs(4) + mod.specs(3)
        + [pl.BlockSpec((N_EXPERTS, D_MODEL), const), pl.BlockSpec((N_EXPERTS, 1), const),
           pl.BlockSpec((tm, tm), const)],
        out_specs=(pl.BlockSpec((tm, D_MODEL), row), pl.BlockSpec((tm, D_MODEL), row),
                   pl.BlockSpec((TOP_K, tm), col), pl.BlockSpec((TOP_K, tm), col), pl.BlockSpec((TOP_K, tm), col),
                   pl.BlockSpec((N_EXPERTS, 128), const)),
        scratch_shapes=[pltpu.VMEM((N_EXPERTS, 1), F32)],
        compiler_params=_params(("arbitrary",)),
        name="post_mixer",
    )(a_prompt, a_sample, wo_bf, x, *mod.operands(2), gffn.reshape(1, -1), *mod.operands(4), *mod.operands(3),
      router_t, router_b.reshape(-1, 1), tri)


def _row_copy(src, src_row, dst, dst_row, sem):
    return pltpu.make_async_copy(src.at[pl.ds(src_row, 1)], dst.at[pl.ds(dst_row, 1)], sem)


def _dispatch_kernel(zb_ref, h_ref, dest_hbm, xs_hbm, dest_smem, zero_ref, sem_idx, sem_zero, sem_rows):
    i = pl.program_id(0)
    te = h_ref.shape[0]
    idx_copy = pltpu.make_async_copy(dest_hbm.at[i], dest_smem, sem_idx)
    idx_copy.start()

    def zero_copy(e):
        return pltpu.make_async_copy(zero_ref, xs_hbm.at[pl.ds(zb_ref[e] * EXPERT_BLOCK, EXPERT_BLOCK)], sem_zero)

    @pl.when(i == 0)
    def _():
        zero_ref[...] = jnp.zeros(zero_ref.shape, F32)

        def start(e, carry):
            @pl.when(zb_ref[e] >= 0)
            def _():
                zero_copy(e).start()
            return carry

        def wait(e, carry):
            @pl.when(zb_ref[e] >= 0)
            def _():
                zero_copy(e).wait()
            return carry

        lax.fori_loop(0, N_EXPERTS, start, 0)
        lax.fori_loop(0, N_EXPERTS, wait, 0)

    idx_copy.wait()

    def issue(t, carry):
        for kk in range(TOP_K):
            _row_copy(h_ref, t, xs_hbm, dest_smem[kk * te + t], sem_rows).start()
        return carry

    def drain(t, carry):
        for kk in range(TOP_K):
            _row_copy(h_ref, t, xs_hbm, dest_smem[kk * te + t], sem_rows).wait()
        return carry

    lax.fori_loop(0, te, issue, 0)
    lax.fori_loop(0, te, drain, 0)


def moe_dispatch(h, dest_tiles, zero_blocks, n_slots):
    t = h.shape[0]
    te = ROUTE_TILE
    return pl.pallas_call(
        _dispatch_kernel,
        out_shape=jax.ShapeDtypeStruct((n_slots, D_MODEL), F32),
        grid_spec=pltpu.PrefetchScalarGridSpec(
            num_scalar_prefetch=1,
            grid=(t // te,),
            in_specs=[pl.BlockSpec((te, D_MODEL), lambda i, zb: (i, 0)), pl.BlockSpec(memory_space=pl.ANY)],
            out_specs=pl.BlockSpec(memory_space=pl.ANY),
            scratch_shapes=[pltpu.SMEM((te * TOP_K,), I32), pltpu.VMEM((EXPERT_BLOCK, D_MODEL), F32),
                            pltpu.SemaphoreType.DMA, pltpu.SemaphoreType.DMA, pltpu.SemaphoreType.DMA],
        ),
        compiler_params=_params(("arbitrary",)),
        name="moe_dispatch",
    )(zero_blocks, h, dest_tiles)


def _expert_kernel(be_ref, nu_ref, xs_ref, wg_ref, wu_ref, wd_ref, ys_ref):
    @pl.when(pl.program_id(0) < nu_ref[0])
    def _():
        x = xs_ref[...].astype(BF16)
        hg = jnp.dot(x, wg_ref[...], preferred_element_type=F32)
        hu = jnp.dot(x, wu_ref[...], preferred_element_type=F32)
        ys_ref[...] = jnp.dot((_silu(hg) * hu).astype(BF16), wd_ref[...], preferred_element_type=F32)


def moe_experts(xs, block_expert, n_used, wg_bf, wu_bf, wd_bf):
    n_slots = xs.shape[0]
    rb = EXPERT_BLOCK
    ff = wg_bf.shape[2]
    rows = lambda b, be, nu: (jnp.minimum(b, nu[0] - 1), 0)
    return pl.pallas_call(
        _expert_kernel,
        out_shape=jax.ShapeDtypeStruct((n_slots, D_MODEL), F32),
        grid_spec=pltpu.PrefetchScalarGridSpec(
            num_scalar_prefetch=2,
            grid=(n_slots // rb,),
            in_specs=[pl.BlockSpec((rb, D_MODEL), rows),
                      pl.BlockSpec((None, D_MODEL, ff), lambda b, be, nu: (be[b], 0, 0)),
                      pl.BlockSpec((None, D_MODEL, ff), lambda b, be, nu: (be[b], 0, 0)),
                      pl.BlockSpec((None, ff, D_MODEL), lambda b, be, nu: (be[b], 0, 0))],
            out_specs=pl.BlockSpec((rb, D_MODEL), rows),
        ),
        compiler_params=_params(("arbitrary",)),
        name="moe_experts",
    )(block_expert, n_used, xs, wg_bf, wu_bf, wd_bf)


def _combine_kernel(npt, final, h_ref, swg_ref, swu_ref, swd_ref, x1_ref, g2s_ref, g2t_ref, w_ref, fg_ref,
                    dest_hbm, ys_hbm, o_ref, dest_smem, ybuf_ref, sem_idx, sem_rows):
    i = pl.program_id(0)
    tg = h_ref.shape[0]
    idx_copy = pltpu.make_async_copy(dest_hbm.at[i], dest_smem, sem_idx)
    idx_copy.start()
    idx_copy.wait()

    def issue(t, carry):
        for kk in range(TOP_K):
            _row_copy(ys_hbm, dest_smem[kk * tg + t], ybuf_ref.at[kk], t, sem_rows).start()
        return carry

    def drain(t, carry):
        for kk in range(TOP_K):
            _row_copy(ys_hbm, dest_smem[kk * tg + t], ybuf_ref.at[kk], t, sem_rows).wait()
        return carry

    lax.fori_loop(0, tg, issue, 0)
    hb = h_ref[...].astype(BF16)
    hid = _silu(jnp.dot(hb, swg_ref[...], preferred_element_type=F32)) * jnp.dot(hb, swu_ref[...],
                                                                                 preferred_element_type=F32)
    acc = jnp.dot(hid.astype(BF16), swd_ref[...], preferred_element_type=F32)
    lax.fori_loop(0, tg, drain, 0)
    w = w_ref[...]
    routed = ybuf_ref[0] * w[:, 0:1]
    for kk in range(1, TOP_K):
        routed = routed + ybuf_ref[kk] * w[:, kk:kk + 1]
    x2 = x1_ref[...] + _pick(i >= npt, g2s_ref, g2t_ref) * (routed + acc)
    if final:
        ms = jnp.mean(x2 * x2, axis=-1, keepdims=True)
        x2 = x2 * lax.rsqrt(ms + NORM_EPS) * fg_ref[...]
    o_ref[...] = x2


def moe_combine(h, swg_bf, swu_bf, swd_bf, x1, mod, w_tok, final_g, dest_tiles, ys, final):
    t = h.shape[0]
    tg = mod.tile
    row = lambda i: (i, 0)
    const = lambda i: (0, 0)
    return pl.pallas_call(
        functools.partial(_combine_kernel, mod.npt, final),
        out_shape=jax.ShapeDtypeStruct((t, D_MODEL), F32),
        grid=(t // tg,),
        in_specs=[pl.BlockSpec((tg, D_MODEL), row), pl.BlockSpec(swg_bf.shape, const), pl.BlockSpec(swu_bf.shape, const),
                  pl.BlockSpec(swd_bf.shape, const), pl.BlockSpec((tg, D_MODEL), row)]
        + mod.specs(5)
        + [pl.BlockSpec((tg, TOP_K), row), pl.BlockSpec((1, D_MODEL), const),
           pl.BlockSpec(memory_space=pl.ANY), pl.BlockSpec(memory_space=pl.ANY)],
        out_specs=pl.BlockSpec((tg, D_MODEL), row),
        scratch_shapes=[pltpu.SMEM((tg * TOP_K,), I32), pltpu.VMEM((TOP_K, tg, D_MODEL), F32),
                        pltpu.SemaphoreType.DMA, pltpu.SemaphoreType.DMA],
        compiler_params=_params(("arbitrary",)),
        name="moe_combine",
    )(h, swg_bf, swu_bf, swd_bf, x1, *mod.operands(5), w_tok, final_g.reshape(1, -1), dest_tiles, ys)


def _slot_kernel(eidx_ref, rank_ref, start_ref, o_ref):
    tm = eidx_ref.shape[1]
    tr = o_ref.shape[2]
    eid = lax.broadcasted_iota(I32, (N_EXPERTS, tm), 0)
    start = start_ref[...]
    for kk in range(TOP_K):
        base = jnp.sum(jnp.where(eid == eidx_ref[kk:kk + 1, :], start, 0.0), axis=0, keepdims=True)
        slot = base.astype(I32) + rank_ref[kk:kk + 1, :]
        for j in range(tm // tr):
            o_ref[j, kk:kk + 1, :] = slot[:, tr * j:tr * (j + 1)]


def assignment_slots(eidx_t, rank_t, pad_start):
    t = eidx_t.shape[1]
    tm = TOKEN_TILE
    tr = ROUTE_TILE
    col = lambda i: (0, i)
    out = pl.pallas_call(
        _slot_kernel,
        out_shape=jax.ShapeDtypeStruct((t // tr, TOP_K, tr), I32),
        grid=(t // tm,),
        in_specs=[pl.BlockSpec((TOP_K, tm), col), pl.BlockSpec((TOP_K, tm), col),
                  pl.BlockSpec((N_EXPERTS, 1), lambda i: (0, 0))],
        out_specs=pl.BlockSpec((tm // tr, TOP_K, tr), lambda i: (i, 0, 0)),
        compiler_params=_params(("parallel",)),
        name="assignment_slots",
    )(eidx_t, rank_t, pad_start.astype(F32).reshape(-1, 1))
    return out.reshape(t // tr, TOP_K * tr)


def _routing_tables(counts, n_blocks):
    rb = EXPERT_BLOCK
    counts = counts.astype(I32)
    padded = (counts + rb - 1) // rb * rb
    pad_end = jnp.cumsum(padded)
    pad_start = pad_end - padded
    n_used = pad_end[-1] // rb
    blocks = jnp.arange(n_blocks, dtype=I32)
    block_expert = jnp.sum((pad_end[None, :] <= (blocks * rb)[:, None]).astype(I32), axis=1)
    last_used = jnp.sum((pad_end <= (n_used - 1) * rb).astype(I32))
    block_expert = jnp.minimum(jnp.where(blocks < n_used, block_expert, last_used), N_EXPERTS - 1)
    zero_blocks = jnp.where(counts % rb != 0, pad_end // rb - 1, -1).astype(I32)
    return pad_start, block_expert, n_used.reshape(1).astype(I32), zero_blocks


def moe_layer(h, x1, eidx_t, w_t, rank_t, counts, mod_route, wg_bf, wu_bf, wd_bf, swg_bf, swu_bf, swd_bf,
              final_g, final):
    t = h.shape[0]
    rb = EXPERT_BLOCK
    tr = ROUTE_TILE
    n_blocks = -(-(t * TOP_K) // rb) + N_EXPERTS
    pad_start, block_expert, n_used, zero_blocks = _routing_tables(counts, n_blocks)
    dest_tiles = assignment_slots(eidx_t, rank_t, pad_start)
    xs = moe_dispatch(h, dest_tiles, zero_blocks, n_blocks * rb)
    ys = moe_experts(xs, block_expert, n_used, wg_bf, wu_bf, wd_bf)
    return moe_combine(h, swg_bf, swu_bf, swd_bf, x1, mod_route, w_t.T, final_g, dest_tiles, ys, final)


def _rope_tables(n_batch, seq_len, n_seq, n_new):
    half = SWA_HEAD_DIM // 2
    inv = ROPE_THETA ** (-jnp.arange(half, dtype=F32) / half)
    pos = jnp.concatenate([jnp.tile(jnp.arange(seq_len, dtype=F32), n_batch),
                           jnp.tile(PAST_LEN + jnp.arange(n_new, dtype=F32), n_seq)])
    ang = pos[:, None] * inv[None, :]
    cos = jnp.tile(jnp.cos(ang), (1, 128 // half))
    sin = jnp.sin(ang)
    sin = jnp.tile(jnp.concatenate([-sin, sin], axis=1), (1, 128 // SWA_HEAD_DIM))
    return cos, sin


def kernel(x_prompt, x_sample, c_prompt, c_sample, cache_swa_k, cache_swa_v, state_gla, norm_mix_g, norm_ffn_g,
           final_g, ada_w, ada_b, swa_wqkv, swa_sinks, swa_wo, gla_win, gla_wa1, gla_wa2, gla_ba, gla_norm_g,
           gla_wo, moe_router, moe_bias, moe_wg, moe_wu, moe_wd, shared_wg, shared_wu, shared_wd):
    n_batch, seq_len, d = x_prompt.shape
    n_seq, n_new, _ = x_sample.shape
    depth = ada_w.shape[0]
    tp = n_batch * seq_len
    ts = n_seq * n_new
    t = tp + ts
    tm = TOKEN_TILE
    tr = ROUTE_TILE

    x = jnp.concatenate([x_prompt.reshape(tp, d), x_sample.reshape(ts, d)], axis=0)
    c_all = jnp.concatenate([jnp.repeat(c_sample, n_new, axis=0), c_prompt], axis=0)
    mod = ada_modulation(c_all, ada_w, ada_b)
    cos_tab, sin_tab = _rope_tables(n_batch, seq_len, n_seq, n_new)
    tri = jnp.triu(jnp.ones((tm, tm), BF16), k=1)

    new_k, new_v, new_s = [], [], []
    new_k_s, new_v_s, new_s_s = [], [], []
    for layer in range(depth):
        mod_tok = _Mod(mod, layer, n_batch, seq_len, tm)
        mod_route = _Mod(mod, layer, n_batch, seq_len, tr)
        m = layer // 2
        if layer % 2 == 0:
            q, k, v = swa_qkv(x, norm_mix_g[layer], mod_tok, swa_wqkv[m].astype(BF16), cos_tab, sin_tab)
            a_p = swa_prompt_attention(q, k, v, swa_sinks[m], n_batch, seq_len)
            nk = SWA_KV_HEADS * SWA_HEAD_DIM
            a_s, ck, cv = swa_sample_attention(q, k, v, cache_swa_k[m].reshape(n_seq, WINDOW, nk),
                                             cache_swa_v[m].reshape(n_seq, WINDOW, nk), swa_sinks[m], tp, n_new)
            kv_shape = (n_batch, WINDOW, SWA_KV_HEADS, SWA_HEAD_DIM)
            new_k.append(k[:tp].reshape(n_batch, seq_len, nk)[:, seq_len - WINDOW:].reshape(kv_shape))
            new_v.append(v[:tp].reshape(n_batch, seq_len, nk)[:, seq_len - WINDOW:].reshape(kv_shape))
            new_k_s.append(ck.reshape(n_seq, WINDOW, SWA_KV_HEADS, SWA_HEAD_DIM))
            new_v_s.append(cv.reshape(n_seq, WINDOW, SWA_KV_HEADS, SWA_HEAD_DIM))
            wo = swa_wo[m]
        else:
            pad = jnp.zeros((d, 128 - GLA_GATE_RANK), F32)
            win_ext = jnp.concatenate([gla_win[m], gla_wa1[m], pad], axis=1).astype(BF16)
            wa2_pad = jnp.concatenate([gla_wa2[m], jnp.zeros((128 - GLA_GATE_RANK, GLA_KEY_DIM), F32)],
                                      axis=0).astype(BF16)
            q, k, v, r, gate = gla_project(x, norm_mix_g[layer], mod_tok, win_ext, wa2_pad, gla_ba[m])
            a_p, s_prompt = gla_prompt(q, k, gate, v, r, gla_norm_g[m], n_batch, seq_len)
            a_s, s_sample = gla_sample(q, k, gate, v, r, gla_norm_g[m], state_gla[m], tp, n_new)
            new_s.append(s_prompt)
            new_s_s.append(s_sample)
            wo = gla_wo[m]
        x1, h, eidx_t, w_t, rank_t, cnt = post_mixer(a_p, a_s, wo.astype(BF16), x, mod_tok, norm_ffn_g[layer],
                                                     moe_router[layer].T, moe_bias[layer], tri)
        x = moe_layer(h, x1, eidx_t, w_t, rank_t, cnt[:, 0], mod_route,
                      moe_wg[layer].astype(BF16), moe_wu[layer].astype(BF16), moe_wd[layer].astype(BF16),
                      shared_wg[layer].astype(BF16), shared_wu[layer].astype(BF16), shared_wd[layer].astype(BF16),
                      final_g, layer == depth - 1)

    y_prompt = x[:tp].reshape(n_batch, seq_len, d)
    y_sample = x[tp:].reshape(n_seq, n_new, d)
    return (y_prompt, y_sample, jnp.stack(new_k), jnp.stack(new_v), jnp.stack(new_k_s), jnp.stack(new_v_s),
            jnp.stack(new_s), jnp.stack(new_s_s))
```

```python
import functools

import jax
import jax.numpy as jnp
from jax import lax
from jax.experimental import pallas as pl
from jax.experimental.pallas import tpu as pltpu

F32 = jnp.float32
BF16 = jnp.bfloat16
I32 = jnp.int32
U32 = jnp.uint32

D_MODEL = 1024
PAST_LEN = 8192
SWA_HEAD_DIM = 64
SWA_HEADS = 16
SWA_KV_HEADS = 4
SWA_GROUP = 4
WINDOW = 128
ROPE_THETA = 10000.0
GLA_HEADS = 4
GLA_DK = 128
GLA_DV = 256
GLA_KEY_DIM = 512
GLA_VAL_DIM = 1024
GLA_GATE_RANK = 16
GLA_GATE_NORMALIZER = 16.0
GLA_CHUNK = 64
N_EXPERTS = 64
TOP_K = 8
N_GROUPS = 8
TOPK_GROUPS = 4
EXPERT_FF = 256
ROUTED_SCALE = 2.5
NORM_EPS = 1e-6

TOKEN_TILE = 512
ROUTE_TILE = 256
EXPERT_BLOCK = 256
ADA_TILE = 512
SAMPLE_SEQS = 8
GLA_PROMPT_BATCH = 2
VMEM_LIMIT = 48 * 1024 * 1024

NT_DIMS = (((1,), (1,)), ((), ()))
TN_DIMS = (((0,), (0,)), ((), ()))


def _params(semantics):
    return pltpu.CompilerParams(dimension_semantics=semantics, vmem_limit_bytes=VMEM_LIMIT)


def _silu(x):
    return x * jax.nn.sigmoid(x)


def _norm_mod(x, g, sc, sh):
    ms = jnp.mean(x * x, axis=-1, keepdims=True)
    return (x * lax.rsqrt(ms + NORM_EPS) * g) * (1.0 + sc) + sh


def _pack_bf16_pairs(x):
    half = x.shape[1] // 2
    xb = x.astype(BF16).astype(F32)
    lo = lax.bitcast_convert_type(xb[:, :half], U32) >> 16
    hi = lax.bitcast_convert_type(xb[:, half:], U32) & jnp.uint32(0xFFFF0000)
    return lo | hi


def _unpack_bf16_pairs(u):
    lo = lax.bitcast_convert_type(u << 16, F32)
    hi = lax.bitcast_convert_type(u & jnp.uint32(0xFFFF0000), F32)
    return jnp.concatenate([lo, hi], axis=1).astype(BF16)


def _split3(x):
    x1 = x.astype(BF16)
    r1 = x - x1.astype(F32)
    x2 = r1.astype(BF16)
    x3 = (r1 - x2.astype(F32)).astype(BF16)
    return x1, x2, x3


def _ada_kernel(c_ref, w_ref, b_ref, o_ref):
    s = _silu(c_ref[...]).astype(BF16)
    o_ref[...] = jnp.dot(s, w_ref[...].astype(BF16), preferred_element_type=F32) + b_ref[...]


def ada_modulation(c_all, ada_w, ada_b):
    depth, d, n = ada_w.shape
    rows = c_all.shape[0]
    return pl.pallas_call(
        _ada_kernel,
        out_shape=jax.ShapeDtypeStruct((depth, rows, n), F32),
        grid=(depth, n // ADA_TILE),
        in_specs=[
            pl.BlockSpec((rows, d), lambda l, j: (0, 0)),
            pl.BlockSpec((None, d, ADA_TILE), lambda l, j: (l, 0, j)),
            pl.BlockSpec((None, 1, ADA_TILE), lambda l, j: (l, 0, j)),
        ],
        out_specs=pl.BlockSpec((None, rows, ADA_TILE), lambda l, j: (l, 0, j)),
        compiler_params=_params(("parallel", "parallel")),
        name="ada_modulation",
    )(c_all, ada_w, ada_b.reshape(depth, 1, n))


class _Mod:
    def __init__(self, mod, layer, n_batch, seq_len, tile):
        depth, rows, n = mod.shape
        self.tile = tile
        self.layer = layer
        self.npt = n_batch * seq_len // tile
        self.mod_tok = mod
        self.mod_seq = mod[:, rows - n_batch:].reshape(depth, n_batch, 1, n)
        self.tiles_per_seq = seq_len // tile

    def operands(self, chunk):
        del chunk
        return [self.mod_seq, self.mod_tok]

    def specs(self, chunk):
        l, npt, tps = self.layer, self.npt, self.tiles_per_seq
        n_seq = self.mod_seq.shape[1]
        seq_spec = pl.BlockSpec((None, None, 1, D_MODEL),
                                lambda i, *_: (l, jnp.minimum(i // tps, n_seq - 1), 0, chunk))
        tok_spec = pl.BlockSpec((None, self.tile, D_MODEL),
                                lambda i, *_: (l, jnp.maximum(i - npt, 0), chunk))
        return [seq_spec, tok_spec]


def _pick(is_sample, seq_ref, tok_ref):
    return jnp.where(is_sample, tok_ref[...], seq_ref[...])


def _swa_qkv_kernel(npt, x_ref, g_ref, scs_ref, sct_ref, shs_ref, sht_ref, w_ref, cos_ref, sin_ref,
                    q_ref, k_ref, v_ref, kd_ref, vd_ref):
    is_s = pl.program_id(0) >= npt
    h = _norm_mod(x_ref[...], g_ref[...], _pick(is_s, scs_ref, sct_ref), _pick(is_s, shs_ref, sht_ref))
    qkv = jnp.dot(h.astype(BF16), w_ref[...], preferred_element_type=F32)
    cos = cos_ref[...]
    sin = sin_ref[...]
    lane = lax.broadcasted_iota(I32, cos.shape, 1)
    first_half = (lane % SWA_HEAD_DIM) < (SWA_HEAD_DIM // 2)

    def rope(xc):
        rot = jnp.where(first_half, pltpu.roll(xc, 128 - SWA_HEAD_DIM // 2, 1), pltpu.roll(xc, SWA_HEAD_DIM // 2, 1))
        return xc * cos + rot * sin

    nq = SWA_HEADS * SWA_HEAD_DIM
    nk = SWA_KV_HEADS * SWA_HEAD_DIM
    for c in range(nq // 128):
        q_ref[:, 128 * c:128 * (c + 1)] = (rope(qkv[:, 128 * c:128 * (c + 1)]) * (SWA_HEAD_DIM ** -0.5)).astype(BF16)
    low = lane < SWA_HEAD_DIM

    def spread(chunk):
        rolled = pltpu.roll(chunk, SWA_HEAD_DIM, 1)
        return jnp.where(low, chunk, rolled).astype(BF16), jnp.where(low, rolled, chunk).astype(BF16)

    for c in range(nk // 128):
        kc = rope(qkv[:, nq + 128 * c:nq + 128 * (c + 1)])
        vc = qkv[:, nq + nk + 128 * c:nq + nk + 128 * (c + 1)]
        k_ref[:, 128 * c:128 * (c + 1)] = kc
        v_ref[:, 128 * c:128 * (c + 1)] = vc
        kd_ref[:, 256 * c:256 * c + 128], kd_ref[:, 256 * c + 128:256 * (c + 1)] = spread(kc)
        vd_ref[:, 256 * c:256 * c + 128], vd_ref[:, 256 * c + 128:256 * (c + 1)] = spread(vc)


def swa_qkv(x, g, mod, w_bf, cos_tab, sin_tab):
    t = x.shape[0]
    tm = mod.tile
    nq = SWA_HEADS * SWA_HEAD_DIM
    nk = SWA_KV_HEADS * SWA_HEAD_DIM
    row = lambda i: (i, 0)
    return pl.pallas_call(
        functools.partial(_swa_qkv_kernel, mod.npt),
        out_shape=(jax.ShapeDtypeStruct((t, nq), BF16), jax.ShapeDtypeStruct((t, nk), F32),
                   jax.ShapeDtypeStruct((t, nk), F32), jax.ShapeDtypeStruct((t, 2 * nk), BF16),
                   jax.ShapeDtypeStruct((t, 2 * nk), BF16)),
        grid=(t // tm,),
        in_specs=[pl.BlockSpec((tm, D_MODEL), row), pl.BlockSpec((1, D_MODEL), lambda i: (0, 0))]
        + mod.specs(1) + mod.specs(0)
        + [pl.BlockSpec(w_bf.shape, lambda i: (0, 0)), pl.BlockSpec((tm, 128), row), pl.BlockSpec((tm, 128), row)],
        out_specs=(pl.BlockSpec((tm, nq), row), pl.BlockSpec((tm, nk), row), pl.BlockSpec((tm, nk), row),
                   pl.BlockSpec((tm, 2 * nk), row), pl.BlockSpec((tm, 2 * nk), row)),
        compiler_params=_params(("parallel",)),
        name="swa_qkv",
    )(x, g.reshape(1, -1), *mod.operands(1), *mod.operands(0), w_bf, cos_tab, sin_tab)


def _sink_softmax(s, sink_col):
    m = jnp.maximum(jnp.max(s, axis=-1, keepdims=True), sink_col)
    e = jnp.exp(s - m)
    den = jnp.sum(e, axis=-1, keepdims=True) + jnp.exp(sink_col - m)
    return e * (1.0 / den)


def _swa_prompt_kernel(sink_ref, q_ref, kc_ref, kp_ref, vc_ref, vp_ref, o_ref):
    j = pl.program_id(1)
    blk = q_ref.shape[0]
    qi = lax.broadcasted_iota(I32, (blk, 2 * blk), 0)
    sj = lax.broadcasted_iota(I32, (blk, 2 * blk), 1)
    rel = qi + blk - sj
    mask = (rel >= 0) & (rel <= WINDOW) & ((sj >= blk) | (j > 0))
    low = lax.broadcasted_iota(I32, (2 * blk, 128), 1) < SWA_HEAD_DIM
    zero = jnp.zeros((2 * blk, 128), BF16)
    for g in range(SWA_KV_HEADS):
        cs = slice(128 * g, 128 * (g + 1))
        kcat = jnp.concatenate([kp_ref[:, cs], kc_ref[:, cs]], axis=0)
        vcat = jnp.concatenate([vp_ref[:, cs], vc_ref[:, cs]], axis=0)
        kblk = jnp.concatenate([jnp.where(low, kcat, zero), jnp.where(low, zero, kcat)], axis=0)
        vblk = jnp.concatenate([jnp.where(low, vcat, zero), jnp.where(low, zero, vcat)], axis=0)
        for pair in range(SWA_GROUP // 2):
            c = (SWA_GROUP // 2) * g + pair
            s = lax.dot_general(q_ref[:, 128 * c:128 * (c + 1)], kblk, NT_DIMS, preferred_element_type=F32)
            probs = []
            for hh in range(2):
                sh = jnp.where(mask, s[:, 2 * blk * hh:2 * blk * (hh + 1)], -jnp.inf)
                probs.append(_sink_softmax(sh, sink_ref[2 * c + hh]))
            p = jnp.concatenate(probs, axis=1).astype(BF16)
            o_ref[:, 128 * c:128 * (c + 1)] = jnp.dot(p, vblk, preferred_element_type=F32).astype(BF16)


def swa_prompt_attention(q, k, v, sinks, n_batch, seq_len):
    blk = WINDOW
    nb = seq_len // blk
    nq = q.shape[1]
    nk = k.shape[1]
    cur = lambda b, j: (b * nb + j, 0)
    prev = lambda b, j: (b * nb + jnp.maximum(j - 1, 0), 0)
    return pl.pallas_call(
        _swa_prompt_kernel,
        out_shape=jax.ShapeDtypeStruct((n_batch * seq_len, nq), BF16),
        grid=(n_batch, nb),
        in_specs=[pl.BlockSpec(memory_space=pltpu.SMEM),
                  pl.BlockSpec((blk, nq), cur),
                  pl.BlockSpec((blk, nk), cur), pl.BlockSpec((blk, nk), prev),
                  pl.BlockSpec((blk, nk), cur), pl.BlockSpec((blk, nk), prev)],
        out_specs=pl.BlockSpec((blk, nq), cur),
        compiler_params=_params(("parallel", "parallel")),
        name="swa_prompt_attention",
    )(sinks, q, k, k, v, v)


def _swa_sample_kernel(n_new, sink_ref, q_ref, kn_ref, vn_ref, ck_ref, cv_ref, o_ref, nk_ref, nv_ref):
    hd = SWA_HEAD_DIM
    win = ck_ref.shape[1]
    nkeys = win + n_new
    rows = SWA_GROUP * n_new
    ti = lax.broadcasted_iota(I32, (rows, nkeys), 0) % n_new
    si = lax.broadcasted_iota(I32, (rows, nkeys), 1)
    mask = (si >= ti) & (si <= ti + WINDOW)
    qf = q_ref[...].astype(F32)
    for sb in range(ck_ref.shape[0]):
        r0 = sb * n_new
        kc = ck_ref[sb]
        vc = cv_ref[sb]
        kn = kn_ref[r0:r0 + n_new, :]
        vn = vn_ref[r0:r0 + n_new, :]
        nk_ref[sb, 0:win - n_new, :] = kc[n_new:]
        nk_ref[sb, win - n_new:win, :] = kn
        nv_ref[sb, 0:win - n_new, :] = vc[n_new:]
        nv_ref[sb, win - n_new:win, :] = vn
        keys = jnp.concatenate([kc, kn], axis=0)
        vals = jnp.concatenate([vc, vn], axis=0)
        outs = []
        for g in range(SWA_KV_HEADS):
            heads = [SWA_GROUP * g + i for i in range(SWA_GROUP)]
            kg = keys[:, hd * g:hd * (g + 1)].astype(BF16)
            vg = vals[:, hd * g:hd * (g + 1)].astype(BF16)
            qg = jnp.concatenate([qf[r0:r0 + n_new, hd * h:hd * (h + 1)] for h in heads], axis=0).astype(BF16)
            s = lax.dot_general(qg, kg, NT_DIMS, preferred_element_type=F32)
            s = jnp.where(mask, s, -jnp.inf)
            sink_col = jnp.concatenate([jnp.full((n_new, 1), sink_ref[h], F32) for h in heads], axis=0)
            p = _sink_softmax(s, sink_col).astype(BF16)
            o = jnp.dot(p, vg, preferred_element_type=F32)
            outs += [o[n_new * i:n_new * (i + 1)] for i in range(SWA_GROUP)]
        o_ref[r0:r0 + n_new, :] = jnp.concatenate(outs, axis=1).astype(BF16)


def swa_sample_attention(q, k, v, cache_k, cache_v, sinks, n_prompt_rows, n_new):
    n_seq, win, nk = cache_k.shape
    sb = SAMPLE_SEQS
    rows = sb * n_new
    nq = q.shape[1]
    base = n_prompt_rows // rows
    tok = lambda i: (base + i, 0)
    seq = lambda i: (i, 0, 0)
    return pl.pallas_call(
        functools.partial(_swa_sample_kernel, n_new),
        out_shape=(jax.ShapeDtypeStruct((n_seq * n_new, nq), BF16),
                   jax.ShapeDtypeStruct(cache_k.shape, F32), jax.ShapeDtypeStruct(cache_v.shape, F32)),
        grid=(n_seq // sb,),
        in_specs=[pl.BlockSpec(memory_space=pltpu.SMEM),
                  pl.BlockSpec((rows, nq), tok), pl.BlockSpec((rows, nk), tok), pl.BlockSpec((rows, nk), tok),
                  pl.BlockSpec((sb, win, nk), seq), pl.BlockSpec((sb, win, nk), seq)],
        out_specs=(pl.BlockSpec((rows, nq), lambda i: (i, 0)), pl.BlockSpec((sb, win, nk), seq),
                   pl.BlockSpec((sb, win, nk), seq)),
        compiler_params=_params(("parallel",)),
        name="swa_sample_attention",
    )(sinks, q, k, v, cache_k, cache_v)


def _gla_proj_kernel(npt, x_ref, g_ref, scs_ref, sct_ref, shs_ref, sht_ref, w_ref, wa2_ref, ba_ref,
                     q_ref, k_ref, v_ref, r_ref, gate_ref):
    is_s = pl.program_id(0) >= npt
    h = _norm_mod(x_ref[...], g_ref[...], _pick(is_s, scs_ref, sct_ref), _pick(is_s, shs_ref, sht_ref))
    proj = jnp.dot(h.astype(BF16), w_ref[...], preferred_element_type=F32)
    kd = GLA_KEY_DIM
    vd = GLA_VAL_DIM
    q_ref[...] = proj[:, :kd] * (GLA_DK ** -0.5)
    k_ref[...] = proj[:, kd:2 * kd]
    v_ref[...] = proj[:, 2 * kd:2 * kd + vd].astype(BF16)
    r_ref[...] = proj[:, 2 * kd + vd:2 * kd + 2 * vd]
    low = proj[:, 2 * kd + 2 * vd:].astype(BF16)
    z = jnp.dot(low, wa2_ref[...], preferred_element_type=F32) + ba_ref[...]
    log_sig = jnp.minimum(z, 0.0) - jnp.log1p(jnp.exp(-jnp.abs(z)))
    gate_ref[...] = log_sig / GLA_GATE_NORMALIZER


def gla_project(x, g, mod, win_ext, wa2_pad, ba):
    t = x.shape[0]
    tm = mod.tile
    kd, vd = GLA_KEY_DIM, GLA_VAL_DIM
    row = lambda i: (i, 0)
    const = lambda i: (0, 0)
    return pl.pallas_call(
        functools.partial(_gla_proj_kernel, mod.npt),
        out_shape=(jax.ShapeDtypeStruct((t, kd), F32), jax.ShapeDtypeStruct((t, kd), F32),
                   jax.ShapeDtypeStruct((t, vd), BF16), jax.ShapeDtypeStruct((t, vd), F32),
                   jax.ShapeDtypeStruct((t, kd), F32)),
        grid=(t // tm,),
        in_specs=[pl.BlockSpec((tm, D_MODEL), row), pl.BlockSpec((1, D_MODEL), const)]
        + mod.specs(1) + mod.specs(0)
        + [pl.BlockSpec(win_ext.shape, const), pl.BlockSpec(wa2_pad.shape, const), pl.BlockSpec((1, kd), const)],
        out_specs=(pl.BlockSpec((tm, kd), row), pl.BlockSpec((tm, kd), row), pl.BlockSpec((tm, vd), row),
                   pl.BlockSpec((tm, vd), row), pl.BlockSpec((tm, kd), row)),
        compiler_params=_params(("parallel",)),
        name="gla_project",
    )(x, g.reshape(1, -1), *mod.operands(1), *mod.operands(0), win_ext, wa2_pad, ba.reshape(1, -1))


def _cumsum_rows(tri, g):
    n = g.shape[1]
    s = jnp.dot(tri, jnp.concatenate(_split3(g), axis=1), preferred_element_type=F32)
    return s[:, :n] + s[:, n:2 * n] + s[:, 2 * n:]


def _diag_attention(q, k, b, n):
    ng = n // 8
    dk = q.shape[1]
    q3 = q.reshape(ng, 8, dk)
    k3 = k.reshape(ng, 8, dk)
    b3 = b.reshape(ng, 8, dk)
    sub = lax.broadcasted_iota(I32, (ng, 8, dk), 1)
    ti = lax.broadcasted_iota(I32, (n, n), 0)
    si = lax.broadcasted_iota(I32, (n, n), 1)
    attn = jnp.zeros((n, n), F32)
    for j in range(8):
        bj = jnp.broadcast_to(b3[:, j:j + 1, :], b3.shape)
        kj = jnp.broadcast_to(k3[:, j:j + 1, :], k3.shape)
        e = jnp.exp(jnp.minimum(b3 - bj, 0.0))
        m = jnp.where(sub >= j, q3 * e * kj, 0.0)
        col = jnp.sum(m, axis=-1, keepdims=True).reshape(n, 1)
        attn = attn + jnp.where(si == (ti // 8) * 8 + j, col, 0.0)
    return attn


def _cross_attention(q, k, b, n):
    ti = lax.broadcasted_iota(I32, (n, n), 0)
    si = lax.broadcasted_iota(I32, (n, n), 1)
    row = lax.broadcasted_iota(I32, b.shape, 0)
    attn = jnp.zeros((n, n), F32)
    m = n // 2
    while m >= 8:
        nblk = n // m
        refq = jnp.concatenate(
            [jnp.broadcast_to(b[i * m - 1:i * m], (m, b.shape[1])) if i % 2 else b[i * m:(i + 1) * m]
             for i in range(nblk)], axis=0)
        refk = jnp.concatenate(
            [b[i * m:(i + 1) * m] if i % 2 else jnp.broadcast_to(b[(i + 1) * m - 1:(i + 1) * m], (m, b.shape[1]))
             for i in range(nblk)], axis=0)
        odd = ((row // m) % 2) == 1
        qt = jnp.where(odd, q * jnp.exp(jnp.minimum(b - refq, 0.0)), 0.0).astype(BF16)
        kt = jnp.where(odd, 0.0, k * jnp.exp(jnp.minimum(refk - b, 0.0))).astype(BF16)
        a = lax.dot_general(qt, kt, NT_DIMS, preferred_element_type=F32)
        keep = (((ti // m) % 2) == 1) & ((si // m) == (ti // m) - 1)
        attn = attn + jnp.where(keep, a, 0.0)
        m //= 2
    return attn


def _gla_epilogue(o, r, ng):
    ms = jnp.mean(o * o, axis=-1, keepdims=True)
    return (o * lax.rsqrt(ms + NORM_EPS) * ng * _silu(r)).astype(BF16)


def _gla_prompt_kernel(*refs):
    nb = GLA_PROMPT_BATCH
    ins, (ng_ref, o_ref, so_ref, st_ref) = refs[:5 * nb], refs[5 * nb:]
    c = pl.program_id(1)
    n = ins[0].shape[0]

    @pl.when(c == 0)
    def _():
        st_ref[...] = jnp.zeros(st_ref.shape, F32)

    ti = lax.broadcasted_iota(I32, (n, n), 0)
    si = lax.broadcasted_iota(I32, (n, n), 1)
    tri = jnp.where(ti >= si, 1.0, 0.0).astype(BF16)
    for i in range(nb):
        q_ref, k_ref, g_ref, v_ref, r_ref = ins[5 * i:5 * (i + 1)]
        b_all = _cumsum_rows(tri, g_ref[...])
        for h in range(GLA_HEADS):
            ks = slice(GLA_DK * h, GLA_DK * (h + 1))
            vs = slice(GLA_DV * h, GLA_DV * (h + 1))
            q = q_ref[:, ks]
            k = k_ref[:, ks]
            v = v_ref[:, vs]
            b = b_all[:, ks]
            s_t = st_ref[i, h]
            o = lax.dot_general((q * jnp.exp(b)).astype(BF16), s_t.astype(BF16), NT_DIMS,
                                preferred_element_type=F32)
            attn = _cross_attention(q, k, b, n) + _diag_attention(q, k, b, n)
            o = o + jnp.dot(attn.astype(BF16), v, preferred_element_type=F32)
            bl = b[n - 1:n, :]
            kd = (k * jnp.exp(bl - b)).astype(BF16)
            s_new = s_t * jnp.exp(bl) + lax.dot_general(v, kd, TN_DIMS, preferred_element_type=F32)
            st_ref[i, h] = s_new
            o_ref[i, :, vs] = _gla_epilogue(o, r_ref[:, vs], ng_ref[...])

            @pl.when(c == pl.num_programs(1) - 1)
            def _():
                so_ref[i, h] = s_new.T


def gla_prompt(q, k, g, v, r, norm_g, n_batch, seq_len):
    n = GLA_CHUNK
    nb = GLA_PROMPT_BATCH
    nc = seq_len // n
    kd, vd = GLA_KEY_DIM, GLA_VAL_DIM
    in_specs, operands = [], []
    for i in range(nb):
        row = lambda b, c, i=i: ((nb * b + i) * nc + c, 0)
        in_specs += [pl.BlockSpec((n, kd), row), pl.BlockSpec((n, kd), row), pl.BlockSpec((n, kd), row),
                     pl.BlockSpec((n, vd), row), pl.BlockSpec((n, vd), row)]
        operands += [q, k, g, v, r]
    o, state = pl.pallas_call(
        _gla_prompt_kernel,
        out_shape=(jax.ShapeDtypeStruct((n_batch, seq_len, vd), BF16),
                   jax.ShapeDtypeStruct((n_batch, GLA_HEADS, GLA_DK, GLA_DV), F32)),
        grid=(n_batch // nb, nc),
        in_specs=in_specs + [pl.BlockSpec((1, GLA_DV), lambda b, c: (0, 0))],
        out_specs=(pl.BlockSpec((nb, n, vd), lambda b, c: (b, c, 0)),
                   pl.BlockSpec((nb, GLA_HEADS, GLA_DK, GLA_DV), lambda b, c: (b, 0, 0, 0))),
        scratch_shapes=[pltpu.VMEM((nb, GLA_HEADS, GLA_DV, GLA_DK), F32)],
        compiler_params=_params(("parallel", "arbitrary")),
        name="gla_prompt",
    )(*operands, norm_g.reshape(1, -1))
    return o.reshape(n_batch * seq_len, vd), state


def _gla_sample_kernel(n_new, q_ref, k_ref, g_ref, v_ref, r_ref, ng_ref, si_ref, o_ref, so_ref):
    n = q_ref.shape[0]
    ti = lax.broadcasted_iota(I32, (n, n), 0)
    si = lax.broadcasted_iota(I32, (n, n), 1)
    tri = jnp.where((ti >= si) & (ti // n_new == si // n_new), 1.0, 0.0).astype(BF16)
    b_all = _cumsum_rows(tri, g_ref[...])
    for h in range(GLA_HEADS):
        ks = slice(GLA_DK * h, GLA_DK * (h + 1))
        vs = slice(GLA_DV * h, GLA_DV * (h + 1))
        q = q_ref[:, ks]
        k = k_ref[:, ks]
        v = v_ref[:, vs]
        b = b_all[:, ks]
        attn = _diag_attention(q, k, b, n)
        o_intra = jnp.dot(attn.astype(BF16), v, preferred_element_type=F32)
        qe = (q * jnp.exp(b)).astype(BF16)
        n_sb = n // n_new
        last = [b[n_new * (sb + 1) - 1:n_new * (sb + 1), :] for sb in range(n_sb)]
        bl_rows = jnp.concatenate([jnp.broadcast_to(bl, (n_new, GLA_DK)) for bl in last], axis=0)
        kd = (k * jnp.exp(bl_rows - b)).astype(BF16)
        seq_of_row = lax.broadcasted_iota(I32, (n, GLA_DV), 0) // n_new
        o = o_intra
        for sb in range(n_sb):
            mine = seq_of_row == sb
            s_t = si_ref[sb, h].T
            o_sb = lax.dot_general(qe, s_t.astype(BF16), NT_DIMS, preferred_element_type=F32)
            o = o + jnp.where(mine, o_sb, 0.0)
            v_sb = jnp.where(mine, v, jnp.zeros_like(v))
            upd = lax.dot_general(v_sb, kd, TN_DIMS, preferred_element_type=F32)
            so_ref[sb, h] = (s_t * jnp.exp(last[sb]) + upd).T
        o_ref[:, vs] = _gla_epilogue(o, r_ref[:, vs], ng_ref[...])


def gla_sample(q, k, g, v, r, norm_g, state, n_prompt_rows, n_new):
    n_seq = state.shape[0]
    sb = SAMPLE_SEQS
    rows = sb * n_new
    kd, vd = GLA_KEY_DIM, GLA_VAL_DIM
    base = n_prompt_rows // rows
    tok = lambda i: (base + i, 0)
    seq = lambda i: (i, 0, 0, 0)
    sblock = (sb, GLA_HEADS, GLA_DK, GLA_DV)
    return pl.pallas_call(
        functools.partial(_gla_sample_kernel, n_new),
        out_shape=(jax.ShapeDtypeStruct((n_seq * n_new, vd), BF16), jax.ShapeDtypeStruct(state.shape, F32)),
        grid=(n_seq // sb,),
        in_specs=[pl.BlockSpec((rows, kd), tok), pl.BlockSpec((rows, kd), tok), pl.BlockSpec((rows, kd), tok),
                  pl.BlockSpec((rows, vd), tok), pl.BlockSpec((rows, vd), tok),
                  pl.BlockSpec((1, GLA_DV), lambda i: (0, 0)),
                  pl.BlockSpec(sblock, seq)],
        out_specs=(pl.BlockSpec((rows, vd), lambda i: (i, 0)), pl.BlockSpec(sblock, seq)),
        compiler_params=_params(("parallel",)),
        name="gla_sample",
    )(q, k, g, v, r, norm_g.reshape(1, -1), state)


def _post_mixer_kernel(npt, ap_ref, as_ref, wo_ref, x_ref, g1s_ref, g1t_ref, gf_ref, scs_ref, sct_ref, shs_ref, sht_ref,
                       rw_ref, rb_ref, tri_ref,
                       x1_ref, h_ref, eidx_ref, w_ref, rank_ref, cnt_ref, carry_ref):
    i = pl.program_id(0)
    is_s = i >= npt

    @pl.when(i == 0)
    def _():
        carry_ref[...] = jnp.zeros(carry_ref.shape, F32)

    a = jnp.where(is_s, as_ref[...], ap_ref[...])
    x1 = x_ref[...] + _pick(is_s, g1s_ref, g1t_ref) * jnp.dot(a, wo_ref[...], preferred_element_type=F32)
    x1_ref[...] = x1
    h = _norm_mod(x1, gf_ref[...], _pick(is_s, scs_ref, sct_ref), _pick(is_s, shs_ref, sht_ref))
    h_ref[...] = _pack_bf16_pairs(h)

    h1, h2, _ = _split3(h)
    r1, r2, _ = _split3(rw_ref[...])
    logits = (lax.dot_general(r1, h1, NT_DIMS, preferred_element_type=F32)
              + lax.dot_general(r1, h2, NT_DIMS, preferred_element_type=F32)
              + lax.dot_general(r2, h1, NT_DIMS, preferred_element_type=F32))
    scores = jax.nn.sigmoid(logits)
    sel = scores + rb_ref[...]
    tm = sel.shape[1]
    gsz = N_EXPERTS // N_GROUPS

    sub = lax.broadcasted_iota(I32, (gsz, tm), 0)
    blocks, gscore = [], []
    for g in range(N_GROUPS):
        blk = sel[gsz * g:gsz * (g + 1)]
        m1 = jnp.max(blk, axis=0, keepdims=True)
        first = jnp.min(jnp.where(blk == m1, sub, gsz), axis=0, keepdims=True)
        m2 = jnp.max(jnp.where(sub == first, -jnp.inf, blk), axis=0, keepdims=True)
        blocks.append(blk)
        gscore.append(m1 + m2)
    masked = []
    for g in range(N_GROUPS):
        beaten = jnp.zeros((1, tm), I32)
        for o in range(N_GROUPS):
            if o == g:
                continue
            wins = (gscore[o] > gscore[g]) | ((gscore[o] == gscore[g]) & (o < g))
            beaten = beaten + wins.astype(I32)
        masked.append(jnp.where(beaten < TOPK_GROUPS, blocks[g], -jnp.inf))
    cur = jnp.concatenate(masked, axis=0)

    eid = lax.broadcasted_iota(I32, (N_EXPERTS, tm), 0)
    picked, weights = [], []
    onehot = jnp.zeros((N_EXPERTS, tm), F32)
    for _ in range(TOP_K):
        m = jnp.max(cur, axis=0, keepdims=True)
        idx = jnp.min(jnp.where(cur == m, eid, N_EXPERTS), axis=0, keepdims=True)
        hit = eid == idx
        picked.append(idx)
        weights.append(jnp.sum(jnp.where(hit, scores, 0.0), axis=0, keepdims=True))
        onehot = jnp.where(hit, 1.0, onehot)
        cur = jnp.where(hit, -jnp.inf, cur)
    wsum = weights[0]
    for wk in weights[1:]:
        wsum = wsum + wk
    scale = ROUTED_SCALE / wsum

    before = jnp.dot(onehot.astype(BF16), tri_ref[...], preferred_element_type=F32) + carry_ref[...]
    carry = carry_ref[...] + jnp.sum(onehot, axis=1, keepdims=True)
    carry_ref[...] = carry
    cnt_ref[...] = jnp.broadcast_to(carry, cnt_ref.shape)
    for kk in range(TOP_K):
        eidx_ref[kk:kk + 1, :] = picked[kk]
        w_ref[kk:kk + 1, :] = weights[kk] * scale
        rank_ref[kk:kk + 1, :] = jnp.sum(jnp.where(eid == picked[kk], before, 0.0), axis=0, keepdims=True).astype(I32)


def post_mixer(a_prompt, a_sample, wo_bf, x, mod, gffn, router_t, router_b, tri):
    t = x.shape[0]
    tm = mod.tile
    npt = mod.npt
    row = lambda i: (i, 0)
    col = lambda i: (0, i)
    const = lambda i: (0, 0)
    return pl.pallas_call(
        functools.partial(_post_mixer_kernel, mod.npt),
        out_shape=(jax.ShapeDtypeStruct((t, D_MODEL), F32), jax.ShapeDtypeStruct((t, D_MODEL // 2), U32),
                   jax.ShapeDtypeStruct((TOP_K, t), I32), jax.ShapeDtypeStruct((TOP_K, t), F32),
                   jax.ShapeDtypeStruct((TOP_K, t), I32), jax.ShapeDtypeStruct((N_EXPERTS, 128), F32)),
        grid=(t // tm,),
        in_specs=[pl.BlockSpec((tm, D_MODEL), lambda i: (jnp.minimum(i, npt - 1), 0)),
                  pl.BlockSpec((tm, D_MODEL), lambda i: (jnp.maximum(i - npt, 0), 0)),
                  pl.BlockSpec((D_MODEL, D_MODEL), const), pl.BlockSpec((tm, D_MODEL), row)]
        + mod.specs(2) + [pl.BlockSpec((1, D_MODEL), const)] + mod.specs(4) + mod.specs(3)
        + [pl.BlockSpec((N_EXPERTS, D_MODEL), const), pl.BlockSpec((N_EXPERTS, 1), const),
           pl.BlockSpec((tm, tm), const)],
        out_specs=(pl.BlockSpec((tm, D_MODEL), row), pl.BlockSpec((tm, D_MODEL // 2), row),
                   pl.BlockSpec((TOP_K, tm), col), pl.BlockSpec((TOP_K, tm), col), pl.BlockSpec((TOP_K, tm), col),
                   pl.BlockSpec((N_EXPERTS, 128), const)),
        scratch_shapes=[pltpu.VMEM((N_EXPERTS, 1), F32)],
        compiler_params=_params(("arbitrary",)),
        name="post_mixer",
    )(a_prompt, a_sample, wo_bf, x, *mod.operands(2), gffn.reshape(1, -1), *mod.operands(4), *mod.operands(3),
      router_t, router_b.reshape(-1, 1), tri)


def _row_copy(src, src_row, dst, dst_row, sem):
    return pltpu.make_async_copy(src.at[pl.ds(src_row, 1)], dst.at[pl.ds(dst_row, 1)], sem)


def _index_tile_copy(dest_hbm, dest_smem, sem, tile, slot):
    return pltpu.make_async_copy(dest_hbm.at[tile], dest_smem.at[slot], sem.at[slot])


def _prefetch_index_tile(dest_hbm, dest_smem, sem):
    i = pl.program_id(0)
    slot = i % 2

    @pl.when(i == 0)
    def _():
        _index_tile_copy(dest_hbm, dest_smem, sem, 0, 0).start()

    @pl.when(i + 1 < pl.num_programs(0))
    def _():
        _index_tile_copy(dest_hbm, dest_smem, sem, i + 1, 1 - slot).start()

    return slot


def _dispatch_kernel(zb_ref, h_ref, dest_hbm, xs_hbm, dest_smem, zero_ref, sem_idx, sem_zero, sem_rows):
    i = pl.program_id(0)
    te = h_ref.shape[0]
    slot = _prefetch_index_tile(dest_hbm, dest_smem, sem_idx)

    def zero_copy(e):
        return pltpu.make_async_copy(zero_ref, xs_hbm.at[pl.ds(zb_ref[e] * EXPERT_BLOCK, EXPERT_BLOCK)], sem_zero)

    @pl.when(i == 0)
    def _():
        zero_ref[...] = jnp.zeros(zero_ref.shape, U32)

        def start(e, carry):
            @pl.when(zb_ref[e] >= 0)
            def _():
                zero_copy(e).start()
            return carry

        def wait(e, carry):
            @pl.when(zb_ref[e] >= 0)
            def _():
                zero_copy(e).wait()
            return carry

        lax.fori_loop(0, zb_ref.shape[0], start, 0)
        lax.fori_loop(0, zb_ref.shape[0], wait, 0)

    _index_tile_copy(dest_hbm, dest_smem, sem_idx, i, slot).wait()

    def issue(t, carry):
        for kk in range(TOP_K):
            _row_copy(h_ref, t, xs_hbm, dest_smem[slot, kk * te + t], sem_rows).start()
        return carry

    def drain(t, carry):
        for kk in range(TOP_K):
            _row_copy(h_ref, t, xs_hbm, dest_smem[slot, kk * te + t], sem_rows).wait()
        return carry

    lax.fori_loop(0, te, issue, 0)
    lax.fori_loop(0, te, drain, 0)


def moe_dispatch(h, dest_tiles, zero_blocks, n_slots):
    t = h.shape[0]
    te = ROUTE_TILE
    return pl.pallas_call(
        _dispatch_kernel,
        out_shape=jax.ShapeDtypeStruct((n_slots, h.shape[1]), U32),
        grid_spec=pltpu.PrefetchScalarGridSpec(
            num_scalar_prefetch=1,
            grid=(t // te,),
            in_specs=[pl.BlockSpec((te, h.shape[1]), lambda i, zb: (i, 0)), pl.BlockSpec(memory_space=pl.ANY)],
            out_specs=pl.BlockSpec(memory_space=pl.ANY),
            scratch_shapes=[pltpu.SMEM((2, te * TOP_K), I32), pltpu.VMEM((EXPERT_BLOCK, h.shape[1]), U32),
                            pltpu.SemaphoreType.DMA((2,)), pltpu.SemaphoreType.DMA, pltpu.SemaphoreType.DMA],
        ),
        compiler_params=_params(("arbitrary",)),
        name="moe_dispatch",
    )(zero_blocks, h, dest_tiles)


def _expert_kernel(be_ref, nu_ref, xs_ref, wg0_ref, wu0_ref, wd0_ref, wg1_ref, wu1_ref, wd1_ref, ys_ref,
                   wg_bf, wu_bf, wd_bf):
    b = pl.program_id(0)
    rb = EXPERT_BLOCK
    used = 2 * b < nu_ref[0]
    for s, (wg, wu, wd) in enumerate(((wg0_ref, wu0_ref, wd0_ref), (wg1_ref, wu1_ref, wd1_ref))):
        j = 2 * b + s
        fresh = (b == 0) | (be_ref[j] != be_ref[jnp.maximum(j - 2, 0)])

        @pl.when(used & fresh)
        def _():
            wg_bf[s] = wg[...].astype(BF16)
            wu_bf[s] = wu[...].astype(BF16)
            wd_bf[s] = wd[...].astype(BF16)

    @pl.when(used)
    def _():
        for s in range(2):
            x = _unpack_bf16_pairs(xs_ref[rb * s:rb * (s + 1), :])
            hg = jnp.dot(x, wg_bf[s], preferred_element_type=F32)
            hu = jnp.dot(x, wu_bf[s], preferred_element_type=F32)
            ys_ref[rb * s:rb * (s + 1), :] = jnp.dot((_silu(hg) * hu).astype(BF16), wd_bf[s],
                                                     preferred_element_type=F32)


def moe_experts(xs, block_expert, n_used, wg, wu, wd, layer):
    n_slots = xs.shape[0]
    rb = EXPERT_BLOCK
    ff = wg.shape[3]
    rows = lambda b, be, nu: (jnp.minimum(b, (nu[0] - 1) // 2), 0)
    w_in = lambda s: (lambda b, be, nu: (layer, be[2 * b + s], 0, 0))
    return pl.pallas_call(
        _expert_kernel,
        out_shape=jax.ShapeDtypeStruct((n_slots, D_MODEL), F32),
        grid_spec=pltpu.PrefetchScalarGridSpec(
            num_scalar_prefetch=2,
            grid=(n_slots // (2 * rb),),
            in_specs=[pl.BlockSpec((2 * rb, xs.shape[1]), rows)]
            + [pl.BlockSpec((None, None, D_MODEL, ff), w_in(0)), pl.BlockSpec((None, None, D_MODEL, ff), w_in(0)),
               pl.BlockSpec((None, None, ff, D_MODEL), w_in(0)),
               pl.BlockSpec((None, None, D_MODEL, ff), w_in(1)), pl.BlockSpec((None, None, D_MODEL, ff), w_in(1)),
               pl.BlockSpec((None, None, ff, D_MODEL), w_in(1))],
            out_specs=pl.BlockSpec((2 * rb, D_MODEL), rows),
            scratch_shapes=[pltpu.VMEM((2, D_MODEL, ff), BF16), pltpu.VMEM((2, D_MODEL, ff), BF16),
                            pltpu.VMEM((2, ff, D_MODEL), BF16)],
        ),
        compiler_params=_params(("arbitrary",)),
        name="moe_experts",
    )(block_expert, n_used, xs, wg, wu, wd, wg, wu, wd)


def _combine_kernel(npt, final, h_ref, swg_ref, swu_ref, swd_ref, x1_ref, g2s_ref, g2t_ref, w_ref, fg_ref,
                    dest_hbm, ys_hbm, o_ref, dest_smem, ybuf_ref, sem_idx, sem_rows):
    i = pl.program_id(0)
    tg = h_ref.shape[0]
    slot = _prefetch_index_tile(dest_hbm, dest_smem, sem_idx)
    _index_tile_copy(dest_hbm, dest_smem, sem_idx, i, slot).wait()

    def issue(t, carry):
        for kk in range(TOP_K):
            _row_copy(ys_hbm, dest_smem[slot, kk * tg + t], ybuf_ref.at[kk], t, sem_rows).start()
        return carry

    def drain(t, carry):
        for kk in range(TOP_K):
            _row_copy(ys_hbm, dest_smem[slot, kk * tg + t], ybuf_ref.at[kk], t, sem_rows).wait()
        return carry

    lax.fori_loop(0, tg, issue, 0)
    hb = _unpack_bf16_pairs(h_ref[...])
    hid = _silu(jnp.dot(hb, swg_ref[...], preferred_element_type=F32)) * jnp.dot(hb, swu_ref[...],
                                                                                 preferred_element_type=F32)
    acc = jnp.dot(hid.astype(BF16), swd_ref[...], preferred_element_type=F32)
    lax.fori_loop(0, tg, drain, 0)
    w = w_ref[...]
    routed = ybuf_ref[0] * w[:, 0:1]
    for kk in range(1, TOP_K):
        routed = routed + ybuf_ref[kk] * w[:, kk:kk + 1]
    x2 = x1_ref[...] + _pick(i >= npt, g2s_ref, g2t_ref) * (routed + acc)
    if final:
        ms = jnp.mean(x2 * x2, axis=-1, keepdims=True)
        x2 = x2 * lax.rsqrt(ms + NORM_EPS) * fg_ref[...]
    o_ref[...] = x2


def moe_combine(h, swg_bf, swu_bf, swd_bf, x1, mod, w_tok, final_g, dest_tiles, ys, final):
    t = h.shape[0]
    tg = mod.tile
    row = lambda i: (i, 0)
    const = lambda i: (0, 0)
    return pl.pallas_call(
        functools.partial(_combine_kernel, mod.npt, final),
        out_shape=jax.ShapeDtypeStruct((t, D_MODEL), F32),
        grid=(t // tg,),
        in_specs=[pl.BlockSpec((tg, h.shape[1]), row), pl.BlockSpec(swg_bf.shape, const), pl.BlockSpec(swu_bf.shape, const),
                  pl.BlockSpec(swd_bf.shape, const), pl.BlockSpec((tg, D_MODEL), row)]
        + mod.specs(5)
        + [pl.BlockSpec((tg, TOP_K), row), pl.BlockSpec((1, D_MODEL), const),
           pl.BlockSpec(memory_space=pl.ANY), pl.BlockSpec(memory_space=pl.ANY)],
        out_specs=pl.BlockSpec((tg, D_MODEL), row),
        scratch_shapes=[pltpu.SMEM((2, tg * TOP_K), I32), pltpu.VMEM((TOP_K, tg, D_MODEL), F32),
                        pltpu.SemaphoreType.DMA((2,)), pltpu.SemaphoreType.DMA],
        compiler_params=_params(("arbitrary",)),
        name="moe_combine",
    )(h, swg_bf, swu_bf, swd_bf, x1, *mod.operands(5), w_tok, final_g.reshape(1, -1), dest_tiles, ys)


def _slot_kernel(eidx_ref, rank_ref, start_ref, o_ref):
    tm = eidx_ref.shape[1]
    tr = o_ref.shape[2]
    eid = lax.broadcasted_iota(I32, (N_EXPERTS, tm), 0)
    start = start_ref[...]
    for kk in range(TOP_K):
        base = jnp.sum(jnp.where(eid == eidx_ref[kk:kk + 1, :], start, 0.0), axis=0, keepdims=True)
        slot = base.astype(I32) + rank_ref[kk:kk + 1, :]
        for j in range(tm // tr):
            o_ref[j, kk:kk + 1, :] = slot[:, tr * j:tr * (j + 1)]


def assignment_slots(eidx_t, rank_t, pad_start):
    t = eidx_t.shape[1]
    tm = TOKEN_TILE
    tr = ROUTE_TILE
    col = lambda i: (0, i)
    out = pl.pallas_call(
        _slot_kernel,
        out_shape=jax.ShapeDtypeStruct((t // tr, TOP_K, tr), I32),
        grid=(t // tm,),
        in_specs=[pl.BlockSpec((TOP_K, tm), col), pl.BlockSpec((TOP_K, tm), col),
                  pl.BlockSpec((N_EXPERTS, 1), lambda i: (0, 0))],
        out_specs=pl.BlockSpec((tm // tr, TOP_K, tr), lambda i: (i, 0, 0)),
        compiler_params=_params(("parallel",)),
        name="assignment_slots",
    )(eidx_t, rank_t, pad_start.astype(F32).reshape(-1, 1))
    return out.reshape(t // tr, TOP_K * tr)


def _routing_tables(counts, n_blocks):
    rb = EXPERT_BLOCK
    counts = counts.astype(I32)
    padded = (counts + rb - 1) // rb * rb
    pad_end = jnp.cumsum(padded)
    pad_start = pad_end - padded
    n_used = pad_end[-1] // rb
    blocks = jnp.arange(n_blocks, dtype=I32)
    block_expert = jnp.sum((pad_end[None, :] <= (blocks * rb)[:, None]).astype(I32), axis=1)
    last_used = jnp.sum((pad_end <= (n_used - 1) * rb).astype(I32))
    block_expert = jnp.minimum(jnp.where(blocks < n_used, block_expert, last_used), N_EXPERTS - 1)
    zero_blocks = jnp.where(counts % rb != 0, pad_end // rb - 1, -1)
    zero_blocks = jnp.concatenate([zero_blocks, jnp.where(n_used % 2 == 1, n_used, -1).reshape(1)]).astype(I32)
    return pad_start, block_expert, n_used.reshape(1).astype(I32), zero_blocks


def moe_layer(h, x1, eidx_t, w_t, rank_t, counts, mod_route, wg, wu, wd, layer, swg_bf, swu_bf, swd_bf,
              final_g, final):
    t = h.shape[0]
    rb = EXPERT_BLOCK
    tr = ROUTE_TILE
    n_blocks = -(-(t * TOP_K) // rb) + N_EXPERTS
    n_blocks += n_blocks % 2
    pad_start, block_expert, n_used, zero_blocks = _routing_tables(counts, n_blocks)
    dest_tiles = assignment_slots(eidx_t, rank_t, pad_start)
    xs = moe_dispatch(h, dest_tiles, zero_blocks, n_blocks * rb)
    ys = moe_experts(xs, block_expert, n_used, wg, wu, wd, layer)
    return moe_combine(h, swg_bf, swu_bf, swd_bf, x1, mod_route, w_t.T, final_g, dest_tiles, ys, final)


def _rope_tables(n_batch, seq_len, n_seq, n_new):
    half = SWA_HEAD_DIM // 2
    inv = ROPE_THETA ** (-jnp.arange(half, dtype=F32) / half)
    pos = jnp.concatenate([jnp.tile(jnp.arange(seq_len, dtype=F32), n_batch),
                           jnp.tile(PAST_LEN + jnp.arange(n_new, dtype=F32), n_seq)])
    ang = pos[:, None] * inv[None, :]
    cos = jnp.tile(jnp.cos(ang), (1, 128 // half))
    sin = jnp.sin(ang)
    sin = jnp.tile(jnp.concatenate([-sin, sin], axis=1), (1, 128 // SWA_HEAD_DIM))
    return cos, sin


def kernel(x_prompt, x_sample, c_prompt, c_sample, cache_swa_k, cache_swa_v, state_gla, norm_mix_g, norm_ffn_g,
           final_g, ada_w, ada_b, swa_wqkv, swa_sinks, swa_wo, gla_win, gla_wa1, gla_wa2, gla_ba, gla_norm_g,
           gla_wo, moe_router, moe_bias, moe_wg, moe_wu, moe_wd, shared_wg, shared_wu, shared_wd):
    n_batch, seq_len, d = x_prompt.shape
    n_seq, n_new, _ = x_sample.shape
    depth = ada_w.shape[0]
    tp = n_batch * seq_len
    ts = n_seq * n_new
    t = tp + ts
    tm = TOKEN_TILE
    tr = ROUTE_TILE

    x = jnp.concatenate([x_prompt.reshape(tp, d), x_sample.reshape(ts, d)], axis=0)
    c_all = jnp.concatenate([jnp.repeat(c_sample, n_new, axis=0), c_prompt], axis=0)
    mod = ada_modulation(c_all, ada_w, ada_b)
    cos_tab, sin_tab = _rope_tables(n_batch, seq_len, n_seq, n_new)
    tri = jnp.triu(jnp.ones((tm, tm), BF16), k=1)

    new_k, new_v, new_s = [], [], []
    new_k_s, new_v_s, new_s_s = [], [], []
    for layer in range(depth):
        mod_tok = _Mod(mod, layer, n_batch, seq_len, tm)
        mod_route = _Mod(mod, layer, n_batch, seq_len, tr)
        m = layer // 2
        if layer % 2 == 0:
            q, k, v, k_dup, v_dup = swa_qkv(x, norm_mix_g[layer], mod_tok, swa_wqkv[m].astype(BF16), cos_tab, sin_tab)
            a_p = swa_prompt_attention(q, k_dup, v_dup, swa_sinks[m], n_batch, seq_len)
            nk = SWA_KV_HEADS * SWA_HEAD_DIM
            a_s, ck, cv = swa_sample_attention(q, k, v, cache_swa_k[m].reshape(n_seq, WINDOW, nk),
                                             cache_swa_v[m].reshape(n_seq, WINDOW, nk), swa_sinks[m], tp, n_new)
            kv_shape = (n_batch, WINDOW, SWA_KV_HEADS, SWA_HEAD_DIM)
            new_k.append(k[:tp].reshape(n_batch, seq_len, nk)[:, seq_len - WINDOW:].reshape(kv_shape))
            new_v.append(v[:tp].reshape(n_batch, seq_len, nk)[:, seq_len - WINDOW:].reshape(kv_shape))
            new_k_s.append(ck.reshape(n_seq, WINDOW, SWA_KV_HEADS, SWA_HEAD_DIM))
            new_v_s.append(cv.reshape(n_seq, WINDOW, SWA_KV_HEADS, SWA_HEAD_DIM))
            wo = swa_wo[m]
        else:
            pad = jnp.zeros((d, 128 - GLA_GATE_RANK), F32)
            win_ext = jnp.concatenate([gla_win[m], gla_wa1[m], pad], axis=1).astype(BF16)
            wa2_pad = jnp.concatenate([gla_wa2[m], jnp.zeros((128 - GLA_GATE_RANK, GLA_KEY_DIM), F32)],
                                      axis=0).astype(BF16)
            q, k, v, r, gate = gla_project(x, norm_mix_g[layer], mod_tok, win_ext, wa2_pad, gla_ba[m])
            a_p, s_prompt = gla_prompt(q, k, gate, v, r, gla_norm_g[m], n_batch, seq_len)
            a_s, s_sample = gla_sample(q, k, gate, v, r, gla_norm_g[m], state_gla[m], tp, n_new)
            new_s.append(s_prompt)
            new_s_s.append(s_sample)
            wo = gla_wo[m]
        x1, h, eidx_t, w_t, rank_t, cnt = post_mixer(a_p, a_s, wo.astype(BF16), x, mod_tok, norm_ffn_g[layer],
                                                     moe_router[layer].T, moe_bias[layer], tri)
        x = moe_layer(h, x1, eidx_t, w_t, rank_t, cnt[:, 0], mod_route,
                      moe_wg, moe_wu, moe_wd, layer,
                      shared_wg[layer].astype(BF16), shared_wu[layer].astype(BF16), shared_wd[layer].astype(BF16),
                      final_g, layer == depth - 1)

    y_prompt = x[:tp].reshape(n_batch, seq_len, d)
    y_sample = x[tp:].reshape(n_seq, n_new, d)
    return (y_prompt, y_sample, jnp.stack(new_k), jnp.stack(new_v), jnp.stack(new_k_s), jnp.stack(new_v_s),
            jnp.stack(new_s), jnp.stack(new_s_s))
```

```python
import functools

import jax
import jax.numpy as jnp
from jax import lax
from jax.experimental import pallas as pl
from jax.experimental.pallas import tpu as pltpu

F32 = jnp.float32
BF16 = jnp.bfloat16
I32 = jnp.int32
U32 = jnp.uint32

D_MODEL = 1024
PAST_LEN = 8192
SWA_HEAD_DIM = 64
SWA_HEADS = 16
SWA_KV_HEADS = 4
SWA_GROUP = 4
WINDOW = 128
ROPE_THETA = 10000.0
GLA_HEADS = 4
GLA_DK = 128
GLA_DV = 256
GLA_KEY_DIM = 512
GLA_VAL_DIM = 1024
GLA_GATE_RANK = 16
GLA_GATE_NORMALIZER = 16.0
GLA_CHUNK = 64
N_EXPERTS = 64
TOP_K = 8
N_GROUPS = 8
TOPK_GROUPS = 4
EXPERT_FF = 256
ROUTED_SCALE = 2.5
NORM_EPS = 1e-6

TOKEN_TILE = 512
ROUTE_TILE = 256
EXPERT_BLOCK = 256
ADA_TILE = 512
SAMPLE_SEQS = 8
GLA_PROMPT_BATCH = 2
VMEM_LIMIT = 48 * 1024 * 1024

NT_DIMS = (((1,), (1,)), ((), ()))
TN_DIMS = (((0,), (0,)), ((), ()))


def _params(semantics):
    return pltpu.CompilerParams(dimension_semantics=semantics, vmem_limit_bytes=VMEM_LIMIT)


def _silu(x):
    return x * jax.nn.sigmoid(x)


def _norm_mod(x, g, sc, sh):
    ms = jnp.mean(x * x, axis=-1, keepdims=True)
    return (x * lax.rsqrt(ms + NORM_EPS) * g) * (1.0 + sc) + sh


def _pack_bf16_pairs(x):
    half = x.shape[1] // 2
    xb = x.astype(BF16).astype(F32)
    lo = lax.bitcast_convert_type(xb[:, :half], U32) >> 16
    hi = lax.bitcast_convert_type(xb[:, half:], U32) & jnp.uint32(0xFFFF0000)
    return lo | hi


def _unpack_bf16_pairs(u):
    lo = lax.bitcast_convert_type(u << 16, F32)
    hi = lax.bitcast_convert_type(u & jnp.uint32(0xFFFF0000), F32)
    return jnp.concatenate([lo, hi], axis=1).astype(BF16)


def _split3(x):
    x1 = x.astype(BF16)
    r1 = x - x1.astype(F32)
    x2 = r1.astype(BF16)
    x3 = (r1 - x2.astype(F32)).astype(BF16)
    return x1, x2, x3


def _ada_kernel(c_ref, w_ref, b_ref, o_ref):
    s = _silu(c_ref[...]).astype(BF16)
    o_ref[...] = jnp.dot(s, w_ref[...].astype(BF16), preferred_element_type=F32) + b_ref[...]


def ada_modulation(c_all, ada_w, ada_b):
    depth, d, n = ada_w.shape
    rows = c_all.shape[0]
    return pl.pallas_call(
        _ada_kernel,
        out_shape=jax.ShapeDtypeStruct((depth, rows, n), F32),
        grid=(depth, n // ADA_TILE),
        in_specs=[
            pl.BlockSpec((rows, d), lambda l, j: (0, 0)),
            pl.BlockSpec((None, d, ADA_TILE), lambda l, j: (l, 0, j)),
            pl.BlockSpec((None, 1, ADA_TILE), lambda l, j: (l, 0, j)),
        ],
        out_specs=pl.BlockSpec((None, rows, ADA_TILE), lambda l, j: (l, 0, j)),
        compiler_params=_params(("parallel", "parallel")),
        name="ada_modulation",
    )(c_all, ada_w, ada_b.reshape(depth, 1, n))


class _Mod:
    def __init__(self, mod, layer, n_batch, seq_len, tile):
        depth, rows, n = mod.shape
        self.tile = tile
        self.layer = layer
        self.npt = n_batch * seq_len // tile
        self.mod_tok = mod
        self.mod_seq = mod[:, rows - n_batch:].reshape(depth, n_batch, 1, n)
        self.tiles_per_seq = seq_len // tile

    def operands(self, chunk):
        del chunk
        return [self.mod_seq, self.mod_tok]

    def specs(self, chunk):
        l, npt, tps = self.layer, self.npt, self.tiles_per_seq
        n_seq = self.mod_seq.shape[1]
        seq_spec = pl.BlockSpec((None, None, 1, D_MODEL),
                                lambda i, *_: (l, jnp.minimum(i // tps, n_seq - 1), 0, chunk))
        tok_spec = pl.BlockSpec((None, self.tile, D_MODEL),
                                lambda i, *_: (l, jnp.maximum(i - npt, 0), chunk))
        return [seq_spec, tok_spec]


def _pick(is_sample, seq_ref, tok_ref):
    return jnp.where(is_sample, tok_ref[...], seq_ref[...])


def _swa_qkv_kernel(npt, x_ref, g_ref, scs_ref, sct_ref, shs_ref, sht_ref, w_ref, cos_ref, sin_ref,
                    q_ref, k_ref, v_ref, kd_ref, vd_ref):
    is_s = pl.program_id(0) >= npt
    h = _norm_mod(x_ref[...], g_ref[...], _pick(is_s, scs_ref, sct_ref), _pick(is_s, shs_ref, sht_ref))
    qkv = jnp.dot(h.astype(BF16), w_ref[...], preferred_element_type=F32)
    cos = cos_ref[...]
    sin = sin_ref[...]
    lane = lax.broadcasted_iota(I32, cos.shape, 1)
    first_half = (lane % SWA_HEAD_DIM) < (SWA_HEAD_DIM // 2)

    def rope(xc):
        rot = jnp.where(first_half, pltpu.roll(xc, 128 - SWA_HEAD_DIM // 2, 1), pltpu.roll(xc, SWA_HEAD_DIM // 2, 1))
        return xc * cos + rot * sin

    nq = SWA_HEADS * SWA_HEAD_DIM
    nk = SWA_KV_HEADS * SWA_HEAD_DIM
    for c in range(nq // 128):
        q_ref[:, 128 * c:128 * (c + 1)] = (rope(qkv[:, 128 * c:128 * (c + 1)]) * (SWA_HEAD_DIM ** -0.5)).astype(BF16)
    low = lane < SWA_HEAD_DIM

    def spread(chunk):
        rolled = pltpu.roll(chunk, SWA_HEAD_DIM, 1)
        return jnp.where(low, chunk, rolled).astype(BF16), jnp.where(low, rolled, chunk).astype(BF16)

    for c in range(nk // 128):
        kc = rope(qkv[:, nq + 128 * c:nq + 128 * (c + 1)])
        vc = qkv[:, nq + nk + 128 * c:nq + nk + 128 * (c + 1)]
        k_ref[:, 128 * c:128 * (c + 1)] = kc
        v_ref[:, 128 * c:128 * (c + 1)] = vc
        kd_ref[:, 256 * c:256 * c + 128], kd_ref[:, 256 * c + 128:256 * (c + 1)] = spread(kc)
        vd_ref[:, 256 * c:256 * c + 128], vd_ref[:, 256 * c + 128:256 * (c + 1)] = spread(vc)


def swa_qkv(x, g, mod, w_bf, cos_tab, sin_tab):
    t = x.shape[0]
    tm = mod.tile
    nq = SWA_HEADS * SWA_HEAD_DIM
    nk = SWA_KV_HEADS * SWA_HEAD_DIM
    row = lambda i: (i, 0)
    return pl.pallas_call(
        functools.partial(_swa_qkv_kernel, mod.npt),
        out_shape=(jax.ShapeDtypeStruct((t, nq), BF16), jax.ShapeDtypeStruct((t, nk), F32),
                   jax.ShapeDtypeStruct((t, nk), F32), jax.ShapeDtypeStruct((t, 2 * nk), BF16),
                   jax.ShapeDtypeStruct((t, 2 * nk), BF16)),
        grid=(t // tm,),
        in_specs=[pl.BlockSpec((tm, D_MODEL), row), pl.BlockSpec((1, D_MODEL), lambda i: (0, 0))]
        + mod.specs(1) + mod.specs(0)
        + [pl.BlockSpec(w_bf.shape, lambda i: (0, 0)), pl.BlockSpec((tm, 128), row), pl.BlockSpec((tm, 128), row)],
        out_specs=(pl.BlockSpec((tm, nq), row), pl.BlockSpec((tm, nk), row), pl.BlockSpec((tm, nk), row),
                   pl.BlockSpec((tm, 2 * nk), row), pl.BlockSpec((tm, 2 * nk), row)),
        compiler_params=_params(("parallel",)),
        name="swa_qkv",
    )(x, g.reshape(1, -1), *mod.operands(1), *mod.operands(0), w_bf, cos_tab, sin_tab)


def _sink_softmax(s, sink_col):
    m = jnp.maximum(jnp.max(s, axis=-1, keepdims=True), sink_col)
    e = jnp.exp(s - m)
    den = jnp.sum(e, axis=-1, keepdims=True) + jnp.exp(sink_col - m)
    return e * (1.0 / den)


def _swa_prompt_kernel(sink_ref, q_ref, kc_ref, kp_ref, vc_ref, vp_ref, o_ref):
    j = pl.program_id(1)
    blk = q_ref.shape[0]
    qi = lax.broadcasted_iota(I32, (blk, 2 * blk), 0)
    sj = lax.broadcasted_iota(I32, (blk, 2 * blk), 1)
    rel = qi + blk - sj
    mask = (rel >= 0) & (rel <= WINDOW) & ((sj >= blk) | (j > 0))
    low = lax.broadcasted_iota(I32, (2 * blk, 128), 1) < SWA_HEAD_DIM
    zero = jnp.zeros((2 * blk, 128), BF16)
    for g in range(SWA_KV_HEADS):
        cs = slice(128 * g, 128 * (g + 1))
        kcat = jnp.concatenate([kp_ref[:, cs], kc_ref[:, cs]], axis=0)
        vcat = jnp.concatenate([vp_ref[:, cs], vc_ref[:, cs]], axis=0)
        kblk = jnp.concatenate([jnp.where(low, kcat, zero), jnp.where(low, zero, kcat)], axis=0)
        vblk = jnp.concatenate([jnp.where(low, vcat, zero), jnp.where(low, zero, vcat)], axis=0)
        for pair in range(SWA_GROUP // 2):
            c = (SWA_GROUP // 2) * g + pair
            s = lax.dot_general(q_ref[:, 128 * c:128 * (c + 1)], kblk, NT_DIMS, preferred_element_type=F32)
            probs = []
            for hh in range(2):
                sh = jnp.where(mask, s[:, 2 * blk * hh:2 * blk * (hh + 1)], -jnp.inf)
                probs.append(_sink_softmax(sh, sink_ref[2 * c + hh]))
            p = jnp.concatenate(probs, axis=1).astype(BF16)
            o_ref[:, 128 * c:128 * (c + 1)] = jnp.dot(p, vblk, preferred_element_type=F32).astype(BF16)


def swa_prompt_attention(q, k, v, sinks, n_batch, seq_len):
    blk = WINDOW
    nb = seq_len // blk
    nq = q.shape[1]
    nk = k.shape[1]
    cur = lambda b, j: (b * nb + j, 0)
    prev = lambda b, j: (b * nb + jnp.maximum(j - 1, 0), 0)
    return pl.pallas_call(
        _swa_prompt_kernel,
        out_shape=jax.ShapeDtypeStruct((n_batch * seq_len, nq), BF16),
        grid=(n_batch, nb),
        in_specs=[pl.BlockSpec(memory_space=pltpu.SMEM),
                  pl.BlockSpec((blk, nq), cur),
                  pl.BlockSpec((blk, nk), cur), pl.BlockSpec((blk, nk), prev),
                  pl.BlockSpec((blk, nk), cur), pl.BlockSpec((blk, nk), prev)],
        out_specs=pl.BlockSpec((blk, nq), cur),
        compiler_params=_params(("parallel", "parallel")),
        name="swa_prompt_attention",
    )(sinks, q, k, k, v, v)


def _swa_sample_kernel(n_new, sink_ref, q_ref, kn_ref, vn_ref, ck_ref, cv_ref, o_ref, nk_ref, nv_ref):
    n_sb, win, _ = ck_ref.shape
    per_seq = win + n_new
    rows = n_sb * n_new
    cols = n_sb * per_seq
    keys, vals = [], []
    for sb in range(n_sb):
        r0 = sb * n_new
        kc = ck_ref[sb]
        vc = cv_ref[sb]
        kn = kn_ref[r0:r0 + n_new, :]
        vn = vn_ref[r0:r0 + n_new, :]
        nk_ref[sb, 0:win - n_new, :] = kc[n_new:]
        nk_ref[sb, win - n_new:win, :] = kn
        nv_ref[sb, 0:win - n_new, :] = vc[n_new:]
        nv_ref[sb, win - n_new:win, :] = vn
        keys += [kc, kn]
        vals += [vc, vn]
    keys = jnp.concatenate(keys, axis=0)
    vals = jnp.concatenate(vals, axis=0)
    ri = lax.broadcasted_iota(I32, (rows, cols), 0)
    ci = lax.broadcasted_iota(I32, (rows, cols), 1)
    ti = ri % n_new
    si = ci % per_seq
    mask = (ri // n_new == ci // per_seq) & (si >= ti) & (si <= ti + WINDOW)
    low = lax.broadcasted_iota(I32, (cols, 128), 1) < SWA_HEAD_DIM
    zero = jnp.zeros((cols, 128), BF16)

    def block_diag(chunk, first):
        rolled = pltpu.roll(chunk, SWA_HEAD_DIM, 1)
        both = (jnp.where(low, chunk, rolled) if first else jnp.where(low, rolled, chunk)).astype(BF16)
        return jnp.concatenate([jnp.where(low, both, zero), jnp.where(low, zero, both)], axis=0)

    for g in range(SWA_KV_HEADS):
        cs = slice(128 * (g // 2), 128 * (g // 2 + 1))
        kblk = block_diag(keys[:, cs], g % 2 == 0)
        vblk = block_diag(vals[:, cs], g % 2 == 0)
        for pair in range(SWA_GROUP // 2):
            c = (SWA_GROUP // 2) * g + pair
            s = lax.dot_general(q_ref[:, 128 * c:128 * (c + 1)], kblk, NT_DIMS, preferred_element_type=F32)
            probs = []
            for hh in range(2):
                sh = jnp.where(mask, s[:, cols * hh:cols * (hh + 1)], -jnp.inf)
                probs.append(_sink_softmax(sh, sink_ref[2 * c + hh]))
            p = jnp.concatenate(probs, axis=1).astype(BF16)
            o_ref[:, 128 * c:128 * (c + 1)] = jnp.dot(p, vblk, preferred_element_type=F32).astype(BF16)


def swa_sample_attention(q, k, v, cache_k, cache_v, sinks, n_prompt_rows, n_new):
    n_seq, win, nk = cache_k.shape
    sb = SAMPLE_SEQS
    rows = sb * n_new
    nq = q.shape[1]
    base = n_prompt_rows // rows
    tok = lambda i: (base + i, 0)
    seq = lambda i: (i, 0, 0)
    return pl.pallas_call(
        functools.partial(_swa_sample_kernel, n_new),
        out_shape=(jax.ShapeDtypeStruct((n_seq * n_new, nq), BF16),
                   jax.ShapeDtypeStruct(cache_k.shape, F32), jax.ShapeDtypeStruct(cache_v.shape, F32)),
        grid=(n_seq // sb,),
        in_specs=[pl.BlockSpec(memory_space=pltpu.SMEM),
                  pl.BlockSpec((rows, nq), tok), pl.BlockSpec((rows, nk), tok), pl.BlockSpec((rows, nk), tok),
                  pl.BlockSpec((sb, win, nk), seq), pl.BlockSpec((sb, win, nk), seq)],
        out_specs=(pl.BlockSpec((rows, nq), lambda i: (i, 0)), pl.BlockSpec((sb, win, nk), seq),
                   pl.BlockSpec((sb, win, nk), seq)),
        compiler_params=_params(("parallel",)),
        name="swa_sample_attention",
    )(sinks, q, k, v, cache_k, cache_v)


def _gla_proj_kernel(npt, x_ref, g_ref, scs_ref, sct_ref, shs_ref, sht_ref, w_ref, wa2_ref, ba_ref,
                     q_ref, k_ref, v_ref, r_ref, gate_ref):
    is_s = pl.program_id(0) >= npt
    h = _norm_mod(x_ref[...], g_ref[...], _pick(is_s, scs_ref, sct_ref), _pick(is_s, shs_ref, sht_ref))
    proj = jnp.dot(h.astype(BF16), w_ref[...], preferred_element_type=F32)
    kd = GLA_KEY_DIM
    vd = GLA_VAL_DIM
    q_ref[...] = proj[:, :kd] * (GLA_DK ** -0.5)
    k_ref[...] = proj[:, kd:2 * kd]
    v_ref[...] = proj[:, 2 * kd:2 * kd + vd].astype(BF16)
    r_ref[...] = proj[:, 2 * kd + vd:2 * kd + 2 * vd]
    low = proj[:, 2 * kd + 2 * vd:].astype(BF16)
    z = jnp.dot(low, wa2_ref[...], preferred_element_type=F32) + ba_ref[...]
    log_sig = jnp.minimum(z, 0.0) - jnp.log1p(jnp.exp(-jnp.abs(z)))
    gate_ref[...] = log_sig / GLA_GATE_NORMALIZER


def gla_project(x, g, mod, win_ext, wa2_pad, ba):
    t = x.shape[0]
    tm = mod.tile
    kd, vd = GLA_KEY_DIM, GLA_VAL_DIM
    row = lambda i: (i, 0)
    const = lambda i: (0, 0)
    return pl.pallas_call(
        functools.partial(_gla_proj_kernel, mod.npt),
        out_shape=(jax.ShapeDtypeStruct((t, kd), F32), jax.ShapeDtypeStruct((t, kd), F32),
                   jax.ShapeDtypeStruct((t, vd), BF16), jax.ShapeDtypeStruct((t, vd), F32),
                   jax.ShapeDtypeStruct((t, kd), F32)),
        grid=(t // tm,),
        in_specs=[pl.BlockSpec((tm, D_MODEL), row), pl.BlockSpec((1, D_MODEL), const)]
        + mod.specs(1) + mod.specs(0)
        + [pl.BlockSpec(win_ext.shape, const), pl.BlockSpec(wa2_pad.shape, const), pl.BlockSpec((1, kd), const)],
        out_specs=(pl.BlockSpec((tm, kd), row), pl.BlockSpec((tm, kd), row), pl.BlockSpec((tm, vd), row),
                   pl.BlockSpec((tm, vd), row), pl.BlockSpec((tm, kd), row)),
        compiler_params=_params(("parallel",)),
        name="gla_project",
    )(x, g.reshape(1, -1), *mod.operands(1), *mod.operands(0), win_ext, wa2_pad, ba.reshape(1, -1))


def _cumsum_rows(tri, g):
    n = g.shape[1]
    s = jnp.dot(tri, jnp.concatenate(_split3(g), axis=1), preferred_element_type=F32)
    return s[:, :n] + s[:, n:2 * n] + s[:, 2 * n:]


def _diag_attention(q, k, b, n):
    ng = n // 8
    dk = q.shape[1]
    q3 = q.reshape(ng, 8, dk)
    k3 = k.reshape(ng, 8, dk)
    b3 = b.reshape(ng, 8, dk)
    sub = lax.broadcasted_iota(I32, (ng, 8, dk), 1)
    ti = lax.broadcasted_iota(I32, (n, n), 0)
    si = lax.broadcasted_iota(I32, (n, n), 1)
    attn = jnp.zeros((n, n), F32)
    for j in range(8):
        bj = jnp.broadcast_to(b3[:, j:j + 1, :], b3.shape)
        kj = jnp.broadcast_to(k3[:, j:j + 1, :], k3.shape)
        e = jnp.exp(jnp.minimum(b3 - bj, 0.0))
        m = jnp.where(sub >= j, q3 * e * kj, 0.0)
        col = jnp.sum(m, axis=-1, keepdims=True).reshape(n, 1)
        attn = attn + jnp.where(si == (ti // 8) * 8 + j, col, 0.0)
    return attn


def _cross_attention(q, k, b, n):
    ti = lax.broadcasted_iota(I32, (n, n), 0)
    si = lax.broadcasted_iota(I32, (n, n), 1)
    row = lax.broadcasted_iota(I32, b.shape, 0)
    attn = jnp.zeros((n, n), F32)
    m = n // 2
    while m >= 8:
        nblk = n // m
        refq = jnp.concatenate(
            [jnp.broadcast_to(b[i * m - 1:i * m], (m, b.shape[1])) if i % 2 else b[i * m:(i + 1) * m]
             for i in range(nblk)], axis=0)
        refk = jnp.concatenate(
            [b[i * m:(i + 1) * m] if i % 2 else jnp.broadcast_to(b[(i + 1) * m - 1:(i + 1) * m], (m, b.shape[1]))
             for i in range(nblk)], axis=0)
        odd = ((row // m) % 2) == 1
        qt = jnp.where(odd, q * jnp.exp(jnp.minimum(b - refq, 0.0)), 0.0).astype(BF16)
        kt = jnp.where(odd, 0.0, k * jnp.exp(jnp.minimum(refk - b, 0.0))).astype(BF16)
        a = lax.dot_general(qt, kt, NT_DIMS, preferred_element_type=F32)
        keep = (((ti // m) % 2) == 1) & ((si // m) == (ti // m) - 1)
        attn = attn + jnp.where(keep, a, 0.0)
        m //= 2
    return attn


def _gla_epilogue(o, r, ng):
    ms = jnp.mean(o * o, axis=-1, keepdims=True)
    return (o * lax.rsqrt(ms + NORM_EPS) * ng * _silu(r)).astype(BF16)


def _gla_prompt_kernel(*refs):
    nb = GLA_PROMPT_BATCH
    ins, (ng_ref, o_ref, so_ref, st_ref) = refs[:5 * nb], refs[5 * nb:]
    c = pl.program_id(1)
    n = ins[0].shape[0]

    @pl.when(c == 0)
    def _():
        st_ref[...] = jnp.zeros(st_ref.shape, F32)

    ti = lax.broadcasted_iota(I32, (n, n), 0)
    si = lax.broadcasted_iota(I32, (n, n), 1)
    tri = jnp.where(ti >= si, 1.0, 0.0).astype(BF16)
    for i in range(nb):
        q_ref, k_ref, g_ref, v_ref, r_ref = ins[5 * i:5 * (i + 1)]
        b_all = _cumsum_rows(tri, g_ref[...])
        for h in range(GLA_HEADS):
            ks = slice(GLA_DK * h, GLA_DK * (h + 1))
            vs = slice(GLA_DV * h, GLA_DV * (h + 1))
            q = q_ref[:, ks]
            k = k_ref[:, ks]
            v = v_ref[:, vs]
            b = b_all[:, ks]
            s_t = st_ref[i, h]
            o = lax.dot_general((q * jnp.exp(b)).astype(BF16), s_t.astype(BF16), NT_DIMS,
                                preferred_element_type=F32)
            attn = _cross_attention(q, k, b, n) + _diag_attention(q, k, b, n)
            o = o + jnp.dot(attn.astype(BF16), v, preferred_element_type=F32)
            bl = b[n - 1:n, :]
            kd = (k * jnp.exp(bl - b)).astype(BF16)
            s_new = s_t * jnp.exp(bl) + lax.dot_general(v, kd, TN_DIMS, preferred_element_type=F32)
            st_ref[i, h] = s_new
            o_ref[i, :, vs] = _gla_epilogue(o, r_ref[:, vs], ng_ref[...])

            @pl.when(c == pl.num_programs(1) - 1)
            def _():
                so_ref[i, h] = s_new.T


def gla_prompt(q, k, g, v, r, norm_g, n_batch, seq_len):
    n = GLA_CHUNK
    nb = GLA_PROMPT_BATCH
    nc = seq_len // n
    kd, vd = GLA_KEY_DIM, GLA_VAL_DIM
    in_specs, operands = [], []
    for i in range(nb):
        row = lambda b, c, i=i: ((nb * b + i) * nc + c, 0)
        in_specs += [pl.BlockSpec((n, kd), row), pl.BlockSpec((n, kd), row), pl.BlockSpec((n, kd), row),
                     pl.BlockSpec((n, vd), row), pl.BlockSpec((n, vd), row)]
        operands += [q, k, g, v, r]
    o, state = pl.pallas_call(
        _gla_prompt_kernel,
        out_shape=(jax.ShapeDtypeStruct((n_batch, seq_len, vd), BF16),
                   jax.ShapeDtypeStruct((n_batch, GLA_HEADS, GLA_DK, GLA_DV), F32)),
        grid=(n_batch // nb, nc),
        in_specs=in_specs + [pl.BlockSpec((1, GLA_DV), lambda b, c: (0, 0))],
        out_specs=(pl.BlockSpec((nb, n, vd), lambda b, c: (b, c, 0)),
                   pl.BlockSpec((nb, GLA_HEADS, GLA_DK, GLA_DV), lambda b, c: (b, 0, 0, 0))),
        scratch_shapes=[pltpu.VMEM((nb, GLA_HEADS, GLA_DV, GLA_DK), F32)],
        compiler_params=_params(("parallel", "arbitrary")),
        name="gla_prompt",
    )(*operands, norm_g.reshape(1, -1))
    return o.reshape(n_batch * seq_len, vd), state


def _gla_sample_kernel(n_new, q_ref, k_ref, g_ref, v_ref, r_ref, ng_ref, si_ref, o_ref, so_ref):
    n = q_ref.shape[0]
    ti = lax.broadcasted_iota(I32, (n, n), 0)
    si = lax.broadcasted_iota(I32, (n, n), 1)
    tri = jnp.where((ti >= si) & (ti // n_new == si // n_new), 1.0, 0.0).astype(BF16)
    b_all = _cumsum_rows(tri, g_ref[...])
    for h in range(GLA_HEADS):
        ks = slice(GLA_DK * h, GLA_DK * (h + 1))
        vs = slice(GLA_DV * h, GLA_DV * (h + 1))
        q = q_ref[:, ks]
        k = k_ref[:, ks]
        v = v_ref[:, vs]
        b = b_all[:, ks]
        attn = _diag_attention(q, k, b, n)
        o_intra = jnp.dot(attn.astype(BF16), v, preferred_element_type=F32)
        qe = (q * jnp.exp(b)).astype(BF16)
        n_sb = n // n_new
        last = [b[n_new * (sb + 1) - 1:n_new * (sb + 1), :] for sb in range(n_sb)]
        bl_rows = jnp.concatenate([jnp.broadcast_to(bl, (n_new, GLA_DK)) for bl in last], axis=0)
        kd = (k * jnp.exp(bl_rows - b)).astype(BF16)
        seq_of_row = lax.broadcasted_iota(I32, (n, GLA_DV), 0) // n_new
        o = o_intra
        for sb in range(n_sb):
            mine = seq_of_row == sb
            s_t = si_ref[sb, h].T
            o_sb = lax.dot_general(qe, s_t.astype(BF16), NT_DIMS, preferred_element_type=F32)
            o = o + jnp.where(mine, o_sb, 0.0)
            v_sb = jnp.where(mine, v, jnp.zeros_like(v))
            upd = lax.dot_general(v_sb, kd, TN_DIMS, preferred_element_type=F32)
            so_ref[sb, h] = (s_t * jnp.exp(last[sb]) + upd).T
        o_ref[:, vs] = _gla_epilogue(o, r_ref[:, vs], ng_ref[...])


def gla_sample(q, k, g, v, r, norm_g, state, n_prompt_rows, n_new):
    n_seq = state.shape[0]
    sb = SAMPLE_SEQS
    rows = sb * n_new
    kd, vd = GLA_KEY_DIM, GLA_VAL_DIM
    base = n_prompt_rows // rows
    tok = lambda i: (base + i, 0)
    seq = lambda i: (i, 0, 0, 0)
    sblock = (sb, GLA_HEADS, GLA_DK, GLA_DV)
    return pl.pallas_call(
        functools.partial(_gla_sample_kernel, n_new),
        out_shape=(jax.ShapeDtypeStruct((n_seq * n_new, vd), BF16), jax.ShapeDtypeStruct(state.shape, F32)),
        grid=(n_seq // sb,),
        in_specs=[pl.BlockSpec((rows, kd), tok), pl.BlockSpec((rows, kd), tok), pl.BlockSpec((rows, kd), tok),
                  pl.BlockSpec((rows, vd), tok), pl.BlockSpec((rows, vd), tok),
                  pl.BlockSpec((1, GLA_DV), lambda i: (0, 0)),
                  pl.BlockSpec(sblock, seq)],
        out_specs=(pl.BlockSpec((rows, vd), lambda i: (i, 0)), pl.BlockSpec(sblock, seq)),
        compiler_params=_params(("parallel",)),
        name="gla_sample",
    )(q, k, g, v, r, norm_g.reshape(1, -1), state)


def _post_mixer_kernel(npt, ap_ref, as_ref, wo_ref, x_ref, g1s_ref, g1t_ref, gf_ref, scs_ref, sct_ref, shs_ref, sht_ref,
                       rw_ref, rb_ref, tri_ref,
                       x1_ref, h_ref, eidx_ref, w_ref, rank_ref, cnt_ref, carry_ref):
    i = pl.program_id(0)
    is_s = i >= npt

    @pl.when(i == 0)
    def _():
        carry_ref[...] = jnp.zeros(carry_ref.shape, F32)

    a = jnp.where(is_s, as_ref[...], ap_ref[...])
    x1 = x_ref[...] + _pick(is_s, g1s_ref, g1t_ref) * jnp.dot(a, wo_ref[...], preferred_element_type=F32)
    x1_ref[...] = x1
    h = _norm_mod(x1, gf_ref[...], _pick(is_s, scs_ref, sct_ref), _pick(is_s, shs_ref, sht_ref))
    h_ref[...] = _pack_bf16_pairs(h)

    h1, h2, _ = _split3(h)
    r1, r2, _ = _split3(rw_ref[...])
    logits = (lax.dot_general(r1, h1, NT_DIMS, preferred_element_type=F32)
              + lax.dot_general(r1, h2, NT_DIMS, preferred_element_type=F32)
              + lax.dot_general(r2, h1, NT_DIMS, preferred_element_type=F32))
    scores = jax.nn.sigmoid(logits)
    sel = scores + rb_ref[...]
    tm = sel.shape[1]
    gsz = N_EXPERTS // N_GROUPS

    sub = lax.broadcasted_iota(I32, (gsz, tm), 0)
    blocks, gscore = [], []
    for g in range(N_GROUPS):
        blk = sel[gsz * g:gsz * (g + 1)]
        m1 = jnp.max(blk, axis=0, keepdims=True)
        first = jnp.min(jnp.where(blk == m1, sub, gsz), axis=0, keepdims=True)
        m2 = jnp.max(jnp.where(sub == first, -jnp.inf, blk), axis=0, keepdims=True)
        blocks.append(blk)
        gscore.append(m1 + m2)
    masked = []
    for g in range(N_GROUPS):
        beaten = jnp.zeros((1, tm), I32)
        for o in range(N_GROUPS):
            if o == g:
                continue
            wins = (gscore[o] > gscore[g]) | ((gscore[o] == gscore[g]) & (o < g))
            beaten = beaten + wins.astype(I32)
        masked.append(jnp.where(beaten < TOPK_GROUPS, blocks[g], -jnp.inf))
    cur = jnp.concatenate(masked, axis=0)

    eid = lax.broadcasted_iota(I32, (N_EXPERTS, tm), 0)
    picked, weights = [], []
    onehot = jnp.zeros((N_EXPERTS, tm), F32)
    for _ in range(TOP_K):
        m = jnp.max(cur, axis=0, keepdims=True)
        idx = jnp.min(jnp.where(cur == m, eid, N_EXPERTS), axis=0, keepdims=True)
        hit = eid == idx
        picked.append(idx)
        weights.append(jnp.sum(jnp.where(hit, scores, 0.0), axis=0, keepdims=True))
        onehot = jnp.where(hit, 1.0, onehot)
        cur = jnp.where(hit, -jnp.inf, cur)
    wsum = weights[0]
    for wk in weights[1:]:
        wsum = wsum + wk
    scale = ROUTED_SCALE / wsum

    before = jnp.dot(onehot.astype(BF16), tri_ref[...], preferred_element_type=F32) + carry_ref[...]
    carry = carry_ref[...] + jnp.sum(onehot, axis=1, keepdims=True)
    carry_ref[...] = carry
    cnt_ref[...] = jnp.broadcast_to(carry, cnt_ref.shape)
    for kk in range(TOP_K):
        eidx_ref[kk:kk + 1, :] = picked[kk]
        w_ref[kk:kk + 1, :] = weights[kk] * scale
        rank_ref[kk:kk + 1, :] = jnp.sum(jnp.where(eid == picked[kk], before, 0.0), axis=0, keepdims=True).astype(I32)


def post_mixer(a_prompt, a_sample, wo_bf, x, mod, gffn, router_t, router_b, tri):
    t = x.shape[0]
    tm = mod.tile
    npt = mod.npt
    row = lambda i: (i, 0)
    col = lambda i: (0, i)
    const = lambda i: (0, 0)
    return pl.pallas_call(
        functools.partial(_post_mixer_kernel, mod.npt),
        out_shape=(jax.ShapeDtypeStruct((t, D_MODEL), F32), jax.ShapeDtypeStruct((t, D_MODEL // 2), U32),
                   jax.ShapeDtypeStruct((TOP_K, t), I32), jax.ShapeDtypeStruct((TOP_K, t), F32),
                   jax.ShapeDtypeStruct((TOP_K, t), I32), jax.ShapeDtypeStruct((N_EXPERTS, 128), F32)),
        grid=(t // tm,),
        in_specs=[pl.BlockSpec((tm, D_MODEL), lambda i: (jnp.minimum(i, npt - 1), 0)),
                  pl.BlockSpec((tm, D_MODEL), lambda i: (jnp.maximum(i - npt, 0), 0)),
                  pl.BlockSpec((D_MODEL, D_MODEL), const), pl.BlockSpec((tm, D_MODEL), row)]
        + mod.specs(2) + [pl.BlockSpec((1, D_MODEL), const)] + mod.specs(4) + mod.specs(3)
        + [pl.BlockSpec((N_EXPERTS, D_MODEL), const), pl.BlockSpec((N_EXPERTS, 1), const),
           pl.BlockSpec((tm, tm), const)],
        out_specs=(pl.BlockSpec((tm, D_MODEL), row), pl.BlockSpec((tm, D_MODEL // 2), row),
                   pl.BlockSpec((TOP_K, tm), col), pl.BlockSpec((TOP_K, tm), col), pl.BlockSpec((TOP_K, tm), col),
                   pl.BlockSpec((N_EXPERTS, 128), const)),
        scratch_shapes=[pltpu.VMEM((N_EXPERTS, 1), F32)],
        compiler_params=_params(("arbitrary",)),
        name="post_mixer",
    )(a_prompt, a_sample, wo_bf, x, *mod.operands(2), gffn.reshape(1, -1), *mod.operands(4), *mod.operands(3),
      router_t, router_b.reshape(-1, 1), tri)


def _row_copy(src, src_row, dst, dst_row, sem):
    return pltpu.make_async_copy(src.at[pl.ds(src_row, 1)], dst.at[pl.ds(dst_row, 1)], sem)


def _by_parity(i, fn):
    @pl.when(i % 2 == 0)
    def _():
        fn(0)

    @pl.when(i % 2 == 1)
    def _():
        fn(1)


def _dispatch_kernel(zb_ref, h_ref, dest_hbm, xs_hbm, idx_a, idx_b, zero_ref, sem_idx, sem_zero, sem_rows):
    i = pl.program_id(0)
    te = h_ref.shape[0]
    idx_bufs = (idx_a, idx_b)

    def idx_copy(tile, p):
        return pltpu.make_async_copy(dest_hbm.at[tile], idx_bufs[p], sem_idx.at[p])

    def zero_copy(e):
        return pltpu.make_async_copy(zero_ref, xs_hbm.at[pl.ds(zb_ref[e] * EXPERT_BLOCK, EXPERT_BLOCK)], sem_zero)

    @pl.when(i == 0)
    def _():
        idx_copy(0, 0).start()
        zero_ref[...] = jnp.zeros(zero_ref.shape, U32)

        def start(e, carry):
            @pl.when(zb_ref[e] >= 0)
            def _():
                zero_copy(e).start()
            return carry

        def wait(e, carry):
            @pl.when(zb_ref[e] >= 0)
            def _():
                zero_copy(e).wait()
            return carry

        lax.fori_loop(0, zb_ref.shape[0], start, 0)
        lax.fori_loop(0, zb_ref.shape[0], wait, 0)

    def step(p):
        @pl.when(i + 1 < pl.num_programs(0))
        def _():
            idx_copy(i + 1, 1 - p).start()

        idx_copy(i, p).wait()
        idx = idx_bufs[p]

        def issue(t, carry):
            for kk in range(TOP_K):
                _row_copy(h_ref, t, xs_hbm, idx[kk * te + t], sem_rows).start()
            return carry

        def drain(t, carry):
            for kk in range(TOP_K):
                _row_copy(h_ref, t, xs_hbm, idx[kk * te + t], sem_rows).wait()
            return carry

        lax.fori_loop(0, te, issue, 0)
        lax.fori_loop(0, te, drain, 0)

    _by_parity(i, step)


def moe_dispatch(h, dest_tiles, zero_blocks, n_slots):
    t = h.shape[0]
    te = ROUTE_TILE
    return pl.pallas_call(
        _dispatch_kernel,
        out_shape=jax.ShapeDtypeStruct((n_slots, h.shape[1]), U32),
        grid_spec=pltpu.PrefetchScalarGridSpec(
            num_scalar_prefetch=1,
            grid=(t // te,),
            in_specs=[pl.BlockSpec((te, h.shape[1]), lambda i, zb: (i, 0)), pl.BlockSpec(memory_space=pl.ANY)],
            out_specs=pl.BlockSpec(memory_space=pl.ANY),
            scratch_shapes=[pltpu.SMEM((te * TOP_K,), I32), pltpu.SMEM((te * TOP_K,), I32),
                            pltpu.VMEM((EXPERT_BLOCK, h.shape[1]), U32),
                            pltpu.SemaphoreType.DMA((2,)), pltpu.SemaphoreType.DMA, pltpu.SemaphoreType.DMA],
        ),
        compiler_params=_params(("arbitrary",)),
        name="moe_dispatch",
    )(zero_blocks, h, dest_tiles)


def _expert_kernel(be_ref, nu_ref, xs_ref, wg0_ref, wu0_ref, wd0_ref, wg1_ref, wu1_ref, wd1_ref, ys_ref,
                   wg_bf, wu_bf, wd_bf):
    b = pl.program_id(0)
    rb = EXPERT_BLOCK
    used = 2 * b < nu_ref[0]
    for s, (wg, wu, wd) in enumerate(((wg0_ref, wu0_ref, wd0_ref), (wg1_ref, wu1_ref, wd1_ref))):
        j = 2 * b + s
        fresh = (b == 0) | (be_ref[j] != be_ref[jnp.maximum(j - 2, 0)])

        @pl.when(used & fresh)
        def _():
            wg_bf[s] = wg[...].astype(BF16)
            wu_bf[s] = wu[...].astype(BF16)
            wd_bf[s] = wd[...].astype(BF16)

    @pl.when(used)
    def _():
        for s in range(2):
            x = _unpack_bf16_pairs(xs_ref[rb * s:rb * (s + 1), :])
            hg = jnp.dot(x, wg_bf[s], preferred_element_type=F32)
            hu = jnp.dot(x, wu_bf[s], preferred_element_type=F32)
            ys_ref[rb * s:rb * (s + 1), :] = jnp.dot((_silu(hg) * hu).astype(BF16), wd_bf[s],
                                                     preferred_element_type=F32)


def moe_experts(xs, block_expert, n_used, wg, wu, wd, layer):
    n_slots = xs.shape[0]
    rb = EXPERT_BLOCK
    ff = wg.shape[3]
    rows = lambda b, be, nu: (jnp.minimum(b, (nu[0] - 1) // 2), 0)
    w_in = lambda s: (lambda b, be, nu: (layer, be[2 * b + s], 0, 0))
    return pl.pallas_call(
        _expert_kernel,
        out_shape=jax.ShapeDtypeStruct((n_slots, D_MODEL), F32),
        grid_spec=pltpu.PrefetchScalarGridSpec(
            num_scalar_prefetch=2,
            grid=(n_slots // (2 * rb),),
            in_specs=[pl.BlockSpec((2 * rb, xs.shape[1]), rows)]
            + [pl.BlockSpec((None, None, D_MODEL, ff), w_in(0)), pl.BlockSpec((None, None, D_MODEL, ff), w_in(0)),
               pl.BlockSpec((None, None, ff, D_MODEL), w_in(0)),
               pl.BlockSpec((None, None, D_MODEL, ff), w_in(1)), pl.BlockSpec((None, None, D_MODEL, ff), w_in(1)),
               pl.BlockSpec((None, None, ff, D_MODEL), w_in(1))],
            out_specs=pl.BlockSpec((2 * rb, D_MODEL), rows),
            scratch_shapes=[pltpu.VMEM((2, D_MODEL, ff), BF16), pltpu.VMEM((2, D_MODEL, ff), BF16),
                            pltpu.VMEM((2, ff, D_MODEL), BF16)],
        ),
        compiler_params=_params(("arbitrary",)),
        name="moe_experts",
    )(block_expert, n_used, xs, wg, wu, wd, wg, wu, wd)


def _combine_kernel(npt, final, h_ref, swg_ref, swu_ref, swd_ref, x1_ref, g2s_ref, g2t_ref, w_ref, fg_ref,
                    dest_hbm, ys_hbm, o_ref, idx_a, idx_b, ybuf_ref, sem_idx, sem_rows):
    i = pl.program_id(0)
    n = pl.num_programs(0)
    tg = h_ref.shape[0]
    idx_bufs = (idx_a, idx_b)

    def idx_copy(tile, p):
        return pltpu.make_async_copy(dest_hbm.at[tile], idx_bufs[p], sem_idx.at[p])

    def gather_rows(p, wait):
        idx = idx_bufs[p]

        def body(t, carry):
            for kk in range(TOP_K):
                src = 0 if wait else idx[kk * tg + t]
                cp = _row_copy(ys_hbm, src, ybuf_ref.at[p, kk], t, sem_rows.at[p])
                cp.wait() if wait else cp.start()
            return carry

        lax.fori_loop(0, tg, body, 0)

    @pl.when(i == 0)
    def _():
        idx_copy(0, 0).start()
        idx_copy(0, 0).wait()
        gather_rows(0, wait=False)

        @pl.when(n > 1)
        def _():
            idx_copy(1, 1).start()

    def prefetch(p):
        @pl.when(i + 1 < n)
        def _():
            idx_copy(i + 1, 1 - p).wait()
            gather_rows(1 - p, wait=False)

        @pl.when(i + 2 < n)
        def _():
            idx_copy(i + 2, p).start()

    _by_parity(i, prefetch)

    hb = _unpack_bf16_pairs(h_ref[...])
    hid = _silu(jnp.dot(hb, swg_ref[...], preferred_element_type=F32)) * jnp.dot(hb, swu_ref[...],
                                                                                 preferred_element_type=F32)
    acc = jnp.dot(hid.astype(BF16), swd_ref[...], preferred_element_type=F32)
    w = w_ref[...]
    gate = _pick(i >= npt, g2s_ref, g2t_ref)

    def finish(p):
        gather_rows(p, wait=True)
        routed = ybuf_ref[p, 0] * w[:, 0:1]
        for kk in range(1, TOP_K):
            routed = routed + ybuf_ref[p, kk] * w[:, kk:kk + 1]
        x2 = x1_ref[...] + gate * (routed + acc)
        if final:
            ms = jnp.mean(x2 * x2, axis=-1, keepdims=True)
            x2 = x2 * lax.rsqrt(ms + NORM_EPS) * fg_ref[...]
        o_ref[...] = x2

    _by_parity(i, finish)


def moe_combine(h, swg_bf, swu_bf, swd_bf, x1, mod, w_tok, final_g, dest_tiles, ys, final):
    t = h.shape[0]
    tg = mod.tile
    row = lambda i: (i, 0)
    const = lambda i: (0, 0)
    return pl.pallas_call(
        functools.partial(_combine_kernel, mod.npt, final),
        out_shape=jax.ShapeDtypeStruct((t, D_MODEL), F32),
        grid=(t // tg,),
        in_specs=[pl.BlockSpec((tg, h.shape[1]), row), pl.BlockSpec(swg_bf.shape, const), pl.BlockSpec(swu_bf.shape, const),
                  pl.BlockSpec(swd_bf.shape, const), pl.BlockSpec((tg, D_MODEL), row)]
        + mod.specs(5)
        + [pl.BlockSpec((tg, TOP_K), row), pl.BlockSpec((1, D_MODEL), const),
           pl.BlockSpec(memory_space=pl.ANY), pl.BlockSpec(memory_space=pl.ANY)],
        out_specs=pl.BlockSpec((tg, D_MODEL), row),
        scratch_shapes=[pltpu.SMEM((tg * TOP_K,), I32), pltpu.SMEM((tg * TOP_K,), I32),
                        pltpu.VMEM((2, TOP_K, tg, D_MODEL), F32),
                        pltpu.SemaphoreType.DMA((2,)), pltpu.SemaphoreType.DMA((2,))],
        compiler_params=_params(("arbitrary",)),
        name="moe_combine",
    )(h, swg_bf, swu_bf, swd_bf, x1, *mod.operands(5), w_tok, final_g.reshape(1, -1), dest_tiles, ys)


def _slot_kernel(eidx_ref, rank_ref, start_ref, o_ref):
    tm = eidx_ref.shape[1]
    tr = o_ref.shape[2]
    eid = lax.broadcasted_iota(I32, (N_EXPERTS, tm), 0)
    start = start_ref[...]
    for kk in range(TOP_K):
        base = jnp.sum(jnp.where(eid == eidx_ref[kk:kk + 1, :], start, 0.0), axis=0, keepdims=True)
        slot = base.astype(I32) + rank_ref[kk:kk + 1, :]
        for j in range(tm // tr):
            o_ref[j, kk:kk + 1, :] = slot[:, tr * j:tr * (j + 1)]


def assignment_slots(eidx_t, rank_t, pad_start):
    t = eidx_t.shape[1]
    tm = TOKEN_TILE
    tr = ROUTE_TILE
    col = lambda i: (0, i)
    out = pl.pallas_call(
        _slot_kernel,
        out_shape=jax.ShapeDtypeStruct((t // tr, TOP_K, tr), I32),
        grid=(t // tm,),
        in_specs=[pl.BlockSpec((TOP_K, tm), col), pl.BlockSpec((TOP_K, tm), col),
                  pl.BlockSpec((N_EXPERTS, 1), lambda i: (0, 0))],
        out_specs=pl.BlockSpec((tm // tr, TOP_K, tr), lambda i: (i, 0, 0)),
        compiler_params=_params(("parallel",)),
        name="assignment_slots",
    )(eidx_t, rank_t, pad_start.astype(F32).reshape(-1, 1))
    return out.reshape(t // tr, TOP_K * tr)


def _routing_tables(counts, n_blocks):
    rb = EXPERT_BLOCK
    counts = counts.astype(I32)
    padded = (counts + rb - 1) // rb * rb
    pad_end = jnp.cumsum(padded)
    pad_start = pad_end - padded
    n_used = pad_end[-1] // rb
    blocks = jnp.arange(n_blocks, dtype=I32)
    block_expert = jnp.sum((pad_end[None, :] <= (blocks * rb)[:, None]).astype(I32), axis=1)
    last_used = jnp.sum((pad_end <= (n_used - 1) * rb).astype(I32))
    block_expert = jnp.minimum(jnp.where(blocks < n_used, block_expert, last_used), N_EXPERTS - 1)
    zero_blocks = jnp.where(counts % rb != 0, pad_end // rb - 1, -1)
    zero_blocks = jnp.concatenate([zero_blocks, jnp.where(n_used % 2 == 1, n_used, -1).reshape(1)]).astype(I32)
    return pad_start, block_expert, n_used.reshape(1).astype(I32), zero_blocks


def moe_layer(h, x1, eidx_t, w_t, rank_t, counts, mod_route, wg, wu, wd, layer, swg_bf, swu_bf, swd_bf,
              final_g, final):
    t = h.shape[0]
    rb = EXPERT_BLOCK
    tr = ROUTE_TILE
    n_blocks = -(-(t * TOP_K) // rb) + N_EXPERTS
    n_blocks += n_blocks % 2
    pad_start, block_expert, n_used, zero_blocks = _routing_tables(counts, n_blocks)
    dest_tiles = assignment_slots(eidx_t, rank_t, pad_start)
    xs = moe_dispatch(h, dest_tiles, zero_blocks, n_blocks * rb)
    ys = moe_experts(xs, block_expert, n_used, wg, wu, wd, layer)
    return moe_combine(h, swg_bf, swu_bf, swd_bf, x1, mod_route, w_t.T, final_g, dest_tiles, ys, final)


def _rope_tables(n_batch, seq_len, n_seq, n_new):
    half = SWA_HEAD_DIM // 2
    inv = ROPE_THETA ** (-jnp.arange(half, dtype=F32) / half)
    pos = jnp.concatenate([jnp.tile(jnp.arange(seq_len, dtype=F32), n_batch),
                           jnp.tile(PAST_LEN + jnp.arange(n_new, dtype=F32), n_seq)])
    ang = pos[:, None] * inv[None, :]
    cos = jnp.tile(jnp.cos(ang), (1, 128 // half))
    sin = jnp.sin(ang)
    sin = jnp.tile(jnp.concatenate([-sin, sin], axis=1), (1, 128 // SWA_HEAD_DIM))
    return cos, sin


def kernel(x_prompt, x_sample, c_prompt, c_sample, cache_swa_k, cache_swa_v, state_gla, norm_mix_g, norm_ffn_g,
           final_g, ada_w, ada_b, swa_wqkv, swa_sinks, swa_wo, gla_win, gla_wa1, gla_wa2, gla_ba, gla_norm_g,
           gla_wo, moe_router, moe_bias, moe_wg, moe_wu, moe_wd, shared_wg, shared_wu, shared_wd):
    n_batch, seq_len, d = x_prompt.shape
    n_seq, n_new, _ = x_sample.shape
    depth = ada_w.shape[0]
    tp = n_batch * seq_len
    ts = n_seq * n_new
    t = tp + ts
    tm = TOKEN_TILE
    tr = ROUTE_TILE

    x = jnp.concatenate([x_prompt.reshape(tp, d), x_sample.reshape(ts, d)], axis=0)
    c_all = jnp.concatenate([jnp.repeat(c_sample, n_new, axis=0), c_prompt], axis=0)
    mod = ada_modulation(c_all, ada_w, ada_b)
    cos_tab, sin_tab = _rope_tables(n_batch, seq_len, n_seq, n_new)
    tri = jnp.triu(jnp.ones((tm, tm), BF16), k=1)

    new_k, new_v, new_s = [], [], []
    new_k_s, new_v_s, new_s_s = [], [], []
    for layer in range(depth):
        mod_tok = _Mod(mod, layer, n_batch, seq_len, tm)
        mod_route = _Mod(mod, layer, n_batch, seq_len, tr)
        m = layer // 2
        if layer % 2 == 0:
            q, k, v, k_dup, v_dup = swa_qkv(x, norm_mix_g[layer], mod_tok, swa_wqkv[m].astype(BF16), cos_tab, sin_tab)
            a_p = swa_prompt_attention(q, k_dup, v_dup, swa_sinks[m], n_batch, seq_len)
            nk = SWA_KV_HEADS * SWA_HEAD_DIM
            a_s, ck, cv = swa_sample_attention(q, k, v, cache_swa_k[m].reshape(n_seq, WINDOW, nk),
                                             cache_swa_v[m].reshape(n_seq, WINDOW, nk), swa_sinks[m], tp, n_new)
            kv_shape = (n_batch, WINDOW, SWA_KV_HEADS, SWA_HEAD_DIM)
            tails = [slice((b + 1) * seq_len - WINDOW, (b + 1) * seq_len) for b in range(n_batch)]
            new_k.append(jnp.stack([k[rows] for rows in tails]).reshape(kv_shape))
            new_v.append(jnp.stack([v[rows] for rows in tails]).reshape(kv_shape))
            new_k_s.append(ck.reshape(n_seq, WINDOW, SWA_KV_HEADS, SWA_HEAD_DIM))
            new_v_s.append(cv.reshape(n_seq, WINDOW, SWA_KV_HEADS, SWA_HEAD_DIM))
            wo = swa_wo[m]
        else:
            pad = jnp.zeros((d, 128 - GLA_GATE_RANK), F32)
            win_ext = jnp.concatenate([gla_win[m], gla_wa1[m], pad], axis=1).astype(BF16)
            wa2_pad = jnp.concatenate([gla_wa2[m], jnp.zeros((128 - GLA_GATE_RANK, GLA_KEY_DIM), F32)],
                                      axis=0).astype(BF16)
            q, k, v, r, gate = gla_project(x, norm_mix_g[layer], mod_tok, win_ext, wa2_pad, gla_ba[m])
            a_p, s_prompt = gla_prompt(q, k, gate, v, r, gla_norm_g[m], n_batch, seq_len)
            a_s, s_sample = gla_sample(q, k, gate, v, r, gla_norm_g[m], state_gla[m], tp, n_new)
            new_s.append(s_prompt)
            new_s_s.append(s_sample)
            wo = gla_wo[m]
        x1, h, eidx_t, w_t, rank_t, cnt = post_mixer(a_p, a_s, wo.astype(BF16), x, mod_tok, norm_ffn_g[layer],
                                                     moe_router[layer].T, moe_bias[layer], tri)
        x = moe_layer(h, x1, eidx_t, w_t, rank_t, cnt[:, 0], mod_route,
                      moe_wg, moe_wu, moe_wd, layer,
                      shared_wg[layer].astype(BF16), shared_wu[layer].astype(BF16), shared_wd[layer].astype(BF16),
                      final_g, layer == depth - 1)

    y_prompt = x[:tp].reshape(n_batch, seq_len, d)
    y_sample = x[tp:].reshape(n_seq, n_new, d)
    return (y_prompt, y_sample, jnp.stack(new_k), jnp.stack(new_v), jnp.stack(new_k_s), jnp.stack(new_v_s),
            jnp.stack(new_s), jnp.stack(new_s_s))
```

```python
import functools

import jax
import jax.numpy as jnp
from jax import lax
from jax.experimental import pallas as pl
from jax.experimental.pallas import tpu as pltpu

F32 = jnp.float32
BF16 = jnp.bfloat16
I32 = jnp.int32
U32 = jnp.uint32

D_MODEL = 1024
PAST_LEN = 8192
SWA_HEAD_DIM = 64
SWA_HEADS = 16
SWA_KV_HEADS = 4
SWA_GROUP = 4
WINDOW = 128
ROPE_THETA = 10000.0
GLA_HEADS = 4
GLA_DK = 128
GLA_DV = 256
GLA_KEY_DIM = 512
GLA_VAL_DIM = 1024
GLA_GATE_RANK = 16
GLA_GATE_NORMALIZER = 16.0
GLA_CHUNK = 64
N_EXPERTS = 64
TOP_K = 8
N_GROUPS = 8
TOPK_GROUPS = 4
EXPERT_FF = 256
ROUTED_SCALE = 2.5
NORM_EPS = 1e-6

TOKEN_TILE = 512
ROUTE_TILE = 256
EXPERT_BLOCK = 256
ADA_TILE = 512
SAMPLE_SEQS = 8
GLA_PROMPT_BATCH = 2
VMEM_LIMIT = 48 * 1024 * 1024

NT_DIMS = (((1,), (1,)), ((), ()))
TN_DIMS = (((0,), (0,)), ((), ()))


def _params(semantics):
    return pltpu.CompilerParams(dimension_semantics=semantics, vmem_limit_bytes=VMEM_LIMIT)


def _silu(x):
    return x * jax.nn.sigmoid(x)


def _norm_mod(x, g, sc, sh):
    ms = jnp.mean(x * x, axis=-1, keepdims=True)
    return (x * lax.rsqrt(ms + NORM_EPS) * g) * (1.0 + sc) + sh


def _pack_bf16_pairs(x):
    half = x.shape[1] // 2
    xb = x.astype(BF16).astype(F32)
    lo = lax.bitcast_convert_type(xb[:, :half], U32) >> 16
    hi = lax.bitcast_convert_type(xb[:, half:], U32) & jnp.uint32(0xFFFF0000)
    return lo | hi


def _unpack_bf16_pairs(u):
    lo = lax.bitcast_convert_type(u << 16, F32)
    hi = lax.bitcast_convert_type(u & jnp.uint32(0xFFFF0000), F32)
    return jnp.concatenate([lo, hi], axis=1).astype(BF16)


def _split3(x):
    x1 = x.astype(BF16)
    r1 = x - x1.astype(F32)
    x2 = r1.astype(BF16)
    x3 = (r1 - x2.astype(F32)).astype(BF16)
    return x1, x2, x3


def _ada_kernel(c_ref, w_ref, b_ref, o_ref):
    s = _silu(c_ref[...]).astype(BF16)
    o_ref[...] = jnp.dot(s, w_ref[...].astype(BF16), preferred_element_type=F32) + b_ref[...]


def ada_modulation(c_all, ada_w, ada_b):
    depth, d, n = ada_w.shape
    rows = c_all.shape[0]
    return pl.pallas_call(
        _ada_kernel,
        out_shape=jax.ShapeDtypeStruct((depth, rows, n), F32),
        grid=(depth, n // ADA_TILE),
        in_specs=[
            pl.BlockSpec((rows, d), lambda l, j: (0, 0)),
            pl.BlockSpec((None, d, ADA_TILE), lambda l, j: (l, 0, j)),
            pl.BlockSpec((None, 1, ADA_TILE), lambda l, j: (l, 0, j)),
        ],
        out_specs=pl.BlockSpec((None, rows, ADA_TILE), lambda l, j: (l, 0, j)),
        compiler_params=_params(("parallel", "parallel")),
        name="ada_modulation",
    )(c_all, ada_w, ada_b.reshape(depth, 1, n))


class _Mod:
    def __init__(self, mod, layer, n_batch, seq_len, tile):
        depth, rows, n = mod.shape
        self.tile = tile
        self.layer = layer
        self.npt = n_batch * seq_len // tile
        self.mod_tok = mod
        self.mod_seq = mod[:, rows - n_batch:].reshape(depth, n_batch, 1, n)
        self.tiles_per_seq = seq_len // tile

    def operands(self, chunk):
        del chunk
        return [self.mod_seq, self.mod_tok]

    def specs(self, chunk):
        l, npt, tps = self.layer, self.npt, self.tiles_per_seq
        n_seq = self.mod_seq.shape[1]
        seq_spec = pl.BlockSpec((None, None, 1, D_MODEL),
                                lambda i, *_: (l, jnp.minimum(i // tps, n_seq - 1), 0, chunk))
        tok_spec = pl.BlockSpec((None, self.tile, D_MODEL),
                                lambda i, *_: (l, jnp.maximum(i - npt, 0), chunk))
        return [seq_spec, tok_spec]


def _pick(is_sample, seq_ref, tok_ref):
    return jnp.where(is_sample, tok_ref[...], seq_ref[...])


def _swa_qkv_kernel(npt, x_ref, g_ref, scs_ref, sct_ref, shs_ref, sht_ref, w_ref, cos_ref, sin_ref,
                    q_ref, k_ref, v_ref, kd_ref, vd_ref):
    is_s = pl.program_id(0) >= npt
    h = _norm_mod(x_ref[...], g_ref[...], _pick(is_s, scs_ref, sct_ref), _pick(is_s, shs_ref, sht_ref))
    qkv = jnp.dot(h.astype(BF16), w_ref[...], preferred_element_type=F32)
    cos = cos_ref[...]
    sin = sin_ref[...]
    lane = lax.broadcasted_iota(I32, cos.shape, 1)
    first_half = (lane % SWA_HEAD_DIM) < (SWA_HEAD_DIM // 2)

    def rope(xc):
        rot = jnp.where(first_half, pltpu.roll(xc, 128 - SWA_HEAD_DIM // 2, 1), pltpu.roll(xc, SWA_HEAD_DIM // 2, 1))
        return xc * cos + rot * sin

    nq = SWA_HEADS * SWA_HEAD_DIM
    nk = SWA_KV_HEADS * SWA_HEAD_DIM
    for c in range(nq // 128):
        q_ref[:, 128 * c:128 * (c + 1)] = (rope(qkv[:, 128 * c:128 * (c + 1)]) * (SWA_HEAD_DIM ** -0.5)).astype(BF16)
    low = lane < SWA_HEAD_DIM

    def spread(chunk):
        rolled = pltpu.roll(chunk, SWA_HEAD_DIM, 1)
        return jnp.where(low, chunk, rolled).astype(BF16), jnp.where(low, rolled, chunk).astype(BF16)

    for c in range(nk // 128):
        kc = rope(qkv[:, nq + 128 * c:nq + 128 * (c + 1)])
        vc = qkv[:, nq + nk + 128 * c:nq + nk + 128 * (c + 1)]
        k_ref[:, 128 * c:128 * (c + 1)] = kc
        v_ref[:, 128 * c:128 * (c + 1)] = vc
        kd_ref[:, 256 * c:256 * c + 128], kd_ref[:, 256 * c + 128:256 * (c + 1)] = spread(kc)
        vd_ref[:, 256 * c:256 * c + 128], vd_ref[:, 256 * c + 128:256 * (c + 1)] = spread(vc)


def swa_qkv(x, g, mod, w_bf, cos_tab, sin_tab):
    t = x.shape[0]
    tm = mod.tile
    nq = SWA_HEADS * SWA_HEAD_DIM
    nk = SWA_KV_HEADS * SWA_HEAD_DIM
    row = lambda i: (i, 0)
    return pl.pallas_call(
        functools.partial(_swa_qkv_kernel, mod.npt),
        out_shape=(jax.ShapeDtypeStruct((t, nq), BF16), jax.ShapeDtypeStruct((t, nk), F32),
                   jax.ShapeDtypeStruct((t, nk), F32), jax.ShapeDtypeStruct((t, 2 * nk), BF16),
                   jax.ShapeDtypeStruct((t, 2 * nk), BF16)),
        grid=(t // tm,),
        in_specs=[pl.BlockSpec((tm, D_MODEL), row), pl.BlockSpec((1, D_MODEL), lambda i: (0, 0))]
        + mod.specs(1) + mod.specs(0)
        + [pl.BlockSpec(w_bf.shape, lambda i: (0, 0)), pl.BlockSpec((tm, 128), row), pl.BlockSpec((tm, 128), row)],
        out_specs=(pl.BlockSpec((tm, nq), row), pl.BlockSpec((tm, nk), row), pl.BlockSpec((tm, nk), row),
                   pl.BlockSpec((tm, 2 * nk), row), pl.BlockSpec((tm, 2 * nk), row)),
        compiler_params=_params(("parallel",)),
        name="swa_qkv",
    )(x, g.reshape(1, -1), *mod.operands(1), *mod.operands(0), w_bf, cos_tab, sin_tab)


def _sink_softmax(s, sink_col):
    m = jnp.maximum(jnp.max(s, axis=-1, keepdims=True), sink_col)
    e = jnp.exp(s - m)
    den = jnp.sum(e, axis=-1, keepdims=True) + jnp.exp(sink_col - m)
    return e * (1.0 / den)


def _pair_attention(sink_ref, q_ref, kblks, vblks, mask, o_ref):
    half = mask.shape[1]
    scores = []
    for c in range(SWA_HEADS // 2):
        s = lax.dot_general(q_ref[:, 128 * c:128 * (c + 1)], kblks[c // (SWA_GROUP // 2)], NT_DIMS,
                            preferred_element_type=F32)
        scores += [s[:, :half], s[:, half:]]
    s_all = jnp.where(mask, jnp.stack(scores), -jnp.inf)
    sinks = jnp.stack([jnp.full((1, 1), sink_ref[h], F32) for h in range(SWA_HEADS)])
    p_all = _sink_softmax(s_all, sinks).astype(BF16)
    for c in range(SWA_HEADS // 2):
        p = jnp.concatenate([p_all[2 * c], p_all[2 * c + 1]], axis=1)
        o_ref[:, 128 * c:128 * (c + 1)] = jnp.dot(p, vblks[c // (SWA_GROUP // 2)],
                                                  preferred_element_type=F32).astype(BF16)


def _swa_prompt_kernel(sink_ref, q_ref, kc_ref, kp_ref, vc_ref, vp_ref, o_ref):
    j = pl.program_id(1)
    blk = q_ref.shape[0]
    qi = lax.broadcasted_iota(I32, (blk, 2 * blk), 0)
    sj = lax.broadcasted_iota(I32, (blk, 2 * blk), 1)
    rel = qi + blk - sj
    mask = (rel >= 0) & (rel <= WINDOW) & ((sj >= blk) | (j > 0))
    low = lax.broadcasted_iota(I32, (2 * blk, 128), 1) < SWA_HEAD_DIM
    zero = jnp.zeros((2 * blk, 128), BF16)
    kblks, vblks = [], []
    for g in range(SWA_KV_HEADS):
        cs = slice(128 * g, 128 * (g + 1))
        kcat = jnp.concatenate([kp_ref[:, cs], kc_ref[:, cs]], axis=0)
        vcat = jnp.concatenate([vp_ref[:, cs], vc_ref[:, cs]], axis=0)
        kblks.append(jnp.concatenate([jnp.where(low, kcat, zero), jnp.where(low, zero, kcat)], axis=0))
        vblks.append(jnp.concatenate([jnp.where(low, vcat, zero), jnp.where(low, zero, vcat)], axis=0))
    _pair_attention(sink_ref, q_ref, kblks, vblks, mask, o_ref)


def swa_prompt_attention(q, k, v, sinks, n_batch, seq_len):
    blk = WINDOW
    nb = seq_len // blk
    nq = q.shape[1]
    nk = k.shape[1]
    cur = lambda b, j: (b * nb + j, 0)
    prev = lambda b, j: (b * nb + jnp.maximum(j - 1, 0), 0)
    return pl.pallas_call(
        _swa_prompt_kernel,
        out_shape=jax.ShapeDtypeStruct((n_batch * seq_len, nq), BF16),
        grid=(n_batch, nb),
        in_specs=[pl.BlockSpec(memory_space=pltpu.SMEM),
                  pl.BlockSpec((blk, nq), cur),
                  pl.BlockSpec((blk, nk), cur), pl.BlockSpec((blk, nk), prev),
                  pl.BlockSpec((blk, nk), cur), pl.BlockSpec((blk, nk), prev)],
        out_specs=pl.BlockSpec((blk, nq), cur),
        compiler_params=_params(("parallel", "parallel")),
        name="swa_prompt_attention",
    )(sinks, q, k, k, v, v)


def _swa_sample_kernel(n_new, sink_ref, q_ref, kn_ref, vn_ref, ck_ref, cv_ref, o_ref, nk_ref, nv_ref):
    n_sb, win, _ = ck_ref.shape
    per_seq = win + n_new
    rows = n_sb * n_new
    cols = n_sb * per_seq
    keys, vals = [], []
    for sb in range(n_sb):
        r0 = sb * n_new
        kc = ck_ref[sb]
        vc = cv_ref[sb]
        kn = kn_ref[r0:r0 + n_new, :]
        vn = vn_ref[r0:r0 + n_new, :]
        nk_ref[sb, 0:win - n_new, :] = kc[n_new:]
        nk_ref[sb, win - n_new:win, :] = kn
        nv_ref[sb, 0:win - n_new, :] = vc[n_new:]
        nv_ref[sb, win - n_new:win, :] = vn
        keys += [kc, kn]
        vals += [vc, vn]
    keys = jnp.concatenate(keys, axis=0)
    vals = jnp.concatenate(vals, axis=0)
    ri = lax.broadcasted_iota(I32, (rows, cols), 0)
    ci = lax.broadcasted_iota(I32, (rows, cols), 1)
    ti = ri % n_new
    si = ci % per_seq
    mask = (ri // n_new == ci // per_seq) & (si >= ti) & (si <= ti + WINDOW)
    low = lax.broadcasted_iota(I32, (cols, 128), 1) < SWA_HEAD_DIM
    zero = jnp.zeros((cols, 128), BF16)

    def block_diag(chunk, first):
        rolled = pltpu.roll(chunk, SWA_HEAD_DIM, 1)
        both = (jnp.where(low, chunk, rolled) if first else jnp.where(low, rolled, chunk)).astype(BF16)
        return jnp.concatenate([jnp.where(low, both, zero), jnp.where(low, zero, both)], axis=0)

    chunks = [slice(128 * (g // 2), 128 * (g // 2 + 1)) for g in range(SWA_KV_HEADS)]
    kblks = [block_diag(keys[:, cs], g % 2 == 0) for g, cs in enumerate(chunks)]
    vblks = [block_diag(vals[:, cs], g % 2 == 0) for g, cs in enumerate(chunks)]
    _pair_attention(sink_ref, q_ref, kblks, vblks, mask, o_ref)


def swa_sample_attention(q, k, v, cache_k, cache_v, sinks, n_prompt_rows, n_new):
    n_seq, win, nk = cache_k.shape
    sb = SAMPLE_SEQS
    rows = sb * n_new
    nq = q.shape[1]
    base = n_prompt_rows // rows
    tok = lambda i: (base + i, 0)
    seq = lambda i: (i, 0, 0)
    return pl.pallas_call(
        functools.partial(_swa_sample_kernel, n_new),
        out_shape=(jax.ShapeDtypeStruct((n_seq * n_new, nq), BF16),
                   jax.ShapeDtypeStruct(cache_k.shape, F32), jax.ShapeDtypeStruct(cache_v.shape, F32)),
        grid=(n_seq // sb,),
        in_specs=[pl.BlockSpec(memory_space=pltpu.SMEM),
                  pl.BlockSpec((rows, nq), tok), pl.BlockSpec((rows, nk), tok), pl.BlockSpec((rows, nk), tok),
                  pl.BlockSpec((sb, win, nk), seq), pl.BlockSpec((sb, win, nk), seq)],
        out_specs=(pl.BlockSpec((rows, nq), lambda i: (i, 0)), pl.BlockSpec((sb, win, nk), seq),
                   pl.BlockSpec((sb, win, nk), seq)),
        compiler_params=_params(("parallel",)),
        name="swa_sample_attention",
    )(sinks, q, k, v, cache_k, cache_v)


def _gla_proj_kernel(npt, x_ref, g_ref, scs_ref, sct_ref, shs_ref, sht_ref, w_ref, wa2_ref, ba_ref,
                     q_ref, k_ref, v_ref, r_ref, gate_ref):
    is_s = pl.program_id(0) >= npt
    h = _norm_mod(x_ref[...], g_ref[...], _pick(is_s, scs_ref, sct_ref), _pick(is_s, shs_ref, sht_ref))
    proj = jnp.dot(h.astype(BF16), w_ref[...], preferred_element_type=F32)
    kd = GLA_KEY_DIM
    vd = GLA_VAL_DIM
    q_ref[...] = proj[:, :kd] * (GLA_DK ** -0.5)
    k_ref[...] = proj[:, kd:2 * kd]
    v_ref[...] = proj[:, 2 * kd:2 * kd + vd].astype(BF16)
    r_ref[...] = proj[:, 2 * kd + vd:2 * kd + 2 * vd]
    low = proj[:, 2 * kd + 2 * vd:].astype(BF16)
    z = jnp.dot(low, wa2_ref[...], preferred_element_type=F32) + ba_ref[...]
    log_sig = jnp.minimum(z, 0.0) - jnp.log1p(jnp.exp(-jnp.abs(z)))
    gate_ref[...] = log_sig / GLA_GATE_NORMALIZER


def gla_project(x, g, mod, win_ext, wa2_pad, ba):
    t = x.shape[0]
    tm = mod.tile
    kd, vd = GLA_KEY_DIM, GLA_VAL_DIM
    row = lambda i: (i, 0)
    const = lambda i: (0, 0)
    return pl.pallas_call(
        functools.partial(_gla_proj_kernel, mod.npt),
        out_shape=(jax.ShapeDtypeStruct((t, kd), F32), jax.ShapeDtypeStruct((t, kd), F32),
                   jax.ShapeDtypeStruct((t, vd), BF16), jax.ShapeDtypeStruct((t, vd), F32),
                   jax.ShapeDtypeStruct((t, kd), F32)),
        grid=(t // tm,),
        in_specs=[pl.BlockSpec((tm, D_MODEL), row), pl.BlockSpec((1, D_MODEL), const)]
        + mod.specs(1) + mod.specs(0)
        + [pl.BlockSpec(win_ext.shape, const), pl.BlockSpec(wa2_pad.shape, const), pl.BlockSpec((1, kd), const)],
        out_specs=(pl.BlockSpec((tm, kd), row), pl.BlockSpec((tm, kd), row), pl.BlockSpec((tm, vd), row),
                   pl.BlockSpec((tm, vd), row), pl.BlockSpec((tm, kd), row)),
        compiler_params=_params(("parallel",)),
        name="gla_project",
    )(x, g.reshape(1, -1), *mod.operands(1), *mod.operands(0), win_ext, wa2_pad, ba.reshape(1, -1))


def _cumsum_rows(tri, g):
    n = g.shape[1]
    s = jnp.dot(tri, jnp.concatenate(_split3(g), axis=1), preferred_element_type=F32)
    return s[:, :n] + s[:, n:2 * n] + s[:, 2 * n:]


def _diag_attention(q, k, b, n):
    ng = n // 8
    dk = q.shape[1]
    q3 = q.reshape(ng, 8, dk)
    k3 = k.reshape(ng, 8, dk)
    b3 = b.reshape(ng, 8, dk)
    sub = lax.broadcasted_iota(I32, (ng, 8, dk), 1)
    ti = lax.broadcasted_iota(I32, (n, n), 0)
    si = lax.broadcasted_iota(I32, (n, n), 1)
    attn = jnp.zeros((n, n), F32)
    for j in range(8):
        bj = jnp.broadcast_to(b3[:, j:j + 1, :], b3.shape)
        kj = jnp.broadcast_to(k3[:, j:j + 1, :], k3.shape)
        e = jnp.exp(jnp.minimum(b3 - bj, 0.0))
        m = jnp.where(sub >= j, q3 * e * kj, 0.0)
        col = jnp.sum(m, axis=-1, keepdims=True).reshape(n, 1)
        attn = attn + jnp.where(si == (ti // 8) * 8 + j, col, 0.0)
    return attn


def _cross_attention(q, k, b, n):
    ti = lax.broadcasted_iota(I32, (n, n), 0)
    si = lax.broadcasted_iota(I32, (n, n), 1)
    row = lax.broadcasted_iota(I32, b.shape, 0)
    attn = jnp.zeros((n, n), F32)
    m = n // 2
    while m >= 8:
        nblk = n // m
        refq = jnp.concatenate(
            [jnp.broadcast_to(b[i * m - 1:i * m], (m, b.shape[1])) if i % 2 else b[i * m:(i + 1) * m]
             for i in range(nblk)], axis=0)
        refk = jnp.concatenate(
            [b[i * m:(i + 1) * m] if i % 2 else jnp.broadcast_to(b[(i + 1) * m - 1:(i + 1) * m], (m, b.shape[1]))
             for i in range(nblk)], axis=0)
        odd = ((row // m) % 2) == 1
        qt = jnp.where(odd, q * jnp.exp(jnp.minimum(b - refq, 0.0)), 0.0).astype(BF16)
        kt = jnp.where(odd, 0.0, k * jnp.exp(jnp.minimum(refk - b, 0.0))).astype(BF16)
        a = lax.dot_general(qt, kt, NT_DIMS, preferred_element_type=F32)
        keep = (((ti // m) % 2) == 1) & ((si // m) == (ti // m) - 1)
        attn = attn + jnp.where(keep, a, 0.0)
        m //= 2
    return attn


def _gla_epilogue(o, r, ng):
    ms = jnp.mean(o * o, axis=-1, keepdims=True)
    return (o * lax.rsqrt(ms + NORM_EPS) * ng * _silu(r)).astype(BF16)


def _gla_prompt_kernel(*refs):
    nb = GLA_PROMPT_BATCH
    ins, (ng_ref, o_ref, so_ref, st_ref) = refs[:5 * nb], refs[5 * nb:]
    c = pl.program_id(1)
    n = ins[0].shape[0]

    @pl.when(c == 0)
    def _():
        st_ref[...] = jnp.zeros(st_ref.shape, F32)

    ti = lax.broadcasted_iota(I32, (n, n), 0)
    si = lax.broadcasted_iota(I32, (n, n), 1)
    tri = jnp.where(ti >= si, 1.0, 0.0).astype(BF16)
    for i in range(nb):
        q_ref, k_ref, g_ref, v_ref, r_ref = ins[5 * i:5 * (i + 1)]
        b_all = _cumsum_rows(tri, g_ref[...])
        for h in range(GLA_HEADS):
            ks = slice(GLA_DK * h, GLA_DK * (h + 1))
            vs = slice(GLA_DV * h, GLA_DV * (h + 1))
            q = q_ref[:, ks]
            k = k_ref[:, ks]
            v = v_ref[:, vs]
            b = b_all[:, ks]
            s_t = st_ref[i, h]
            o = lax.dot_general((q * jnp.exp(b)).astype(BF16), s_t.astype(BF16), NT_DIMS,
                                preferred_element_type=F32)
            attn = _cross_attention(q, k, b, n) + _diag_attention(q, k, b, n)
            o = o + jnp.dot(attn.astype(BF16), v, preferred_element_type=F32)
            bl = b[n - 1:n, :]
            kd = (k * jnp.exp(bl - b)).astype(BF16)
            s_new = s_t * jnp.exp(bl) + lax.dot_general(v, kd, TN_DIMS, preferred_element_type=F32)
            st_ref[i, h] = s_new
            o_ref[i, :, vs] = _gla_epilogue(o, r_ref[:, vs], ng_ref[...])

    @pl.when(c == pl.num_programs(1) - 1)
    def _():
        for i in range(nb):
            for h in range(GLA_HEADS):
                so_ref[i, h] = st_ref[i, h].T


def gla_prompt(q, k, g, v, r, norm_g, n_batch, seq_len):
    n = GLA_CHUNK
    nb = GLA_PROMPT_BATCH
    nc = seq_len // n
    kd, vd = GLA_KEY_DIM, GLA_VAL_DIM
    in_specs, operands = [], []
    for i in range(nb):
        row = lambda b, c, i=i: ((nb * b + i) * nc + c, 0)
        in_specs += [pl.BlockSpec((n, kd), row), pl.BlockSpec((n, kd), row), pl.BlockSpec((n, kd), row),
                     pl.BlockSpec((n, vd), row), pl.BlockSpec((n, vd), row)]
        operands += [q, k, g, v, r]
    o, state = pl.pallas_call(
        _gla_prompt_kernel,
        out_shape=(jax.ShapeDtypeStruct((n_batch, seq_len, vd), BF16),
                   jax.ShapeDtypeStruct((n_batch, GLA_HEADS, GLA_DK, GLA_DV), F32)),
        grid=(n_batch // nb, nc),
        in_specs=in_specs + [pl.BlockSpec((1, GLA_DV), lambda b, c: (0, 0))],
        out_specs=(pl.BlockSpec((nb, n, vd), lambda b, c: (b, c, 0)),
                   pl.BlockSpec((nb, GLA_HEADS, GLA_DK, GLA_DV), lambda b, c: (b, 0, 0, 0))),
        scratch_shapes=[pltpu.VMEM((nb, GLA_HEADS, GLA_DV, GLA_DK), F32)],
        compiler_params=_params(("parallel", "arbitrary")),
        name="gla_prompt",
    )(*operands, norm_g.reshape(1, -1))
    return o.reshape(n_batch * seq_len, vd), state


def _gla_sample_kernel(n_new, q_ref, k_ref, g_ref, v_ref, r_ref, ng_ref, si_ref, o_ref, so_ref):
    n = q_ref.shape[0]
    ti = lax.broadcasted_iota(I32, (n, n), 0)
    si = lax.broadcasted_iota(I32, (n, n), 1)
    tri = jnp.where((ti >= si) & (ti // n_new == si // n_new), 1.0, 0.0).astype(BF16)
    b_all = _cumsum_rows(tri, g_ref[...])
    for h in range(GLA_HEADS):
        ks = slice(GLA_DK * h, GLA_DK * (h + 1))
        vs = slice(GLA_DV * h, GLA_DV * (h + 1))
        q = q_ref[:, ks]
        k = k_ref[:, ks]
        v = v_ref[:, vs]
        b = b_all[:, ks]
        attn = _diag_attention(q, k, b, n)
        o_intra = jnp.dot(attn.astype(BF16), v, preferred_element_type=F32)
        qe = (q * jnp.exp(b)).astype(BF16)
        n_sb = n // n_new
        last = [b[n_new * (sb + 1) - 1:n_new * (sb + 1), :] for sb in range(n_sb)]
        bl_rows = jnp.concatenate([jnp.broadcast_to(bl, (n_new, GLA_DK)) for bl in last], axis=0)
        kd = (k * jnp.exp(bl_rows - b)).astype(BF16)
        seq_of_row = lax.broadcasted_iota(I32, (n, GLA_DV), 0) // n_new
        o = o_intra
        for sb in range(n_sb):
            mine = seq_of_row == sb
            s_t = si_ref[sb, h].T
            o_sb = lax.dot_general(qe, s_t.astype(BF16), NT_DIMS, preferred_element_type=F32)
            o = o + jnp.where(mine, o_sb, 0.0)
            v_sb = jnp.where(mine, v, jnp.zeros_like(v))
            upd = lax.dot_general(v_sb, kd, TN_DIMS, preferred_element_type=F32)
            so_ref[sb, h] = (s_t * jnp.exp(last[sb]) + upd).T
        o_ref[:, vs] = _gla_epilogue(o, r_ref[:, vs], ng_ref[...])


def gla_sample(q, k, g, v, r, norm_g, state, n_prompt_rows, n_new):
    n_seq = state.shape[0]
    sb = SAMPLE_SEQS
    rows = sb * n_new
    kd, vd = GLA_KEY_DIM, GLA_VAL_DIM
    base = n_prompt_rows // rows
    tok = lambda i: (base + i, 0)
    seq = lambda i: (i, 0, 0, 0)
    sblock = (sb, GLA_HEADS, GLA_DK, GLA_DV)
    return pl.pallas_call(
        functools.partial(_gla_sample_kernel, n_new),
        out_shape=(jax.ShapeDtypeStruct((n_seq * n_new, vd), BF16), jax.ShapeDtypeStruct(state.shape, F32)),
        grid=(n_seq // sb,),
        in_specs=[pl.BlockSpec((rows, kd), tok), pl.BlockSpec((rows, kd), tok), pl.BlockSpec((rows, kd), tok),
                  pl.BlockSpec((rows, vd), tok), pl.BlockSpec((rows, vd), tok),
                  pl.BlockSpec((1, GLA_DV), lambda i: (0, 0)),
                  pl.BlockSpec(sblock, seq)],
        out_specs=(pl.BlockSpec((rows, vd), lambda i: (i, 0)), pl.BlockSpec(sblock, seq)),
        compiler_params=_params(("parallel",)),
        name="gla_sample",
    )(q, k, g, v, r, norm_g.reshape(1, -1), state)


def _post_mixer_kernel(npt, ap_ref, as_ref, wo_ref, x_ref, g1s_ref, g1t_ref, gf_ref, scs_ref, sct_ref, shs_ref, sht_ref,
                       rw_ref, rb_ref, tri_ref,
                       x1_ref, h_ref, eidx_ref, w_ref, rank_ref, cnt_ref, carry_ref):
    i = pl.program_id(0)
    is_s = i >= npt

    @pl.when(i == 0)
    def _():
        carry_ref[...] = jnp.zeros(carry_ref.shape, F32)

    a = jnp.where(is_s, as_ref[...], ap_ref[...])
    x1 = x_ref[...] + _pick(is_s, g1s_ref, g1t_ref) * jnp.dot(a, wo_ref[...], preferred_element_type=F32)
    x1_ref[...] = x1
    h = _norm_mod(x1, gf_ref[...], _pick(is_s, scs_ref, sct_ref), _pick(is_s, shs_ref, sht_ref))
    h_ref[...] = _pack_bf16_pairs(h)

    h1, h2, _ = _split3(h)
    r1, r2, _ = _split3(rw_ref[...])
    logits = (lax.dot_general(r1, h1, NT_DIMS, preferred_element_type=F32)
              + lax.dot_general(r1, h2, NT_DIMS, preferred_element_type=F32)
              + lax.dot_general(r2, h1, NT_DIMS, preferred_element_type=F32))
    scores = jax.nn.sigmoid(logits)
    sel = scores + rb_ref[...]
    tm = sel.shape[1]
    gsz = N_EXPERTS // N_GROUPS

    sub = lax.broadcasted_iota(I32, (gsz, tm), 0)
    blocks, gscore = [], []
    for g in range(N_GROUPS):
        blk = sel[gsz * g:gsz * (g + 1)]
        m1 = jnp.max(blk, axis=0, keepdims=True)
        first = jnp.min(jnp.where(blk == m1, sub, gsz), axis=0, keepdims=True)
        m2 = jnp.max(jnp.where(sub == first, -jnp.inf, blk), axis=0, keepdims=True)
        blocks.append(blk)
        gscore.append(m1 + m2)
    masked = []
    for g in range(N_GROUPS):
        beaten = jnp.zeros((1, tm), I32)
        for o in range(N_GROUPS):
            if o == g:
                continue
            wins = (gscore[o] > gscore[g]) | ((gscore[o] == gscore[g]) & (o < g))
            beaten = beaten + wins.astype(I32)
        masked.append(jnp.where(beaten < TOPK_GROUPS, blocks[g], -jnp.inf))
    cur = jnp.concatenate(masked, axis=0)

    eid = lax.broadcasted_iota(I32, (N_EXPERTS, tm), 0)
    picked, weights = [], []
    onehot = jnp.zeros((N_EXPERTS, tm), F32)
    for _ in range(TOP_K):
        m = jnp.max(cur, axis=0, keepdims=True)
        idx = jnp.min(jnp.where(cur == m, eid, N_EXPERTS), axis=0, keepdims=True)
        hit = eid == idx
        picked.append(idx)
        weights.append(jnp.sum(jnp.where(hit, scores, 0.0), axis=0, keepdims=True))
        onehot = jnp.where(hit, 1.0, onehot)
        cur = jnp.where(hit, -jnp.inf, cur)
    wsum = weights[0]
    for wk in weights[1:]:
        wsum = wsum + wk
    scale = ROUTED_SCALE / wsum

    before = jnp.dot(onehot.astype(BF16), tri_ref[...], preferred_element_type=F32) + carry_ref[...]
    carry = carry_ref[...] + jnp.sum(onehot, axis=1, keepdims=True)
    carry_ref[...] = carry
    cnt_ref[...] = jnp.broadcast_to(carry, cnt_ref.shape)
    for kk in range(TOP_K):
        eidx_ref[kk:kk + 1, :] = picked[kk]
        w_ref[kk:kk + 1, :] = weights[kk] * scale
        rank_ref[kk:kk + 1, :] = jnp.sum(jnp.where(eid == picked[kk], before, 0.0), axis=0, keepdims=True).astype(I32)


def post_mixer(a_prompt, a_sample, wo_bf, x, mod, gffn, router_t, router_b, tri):
    t = x.shape[0]
    tm = mod.tile
    npt = mod.npt
    row = lambda i: (i, 0)
    col = lambda i: (0, i)
    const = lambda i: (0, 0)
    return pl.pallas_call(
        functools.partial(_post_mixer_kernel, mod.npt),
        out_shape=(jax.ShapeDtypeStruct((t, D_MODEL), F32), jax.ShapeDtypeStruct((t, D_MODEL // 2), U32),
                   jax.ShapeDtypeStruct((TOP_K, t), I32), jax.ShapeDtypeStruct((TOP_K, t), F32),
                   jax.ShapeDtypeStruct((TOP_K, t), I32), jax.ShapeDtypeStruct((N_EXPERTS, 128), F32)),
        grid=(t // tm,),
        in_specs=[pl.BlockSpec((tm, D_MODEL), lambda i: (jnp.minimum(i, npt - 1), 0)),
                  pl.BlockSpec((tm, D_MODEL), lambda i: (jnp.maximum(i - npt, 0), 0)),
                  pl.BlockSpec((D_MODEL, D_MODEL), const), pl.BlockSpec((tm, D_MODEL), row)]
        + mod.specs(2) + [pl.BlockSpec((1, D_MODEL), const)] + mod.specs(4) + mod.specs(3)
        + [pl.BlockSpec((N_EXPERTS, D_MODEL), const), pl.BlockSpec((N_EXPERTS, 1), const),
           pl.BlockSpec((tm, tm), const)],
        out_specs=(pl.BlockSpec((tm, D_MODEL), row), pl.BlockSpec((tm, D_MODEL // 2), row),
                   pl.BlockSpec((TOP_K, tm), col), pl.BlockSpec((TOP_K, tm), col), pl.BlockSpec((TOP_K, tm), col),
                   pl.BlockSpec((N_EXPERTS, 128), const)),
        scratch_shapes=[pltpu.VMEM((N_EXPERTS, 1), F32)],
        compiler_params=_params(("arbitrary",)),
        name="post_mixer",
    )(a_prompt, a_sample, wo_bf, x, *mod.operands(2), gffn.reshape(1, -1), *mod.operands(4), *mod.operands(3),
      router_t, router_b.reshape(-1, 1), tri)


def _row_copy(src, src_row, dst, dst_row, sem):
    return pltpu.make_async_copy(src.at[pl.ds(src_row, 1)], dst.at[pl.ds(dst_row, 1)], sem)


def _by_parity(i, fn):
    @pl.when(i % 2 == 0)
    def _():
        fn(0)

    @pl.when(i % 2 == 1)
    def _():
        fn(1)


def _dispatch_kernel(zb_ref, h_ref, dest_hbm, xs_hbm, idx_a, idx_b, zero_ref, sem_idx, sem_zero, sem_rows):
    i = pl.program_id(0)
    te = h_ref.shape[0]
    idx_bufs = (idx_a, idx_b)

    def idx_copy(tile, p):
        return pltpu.make_async_copy(dest_hbm.at[tile], idx_bufs[p], sem_idx.at[p])

    def zero_copy(e):
        return pltpu.make_async_copy(zero_ref, xs_hbm.at[pl.ds(zb_ref[e] * EXPERT_BLOCK, EXPERT_BLOCK)], sem_zero)

    @pl.when(i == 0)
    def _():
        idx_copy(0, 0).start()
        zero_ref[...] = jnp.zeros(zero_ref.shape, U32)

        def start(e, carry):
            @pl.when(zb_ref[e] >= 0)
            def _():
                zero_copy(e).start()
            return carry

        def wait(e, carry):
            @pl.when(zb_ref[e] >= 0)
            def _():
                zero_copy(e).wait()
            return carry

        lax.fori_loop(0, zb_ref.shape[0], start, 0)
        lax.fori_loop(0, zb_ref.shape[0], wait, 0)

    def step(p):
        @pl.when(i + 1 < pl.num_programs(0))
        def _():
            idx_copy(i + 1, 1 - p).start()

        idx_copy(i, p).wait()
        idx = idx_bufs[p]

        def issue(t, carry):
            for kk in range(TOP_K):
                _row_copy(h_ref, t, xs_hbm, idx[kk * te + t], sem_rows).start()
            return carry

        def drain(t, carry):
            for kk in range(TOP_K):
                _row_copy(h_ref, t, xs_hbm, idx[kk * te + t], sem_rows).wait()
            return carry

        lax.fori_loop(0, te, issue, 0)
        lax.fori_loop(0, te, drain, 0)

    _by_parity(i, step)


def moe_dispatch(h, dest_tiles, zero_blocks, n_slots):
    t = h.shape[0]
    te = ROUTE_TILE
    return pl.pallas_call(
        _dispatch_kernel,
        out_shape=jax.ShapeDtypeStruct((n_slots, h.shape[1]), U32),
        grid_spec=pltpu.PrefetchScalarGridSpec(
            num_scalar_prefetch=1,
            grid=(t // te,),
            in_specs=[pl.BlockSpec((te, h.shape[1]), lambda i, zb: (i, 0)), pl.BlockSpec(memory_space=pl.ANY)],
            out_specs=pl.BlockSpec(memory_space=pl.ANY),
            scratch_shapes=[pltpu.SMEM((te * TOP_K,), I32), pltpu.SMEM((te * TOP_K,), I32),
                            pltpu.VMEM((EXPERT_BLOCK, h.shape[1]), U32),
                            pltpu.SemaphoreType.DMA((2,)), pltpu.SemaphoreType.DMA, pltpu.SemaphoreType.DMA],
        ),
        compiler_params=_params(("arbitrary",)),
        name="moe_dispatch",
    )(zero_blocks, h, dest_tiles)


def _expert_kernel(be_ref, nu_ref, xs_ref, wg0_ref, wu0_ref, wd0_ref, wg1_ref, wu1_ref, wd1_ref, ys_ref,
                   wg_bf, wu_bf, wd_bf):
    b = pl.program_id(0)
    rb = EXPERT_BLOCK
    used = 2 * b < nu_ref[0]
    for s, (wg, wu, wd) in enumerate(((wg0_ref, wu0_ref, wd0_ref), (wg1_ref, wu1_ref, wd1_ref))):
        j = 2 * b + s
        fresh = (b == 0) | (be_ref[j] != be_ref[jnp.maximum(j - 2, 0)])

        @pl.when(used & fresh)
        def _():
            wg_bf[s] = wg[...].astype(BF16)
            wu_bf[s] = wu[...].astype(BF16)
            wd_bf[s] = wd[...].astype(BF16)

    @pl.when(used)
    def _():
        for s in range(2):
            x = _unpack_bf16_pairs(xs_ref[rb * s:rb * (s + 1), :])
            hg = jnp.dot(x, wg_bf[s], preferred_element_type=F32)
            hu = jnp.dot(x, wu_bf[s], preferred_element_type=F32)
            ys_ref[rb * s:rb * (s + 1), :] = jnp.dot((_silu(hg) * hu).astype(BF16), wd_bf[s],
                                                     preferred_element_type=F32)


def moe_experts(xs, block_expert, n_used, wg, wu, wd, layer):
    n_slots = xs.shape[0]
    rb = EXPERT_BLOCK
    ff = wg.shape[3]
    rows = lambda b, be, nu: (jnp.minimum(b, (nu[0] - 1) // 2), 0)
    w_in = lambda s: (lambda b, be, nu: (layer, be[2 * b + s], 0, 0))
    return pl.pallas_call(
        _expert_kernel,
        out_shape=jax.ShapeDtypeStruct((n_slots, D_MODEL), F32),
        grid_spec=pltpu.PrefetchScalarGridSpec(
            num_scalar_prefetch=2,
            grid=(n_slots // (2 * rb),),
            in_specs=[pl.BlockSpec((2 * rb, xs.shape[1]), rows)]
            + [pl.BlockSpec((None, None, D_MODEL, ff), w_in(0)), pl.BlockSpec((None, None, D_MODEL, ff), w_in(0)),
               pl.BlockSpec((None, None, ff, D_MODEL), w_in(0)),
               pl.BlockSpec((None, None, D_MODEL, ff), w_in(1)), pl.BlockSpec((None, None, D_MODEL, ff), w_in(1)),
               pl.BlockSpec((None, None, ff, D_MODEL), w_in(1))],
            out_specs=pl.BlockSpec((2 * rb, D_MODEL), rows),
            scratch_shapes=[pltpu.VMEM((2, D_MODEL, ff), BF16), pltpu.VMEM((2, D_MODEL, ff), BF16),
                            pltpu.VMEM((2, ff, D_MODEL), BF16)],
        ),
        compiler_params=_params(("arbitrary",)),
        name="moe_experts",
    )(block_expert, n_used, xs, wg, wu, wd, wg, wu, wd)


def _combine_kernel(npt, final, h_ref, swg_ref, swu_ref, swd_ref, x1_ref, g2s_ref, g2t_ref, w_ref, fg_ref,
                    dest_hbm, ys_hbm, o_ref, idx_a, idx_b, ybuf_ref, sem_idx, sem_rows):
    i = pl.program_id(0)
    n = pl.num_programs(0)
    tg = h_ref.shape[0]
    idx_bufs = (idx_a, idx_b)

    def idx_copy(tile, p):
        return pltpu.make_async_copy(dest_hbm.at[tile], idx_bufs[p], sem_idx.at[p])

    def gather_rows(p, wait):
        idx = idx_bufs[p]

        def body(t, carry):
            for kk in range(TOP_K):
                src = 0 if wait else idx[kk * tg + t]
                cp = _row_copy(ys_hbm, src, ybuf_ref.at[p, kk], t, sem_rows.at[p])
                cp.wait() if wait else cp.start()
            return carry

        lax.fori_loop(0, tg, body, 0)

    @pl.when(i == 0)
    def _():
        idx_copy(0, 0).start()
        idx_copy(0, 0).wait()
        gather_rows(0, wait=False)

        @pl.when(n > 1)
        def _():
            idx_copy(1, 1).start()

    def prefetch(p):
        @pl.when(i + 1 < n)
        def _():
            idx_copy(i + 1, 1 - p).wait()
            gather_rows(1 - p, wait=False)

        @pl.when(i + 2 < n)
        def _():
            idx_copy(i + 2, p).start()

    _by_parity(i, prefetch)

    hb = _unpack_bf16_pairs(h_ref[...])
    hid = _silu(jnp.dot(hb, swg_ref[...], preferred_element_type=F32)) * jnp.dot(hb, swu_ref[...],
                                                                                 preferred_element_type=F32)
    acc = jnp.dot(hid.astype(BF16), swd_ref[...], preferred_element_type=F32)
    w = w_ref[...]
    gate = _pick(i >= npt, g2s_ref, g2t_ref)

    def finish(p):
        gather_rows(p, wait=True)
        routed = ybuf_ref[p, 0] * w[:, 0:1]
        for kk in range(1, TOP_K):
            routed = routed + ybuf_ref[p, kk] * w[:, kk:kk + 1]
        x2 = x1_ref[...] + gate * (routed + acc)
        if final:
            ms = jnp.mean(x2 * x2, axis=-1, keepdims=True)
            x2 = x2 * lax.rsqrt(ms + NORM_EPS) * fg_ref[...]
        o_ref[...] = x2

    _by_parity(i, finish)


def moe_combine(h, swg_bf, swu_bf, swd_bf, x1, mod, w_tok, final_g, dest_tiles, ys, final):
    t = h.shape[0]
    tg = mod.tile
    row = lambda i: (i, 0)
    const = lambda i: (0, 0)
    return pl.pallas_call(
        functools.partial(_combine_kernel, mod.npt, final),
        out_shape=jax.ShapeDtypeStruct((t, D_MODEL), F32),
        grid=(t // tg,),
        in_specs=[pl.BlockSpec((tg, h.shape[1]), row), pl.BlockSpec(swg_bf.shape, const), pl.BlockSpec(swu_bf.shape, const),
                  pl.BlockSpec(swd_bf.shape, const), pl.BlockSpec((tg, D_MODEL), row)]
        + mod.specs(5)
        + [pl.BlockSpec((tg, TOP_K), row), pl.BlockSpec((1, D_MODEL), const),
           pl.BlockSpec(memory_space=pl.ANY), pl.BlockSpec(memory_space=pl.ANY)],
        out_specs=pl.BlockSpec((tg, D_MODEL), row),
        scratch_shapes=[pltpu.SMEM((tg * TOP_K,), I32), pltpu.SMEM((tg * TOP_K,), I32),
                        pltpu.VMEM((2, TOP_K, tg, D_MODEL), F32),
                        pltpu.SemaphoreType.DMA((2,)), pltpu.SemaphoreType.DMA((2,))],
        compiler_params=_params(("arbitrary",)),
        name="moe_combine",
    )(h, swg_bf, swu_bf, swd_bf, x1, *mod.operands(5), w_tok, final_g.reshape(1, -1), dest_tiles, ys)


def _slot_kernel(eidx_ref, rank_ref, start_ref, o_ref):
    tm = eidx_ref.shape[1]
    tr = o_ref.shape[2]
    eid = lax.broadcasted_iota(I32, (N_EXPERTS, tm), 0)
    start = start_ref[...]
    for kk in range(TOP_K):
        base = jnp.sum(jnp.where(eid == eidx_ref[kk:kk + 1, :], start, 0.0), axis=0, keepdims=True)
        slot = base.astype(I32) + rank_ref[kk:kk + 1, :]
        for j in range(tm // tr):
            o_ref[j, kk:kk + 1, :] = slot[:, tr * j:tr * (j + 1)]


def assignment_slots(eidx_t, rank_t, pad_start):
    t = eidx_t.shape[1]
    tm = TOKEN_TILE
    tr = ROUTE_TILE
    col = lambda i: (0, i)
    out = pl.pallas_call(
        _slot_kernel,
        out_shape=jax.ShapeDtypeStruct((t // tr, TOP_K, tr), I32),
        grid=(t // tm,),
        in_specs=[pl.BlockSpec((TOP_K, tm), col), pl.BlockSpec((TOP_K, tm), col),
                  pl.BlockSpec((N_EXPERTS, 1), lambda i: (0, 0))],
        out_specs=pl.BlockSpec((tm // tr, TOP_K, tr), lambda i: (i, 0, 0)),
        compiler_params=_params(("parallel",)),
        name="assignment_slots",
    )(eidx_t, rank_t, pad_start.astype(F32).reshape(-1, 1))
    return out.reshape(t // tr, TOP_K * tr)


def _routing_tables(counts, n_blocks):
    rb = EXPERT_BLOCK
    counts = counts.astype(I32)
    padded = (counts + rb - 1) // rb * rb
    pad_end = jnp.cumsum(padded)
    pad_start = pad_end - padded
    n_used = pad_end[-1] // rb
    blocks = jnp.arange(n_blocks, dtype=I32)
    block_expert = jnp.sum((pad_end[None, :] <= (blocks * rb)[:, None]).astype(I32), axis=1)
    last_used = jnp.sum((pad_end <= (n_used - 1) * rb).astype(I32))
    block_expert = jnp.minimum(jnp.where(blocks < n_used, block_expert, last_used), N_EXPERTS - 1)
    zero_blocks = jnp.where(counts % rb != 0, pad_end // rb - 1, -1)
    zero_blocks = jnp.concatenate([zero_blocks, jnp.where(n_used % 2 == 1, n_used, -1).reshape(1)]).astype(I32)
    return pad_start, block_expert, n_used.reshape(1).astype(I32), zero_blocks


def moe_layer(h, x1, eidx_t, w_t, rank_t, counts, mod_route, wg, wu, wd, layer, swg_bf, swu_bf, swd_bf,
              final_g, final):
    t = h.shape[0]
    rb = EXPERT_BLOCK
    tr = ROUTE_TILE
    n_blocks = -(-(t * TOP_K) // rb) + N_EXPERTS
    n_blocks += n_blocks % 2
    pad_start, block_expert, n_used, zero_blocks = _routing_tables(counts, n_blocks)
    dest_tiles = assignment_slots(eidx_t, rank_t, pad_start)
    xs = moe_dispatch(h, dest_tiles, zero_blocks, n_blocks * rb)
    ys = moe_experts(xs, block_expert, n_used, wg, wu, wd, layer)
    return moe_combine(h, swg_bf, swu_bf, swd_bf, x1, mod_route, w_t.T, final_g, dest_tiles, ys, final)


def _rope_tables(n_batch, seq_len, n_seq, n_new):
    half = SWA_HEAD_DIM // 2
    inv = ROPE_THETA ** (-jnp.arange(half, dtype=F32) / half)
    pos = jnp.concatenate([jnp.tile(jnp.arange(seq_len, dtype=F32), n_batch),
                           jnp.tile(PAST_LEN + jnp.arange(n_new, dtype=F32), n_seq)])
    ang = pos[:, None] * inv[None, :]
    cos = jnp.tile(jnp.cos(ang), (1, 128 // half))
    sin = jnp.sin(ang)
    sin = jnp.tile(jnp.concatenate([-sin, sin], axis=1), (1, 128 // SWA_HEAD_DIM))
    return cos, sin


def kernel(x_prompt, x_sample, c_prompt, c_sample, cache_swa_k, cache_swa_v, state_gla, norm_mix_g, norm_ffn_g,
           final_g, ada_w, ada_b, swa_wqkv, swa_sinks, swa_wo, gla_win, gla_wa1, gla_wa2, gla_ba, gla_norm_g,
           gla_wo, moe_router, moe_bias, moe_wg, moe_wu, moe_wd, shared_wg, shared_wu, shared_wd):
    n_batch, seq_len, d = x_prompt.shape
    n_seq, n_new, _ = x_sample.shape
    depth = ada_w.shape[0]
    tp = n_batch * seq_len
    ts = n_seq * n_new
    t = tp + ts
    tm = TOKEN_TILE
    tr = ROUTE_TILE

    x = jnp.concatenate([x_prompt.reshape(tp, d), x_sample.reshape(ts, d)], axis=0)
    c_all = jnp.concatenate([jnp.repeat(c_sample, n_new, axis=0), c_prompt], axis=0)
    mod = ada_modulation(c_all, ada_w, ada_b)
    cos_tab, sin_tab = _rope_tables(n_batch, seq_len, n_seq, n_new)
    tri = jnp.triu(jnp.ones((tm, tm), BF16), k=1)

    new_k, new_v, new_s = [], [], []
    new_k_s, new_v_s, new_s_s = [], [], []
    for layer in range(depth):
        mod_tok = _Mod(mod, layer, n_batch, seq_len, tm)
        mod_route = _Mod(mod, layer, n_batch, seq_len, tr)
        m = layer // 2
        if layer % 2 == 0:
            q, k, v, k_dup, v_dup = swa_qkv(x, norm_mix_g[layer], mod_tok, swa_wqkv[m].astype(BF16), cos_tab, sin_tab)
            a_p = swa_prompt_attention(q, k_dup, v_dup, swa_sinks[m], n_batch, seq_len)
            nk = SWA_KV_HEADS * SWA_HEAD_DIM
            a_s, ck, cv = swa_sample_attention(q, k, v, cache_swa_k[m].reshape(n_seq, WINDOW, nk),
                                             cache_swa_v[m].reshape(n_seq, WINDOW, nk), swa_sinks[m], tp, n_new)
            kv_shape = (n_batch, WINDOW, SWA_KV_HEADS, SWA_HEAD_DIM)
            tails = [slice((b + 1) * seq_len - WINDOW, (b + 1) * seq_len) for b in range(n_batch)]
            new_k.append(jnp.stack([k[rows] for rows in tails]).reshape(kv_shape))
            new_v.append(jnp.stack([v[rows] for rows in tails]).reshape(kv_shape))
            new_k_s.append(ck.reshape(n_seq, WINDOW, SWA_KV_HEADS, SWA_HEAD_DIM))
            new_v_s.append(cv.reshape(n_seq, WINDOW, SWA_KV_HEADS, SWA_HEAD_DIM))
            wo = swa_wo[m]
        else:
            pad = jnp.zeros((d, 128 - GLA_GATE_RANK), F32)
            win_ext = jnp.concatenate([gla_win[m], gla_wa1[m], pad], axis=1).astype(BF16)
            wa2_pad = jnp.concatenate([gla_wa2[m], jnp.zeros((128 - GLA_GATE_RANK, GLA_KEY_DIM), F32)],
                                      axis=0).astype(BF16)
            q, k, v, r, gate = gla_project(x, norm_mix_g[layer], mod_tok, win_ext, wa2_pad, gla_ba[m])
            a_p, s_prompt = gla_prompt(q, k, gate, v, r, gla_norm_g[m], n_batch, seq_len)
            a_s, s_sample = gla_sample(q, k, gate, v, r, gla_norm_g[m], state_gla[m], tp, n_new)
            new_s.append(s_prompt)
            new_s_s.append(s_sample)
            wo = gla_wo[m]
        x1, h, eidx_t, w_t, rank_t, cnt = post_mixer(a_p, a_s, wo.astype(BF16), x, mod_tok, norm_ffn_g[layer],
                                                     moe_router[layer].T, moe_bias[layer], tri)
        x = moe_layer(h, x1, eidx_t, w_t, rank_t, cnt[:, 0], mod_route,
                      moe_wg, moe_wu, moe_wd, layer,
                      shared_wg[layer].astype(BF16), shared_wu[layer].astype(BF16), shared_wd[layer].astype(BF16),
                      final_g, layer == depth - 1)

    y_prompt = x[:tp].reshape(n_batch, seq_len, d)
    y_sample = x[tp:].reshape(n_seq, n_new, d)
    return (y_prompt, y_sample, jnp.stack(new_k), jnp.stack(new_v), jnp.stack(new_k_s), jnp.stack(new_v_s),
            jnp.stack(new_s), jnp.stack(new_s_s))
```

```python
import functools

import jax
import jax.numpy as jnp
from jax import lax
from jax.experimental import pallas as pl
from jax.experimental.pallas import tpu as pltpu

F32 = jnp.float32
BF16 = jnp.bfloat16
I32 = jnp.int32
U32 = jnp.uint32

D_MODEL = 1024
PAST_LEN = 8192
SWA_HEAD_DIM = 64
SWA_HEADS = 16
SWA_KV_HEADS = 4
SWA_GROUP = 4
WINDOW = 128
ROPE_THETA = 10000.0
GLA_HEADS = 4
GLA_DK = 128
GLA_DV = 256
GLA_KEY_DIM = 512
GLA_VAL_DIM = 1024
GLA_GATE_RANK = 16
GLA_GATE_NORMALIZER = 16.0
GLA_CHUNK = 64
N_EXPERTS = 64
TOP_K = 8
N_GROUPS = 8
TOPK_GROUPS = 4
EXPERT_FF = 256
ROUTED_SCALE = 2.5
NORM_EPS = 1e-6

TOKEN_TILE = 512
ROUTE_TILE = 256
EXPERT_BLOCK = 256
ADA_TILE = 512
SAMPLE_SEQS = 8
GLA_PROMPT_BATCH = 2
VMEM_LIMIT = 48 * 1024 * 1024

NT_DIMS = (((1,), (1,)), ((), ()))
TN_DIMS = (((0,), (0,)), ((), ()))


def _params(semantics):
    return pltpu.CompilerParams(dimension_semantics=semantics, vmem_limit_bytes=VMEM_LIMIT)


def _silu(x):
    return x * jax.nn.sigmoid(x)


def _norm_mod(x, g, sc, sh):
    ms = jnp.mean(x * x, axis=-1, keepdims=True)
    return (x * lax.rsqrt(ms + NORM_EPS) * g) * (1.0 + sc) + sh


def _pack_bf16_pairs(x):
    half = x.shape[1] // 2
    xb = x.astype(BF16).astype(F32)
    lo = lax.bitcast_convert_type(xb[:, :half], U32) >> 16
    hi = lax.bitcast_convert_type(xb[:, half:], U32) & jnp.uint32(0xFFFF0000)
    return lo | hi


def _unpack_bf16_pairs(u):
    lo = lax.bitcast_convert_type(u << 16, F32)
    hi = lax.bitcast_convert_type(u & jnp.uint32(0xFFFF0000), F32)
    return jnp.concatenate([lo, hi], axis=1).astype(BF16)


def _split3(x):
    x1 = x.astype(BF16)
    r1 = x - x1.astype(F32)
    x2 = r1.astype(BF16)
    x3 = (r1 - x2.astype(F32)).astype(BF16)
    return x1, x2, x3


def _ada_kernel(c_ref, w_ref, b_ref, o_ref):
    s = _silu(c_ref[...]).astype(BF16)
    o_ref[...] = jnp.dot(s, w_ref[...].astype(BF16), preferred_element_type=F32) + b_ref[...]


def ada_modulation(c_all, ada_w, ada_b):
    depth, d, n = ada_w.shape
    rows = c_all.shape[0]
    return pl.pallas_call(
        _ada_kernel,
        out_shape=jax.ShapeDtypeStruct((depth, rows, n), F32),
        grid=(depth, n // ADA_TILE),
        in_specs=[
            pl.BlockSpec((rows, d), lambda l, j: (0, 0)),
            pl.BlockSpec((None, d, ADA_TILE), lambda l, j: (l, 0, j)),
            pl.BlockSpec((None, 1, ADA_TILE), lambda l, j: (l, 0, j)),
        ],
        out_specs=pl.BlockSpec((None, rows, ADA_TILE), lambda l, j: (l, 0, j)),
        compiler_params=_params(("parallel", "parallel")),
        name="ada_modulation",
    )(c_all, ada_w, ada_b.reshape(depth, 1, n))


class _Mod:
    def __init__(self, mod, layer, n_batch, seq_len, tile):
        depth, rows, n = mod.shape
        self.tile = tile
        self.layer = layer
        self.npt = n_batch * seq_len // tile
        self.mod_tok = mod
        self.mod_seq = mod[:, rows - n_batch:].reshape(depth, n_batch, 1, n)
        self.tiles_per_seq = seq_len // tile

    def operands(self, chunk):
        del chunk
        return [self.mod_seq, self.mod_tok]

    def specs(self, chunk):
        l, npt, tps = self.layer, self.npt, self.tiles_per_seq
        n_seq = self.mod_seq.shape[1]
        seq_spec = pl.BlockSpec((None, None, 1, D_MODEL),
                                lambda i, *_: (l, jnp.minimum(i // tps, n_seq - 1), 0, chunk))
        tok_spec = pl.BlockSpec((None, self.tile, D_MODEL),
                                lambda i, *_: (l, jnp.maximum(i - npt, 0), chunk))
        return [seq_spec, tok_spec]


def _pick(is_sample, seq_ref, tok_ref):
    return jnp.where(is_sample, tok_ref[...], seq_ref[...])


def _swa_qkv_kernel(npt, x_ref, g_ref, scs_ref, sct_ref, shs_ref, sht_ref, w_ref, cos_ref, sin_ref,
                    q_ref, k_ref, v_ref, kd_ref, vd_ref):
    is_s = pl.program_id(0) >= npt
    h = _norm_mod(x_ref[...], g_ref[...], _pick(is_s, scs_ref, sct_ref), _pick(is_s, shs_ref, sht_ref))
    qkv = jnp.dot(h.astype(BF16), w_ref[...], preferred_element_type=F32)
    cos = cos_ref[...]
    sin = sin_ref[...]
    lane = lax.broadcasted_iota(I32, cos.shape, 1)
    first_half = (lane % SWA_HEAD_DIM) < (SWA_HEAD_DIM // 2)

    def rope(xc):
        rot = jnp.where(first_half, pltpu.roll(xc, 128 - SWA_HEAD_DIM // 2, 1), pltpu.roll(xc, SWA_HEAD_DIM // 2, 1))
        return xc * cos + rot * sin

    nq = SWA_HEADS * SWA_HEAD_DIM
    nk = SWA_KV_HEADS * SWA_HEAD_DIM
    for c in range(nq // 128):
        q_ref[:, 128 * c:128 * (c + 1)] = (rope(qkv[:, 128 * c:128 * (c + 1)]) * (SWA_HEAD_DIM ** -0.5)).astype(BF16)
    low = lane < SWA_HEAD_DIM

    def spread(chunk):
        rolled = pltpu.roll(chunk, SWA_HEAD_DIM, 1)
        return jnp.where(low, chunk, rolled).astype(BF16), jnp.where(low, rolled, chunk).astype(BF16)

    for c in range(nk // 128):
        kc = rope(qkv[:, nq + 128 * c:nq + 128 * (c + 1)])
        vc = qkv[:, nq + nk + 128 * c:nq + nk + 128 * (c + 1)]
        k_ref[:, 128 * c:128 * (c + 1)] = kc
        v_ref[:, 128 * c:128 * (c + 1)] = vc
        kd_ref[:, 256 * c:256 * c + 128], kd_ref[:, 256 * c + 128:256 * (c + 1)] = spread(kc)
        vd_ref[:, 256 * c:256 * c + 128], vd_ref[:, 256 * c + 128:256 * (c + 1)] = spread(vc)


def swa_qkv(x, g, mod, w_bf, cos_tab, sin_tab):
    t = x.shape[0]
    tm = mod.tile
    nq = SWA_HEADS * SWA_HEAD_DIM
    nk = SWA_KV_HEADS * SWA_HEAD_DIM
    row = lambda i: (i, 0)
    return pl.pallas_call(
        functools.partial(_swa_qkv_kernel, mod.npt),
        out_shape=(jax.ShapeDtypeStruct((t, nq), BF16), jax.ShapeDtypeStruct((t, nk), F32),
                   jax.ShapeDtypeStruct((t, nk), F32), jax.ShapeDtypeStruct((t, 2 * nk), BF16),
                   jax.ShapeDtypeStruct((t, 2 * nk), BF16)),
        grid=(t // tm,),
        in_specs=[pl.BlockSpec((tm, D_MODEL), row), pl.BlockSpec((1, D_MODEL), lambda i: (0, 0))]
        + mod.specs(1) + mod.specs(0)
        + [pl.BlockSpec(w_bf.shape, lambda i: (0, 0)), pl.BlockSpec((tm, 128), row), pl.BlockSpec((tm, 128), row)],
        out_specs=(pl.BlockSpec((tm, nq), row), pl.BlockSpec((tm, nk), row), pl.BlockSpec((tm, nk), row),
                   pl.BlockSpec((tm, 2 * nk), row), pl.BlockSpec((tm, 2 * nk), row)),
        compiler_params=_params(("parallel",)),
        name="swa_qkv",
    )(x, g.reshape(1, -1), *mod.operands(1), *mod.operands(0), w_bf, cos_tab, sin_tab)


def _sink_softmax(s, sink_col):
    m = jnp.maximum(jnp.max(s, axis=-1, keepdims=True), sink_col)
    e = jnp.exp(s - m)
    den = jnp.sum(e, axis=-1, keepdims=True) + jnp.exp(sink_col - m)
    return e * (1.0 / den)


def _pair_attention(sink_ref, q_ref, kblks, vblks, mask, o_ref):
    half = mask.shape[1]
    scores = []
    for c in range(SWA_HEADS // 2):
        s = lax.dot_general(q_ref[:, 128 * c:128 * (c + 1)], kblks[c // (SWA_GROUP // 2)], NT_DIMS,
                            preferred_element_type=F32)
        scores += [s[:, :half], s[:, half:]]
    s_all = jnp.where(mask, jnp.stack(scores), -jnp.inf)
    sinks = jnp.stack([jnp.full((1, 1), sink_ref[h], F32) for h in range(SWA_HEADS)])
    p_all = _sink_softmax(s_all, sinks).astype(BF16)
    for c in range(SWA_HEADS // 2):
        p = jnp.concatenate([p_all[2 * c], p_all[2 * c + 1]], axis=1)
        o_ref[:, 128 * c:128 * (c + 1)] = jnp.dot(p, vblks[c // (SWA_GROUP // 2)],
                                                  preferred_element_type=F32).astype(BF16)


def _swa_prompt_kernel(sink_ref, q_ref, kc_ref, kp_ref, vc_ref, vp_ref, o_ref):
    j = pl.program_id(1)
    blk = q_ref.shape[0]
    qi = lax.broadcasted_iota(I32, (blk, 2 * blk), 0)
    sj = lax.broadcasted_iota(I32, (blk, 2 * blk), 1)
    rel = qi + blk - sj
    mask = (rel >= 0) & (rel <= WINDOW) & ((sj >= blk) | (j > 0))
    low = lax.broadcasted_iota(I32, (2 * blk, 128), 1) < SWA_HEAD_DIM
    zero = jnp.zeros((2 * blk, 128), BF16)
    kblks, vblks = [], []
    for g in range(SWA_KV_HEADS):
        cs = slice(128 * g, 128 * (g + 1))
        kcat = jnp.concatenate([kp_ref[:, cs], kc_ref[:, cs]], axis=0)
        vcat = jnp.concatenate([vp_ref[:, cs], vc_ref[:, cs]], axis=0)
        kblks.append(jnp.concatenate([jnp.where(low, kcat, zero), jnp.where(low, zero, kcat)], axis=0))
        vblks.append(jnp.concatenate([jnp.where(low, vcat, zero), jnp.where(low, zero, vcat)], axis=0))
    _pair_attention(sink_ref, q_ref, kblks, vblks, mask, o_ref)


def swa_prompt_attention(q, k, v, sinks, n_batch, seq_len):
    blk = WINDOW
    nb = seq_len // blk
    nq = q.shape[1]
    nk = k.shape[1]
    cur = lambda b, j: (b * nb + j, 0)
    prev = lambda b, j: (b * nb + jnp.maximum(j - 1, 0), 0)
    return pl.pallas_call(
        _swa_prompt_kernel,
        out_shape=jax.ShapeDtypeStruct((n_batch * seq_len, nq), BF16),
        grid=(n_batch, nb),
        in_specs=[pl.BlockSpec(memory_space=pltpu.SMEM),
                  pl.BlockSpec((blk, nq), cur),
                  pl.BlockSpec((blk, nk), cur), pl.BlockSpec((blk, nk), prev),
                  pl.BlockSpec((blk, nk), cur), pl.BlockSpec((blk, nk), prev)],
        out_specs=pl.BlockSpec((blk, nq), cur),
        compiler_params=_params(("parallel", "parallel")),
        name="swa_prompt_attention",
    )(sinks, q, k, k, v, v)


def _swa_sample_kernel(n_new, sink_ref, q_ref, kn_ref, vn_ref, ck_ref, cv_ref, o_ref, nk_ref, nv_ref):
    n_sb, win, _ = ck_ref.shape
    per_seq = win + n_new
    rows = n_sb * n_new
    cols = n_sb * per_seq
    keys, vals = [], []
    for sb in range(n_sb):
        r0 = sb * n_new
        kc = ck_ref[sb]
        vc = cv_ref[sb]
        kn = kn_ref[r0:r0 + n_new, :]
        vn = vn_ref[r0:r0 + n_new, :]
        nk_ref[sb, 0:win - n_new, :] = kc[n_new:]
        nk_ref[sb, win - n_new:win, :] = kn
        nv_ref[sb, 0:win - n_new, :] = vc[n_new:]
        nv_ref[sb, win - n_new:win, :] = vn
        keys += [kc, kn]
        vals += [vc, vn]
    keys = jnp.concatenate(keys, axis=0)
    vals = jnp.concatenate(vals, axis=0)
    ri = lax.broadcasted_iota(I32, (rows, cols), 0)
    ci = lax.broadcasted_iota(I32, (rows, cols), 1)
    ti = ri % n_new
    si = ci % per_seq
    mask = (ri // n_new == ci // per_seq) & (si >= ti) & (si <= ti + WINDOW)
    low = lax.broadcasted_iota(I32, (cols, 128), 1) < SWA_HEAD_DIM
    zero = jnp.zeros((cols, 128), BF16)

    def block_diag(chunk, first):
        rolled = pltpu.roll(chunk, SWA_HEAD_DIM, 1)
        both = (jnp.where(low, chunk, rolled) if first else jnp.where(low, rolled, chunk)).astype(BF16)
        return jnp.concatenate([jnp.where(low, both, zero), jnp.where(low, zero, both)], axis=0)

    chunks = [slice(128 * (g // 2), 128 * (g // 2 + 1)) for g in range(SWA_KV_HEADS)]
    kblks = [block_diag(keys[:, cs], g % 2 == 0) for g, cs in enumerate(chunks)]
    vblks = [block_diag(vals[:, cs], g % 2 == 0) for g, cs in enumerate(chunks)]
    _pair_attention(sink_ref, q_ref, kblks, vblks, mask, o_ref)


def swa_sample_attention(q, k, v, cache_k, cache_v, sinks, n_prompt_rows, n_new):
    n_seq, win, nk = cache_k.shape
    sb = SAMPLE_SEQS
    rows = sb * n_new
    nq = q.shape[1]
    base = n_prompt_rows // rows
    tok = lambda i: (base + i, 0)
    seq = lambda i: (i, 0, 0)
    return pl.pallas_call(
        functools.partial(_swa_sample_kernel, n_new),
        out_shape=(jax.ShapeDtypeStruct((n_seq * n_new, nq), BF16),
                   jax.ShapeDtypeStruct(cache_k.shape, F32), jax.ShapeDtypeStruct(cache_v.shape, F32)),
        grid=(n_seq // sb,),
        in_specs=[pl.BlockSpec(memory_space=pltpu.SMEM),
                  pl.BlockSpec((rows, nq), tok), pl.BlockSpec((rows, nk), tok), pl.BlockSpec((rows, nk), tok),
                  pl.BlockSpec((sb, win, nk), seq), pl.BlockSpec((sb, win, nk), seq)],
        out_specs=(pl.BlockSpec((rows, nq), lambda i: (i, 0)), pl.BlockSpec((sb, win, nk), seq),
                   pl.BlockSpec((sb, win, nk), seq)),
        compiler_params=_params(("parallel",)),
        name="swa_sample_attention",
    )(sinks, q, k, v, cache_k, cache_v)


def _gla_proj_kernel(npt, x_ref, g_ref, scs_ref, sct_ref, shs_ref, sht_ref, w_ref, wa2_ref, ba_ref,
                     q_ref, k_ref, v_ref, r_ref, gate_ref):
    is_s = pl.program_id(0) >= npt
    h = _norm_mod(x_ref[...], g_ref[...], _pick(is_s, scs_ref, sct_ref), _pick(is_s, shs_ref, sht_ref))
    proj = jnp.dot(h.astype(BF16), w_ref[...], preferred_element_type=F32)
    kd = GLA_KEY_DIM
    vd = GLA_VAL_DIM
    q_ref[...] = proj[:, :kd] * (GLA_DK ** -0.5)
    k_ref[...] = proj[:, kd:2 * kd]
    v_ref[...] = proj[:, 2 * kd:2 * kd + vd].astype(BF16)
    r_ref[...] = proj[:, 2 * kd + vd:2 * kd + 2 * vd]
    low = proj[:, 2 * kd + 2 * vd:].astype(BF16)
    z = jnp.dot(low, wa2_ref[...], preferred_element_type=F32) + ba_ref[...]
    log_sig = jnp.minimum(z, 0.0) - jnp.log1p(jnp.exp(-jnp.abs(z)))
    gate_ref[...] = log_sig / GLA_GATE_NORMALIZER


def gla_project(x, g, mod, win_ext, wa2_pad, ba):
    t = x.shape[0]
    tm = mod.tile
    kd, vd = GLA_KEY_DIM, GLA_VAL_DIM
    row = lambda i: (i, 0)
    const = lambda i: (0, 0)
    return pl.pallas_call(
        functools.partial(_gla_proj_kernel, mod.npt),
        out_shape=(jax.ShapeDtypeStruct((t, kd), F32), jax.ShapeDtypeStruct((t, kd), F32),
                   jax.ShapeDtypeStruct((t, vd), BF16), jax.ShapeDtypeStruct((t, vd), F32),
                   jax.ShapeDtypeStruct((t, kd), F32)),
        grid=(t // tm,),
        in_specs=[pl.BlockSpec((tm, D_MODEL), row), pl.BlockSpec((1, D_MODEL), const)]
        + mod.specs(1) + mod.specs(0)
        + [pl.BlockSpec(win_ext.shape, const), pl.BlockSpec(wa2_pad.shape, const), pl.BlockSpec((1, kd), const)],
        out_specs=(pl.BlockSpec((tm, kd), row), pl.BlockSpec((tm, kd), row), pl.BlockSpec((tm, vd), row),
                   pl.BlockSpec((tm, vd), row), pl.BlockSpec((tm, kd), row)),
        compiler_params=_params(("parallel",)),
        name="gla_project",
    )(x, g.reshape(1, -1), *mod.operands(1), *mod.operands(0), win_ext, wa2_pad, ba.reshape(1, -1))


def _cumsum_rows(tri, g):
    n = g.shape[1]
    s = jnp.dot(tri, jnp.concatenate(_split3(g), axis=1), preferred_element_type=F32)
    return s[:, :n] + s[:, n:2 * n] + s[:, 2 * n:]


def _diag_attention(q, k, b, n):
    ng = n // 8
    dk = q.shape[1]
    q3 = q.reshape(ng, 8, dk)
    k3 = k.reshape(ng, 8, dk)
    b3 = b.reshape(ng, 8, dk)
    sub = lax.broadcasted_iota(I32, (ng, 8, dk), 1)
    ti = lax.broadcasted_iota(I32, (n, n), 0)
    si = lax.broadcasted_iota(I32, (n, n), 1)
    attn = jnp.zeros((n, n), F32)
    for j in range(8):
        bj = jnp.broadcast_to(b3[:, j:j + 1, :], b3.shape)
        kj = jnp.broadcast_to(k3[:, j:j + 1, :], k3.shape)
        e = jnp.exp(jnp.minimum(b3 - bj, 0.0))
        m = jnp.where(sub >= j, q3 * e * kj, 0.0)
        col = jnp.sum(m, axis=-1, keepdims=True).reshape(n, 1)
        attn = attn + jnp.where(si == (ti // 8) * 8 + j, col, 0.0)
    return attn


def _cross_attention(q, k, b, n):
    ti = lax.broadcasted_iota(I32, (n, n), 0)
    si = lax.broadcasted_iota(I32, (n, n), 1)
    row = lax.broadcasted_iota(I32, b.shape, 0)
    attn = jnp.zeros((n, n), F32)
    m = n // 2
    while m >= 8:
        nblk = n // m
        refq = jnp.concatenate(
            [jnp.broadcast_to(b[i * m - 1:i * m], (m, b.shape[1])) if i % 2 else b[i * m:(i + 1) * m]
             for i in range(nblk)], axis=0)
        refk = jnp.concatenate(
            [b[i * m:(i + 1) * m] if i % 2 else jnp.broadcast_to(b[(i + 1) * m - 1:(i + 1) * m], (m, b.shape[1]))
             for i in range(nblk)], axis=0)
        odd = ((row // m) % 2) == 1
        qt = jnp.where(odd, q * jnp.exp(jnp.minimum(b - refq, 0.0)), 0.0).astype(BF16)
        kt = jnp.where(odd, 0.0, k * jnp.exp(jnp.minimum(refk - b, 0.0))).astype(BF16)
        a = lax.dot_general(qt, kt, NT_DIMS, preferred_element_type=F32)
        keep = (((ti // m) % 2) == 1) & ((si // m) == (ti // m) - 1)
        attn = attn + jnp.where(keep, a, 0.0)
        m //= 2
    return attn


def _gla_epilogue(o, r, ng):
    ms = jnp.mean(o * o, axis=-1, keepdims=True)
    return (o * lax.rsqrt(ms + NORM_EPS) * ng * _silu(r)).astype(BF16)


def _gla_prompt_kernel(*refs):
    nb = GLA_PROMPT_BATCH
    ins, (ng_ref, o_ref, so_ref, st_ref) = refs[:5 * nb], refs[5 * nb:]
    c = pl.program_id(1)
    n = ins[0].shape[0]

    @pl.when(c == 0)
    def _():
        st_ref[...] = jnp.zeros(st_ref.shape, F32)

    ti = lax.broadcasted_iota(I32, (n, n), 0)
    si = lax.broadcasted_iota(I32, (n, n), 1)
    tri = jnp.where(ti >= si, 1.0, 0.0).astype(BF16)
    for i in range(nb):
        q_ref, k_ref, g_ref, v_ref, r_ref = ins[5 * i:5 * (i + 1)]
        b_all = _cumsum_rows(tri, g_ref[...])
        for h in range(GLA_HEADS):
            ks = slice(GLA_DK * h, GLA_DK * (h + 1))
            vs = slice(GLA_DV * h, GLA_DV * (h + 1))
            q = q_ref[:, ks]
            k = k_ref[:, ks]
            v = v_ref[:, vs]
            b = b_all[:, ks]
            s_t = st_ref[i, h]
            o = lax.dot_general((q * jnp.exp(b)).astype(BF16), s_t.astype(BF16), NT_DIMS,
                                preferred_element_type=F32)
            attn = _cross_attention(q, k, b, n) + _diag_attention(q, k, b, n)
            o = o + jnp.dot(attn.astype(BF16), v, preferred_element_type=F32)
            bl = b[n - 1:n, :]
            kd = (k * jnp.exp(bl - b)).astype(BF16)
            s_new = s_t * jnp.exp(bl) + lax.dot_general(v, kd, TN_DIMS, preferred_element_type=F32)
            st_ref[i, h] = s_new
            o_ref[i, :, vs] = _gla_epilogue(o, r_ref[:, vs], ng_ref[...])

    @pl.when(c == pl.num_programs(1) - 1)
    def _():
        for i in range(nb):
            for h in range(GLA_HEADS):
                so_ref[i, h] = st_ref[i, h].T


def gla_prompt(q, k, g, v, r, norm_g, n_batch, seq_len):
    n = GLA_CHUNK
    nb = GLA_PROMPT_BATCH
    nc = seq_len // n
    kd, vd = GLA_KEY_DIM, GLA_VAL_DIM
    in_specs, operands = [], []
    for i in range(nb):
        row = lambda b, c, i=i: ((nb * b + i) * nc + c, 0)
        in_specs += [pl.BlockSpec((n, kd), row), pl.BlockSpec((n, kd), row), pl.BlockSpec((n, kd), row),
                     pl.BlockSpec((n, vd), row), pl.BlockSpec((n, vd), row)]
        operands += [q, k, g, v, r]
    o, state = pl.pallas_call(
        _gla_prompt_kernel,
        out_shape=(jax.ShapeDtypeStruct((n_batch, seq_len, vd), BF16),
                   jax.ShapeDtypeStruct((n_batch, GLA_HEADS, GLA_DK, GLA_DV), F32)),
        grid=(n_batch // nb, nc),
        in_specs=in_specs + [pl.BlockSpec((1, GLA_DV), lambda b, c: (0, 0))],
        out_specs=(pl.BlockSpec((nb, n, vd), lambda b, c: (b, c, 0)),
                   pl.BlockSpec((nb, GLA_HEADS, GLA_DK, GLA_DV), lambda b, c: (b, 0, 0, 0))),
        scratch_shapes=[pltpu.VMEM((nb, GLA_HEADS, GLA_DV, GLA_DK), F32)],
        compiler_params=_params(("parallel", "arbitrary")),
        name="gla_prompt",
    )(*operands, norm_g.reshape(1, -1))
    return o.reshape(n_batch * seq_len, vd), state


def _gla_sample_kernel(n_new, q_ref, k_ref, g_ref, v_ref, r_ref, ng_ref, si_ref, o_ref, so_ref):
    n = q_ref.shape[0]
    ti = lax.broadcasted_iota(I32, (n, n), 0)
    si = lax.broadcasted_iota(I32, (n, n), 1)
    tri = jnp.where((ti >= si) & (ti // n_new == si // n_new), 1.0, 0.0).astype(BF16)
    b_all = _cumsum_rows(tri, g_ref[...])
    for h in range(GLA_HEADS):
        ks = slice(GLA_DK * h, GLA_DK * (h + 1))
        vs = slice(GLA_DV * h, GLA_DV * (h + 1))
        q = q_ref[:, ks]
        k = k_ref[:, ks]
        v = v_ref[:, vs]
        b = b_all[:, ks]
        attn = _diag_attention(q, k, b, n)
        o_intra = jnp.dot(attn.astype(BF16), v, preferred_element_type=F32)
        qe = (q * jnp.exp(b)).astype(BF16)
        n_sb = n // n_new
        last = [b[n_new * (sb + 1) - 1:n_new * (sb + 1), :] for sb in range(n_sb)]
        bl_rows = jnp.concatenate([jnp.broadcast_to(bl, (n_new, GLA_DK)) for bl in last], axis=0)
        kd = (k * jnp.exp(bl_rows - b)).astype(BF16)
        seq_of_row = lax.broadcasted_iota(I32, (n, GLA_DV), 0) // n_new
        o = o_intra
        for sb in range(n_sb):
            mine = seq_of_row == sb
            s_t = si_ref[sb, h].T
            o_sb = lax.dot_general(qe, s_t.astype(BF16), NT_DIMS, preferred_element_type=F32)
            o = o + jnp.where(mine, o_sb, 0.0)
            v_sb = jnp.where(mine, v, jnp.zeros_like(v))
            upd = lax.dot_general(v_sb, kd, TN_DIMS, preferred_element_type=F32)
            so_ref[sb, h] = (s_t * jnp.exp(last[sb]) + upd).T
        o_ref[:, vs] = _gla_epilogue(o, r_ref[:, vs], ng_ref[...])


def gla_sample(q, k, g, v, r, norm_g, state, n_prompt_rows, n_new):
    n_seq = state.shape[0]
    sb = SAMPLE_SEQS
    rows = sb * n_new
    kd, vd = GLA_KEY_DIM, GLA_VAL_DIM
    base = n_prompt_rows // rows
    tok = lambda i: (base + i, 0)
    seq = lambda i: (i, 0, 0, 0)
    sblock = (sb, GLA_HEADS, GLA_DK, GLA_DV)
    return pl.pallas_call(
        functools.partial(_gla_sample_kernel, n_new),
        out_shape=(jax.ShapeDtypeStruct((n_seq * n_new, vd), BF16), jax.ShapeDtypeStruct(state.shape, F32)),
        grid=(n_seq // sb,),
        in_specs=[pl.BlockSpec((rows, kd), tok), pl.BlockSpec((rows, kd), tok), pl.BlockSpec((rows, kd), tok),
                  pl.BlockSpec((rows, vd), tok), pl.BlockSpec((rows, vd), tok),
                  pl.BlockSpec((1, GLA_DV), lambda i: (0, 0)),
                  pl.BlockSpec(sblock, seq)],
        out_specs=(pl.BlockSpec((rows, vd), lambda i: (i, 0)), pl.BlockSpec(sblock, seq)),
        compiler_params=_params(("parallel",)),
        name="gla_sample",
    )(q, k, g, v, r, norm_g.reshape(1, -1), state)


def _post_mixer_kernel(npt, ap_ref, as_ref, wo_ref, x_ref, g1s_ref, g1t_ref, gf_ref, scs_ref, sct_ref, shs_ref, sht_ref,
                       rw_ref, rb_ref, tri_ref,
                       x1_ref, h_ref, eidx_ref, w_ref, rank_ref, cnt_ref, carry_ref):
    i = pl.program_id(0)
    is_s = i >= npt

    @pl.when(i == 0)
    def _():
        carry_ref[...] = jnp.zeros(carry_ref.shape, F32)

    a = jnp.where(is_s, as_ref[...], ap_ref[...])
    x1 = x_ref[...] + _pick(is_s, g1s_ref, g1t_ref) * jnp.dot(a, wo_ref[...], preferred_element_type=F32)
    x1_ref[...] = x1
    h = _norm_mod(x1, gf_ref[...], _pick(is_s, scs_ref, sct_ref), _pick(is_s, shs_ref, sht_ref))
    h_ref[...] = _pack_bf16_pairs(h)

    h1, h2, _ = _split3(h)
    r1, r2, _ = _split3(rw_ref[...])
    logits = (lax.dot_general(r1, h1, NT_DIMS, preferred_element_type=F32)
              + lax.dot_general(r1, h2, NT_DIMS, preferred_element_type=F32)
              + lax.dot_general(r2, h1, NT_DIMS, preferred_element_type=F32))
    scores = jax.nn.sigmoid(logits)
    sel = scores + rb_ref[...]
    tm = sel.shape[1]
    gsz = N_EXPERTS // N_GROUPS

    sub = lax.broadcasted_iota(I32, (gsz, tm), 0)
    blocks, gscore = [], []
    for g in range(N_GROUPS):
        blk = sel[gsz * g:gsz * (g + 1)]
        m1 = jnp.max(blk, axis=0, keepdims=True)
        first = jnp.min(jnp.where(blk == m1, sub, gsz), axis=0, keepdims=True)
        m2 = jnp.max(jnp.where(sub == first, -jnp.inf, blk), axis=0, keepdims=True)
        blocks.append(blk)
        gscore.append(m1 + m2)
    masked = []
    for g in range(N_GROUPS):
        beaten = jnp.zeros((1, tm), I32)
        for o in range(N_GROUPS):
            if o == g:
                continue
            wins = (gscore[o] > gscore[g]) | ((gscore[o] == gscore[g]) & (o < g))
            beaten = beaten + wins.astype(I32)
        masked.append(jnp.where(beaten < TOPK_GROUPS, blocks[g], -jnp.inf))
    cur = jnp.concatenate(masked, axis=0)

    eid = lax.broadcasted_iota(I32, (N_EXPERTS, tm), 0)
    picked, weights = [], []
    onehot = jnp.zeros((N_EXPERTS, tm), F32)
    for _ in range(TOP_K):
        m = jnp.max(cur, axis=0, keepdims=True)
        idx = jnp.min(jnp.where(cur == m, eid, N_EXPERTS), axis=0, keepdims=True)
        hit = eid == idx
        picked.append(idx)
        weights.append(jnp.sum(jnp.where(hit, scores, 0.0), axis=0, keepdims=True))
        onehot = jnp.where(hit, 1.0, onehot)
        cur = jnp.where(hit, -jnp.inf, cur)
    wsum = weights[0]
    for wk in weights[1:]:
        wsum = wsum + wk
    scale = ROUTED_SCALE / wsum

    before = jnp.dot(onehot.astype(BF16), tri_ref[...], preferred_element_type=F32) + carry_ref[...]
    carry = carry_ref[...] + jnp.sum(onehot, axis=1, keepdims=True)
    carry_ref[...] = carry
    cnt_ref[...] = jnp.broadcast_to(carry, cnt_ref.shape)
    for kk in range(TOP_K):
        eidx_ref[kk:kk + 1, :] = picked[kk]
        w_ref[kk:kk + 1, :] = weights[kk] * scale
        rank_ref[kk:kk + 1, :] = jnp.sum(jnp.where(eid == picked[kk], before, 0.0), axis=0, keepdims=True).astype(I32)


def post_mixer(a_prompt, a_sample, wo_bf, x, mod, gffn, router_t, router_b, tri):
    t = x.shape[0]
    tm = mod.tile
    npt = mod.npt
    row = lambda i: (i, 0)
    col = lambda i: (0, i)
    const = lambda i: (0, 0)
    return pl.pallas_call(
        functools.partial(_post_mixer_kernel, mod.npt),
        out_shape=(jax.ShapeDtypeStruct((t, D_MODEL), F32), jax.ShapeDtypeStruct((t, D_MODEL // 2), U32),
                   jax.ShapeDtypeStruct((TOP_K, t), I32), jax.ShapeDtypeStruct((TOP_K, t), F32),
                   jax.ShapeDtypeStruct((TOP_K, t), I32), jax.ShapeDtypeStruct((N_EXPERTS, 128), F32)),
        grid=(t // tm,),
        in_specs=[pl.BlockSpec((tm, D_MODEL), lambda i: (jnp.minimum(i, npt - 1), 0)),
                  pl.BlockSpec((tm, D_MODEL), lambda i: (jnp.maximum(i - npt, 0), 0)),
                  pl.BlockSpec((D_MODEL, D_MODEL), const), pl.BlockSpec((tm, D_MODEL), row)]
        + mod.specs(2) + [pl.BlockSpec((1, D_MODEL), const)] + mod.specs(4) + mod.specs(3)
        + [pl.BlockSpec((N_EXPERTS, D_MODEL), const), pl.BlockSpec((N_EXPERTS, 1), const),
           pl.BlockSpec((tm, tm), const)],
        out_specs=(pl.BlockSpec((tm, D_MODEL), row), pl.BlockSpec((tm, D_MODEL // 2), row),
                   pl.BlockSpec((TOP_K, tm), col), pl.BlockSpec((TOP_K, tm), col), pl.BlockSpec((TOP_K, tm), col),
                   pl.BlockSpec((N_EXPERTS, 128), const)),
        scratch_shapes=[pltpu.VMEM((N_EXPERTS, 1), F32)],
        compiler_params=_params(("arbitrary",)),
        name="post_mixer",
    )(a_prompt, a_sample, wo_bf, x, *mod.operands(2), gffn.reshape(1, -1), *mod.operands(4), *mod.operands(3),
      router_t, router_b.reshape(-1, 1), tri)


def _row_copy(src, src_row, dst, dst_row, sem):
    return pltpu.make_async_copy(src.at[pl.ds(src_row, 1)], dst.at[pl.ds(dst_row, 1)], sem)


def _by_parity(i, fn):
    @pl.when(i % 2 == 0)
    def _():
        fn(0)

    @pl.when(i % 2 == 1)
    def _():
        fn(1)


def _dispatch_kernel(zb_ref, h_ref, dest_hbm, xs_hbm, idx_a, idx_b, zero_ref, sem_idx, sem_zero, sem_rows):
    i = pl.program_id(0)
    te = h_ref.shape[0]
    idx_bufs = (idx_a, idx_b)

    def idx_copy(tile, p):
        return pltpu.make_async_copy(dest_hbm.at[tile], idx_bufs[p], sem_idx.at[p])

    def zero_copy(e):
        return pltpu.make_async_copy(zero_ref, xs_hbm.at[pl.ds(zb_ref[e] * EXPERT_BLOCK, EXPERT_BLOCK)], sem_zero)

    @pl.when(i == 0)
    def _():
        idx_copy(0, 0).start()
        zero_ref[...] = jnp.zeros(zero_ref.shape, U32)

        def start(e, carry):
            @pl.when(zb_ref[e] >= 0)
            def _():
                zero_copy(e).start()
            return carry

        def wait(e, carry):
            @pl.when(zb_ref[e] >= 0)
            def _():
                zero_copy(e).wait()
            return carry

        lax.fori_loop(0, zb_ref.shape[0], start, 0)
        lax.fori_loop(0, zb_ref.shape[0], wait, 0)

    def step(p):
        @pl.when(i + 1 < pl.num_programs(0))
        def _():
            idx_copy(i + 1, 1 - p).start()

        idx_copy(i, p).wait()
        idx = idx_bufs[p]

        def issue(t, carry):
            for kk in range(TOP_K):
                _row_copy(h_ref, t, xs_hbm, idx[kk * te + t], sem_rows).start(priority=kk % 2)
            return carry

        def drain(t, carry):
            for kk in range(TOP_K):
                _row_copy(h_ref, t, xs_hbm, idx[kk * te + t], sem_rows).wait()
            return carry

        lax.fori_loop(0, te, issue, 0)
        lax.fori_loop(0, te, drain, 0)

    _by_parity(i, step)


def moe_dispatch(h, dest_tiles, zero_blocks, n_slots):
    t = h.shape[0]
    te = ROUTE_TILE
    return pl.pallas_call(
        _dispatch_kernel,
        out_shape=jax.ShapeDtypeStruct((n_slots, h.shape[1]), U32),
        grid_spec=pltpu.PrefetchScalarGridSpec(
            num_scalar_prefetch=1,
            grid=(t // te,),
            in_specs=[pl.BlockSpec((te, h.shape[1]), lambda i, zb: (i, 0)), pl.BlockSpec(memory_space=pl.ANY)],
            out_specs=pl.BlockSpec(memory_space=pl.ANY),
            scratch_shapes=[pltpu.SMEM((te * TOP_K,), I32), pltpu.SMEM((te * TOP_K,), I32),
                            pltpu.VMEM((EXPERT_BLOCK, h.shape[1]), U32),
                            pltpu.SemaphoreType.DMA((2,)), pltpu.SemaphoreType.DMA, pltpu.SemaphoreType.DMA],
        ),
        compiler_params=_params(("arbitrary",)),
        name="moe_dispatch",
    )(zero_blocks, h, dest_tiles)


def _expert_kernel(be_ref, nu_ref, xs_ref, wg0_ref, wu0_ref, wd0_ref, wg1_ref, wu1_ref, wd1_ref, ys_ref,
                   wg_bf, wu_bf, wd_bf):
    b = pl.program_id(0)
    rb = EXPERT_BLOCK
    used = 2 * b < nu_ref[0]
    for s, (wg, wu, wd) in enumerate(((wg0_ref, wu0_ref, wd0_ref), (wg1_ref, wu1_ref, wd1_ref))):
        j = 2 * b + s
        fresh = (b == 0) | (be_ref[j] != be_ref[jnp.maximum(j - 2, 0)])

        @pl.when(used & fresh)
        def _():
            wg_bf[s] = wg[...].astype(BF16)
            wu_bf[s] = wu[...].astype(BF16)
            wd_bf[s] = wd[...].astype(BF16)

    @pl.when(used)
    def _():
        for s in range(2):
            x = _unpack_bf16_pairs(xs_ref[rb * s:rb * (s + 1), :])
            hg = jnp.dot(x, wg_bf[s], preferred_element_type=F32)
            hu = jnp.dot(x, wu_bf[s], preferred_element_type=F32)
            ys_ref[rb * s:rb * (s + 1), :] = jnp.dot((_silu(hg) * hu).astype(BF16), wd_bf[s],
                                                     preferred_element_type=F32)


def moe_experts(xs, block_expert, n_used, wg, wu, wd, layer):
    n_slots = xs.shape[0]
    rb = EXPERT_BLOCK
    ff = wg.shape[3]
    rows = lambda b, be, nu: (jnp.minimum(b, (nu[0] - 1) // 2), 0)
    w_in = lambda s: (lambda b, be, nu: (layer, be[2 * b + s], 0, 0))
    return pl.pallas_call(
        _expert_kernel,
        out_shape=jax.ShapeDtypeStruct((n_slots, D_MODEL), F32),
        grid_spec=pltpu.PrefetchScalarGridSpec(
            num_scalar_prefetch=2,
            grid=(n_slots // (2 * rb),),
            in_specs=[pl.BlockSpec((2 * rb, xs.shape[1]), rows)]
            + [pl.BlockSpec((None, None, D_MODEL, ff), w_in(0)), pl.BlockSpec((None, None, D_MODEL, ff), w_in(0)),
               pl.BlockSpec((None, None, ff, D_MODEL), w_in(0)),
               pl.BlockSpec((None, None, D_MODEL, ff), w_in(1)), pl.BlockSpec((None, None, D_MODEL, ff), w_in(1)),
               pl.BlockSpec((None, None, ff, D_MODEL), w_in(1))],
            out_specs=pl.BlockSpec((2 * rb, D_MODEL), rows),
            scratch_shapes=[pltpu.VMEM((2, D_MODEL, ff), BF16), pltpu.VMEM((2, D_MODEL, ff), BF16),
                            pltpu.VMEM((2, ff, D_MODEL), BF16)],
        ),
        compiler_params=_params(("arbitrary",)),
        name="moe_experts",
    )(block_expert, n_used, xs, wg, wu, wd, wg, wu, wd)


def _combine_kernel(npt, final, h_ref, swg_ref, swu_ref, swd_ref, x1_ref, g2s_ref, g2t_ref, w_ref, fg_ref,
                    dest_hbm, ys_hbm, o_ref, idx_a, idx_b, ybuf_ref, sem_idx, sem_rows):
    i = pl.program_id(0)
    n = pl.num_programs(0)
    tg = h_ref.shape[0]
    idx_bufs = (idx_a, idx_b)

    def idx_copy(tile, p):
        return pltpu.make_async_copy(dest_hbm.at[tile], idx_bufs[p], sem_idx.at[p])

    def gather_rows(p, wait):
        idx = idx_bufs[p]

        def body(t, carry):
            for kk in range(TOP_K):
                src = 0 if wait else idx[kk * tg + t]
                cp = _row_copy(ys_hbm, src, ybuf_ref.at[p, kk], t, sem_rows.at[p])
                cp.wait() if wait else cp.start(priority=kk % 2)
            return carry

        lax.fori_loop(0, tg, body, 0)

    @pl.when(i == 0)
    def _():
        idx_copy(0, 0).start()
        idx_copy(0, 0).wait()
        gather_rows(0, wait=False)

        @pl.when(n > 1)
        def _():
            idx_copy(1, 1).start()

    def prefetch(p):
        @pl.when(i + 1 < n)
        def _():
            idx_copy(i + 1, 1 - p).wait()
            gather_rows(1 - p, wait=False)

        @pl.when(i + 2 < n)
        def _():
            idx_copy(i + 2, p).start()

    _by_parity(i, prefetch)

    hb = _unpack_bf16_pairs(h_ref[...])
    hid = _silu(jnp.dot(hb, swg_ref[...], preferred_element_type=F32)) * jnp.dot(hb, swu_ref[...],
                                                                                 preferred_element_type=F32)
    acc = jnp.dot(hid.astype(BF16), swd_ref[...], preferred_element_type=F32)
    w = w_ref[...]
    gate = _pick(i >= npt, g2s_ref, g2t_ref)

    def finish(p):
        gather_rows(p, wait=True)
        routed = ybuf_ref[p, 0] * w[:, 0:1]
        for kk in range(1, TOP_K):
            routed = routed + ybuf_ref[p, kk] * w[:, kk:kk + 1]
        x2 = x1_ref[...] + gate * (routed + acc)
        if final:
            ms = jnp.mean(x2 * x2, axis=-1, keepdims=True)
            x2 = x2 * lax.rsqrt(ms + NORM_EPS) * fg_ref[...]
        o_ref[...] = x2

    _by_parity(i, finish)


def moe_combine(h, swg_bf, swu_bf, swd_bf, x1, mod, w_tok, final_g, dest_tiles, ys, final):
    t = h.shape[0]
    tg = mod.tile
    row = lambda i: (i, 0)
    const = lambda i: (0, 0)
    return pl.pallas_call(
        functools.partial(_combine_kernel, mod.npt, final),
        out_shape=jax.ShapeDtypeStruct((t, D_MODEL), F32),
        grid=(t // tg,),
        in_specs=[pl.BlockSpec((tg, h.shape[1]), row), pl.BlockSpec(swg_bf.shape, const), pl.BlockSpec(swu_bf.shape, const),
                  pl.BlockSpec(swd_bf.shape, const), pl.BlockSpec((tg, D_MODEL), row)]
        + mod.specs(5)
        + [pl.BlockSpec((tg, TOP_K), row), pl.BlockSpec((1, D_MODEL), const),
           pl.BlockSpec(memory_space=pl.ANY), pl.BlockSpec(memory_space=pl.ANY)],
        out_specs=pl.BlockSpec((tg, D_MODEL), row),
        scratch_shapes=[pltpu.SMEM((tg * TOP_K,), I32), pltpu.SMEM((tg * TOP_K,), I32),
                        pltpu.VMEM((2, TOP_K, tg, D_MODEL), F32),
                        pltpu.SemaphoreType.DMA((2,)), pltpu.SemaphoreType.DMA((2,))],
        compiler_params=_params(("arbitrary",)),
        name="moe_combine",
    )(h, swg_bf, swu_bf, swd_bf, x1, *mod.operands(5), w_tok, final_g.reshape(1, -1), dest_tiles, ys)


def _slot_kernel(eidx_ref, rank_ref, start_ref, o_ref):
    tm = eidx_ref.shape[1]
    tr = o_ref.shape[2]
    eid = lax.broadcasted_iota(I32, (N_EXPERTS, tm), 0)
    start = start_ref[...]
    for kk in range(TOP_K):
        base = jnp.sum(jnp.where(eid == eidx_ref[kk:kk + 1, :], start, 0.0), axis=0, keepdims=True)
        slot = base.astype(I32) + rank_ref[kk:kk + 1, :]
        for j in range(tm // tr):
            o_ref[j, kk:kk + 1, :] = slot[:, tr * j:tr * (j + 1)]


def assignment_slots(eidx_t, rank_t, pad_start):
    t = eidx_t.shape[1]
    tm = TOKEN_TILE
    tr = ROUTE_TILE
    col = lambda i: (0, i)
    out = pl.pallas_call(
        _slot_kernel,
        out_shape=jax.ShapeDtypeStruct((t // tr, TOP_K, tr), I32),
        grid=(t // tm,),
        in_specs=[pl.BlockSpec((TOP_K, tm), col), pl.BlockSpec((TOP_K, tm), col),
                  pl.BlockSpec((N_EXPERTS, 1), lambda i: (0, 0))],
        out_specs=pl.BlockSpec((tm // tr, TOP_K, tr), lambda i: (i, 0, 0)),
        compiler_params=_params(("parallel",)),
        name="assignment_slots",
    )(eidx_t, rank_t, pad_start.astype(F32).reshape(-1, 1))
    return out.reshape(t // tr, TOP_K * tr)


def _routing_tables(counts, n_blocks):
    rb = EXPERT_BLOCK
    counts = counts.astype(I32)
    padded = (counts + rb - 1) // rb * rb
    pad_end = jnp.cumsum(padded)
    pad_start = pad_end - padded
    n_used = pad_end[-1] // rb
    blocks = jnp.arange(n_blocks, dtype=I32)
    block_expert = jnp.sum((pad_end[None, :] <= (blocks * rb)[:, None]).astype(I32), axis=1)
    last_used = jnp.sum((pad_end <= (n_used - 1) * rb).astype(I32))
    block_expert = jnp.minimum(jnp.where(blocks < n_used, block_expert, last_used), N_EXPERTS - 1)
    zero_blocks = jnp.where(counts % rb != 0, pad_end // rb - 1, -1)
    zero_blocks = jnp.concatenate([zero_blocks, jnp.where(n_used % 2 == 1, n_used, -1).reshape(1)]).astype(I32)
    return pad_start, block_expert, n_used.reshape(1).astype(I32), zero_blocks


def moe_layer(h, x1, eidx_t, w_t, rank_t, counts, mod_route, wg, wu, wd, layer, swg_bf, swu_bf, swd_bf,
              final_g, final):
    t = h.shape[0]
    rb = EXPERT_BLOCK
    tr = ROUTE_TILE
    n_blocks = -(-(t * TOP_K) // rb) + N_EXPERTS
    n_blocks += n_blocks % 2
    pad_start, block_expert, n_used, zero_blocks = _routing_tables(counts, n_blocks)
    dest_tiles = assignment_slots(eidx_t, rank_t, pad_start)
    xs = moe_dispatch(h, dest_tiles, zero_blocks, n_blocks * rb)
    ys = moe_experts(xs, block_expert, n_used, wg, wu, wd, layer)
    return moe_combine(h, swg_bf, swu_bf, swd_bf, x1, mod_route, w_t.T, final_g, dest_tiles, ys, final)


def _rope_tables(n_batch, seq_len, n_seq, n_new):
    half = SWA_HEAD_DIM // 2
    inv = ROPE_THETA ** (-jnp.arange(half, dtype=F32) / half)
    pos = jnp.concatenate([jnp.tile(jnp.arange(seq_len, dtype=F32), n_batch),
                           jnp.tile(PAST_LEN + jnp.arange(n_new, dtype=F32), n_seq)])
    ang = pos[:, None] * inv[None, :]
    cos = jnp.tile(jnp.cos(ang), (1, 128 // half))
    sin = jnp.sin(ang)
    sin = jnp.tile(jnp.concatenate([-sin, sin], axis=1), (1, 128 // SWA_HEAD_DIM))
    return cos, sin


def kernel(x_prompt, x_sample, c_prompt, c_sample, cache_swa_k, cache_swa_v, state_gla, norm_mix_g, norm_ffn_g,
           final_g, ada_w, ada_b, swa_wqkv, swa_sinks, swa_wo, gla_win, gla_wa1, gla_wa2, gla_ba, gla_norm_g,
           gla_wo, moe_router, moe_bias, moe_wg, moe_wu, moe_wd, shared_wg, shared_wu, shared_wd):
    n_batch, seq_len, d = x_prompt.shape
    n_seq, n_new, _ = x_sample.shape
    depth = ada_w.shape[0]
    tp = n_batch * seq_len
    ts = n_seq * n_new
    t = tp + ts
    tm = TOKEN_TILE
    tr = ROUTE_TILE

    x = jnp.concatenate([x_prompt.reshape(tp, d), x_sample.reshape(ts, d)], axis=0)
    c_all = jnp.concatenate([jnp.repeat(c_sample, n_new, axis=0), c_prompt], axis=0)
    mod = ada_modulation(c_all, ada_w, ada_b)
    cos_tab, sin_tab = _rope_tables(n_batch, seq_len, n_seq, n_new)
    tri = jnp.triu(jnp.ones((tm, tm), BF16), k=1)

    new_k, new_v, new_s = [], [], []
    new_k_s, new_v_s, new_s_s = [], [], []
    for layer in range(depth):
        mod_tok = _Mod(mod, layer, n_batch, seq_len, tm)
        mod_route = _Mod(mod, layer, n_batch, seq_len, tr)
        m = layer // 2
        if layer % 2 == 0:
            q, k, v, k_dup, v_dup = swa_qkv(x, norm_mix_g[layer], mod_tok, swa_wqkv[m].astype(BF16), cos_tab, sin_tab)
            a_p = swa_prompt_attention(q, k_dup, v_dup, swa_sinks[m], n_batch, seq_len)
            nk = SWA_KV_HEADS * SWA_HEAD_DIM
            a_s, ck, cv = swa_sample_attention(q, k, v, cache_swa_k[m].reshape(n_seq, WINDOW, nk),
                                             cache_swa_v[m].reshape(n_seq, WINDOW, nk), swa_sinks[m], tp, n_new)
            kv_shape = (n_batch, WINDOW, SWA_KV_HEADS, SWA_HEAD_DIM)
            tails = [slice((b + 1) * seq_len - WINDOW, (b + 1) * seq_len) for b in range(n_batch)]
            new_k.append(jnp.stack([k[rows] for rows in tails]).reshape(kv_shape))
            new_v.append(jnp.stack([v[rows] for rows in tails]).reshape(kv_shape))
            new_k_s.append(ck.reshape(n_seq, WINDOW, SWA_KV_HEADS, SWA_HEAD_DIM))
            new_v_s.append(cv.reshape(n_seq, WINDOW, SWA_KV_HEADS, SWA_HEAD_DIM))
            wo = swa_wo[m]
        else:
            pad = jnp.zeros((d, 128 - GLA_GATE_RANK), F32)
            win_ext = jnp.concatenate([gla_win[m], gla_wa1[m], pad], axis=1).astype(BF16)
            wa2_pad = jnp.concatenate([gla_wa2[m], jnp.zeros((128 - GLA_GATE_RANK, GLA_KEY_DIM), F32)],
                                      axis=0).astype(BF16)
            q, k, v, r, gate = gla_project(x, norm_mix_g[layer], mod_tok, win_ext, wa2_pad, gla_ba[m])
            a_p, s_prompt = gla_prompt(q, k, gate, v, r, gla_norm_g[m], n_batch, seq_len)
            a_s, s_sample = gla_sample(q, k, gate, v, r, gla_norm_g[m], state_gla[m], tp, n_new)
            new_s.append(s_prompt)
            new_s_s.append(s_sample)
            wo = gla_wo[m]
        x1, h, eidx_t, w_t, rank_t, cnt = post_mixer(a_p, a_s, wo.astype(BF16), x, mod_tok, norm_ffn_g[layer],
                                                     moe_router[layer].T, moe_bias[layer], tri)
        x = moe_layer(h, x1, eidx_t, w_t, rank_t, cnt[:, 0], mod_route,
                      moe_wg, moe_wu, moe_wd, layer,
                      shared_wg[layer].astype(BF16), shared_wu[layer].astype(BF16), shared_wd[layer].astype(BF16),
                      final_g, layer == depth - 1)

    y_prompt = x[:tp].reshape(n_batch, seq_len, d)
    y_sample = x[tp:].reshape(n_seq, n_new, d)
    return (y_prompt, y_sample, jnp.stack(new_k), jnp.stack(new_v), jnp.stack(new_k_s), jnp.stack(new_v_s),
            jnp.stack(new_s), jnp.stack(new_s_s))
```

```python
import functools

import jax
import jax.numpy as jnp
from jax import lax
from jax.experimental import pallas as pl
from jax.experimental.pallas import tpu as pltpu

F32 = jnp.float32
BF16 = jnp.bfloat16
I32 = jnp.int32
U32 = jnp.uint32

D_MODEL = 1024
PAST_LEN = 8192
SWA_HEAD_DIM = 64
SWA_HEADS = 16
SWA_KV_HEADS = 4
SWA_GROUP = 4
WINDOW = 128
ROPE_THETA = 10000.0
GLA_HEADS = 4
GLA_DK = 128
GLA_DV = 256
GLA_KEY_DIM = 512
GLA_VAL_DIM = 1024
GLA_GATE_RANK = 16
GLA_GATE_NORMALIZER = 16.0
GLA_CHUNK = 64
N_EXPERTS = 64
TOP_K = 8
N_GROUPS = 8
TOPK_GROUPS = 4
EXPERT_FF = 256
ROUTED_SCALE = 2.5
NORM_EPS = 1e-6

TOKEN_TILE = 512
ROUTE_TILE = 256
EXPERT_BLOCK = 256
ADA_TILE = 512
SAMPLE_SEQS = 8
GLA_PROMPT_BATCH = 2
VMEM_LIMIT = 48 * 1024 * 1024

NT_DIMS = (((1,), (1,)), ((), ()))
TN_DIMS = (((0,), (0,)), ((), ()))


def _params(semantics):
    return pltpu.CompilerParams(dimension_semantics=semantics, vmem_limit_bytes=VMEM_LIMIT)


def _silu(x):
    return x * jax.nn.sigmoid(x)


def _norm_mod(x, g, sc, sh):
    ms = jnp.mean(x * x, axis=-1, keepdims=True)
    return (x * lax.rsqrt(ms + NORM_EPS) * g) * (1.0 + sc) + sh


def _pack_bf16_pairs(x):
    half = x.shape[1] // 2
    xb = x.astype(BF16).astype(F32)
    lo = lax.bitcast_convert_type(xb[:, :half], U32) >> 16
    hi = lax.bitcast_convert_type(xb[:, half:], U32) & jnp.uint32(0xFFFF0000)
    return lo | hi


def _unpack_bf16_pairs(u):
    lo = lax.bitcast_convert_type(u << 16, F32)
    hi = lax.bitcast_convert_type(u & jnp.uint32(0xFFFF0000), F32)
    return jnp.concatenate([lo, hi], axis=1).astype(BF16)


def _split3(x):
    x1 = x.astype(BF16)
    r1 = x - x1.astype(F32)
    x2 = r1.astype(BF16)
    x3 = (r1 - x2.astype(F32)).astype(BF16)
    return x1, x2, x3


def _ada_kernel(c_ref, w_ref, b_ref, o_ref):
    s = _silu(c_ref[...]).astype(BF16)
    o_ref[...] = jnp.dot(s, w_ref[...].astype(BF16), preferred_element_type=F32) + b_ref[...]


def ada_modulation(c_all, ada_w, ada_b):
    depth, d, n = ada_w.shape
    rows = c_all.shape[0]
    return pl.pallas_call(
        _ada_kernel,
        out_shape=jax.ShapeDtypeStruct((depth, rows, n), F32),
        grid=(depth, n // ADA_TILE),
        in_specs=[
            pl.BlockSpec((rows, d), lambda l, j: (0, 0)),
            pl.BlockSpec((None, d, ADA_TILE), lambda l, j: (l, 0, j)),
            pl.BlockSpec((None, 1, ADA_TILE), lambda l, j: (l, 0, j)),
        ],
        out_specs=pl.BlockSpec((None, rows, ADA_TILE), lambda l, j: (l, 0, j)),
        compiler_params=_params(("parallel", "parallel")),
        name="ada_modulation",
    )(c_all, ada_w, ada_b.reshape(depth, 1, n))


class _Mod:
    def __init__(self, mod, layer, n_batch, seq_len, tile):
        depth, rows, n = mod.shape
        self.tile = tile
        self.layer = layer
        self.npt = n_batch * seq_len // tile
        self.mod_tok = mod
        self.mod_seq = mod[:, rows - n_batch:].reshape(depth, n_batch, 1, n)
        self.tiles_per_seq = seq_len // tile

    def operands(self, chunk):
        del chunk
        return [self.mod_seq, self.mod_tok]

    def specs(self, chunk):
        l, npt, tps = self.layer, self.npt, self.tiles_per_seq
        n_seq = self.mod_seq.shape[1]
        seq_spec = pl.BlockSpec((None, None, 1, D_MODEL),
                                lambda i, *_: (l, jnp.minimum(i // tps, n_seq - 1), 0, chunk))
        tok_spec = pl.BlockSpec((None, self.tile, D_MODEL),
                                lambda i, *_: (l, jnp.maximum(i - npt, 0), chunk))
        return [seq_spec, tok_spec]


def _pick(is_sample, seq_ref, tok_ref):
    return jnp.where(is_sample, tok_ref[...], seq_ref[...])


def _swa_qkv_kernel(npt, x_ref, g_ref, scs_ref, sct_ref, shs_ref, sht_ref, w_ref, cos_ref, sin_ref,
                    q_ref, k_ref, v_ref, kd_ref, vd_ref):
    is_s = pl.program_id(0) >= npt
    h = _norm_mod(x_ref[...], g_ref[...], _pick(is_s, scs_ref, sct_ref), _pick(is_s, shs_ref, sht_ref))
    qkv = jnp.dot(h.astype(BF16), w_ref[...], preferred_element_type=F32)
    cos = cos_ref[...]
    sin = sin_ref[...]
    lane = lax.broadcasted_iota(I32, cos.shape, 1)
    first_half = (lane % SWA_HEAD_DIM) < (SWA_HEAD_DIM // 2)

    def rope(xc):
        rot = jnp.where(first_half, pltpu.roll(xc, 128 - SWA_HEAD_DIM // 2, 1), pltpu.roll(xc, SWA_HEAD_DIM // 2, 1))
        return xc * cos + rot * sin

    nq = SWA_HEADS * SWA_HEAD_DIM
    nk = SWA_KV_HEADS * SWA_HEAD_DIM
    for c in range(nq // 128):
        q_ref[:, 128 * c:128 * (c + 1)] = (rope(qkv[:, 128 * c:128 * (c + 1)]) * (SWA_HEAD_DIM ** -0.5)).astype(BF16)
    low = lane < SWA_HEAD_DIM

    def spread(chunk):
        rolled = pltpu.roll(chunk, SWA_HEAD_DIM, 1)
        return jnp.where(low, chunk, rolled).astype(BF16), jnp.where(low, rolled, chunk).astype(BF16)

    for c in range(nk // 128):
        kc = rope(qkv[:, nq + 128 * c:nq + 128 * (c + 1)])
        vc = qkv[:, nq + nk + 128 * c:nq + nk + 128 * (c + 1)]
        k_ref[:, 128 * c:128 * (c + 1)] = kc
        v_ref[:, 128 * c:128 * (c + 1)] = vc
        kd_ref[:, 256 * c:256 * c + 128], kd_ref[:, 256 * c + 128:256 * (c + 1)] = spread(kc)
        vd_ref[:, 256 * c:256 * c + 128], vd_ref[:, 256 * c + 128:256 * (c + 1)] = spread(vc)


def swa_qkv(x, g, mod, w_bf, cos_tab, sin_tab):
    t = x.shape[0]
    tm = mod.tile
    nq = SWA_HEADS * SWA_HEAD_DIM
    nk = SWA_KV_HEADS * SWA_HEAD_DIM
    row = lambda i: (i, 0)
    return pl.pallas_call(
        functools.partial(_swa_qkv_kernel, mod.npt),
        out_shape=(jax.ShapeDtypeStruct((t, nq), BF16), jax.ShapeDtypeStruct((t, nk), F32),
                   jax.ShapeDtypeStruct((t, nk), F32), jax.ShapeDtypeStruct((t, 2 * nk), BF16),
                   jax.ShapeDtypeStruct((t, 2 * nk), BF16)),
        grid=(t // tm,),
        in_specs=[pl.BlockSpec((tm, D_MODEL), row), pl.BlockSpec((1, D_MODEL), lambda i: (0, 0))]
        + mod.specs(1) + mod.specs(0)
        + [pl.BlockSpec(w_bf.shape, lambda i: (0, 0)), pl.BlockSpec((tm, 128), row), pl.BlockSpec((tm, 128), row)],
        out_specs=(pl.BlockSpec((tm, nq), row), pl.BlockSpec((tm, nk), row), pl.BlockSpec((tm, nk), row),
                   pl.BlockSpec((tm, 2 * nk), row), pl.BlockSpec((tm, 2 * nk), row)),
        compiler_params=_params(("parallel",)),
        name="swa_qkv",
    )(x, g.reshape(1, -1), *mod.operands(1), *mod.operands(0), w_bf, cos_tab, sin_tab)


def _sink_softmax(s, sink_col):
    m = jnp.maximum(jnp.max(s, axis=-1, keepdims=True), sink_col)
    e = jnp.exp(s - m)
    den = jnp.sum(e, axis=-1, keepdims=True) + jnp.exp(sink_col - m)
    return e * (1.0 / den)


def _pair_attention(sink_ref, q_ref, kblks, vblks, mask, o_ref):
    rows = q_ref.shape[0]
    half = mask.shape[1]
    per_group = SWA_GROUP // 2
    scores = []
    for g in range(SWA_KV_HEADS):
        q2 = jnp.concatenate([q_ref[:, 128 * c:128 * (c + 1)] for c in range(per_group * g, per_group * (g + 1))],
                             axis=0)
        s = lax.dot_general(q2, kblks[g], NT_DIMS, preferred_element_type=F32)
        for j in range(per_group):
            scores += [s[rows * j:rows * (j + 1), :half], s[rows * j:rows * (j + 1), half:]]
    s_all = jnp.where(mask, jnp.stack(scores), -jnp.inf)
    sinks = jnp.stack([jnp.full((1, 1), sink_ref[h], F32) for h in range(SWA_HEADS)])
    p_all = _sink_softmax(s_all, sinks).astype(BF16)
    for g in range(SWA_KV_HEADS):
        chunks = range(per_group * g, per_group * (g + 1))
        p2 = jnp.concatenate([jnp.concatenate([p_all[2 * c], p_all[2 * c + 1]], axis=1) for c in chunks], axis=0)
        o = jnp.dot(p2, vblks[g], preferred_element_type=F32).astype(BF16)
        for j, c in enumerate(chunks):
            o_ref[:, 128 * c:128 * (c + 1)] = o[rows * j:rows * (j + 1)]


def _swa_prompt_kernel(sink_ref, q_ref, kc_ref, kp_ref, vc_ref, vp_ref, o_ref):
    j = pl.program_id(1)
    blk = q_ref.shape[0]
    qi = lax.broadcasted_iota(I32, (blk, 2 * blk), 0)
    sj = lax.broadcasted_iota(I32, (blk, 2 * blk), 1)
    rel = qi + blk - sj
    mask = (rel >= 0) & (rel <= WINDOW) & ((sj >= blk) | (j > 0))
    low = lax.broadcasted_iota(I32, (2 * blk, 128), 1) < SWA_HEAD_DIM
    zero = jnp.zeros((2 * blk, 128), BF16)
    kblks, vblks = [], []
    for g in range(SWA_KV_HEADS):
        cs = slice(128 * g, 128 * (g + 1))
        kcat = jnp.concatenate([kp_ref[:, cs], kc_ref[:, cs]], axis=0)
        vcat = jnp.concatenate([vp_ref[:, cs], vc_ref[:, cs]], axis=0)
        kblks.append(jnp.concatenate([jnp.where(low, kcat, zero), jnp.where(low, zero, kcat)], axis=0))
        vblks.append(jnp.concatenate([jnp.where(low, vcat, zero), jnp.where(low, zero, vcat)], axis=0))
    _pair_attention(sink_ref, q_ref, kblks, vblks, mask, o_ref)


def swa_prompt_attention(q, k, v, sinks, n_batch, seq_len):
    blk = WINDOW
    nb = seq_len // blk
    nq = q.shape[1]
    nk = k.shape[1]
    cur = lambda b, j: (b * nb + j, 0)
    prev = lambda b, j: (b * nb + jnp.maximum(j - 1, 0), 0)
    return pl.pallas_call(
        _swa_prompt_kernel,
        out_shape=jax.ShapeDtypeStruct((n_batch * seq_len, nq), BF16),
        grid=(n_batch, nb),
        in_specs=[pl.BlockSpec(memory_space=pltpu.SMEM),
                  pl.BlockSpec((blk, nq), cur),
                  pl.BlockSpec((blk, nk), cur), pl.BlockSpec((blk, nk), prev),
                  pl.BlockSpec((blk, nk), cur), pl.BlockSpec((blk, nk), prev)],
        out_specs=pl.BlockSpec((blk, nq), cur),
        compiler_params=_params(("parallel", "parallel")),
        name="swa_prompt_attention",
    )(sinks, q, k, k, v, v)


def _swa_sample_kernel(n_new, sink_ref, q_ref, kn_ref, vn_ref, ck_ref, cv_ref, o_ref, nk_ref, nv_ref):
    n_sb, win, _ = ck_ref.shape
    per_seq = win + n_new
    rows = n_sb * n_new
    cols = n_sb * per_seq
    keys, vals = [], []
    for sb in range(n_sb):
        r0 = sb * n_new
        kc = ck_ref[sb]
        vc = cv_ref[sb]
        kn = kn_ref[r0:r0 + n_new, :]
        vn = vn_ref[r0:r0 + n_new, :]
        nk_ref[sb, 0:win - n_new, :] = kc[n_new:]
        nk_ref[sb, win - n_new:win, :] = kn
        nv_ref[sb, 0:win - n_new, :] = vc[n_new:]
        nv_ref[sb, win - n_new:win, :] = vn
        keys += [kc, kn]
        vals += [vc, vn]
    keys = jnp.concatenate(keys, axis=0)
    vals = jnp.concatenate(vals, axis=0)
    ri = lax.broadcasted_iota(I32, (rows, cols), 0)
    ci = lax.broadcasted_iota(I32, (rows, cols), 1)
    ti = ri % n_new
    si = ci % per_seq
    mask = (ri // n_new == ci // per_seq) & (si >= ti) & (si <= ti + WINDOW)
    low = lax.broadcasted_iota(I32, (cols, 128), 1) < SWA_HEAD_DIM
    zero = jnp.zeros((cols, 128), BF16)

    def block_diag(chunk, first):
        rolled = pltpu.roll(chunk, SWA_HEAD_DIM, 1)
        both = (jnp.where(low, chunk, rolled) if first else jnp.where(low, rolled, chunk)).astype(BF16)
        return jnp.concatenate([jnp.where(low, both, zero), jnp.where(low, zero, both)], axis=0)

    chunks = [slice(128 * (g // 2), 128 * (g // 2 + 1)) for g in range(SWA_KV_HEADS)]
    kblks = [block_diag(keys[:, cs], g % 2 == 0) for g, cs in enumerate(chunks)]
    vblks = [block_diag(vals[:, cs], g % 2 == 0) for g, cs in enumerate(chunks)]
    _pair_attention(sink_ref, q_ref, kblks, vblks, mask, o_ref)


def swa_sample_attention(q, k, v, cache_k, cache_v, sinks, n_prompt_rows, n_new):
    n_seq, win, nk = cache_k.shape
    sb = SAMPLE_SEQS
    rows = sb * n_new
    nq = q.shape[1]
    base = n_prompt_rows // rows
    tok = lambda i: (base + i, 0)
    seq = lambda i: (i, 0, 0)
    return pl.pallas_call(
        functools.partial(_swa_sample_kernel, n_new),
        out_shape=(jax.ShapeDtypeStruct((n_seq * n_new, nq), BF16),
                   jax.ShapeDtypeStruct(cache_k.shape, F32), jax.ShapeDtypeStruct(cache_v.shape, F32)),
        grid=(n_seq // sb,),
        in_specs=[pl.BlockSpec(memory_space=pltpu.SMEM),
                  pl.BlockSpec((rows, nq), tok), pl.BlockSpec((rows, nk), tok), pl.BlockSpec((rows, nk), tok),
                  pl.BlockSpec((sb, win, nk), seq), pl.BlockSpec((sb, win, nk), seq)],
        out_specs=(pl.BlockSpec((rows, nq), lambda i: (i, 0)), pl.BlockSpec((sb, win, nk), seq),
                   pl.BlockSpec((sb, win, nk), seq)),
        compiler_params=_params(("parallel",)),
        name="swa_sample_attention",
    )(sinks, q, k, v, cache_k, cache_v)


def _gla_proj_kernel(npt, x_ref, g_ref, scs_ref, sct_ref, shs_ref, sht_ref, w_ref, wa2_ref, ba_ref,
                     q_ref, k_ref, v_ref, r_ref, gate_ref):
    is_s = pl.program_id(0) >= npt
    h = _norm_mod(x_ref[...], g_ref[...], _pick(is_s, scs_ref, sct_ref), _pick(is_s, shs_ref, sht_ref))
    proj = jnp.dot(h.astype(BF16), w_ref[...], preferred_element_type=F32)
    kd = GLA_KEY_DIM
    vd = GLA_VAL_DIM
    q_ref[...] = proj[:, :kd] * (GLA_DK ** -0.5)
    k_ref[...] = proj[:, kd:2 * kd]
    v_ref[...] = proj[:, 2 * kd:2 * kd + vd].astype(BF16)
    r_ref[...] = proj[:, 2 * kd + vd:2 * kd + 2 * vd]
    low = proj[:, 2 * kd + 2 * vd:].astype(BF16)
    z = jnp.dot(low, wa2_ref[...], preferred_element_type=F32) + ba_ref[...]
    log_sig = jnp.minimum(z, 0.0) - jnp.log1p(jnp.exp(-jnp.abs(z)))
    gate_ref[...] = log_sig / GLA_GATE_NORMALIZER


def gla_project(x, g, mod, win_ext, wa2_pad, ba):
    t = x.shape[0]
    tm = mod.tile
    kd, vd = GLA_KEY_DIM, GLA_VAL_DIM
    row = lambda i: (i, 0)
    const = lambda i: (0, 0)
    return pl.pallas_call(
        functools.partial(_gla_proj_kernel, mod.npt),
        out_shape=(jax.ShapeDtypeStruct((t, kd), F32), jax.ShapeDtypeStruct((t, kd), F32),
                   jax.ShapeDtypeStruct((t, vd), BF16), jax.ShapeDtypeStruct((t, vd), F32),
                   jax.ShapeDtypeStruct((t, kd), F32)),
        grid=(t // tm,),
        in_specs=[pl.BlockSpec((tm, D_MODEL), row), pl.BlockSpec((1, D_MODEL), const)]
        + mod.specs(1) + mod.specs(0)
        + [pl.BlockSpec(win_ext.shape, const), pl.BlockSpec(wa2_pad.shape, const), pl.BlockSpec((1, kd), const)],
        out_specs=(pl.BlockSpec((tm, kd), row), pl.BlockSpec((tm, kd), row), pl.BlockSpec((tm, vd), row),
                   pl.BlockSpec((tm, vd), row), pl.BlockSpec((tm, kd), row)),
        compiler_params=_params(("parallel",)),
        name="gla_project",
    )(x, g.reshape(1, -1), *mod.operands(1), *mod.operands(0), win_ext, wa2_pad, ba.reshape(1, -1))


def _cumsum_rows(tri, g):
    n = g.shape[1]
    s = jnp.dot(tri, jnp.concatenate(_split3(g), axis=1), preferred_element_type=F32)
    return s[:, :n] + s[:, n:2 * n] + s[:, 2 * n:]


def _diag_attention(q, k, b, n):
    ng = n // 8
    dk = q.shape[1]
    q3 = q.reshape(ng, 8, dk)
    k3 = k.reshape(ng, 8, dk)
    b3 = b.reshape(ng, 8, dk)
    sub = lax.broadcasted_iota(I32, (ng, 8, dk), 1)
    ti = lax.broadcasted_iota(I32, (n, n), 0)
    si = lax.broadcasted_iota(I32, (n, n), 1)
    attn = jnp.zeros((n, n), F32)
    for j in range(8):
        bj = jnp.broadcast_to(b3[:, j:j + 1, :], b3.shape)
        kj = jnp.broadcast_to(k3[:, j:j + 1, :], k3.shape)
        e = jnp.exp(jnp.minimum(b3 - bj, 0.0))
        m = jnp.where(sub >= j, q3 * e * kj, 0.0)
        col = jnp.sum(m, axis=-1, keepdims=True).reshape(n, 1)
        attn = attn + jnp.where(si == (ti // 8) * 8 + j, col, 0.0)
    return attn


def _cross_attention(q, k, b, n):
    ti = lax.broadcasted_iota(I32, (n, n), 0)
    si = lax.broadcasted_iota(I32, (n, n), 1)
    row = lax.broadcasted_iota(I32, b.shape, 0)
    attn = jnp.zeros((n, n), F32)
    m = n // 2
    while m >= 8:
        nblk = n // m
        refq = jnp.concatenate(
            [jnp.broadcast_to(b[i * m - 1:i * m], (m, b.shape[1])) if i % 2 else b[i * m:(i + 1) * m]
             for i in range(nblk)], axis=0)
        refk = jnp.concatenate(
            [b[i * m:(i + 1) * m] if i % 2 else jnp.broadcast_to(b[(i + 1) * m - 1:(i + 1) * m], (m, b.shape[1]))
             for i in range(nblk)], axis=0)
        odd = ((row // m) % 2) == 1
        qt = jnp.where(odd, q * jnp.exp(jnp.minimum(b - refq, 0.0)), 0.0).astype(BF16)
        kt = jnp.where(odd, 0.0, k * jnp.exp(jnp.minimum(refk - b, 0.0))).astype(BF16)
        a = lax.dot_general(qt, kt, NT_DIMS, preferred_element_type=F32)
        keep = (((ti // m) % 2) == 1) & ((si // m) == (ti // m) - 1)
        attn = attn + jnp.where(keep, a, 0.0)
        m //= 2
    return attn


def _gla_epilogue(o, r, ng):
    ms = jnp.mean(o * o, axis=-1, keepdims=True)
    return (o * lax.rsqrt(ms + NORM_EPS) * ng * _silu(r)).astype(BF16)


def _gla_prompt_kernel(*refs):
    nb = GLA_PROMPT_BATCH
    ins, (ng_ref, o_ref, so_ref, st_ref) = refs[:5 * nb], refs[5 * nb:]
    c = pl.program_id(1)
    n = ins[0].shape[0]

    @pl.when(c == 0)
    def _():
        st_ref[...] = jnp.zeros(st_ref.shape, F32)

    ti = lax.broadcasted_iota(I32, (n, n), 0)
    si = lax.broadcasted_iota(I32, (n, n), 1)
    tri = jnp.where(ti >= si, 1.0, 0.0).astype(BF16)
    for i in range(nb):
        q_ref, k_ref, g_ref, v_ref, r_ref = ins[5 * i:5 * (i + 1)]
        b_all = _cumsum_rows(tri, g_ref[...])
        for h in range(GLA_HEADS):
            ks = slice(GLA_DK * h, GLA_DK * (h + 1))
            vs = slice(GLA_DV * h, GLA_DV * (h + 1))
            q = q_ref[:, ks]
            k = k_ref[:, ks]
            v = v_ref[:, vs]
            b = b_all[:, ks]
            s_t = st_ref[i, h]
            o = lax.dot_general((q * jnp.exp(b)).astype(BF16), s_t.astype(BF16), NT_DIMS,
                                preferred_element_type=F32)
            attn = _cross_attention(q, k, b, n) + _diag_attention(q, k, b, n)
            o = o + jnp.dot(attn.astype(BF16), v, preferred_element_type=F32)
            bl = b[n - 1:n, :]
            kd = (k * jnp.exp(bl - b)).astype(BF16)
            s_new = s_t * jnp.exp(bl) + lax.dot_general(v, kd, TN_DIMS, preferred_element_type=F32)
            st_ref[i, h] = s_new
            o_ref[i, :, vs] = _gla_epilogue(o, r_ref[:, vs], ng_ref[...])

    @pl.when(c == pl.num_programs(1) - 1)
    def _():
        for i in range(nb):
            for h in range(GLA_HEADS):
                so_ref[i, h] = st_ref[i, h].T


def gla_prompt(q, k, g, v, r, norm_g, n_batch, seq_len):
    n = GLA_CHUNK
    nb = GLA_PROMPT_BATCH
    nc = seq_len // n
    kd, vd = GLA_KEY_DIM, GLA_VAL_DIM
    in_specs, operands = [], []
    for i in range(nb):
        row = lambda b, c, i=i: ((nb * b + i) * nc + c, 0)
        in_specs += [pl.BlockSpec((n, kd), row), pl.BlockSpec((n, kd), row), pl.BlockSpec((n, kd), row),
                     pl.BlockSpec((n, vd), row), pl.BlockSpec((n, vd), row)]
        operands += [q, k, g, v, r]
    o, state = pl.pallas_call(
        _gla_prompt_kernel,
        out_shape=(jax.ShapeDtypeStruct((n_batch, seq_len, vd), BF16),
                   jax.ShapeDtypeStruct((n_batch, GLA_HEADS, GLA_DK, GLA_DV), F32)),
        grid=(n_batch // nb, nc),
        in_specs=in_specs + [pl.BlockSpec((1, GLA_DV), lambda b, c: (0, 0))],
        out_specs=(pl.BlockSpec((nb, n, vd), lambda b, c: (b, c, 0)),
                   pl.BlockSpec((nb, GLA_HEADS, GLA_DK, GLA_DV), lambda b, c: (b, 0, 0, 0))),
        scratch_shapes=[pltpu.VMEM((nb, GLA_HEADS, GLA_DV, GLA_DK), F32)],
        compiler_params=_params(("parallel", "arbitrary")),
        name="gla_prompt",
    )(*operands, norm_g.reshape(1, -1))
    return o.reshape(n_batch * seq_len, vd), state


def _gla_sample_kernel(n_new, q_ref, k_ref, g_ref, v_ref, r_ref, ng_ref, si_ref, o_ref, so_ref):
    n = q_ref.shape[0]
    ti = lax.broadcasted_iota(I32, (n, n), 0)
    si = lax.broadcasted_iota(I32, (n, n), 1)
    tri = jnp.where((ti >= si) & (ti // n_new == si // n_new), 1.0, 0.0).astype(BF16)
    b_all = _cumsum_rows(tri, g_ref[...])
    for h in range(GLA_HEADS):
        ks = slice(GLA_DK * h, GLA_DK * (h + 1))
        vs = slice(GLA_DV * h, GLA_DV * (h + 1))
        q = q_ref[:, ks]
        k = k_ref[:, ks]
        v = v_ref[:, vs]
        b = b_all[:, ks]
        attn = _diag_attention(q, k, b, n)
        o_intra = jnp.dot(attn.astype(BF16), v, preferred_element_type=F32)
        qe = (q * jnp.exp(b)).astype(BF16)
        n_sb = n // n_new
        last = [b[n_new * (sb + 1) - 1:n_new * (sb + 1), :] for sb in range(n_sb)]
        bl_rows = jnp.concatenate([jnp.broadcast_to(bl, (n_new, GLA_DK)) for bl in last], axis=0)
        kd = (k * jnp.exp(bl_rows - b)).astype(BF16)
        seq_of_row = lax.broadcasted_iota(I32, (n, GLA_DV), 0) // n_new
        o = o_intra
        for sb in range(n_sb):
            mine = seq_of_row == sb
            s_t = si_ref[sb, h].T
            o_sb = lax.dot_general(qe, s_t.astype(BF16), NT_DIMS, preferred_element_type=F32)
            o = o + jnp.where(mine, o_sb, 0.0)
            v_sb = jnp.where(mine, v, jnp.zeros_like(v))
            upd = lax.dot_general(v_sb, kd, TN_DIMS, preferred_element_type=F32)
            so_ref[sb, h] = (s_t * jnp.exp(last[sb]) + upd).T
        o_ref[:, vs] = _gla_epilogue(o, r_ref[:, vs], ng_ref[...])


def gla_sample(q, k, g, v, r, norm_g, state, n_prompt_rows, n_new):
    n_seq = state.shape[0]
    sb = SAMPLE_SEQS
    rows = sb * n_new
    kd, vd = GLA_KEY_DIM, GLA_VAL_DIM
    base = n_prompt_rows // rows
    tok = lambda i: (base + i, 0)
    seq = lambda i: (i, 0, 0, 0)
    sblock = (sb, GLA_HEADS, GLA_DK, GLA_DV)
    return pl.pallas_call(
        functools.partial(_gla_sample_kernel, n_new),
        out_shape=(jax.ShapeDtypeStruct((n_seq * n_new, vd), BF16), jax.ShapeDtypeStruct(state.shape, F32)),
        grid=(n_seq // sb,),
        in_specs=[pl.BlockSpec((rows, kd), tok), pl.BlockSpec((rows, kd), tok), pl.BlockSpec((rows, kd), tok),
                  pl.BlockSpec((rows, vd), tok), pl.BlockSpec((rows, vd), tok),
                  pl.BlockSpec((1, GLA_DV), lambda i: (0, 0)),
                  pl.BlockSpec(sblock, seq)],
        out_specs=(pl.BlockSpec((rows, vd), lambda i: (i, 0)), pl.BlockSpec(sblock, seq)),
        compiler_params=_params(("parallel",)),
        name="gla_sample",
    )(q, k, g, v, r, norm_g.reshape(1, -1), state)


def _post_mixer_kernel(npt, ap_ref, as_ref, wo_ref, x_ref, g1s_ref, g1t_ref, gf_ref, scs_ref, sct_ref, shs_ref, sht_ref,
                       rw_ref, rb_ref, tri_ref,
                       x1_ref, h_ref, eidx_ref, w_ref, rank_ref, cnt_ref, carry_ref):
    i = pl.program_id(0)
    is_s = i >= npt

    @pl.when(i == 0)
    def _():
        carry_ref[...] = jnp.zeros(carry_ref.shape, F32)

    a = jnp.where(is_s, as_ref[...], ap_ref[...])
    x1 = x_ref[...] + _pick(is_s, g1s_ref, g1t_ref) * jnp.dot(a, wo_ref[...], preferred_element_type=F32)
    x1_ref[...] = x1
    h = _norm_mod(x1, gf_ref[...], _pick(is_s, scs_ref, sct_ref), _pick(is_s, shs_ref, sht_ref))
    h_ref[...] = _pack_bf16_pairs(h)

    h1, h2, _ = _split3(h)
    r1, r2, _ = _split3(rw_ref[...])
    logits = (lax.dot_general(r1, h1, NT_DIMS, preferred_element_type=F32)
              + lax.dot_general(r1, h2, NT_DIMS, preferred_element_type=F32)
              + lax.dot_general(r2, h1, NT_DIMS, preferred_element_type=F32))
    scores = jax.nn.sigmoid(logits)
    sel = scores + rb_ref[...]
    tm = sel.shape[1]
    gsz = N_EXPERTS // N_GROUPS

    sub = lax.broadcasted_iota(I32, (gsz, tm), 0)
    blocks, gscore = [], []
    for g in range(N_GROUPS):
        blk = sel[gsz * g:gsz * (g + 1)]
        m1 = jnp.max(blk, axis=0, keepdims=True)
        first = jnp.min(jnp.where(blk == m1, sub, gsz), axis=0, keepdims=True)
        m2 = jnp.max(jnp.where(sub == first, -jnp.inf, blk), axis=0, keepdims=True)
        blocks.append(blk)
        gscore.append(m1 + m2)
    masked = []
    for g in range(N_GROUPS):
        beaten = jnp.zeros((1, tm), I32)
        for o in range(N_GROUPS):
            if o == g:
                continue
            wins = (gscore[o] > gscore[g]) | ((gscore[o] == gscore[g]) & (o < g))
            beaten = beaten + wins.astype(I32)
        masked.append(jnp.where(beaten < TOPK_GROUPS, blocks[g], -jnp.inf))
    cur = jnp.concatenate(masked, axis=0)

    eid = lax.broadcasted_iota(I32, (N_EXPERTS, tm), 0)
    picked, weights = [], []
    onehot = jnp.zeros((N_EXPERTS, tm), F32)
    for _ in range(TOP_K):
        m = jnp.max(cur, axis=0, keepdims=True)
        idx = jnp.min(jnp.where(cur == m, eid, N_EXPERTS), axis=0, keepdims=True)
        hit = eid == idx
        picked.append(idx)
        weights.append(jnp.sum(jnp.where(hit, scores, 0.0), axis=0, keepdims=True))
        onehot = jnp.where(hit, 1.0, onehot)
        cur = jnp.where(hit, -jnp.inf, cur)
    wsum = weights[0]
    for wk in weights[1:]:
        wsum = wsum + wk
    scale = ROUTED_SCALE / wsum

    before = jnp.dot(onehot.astype(BF16), tri_ref[...], preferred_element_type=F32) + carry_ref[...]
    carry = carry_ref[...] + jnp.sum(onehot, axis=1, keepdims=True)
    carry_ref[...] = carry
    cnt_ref[...] = jnp.broadcast_to(carry, cnt_ref.shape)
    for kk in range(TOP_K):
        eidx_ref[kk:kk + 1, :] = picked[kk]
        w_ref[kk:kk + 1, :] = weights[kk] * scale
        rank_ref[kk:kk + 1, :] = jnp.sum(jnp.where(eid == picked[kk], before, 0.0), axis=0, keepdims=True).astype(I32)


def post_mixer(a_prompt, a_sample, wo_bf, x, mod, gffn, router_t, router_b, tri):
    t = x.shape[0]
    tm = mod.tile
    npt = mod.npt
    row = lambda i: (i, 0)
    col = lambda i: (0, i)
    const = lambda i: (0, 0)
    return pl.pallas_call(
        functools.partial(_post_mixer_kernel, mod.npt),
        out_shape=(jax.ShapeDtypeStruct((t, D_MODEL), F32), jax.ShapeDtypeStruct((t, D_MODEL // 2), U32),
                   jax.ShapeDtypeStruct((TOP_K, t), I32), jax.ShapeDtypeStruct((TOP_K, t), F32),
                   jax.ShapeDtypeStruct((TOP_K, t), I32), jax.ShapeDtypeStruct((N_EXPERTS, 128), F32)),
        grid=(t // tm,),
        in_specs=[pl.BlockSpec((tm, D_MODEL), lambda i: (jnp.minimum(i, npt - 1), 0)),
                  pl.BlockSpec((tm, D_MODEL), lambda i: (jnp.maximum(i - npt, 0), 0)),
                  pl.BlockSpec((D_MODEL, D_MODEL), const), pl.BlockSpec((tm, D_MODEL), row)]
        + mod.specs(2) + [pl.BlockSpec((1, D_MODEL), const)] + mod.specs(4) + mod.specs(3)
        + [pl.BlockSpec((N_EXPERTS, D_MODEL), const), pl.BlockSpec((N_EXPERTS, 1), const),
           pl.BlockSpec((tm, tm), const)],
        out_specs=(pl.BlockSpec((tm, D_MODEL), row), pl.BlockSpec((tm, D_MODEL // 2), row),
                   pl.BlockSpec((TOP_K, tm), col), pl.BlockSpec((TOP_K, tm), col), pl.BlockSpec((TOP_K, tm), col),
                   pl.BlockSpec((N_EXPERTS, 128), const)),
        scratch_shapes=[pltpu.VMEM((N_EXPERTS, 1), F32)],
        compiler_params=_params(("arbitrary",)),
        name="post_mixer",
    )(a_prompt, a_sample, wo_bf, x, *mod.operands(2), gffn.reshape(1, -1), *mod.operands(4), *mod.operands(3),
      router_t, router_b.reshape(-1, 1), tri)


def _row_copy(src, src_row, dst, dst_row, sem):
    return pltpu.make_async_copy(src.at[pl.ds(src_row, 1)], dst.at[pl.ds(dst_row, 1)], sem)


def _by_parity(i, fn):
    @pl.when(i % 2 == 0)
    def _():
        fn(0)

    @pl.when(i % 2 == 1)
    def _():
        fn(1)


def _dispatch_kernel(zb_ref, h_ref, dest_hbm, xs_hbm, idx_a, idx_b, zero_ref, sem_idx, sem_zero, sem_rows):
    i = pl.program_id(0)
    te = h_ref.shape[0]
    idx_bufs = (idx_a, idx_b)

    def idx_copy(tile, p):
        return pltpu.make_async_copy(dest_hbm.at[tile], idx_bufs[p], sem_idx.at[p])

    def zero_copy(e):
        return pltpu.make_async_copy(zero_ref, xs_hbm.at[pl.ds(zb_ref[e] * EXPERT_BLOCK, EXPERT_BLOCK)], sem_zero)

    @pl.when(i == 0)
    def _():
        idx_copy(0, 0).start()
        zero_ref[...] = jnp.zeros(zero_ref.shape, U32)

        def start(e, carry):
            @pl.when(zb_ref[e] >= 0)
            def _():
                zero_copy(e).start()
            return carry

        def wait(e, carry):
            @pl.when(zb_ref[e] >= 0)
            def _():
                zero_copy(e).wait()
            return carry

        lax.fori_loop(0, zb_ref.shape[0], start, 0)
        lax.fori_loop(0, zb_ref.shape[0], wait, 0)

    def step(p):
        @pl.when(i + 1 < pl.num_programs(0))
        def _():
            idx_copy(i + 1, 1 - p).start()

        idx_copy(i, p).wait()
        idx = idx_bufs[p]

        def issue(t, carry):
            for kk in range(TOP_K):
                _row_copy(h_ref, t, xs_hbm, idx[kk * te + t], sem_rows).start(priority=kk % 2)
            return carry

        def drain(t, carry):
            for kk in range(TOP_K):
                _row_copy(h_ref, t, xs_hbm, idx[kk * te + t], sem_rows).wait()
            return carry

        lax.fori_loop(0, te, issue, 0)
        lax.fori_loop(0, te, drain, 0)

    _by_parity(i, step)


def moe_dispatch(h, dest_tiles, zero_blocks, n_slots):
    t = h.shape[0]
    te = ROUTE_TILE
    return pl.pallas_call(
        _dispatch_kernel,
        out_shape=jax.ShapeDtypeStruct((n_slots, h.shape[1]), U32),
        grid_spec=pltpu.PrefetchScalarGridSpec(
            num_scalar_prefetch=1,
            grid=(t // te,),
            in_specs=[pl.BlockSpec((te, h.shape[1]), lambda i, zb: (i, 0)), pl.BlockSpec(memory_space=pl.ANY)],
            out_specs=pl.BlockSpec(memory_space=pl.ANY),
            scratch_shapes=[pltpu.SMEM((te * TOP_K,), I32), pltpu.SMEM((te * TOP_K,), I32),
                            pltpu.VMEM((EXPERT_BLOCK, h.shape[1]), U32),
                            pltpu.SemaphoreType.DMA((2,)), pltpu.SemaphoreType.DMA, pltpu.SemaphoreType.DMA],
        ),
        compiler_params=_params(("arbitrary",)),
        name="moe_dispatch",
    )(zero_blocks, h, dest_tiles)


def _expert_kernel(be_ref, nu_ref, xs_ref, wg0_ref, wu0_ref, wd0_ref, wg1_ref, wu1_ref, wd1_ref, ys_ref,
                   wg_bf, wu_bf, wd_bf):
    b = pl.program_id(0)
    rb = EXPERT_BLOCK
    used = 2 * b < nu_ref[0]
    for s, (wg, wu, wd) in enumerate(((wg0_ref, wu0_ref, wd0_ref), (wg1_ref, wu1_ref, wd1_ref))):
        j = 2 * b + s
        fresh = (b == 0) | (be_ref[j] != be_ref[jnp.maximum(j - 2, 0)])

        @pl.when(used & fresh)
        def _():
            wg_bf[s] = wg[...].astype(BF16)
            wu_bf[s] = wu[...].astype(BF16)
            wd_bf[s] = wd[...].astype(BF16)

    @pl.when(used)
    def _():
        for s in range(2):
            x = _unpack_bf16_pairs(xs_ref[rb * s:rb * (s + 1), :])
            hg = jnp.dot(x, wg_bf[s], preferred_element_type=F32)
            hu = jnp.dot(x, wu_bf[s], preferred_element_type=F32)
            y = jnp.dot((_silu(hg) * hu).astype(BF16), wd_bf[s], preferred_element_type=F32)
            ys_ref[rb * s:rb * (s + 1), :] = _pack_bf16_pairs(y)


def moe_experts(xs, block_expert, n_used, wg, wu, wd, layer):
    n_slots = xs.shape[0]
    rb = EXPERT_BLOCK
    ff = wg.shape[3]
    rows = lambda b, be, nu: (jnp.minimum(b, (nu[0] - 1) // 2), 0)
    w_in = lambda s: (lambda b, be, nu: (layer, be[2 * b + s], 0, 0))
    return pl.pallas_call(
        _expert_kernel,
        out_shape=jax.ShapeDtypeStruct((n_slots, D_MODEL // 2), U32),
        grid_spec=pltpu.PrefetchScalarGridSpec(
            num_scalar_prefetch=2,
            grid=(n_slots // (2 * rb),),
            in_specs=[pl.BlockSpec((2 * rb, xs.shape[1]), rows)]
            + [pl.BlockSpec((None, None, D_MODEL, ff), w_in(0)), pl.BlockSpec((None, None, D_MODEL, ff), w_in(0)),
               pl.BlockSpec((None, None, ff, D_MODEL), w_in(0)),
               pl.BlockSpec((None, None, D_MODEL, ff), w_in(1)), pl.BlockSpec((None, None, D_MODEL, ff), w_in(1)),
               pl.BlockSpec((None, None, ff, D_MODEL), w_in(1))],
            out_specs=pl.BlockSpec((2 * rb, D_MODEL // 2), rows),
            scratch_shapes=[pltpu.VMEM((2, D_MODEL, ff), BF16), pltpu.VMEM((2, D_MODEL, ff), BF16),
                            pltpu.VMEM((2, ff, D_MODEL), BF16)],
        ),
        compiler_params=_params(("arbitrary",)),
        name="moe_experts",
    )(block_expert, n_used, xs, wg, wu, wd, wg, wu, wd)


def _combine_kernel(npt, final, h_ref, swg_ref, swu_ref, swd_ref, x1_ref, g2s_ref, g2t_ref, w_ref, fg_ref,
                    dest_hbm, ys_hbm, o_ref, idx_a, idx_b, ybuf_ref, sem_idx, sem_rows):
    i = pl.program_id(0)
    n = pl.num_programs(0)
    tg = h_ref.shape[0]
    idx_bufs = (idx_a, idx_b)

    def idx_copy(tile, p):
        return pltpu.make_async_copy(dest_hbm.at[tile], idx_bufs[p], sem_idx.at[p])

    def gather_rows(p, wait):
        idx = idx_bufs[p]

        def body(t, carry):
            for kk in range(TOP_K):
                src = 0 if wait else idx[kk * tg + t]
                cp = _row_copy(ys_hbm, src, ybuf_ref.at[p, kk], t, sem_rows.at[p])
                cp.wait() if wait else cp.start(priority=kk % 2)
            return carry

        lax.fori_loop(0, tg, body, 0)

    @pl.when(i == 0)
    def _():
        idx_copy(0, 0).start()
        idx_copy(0, 0).wait()
        gather_rows(0, wait=False)

        @pl.when(n > 1)
        def _():
            idx_copy(1, 1).start()

    def prefetch(p):
        @pl.when(i + 1 < n)
        def _():
            idx_copy(i + 1, 1 - p).wait()
            gather_rows(1 - p, wait=False)

        @pl.when(i + 2 < n)
        def _():
            idx_copy(i + 2, p).start()

    _by_parity(i, prefetch)

    hb = _unpack_bf16_pairs(h_ref[...])
    hid = _silu(jnp.dot(hb, swg_ref[...], preferred_element_type=F32)) * jnp.dot(hb, swu_ref[...],
                                                                                 preferred_element_type=F32)
    acc = jnp.dot(hid.astype(BF16), swd_ref[...], preferred_element_type=F32)
    w = w_ref[...]
    gate = _pick(i >= npt, g2s_ref, g2t_ref)

    def finish(p):
        gather_rows(p, wait=True)
        half = D_MODEL // 2
        lo = jnp.zeros((tg, half), F32)
        hi = jnp.zeros((tg, half), F32)
        for kk in range(TOP_K):
            u = ybuf_ref[p, kk]
            wk = w[:, kk:kk + 1]
            lo = lo + lax.bitcast_convert_type(u << 16, F32) * wk
            hi = hi + lax.bitcast_convert_type(u & jnp.uint32(0xFFFF0000), F32) * wk
        routed = jnp.concatenate([lo, hi], axis=1)
        x2 = x1_ref[...] + gate * (routed + acc)
        if final:
            ms = jnp.mean(x2 * x2, axis=-1, keepdims=True)
            x2 = x2 * lax.rsqrt(ms + NORM_EPS) * fg_ref[...]
        o_ref[...] = x2

    _by_parity(i, finish)


def moe_combine(h, swg_bf, swu_bf, swd_bf, x1, mod, w_tok, final_g, dest_tiles, ys, final):
    t = h.shape[0]
    tg = mod.tile
    row = lambda i: (i, 0)
    const = lambda i: (0, 0)
    return pl.pallas_call(
        functools.partial(_combine_kernel, mod.npt, final),
        out_shape=jax.ShapeDtypeStruct((t, D_MODEL), F32),
        grid=(t // tg,),
        in_specs=[pl.BlockSpec((tg, h.shape[1]), row), pl.BlockSpec(swg_bf.shape, const), pl.BlockSpec(swu_bf.shape, const),
                  pl.BlockSpec(swd_bf.shape, const), pl.BlockSpec((tg, D_MODEL), row)]
        + mod.specs(5)
        + [pl.BlockSpec((tg, TOP_K), row), pl.BlockSpec((1, D_MODEL), const),
           pl.BlockSpec(memory_space=pl.ANY), pl.BlockSpec(memory_space=pl.ANY)],
        out_specs=pl.BlockSpec((tg, D_MODEL), row),
        scratch_shapes=[pltpu.SMEM((tg * TOP_K,), I32), pltpu.SMEM((tg * TOP_K,), I32),
                        pltpu.VMEM((2, TOP_K, tg, D_MODEL // 2), U32),
                        pltpu.SemaphoreType.DMA((2,)), pltpu.SemaphoreType.DMA((2,))],
        compiler_params=_params(("arbitrary",)),
        name="moe_combine",
    )(h, swg_bf, swu_bf, swd_bf, x1, *mod.operands(5), w_tok, final_g.reshape(1, -1), dest_tiles, ys)


def _slot_kernel(eidx_ref, rank_ref, start_ref, o_ref):
    tm = eidx_ref.shape[1]
    tr = o_ref.shape[2]
    eid = lax.broadcasted_iota(I32, (N_EXPERTS, tm), 0)
    start = start_ref[...]
    for kk in range(TOP_K):
        base = jnp.sum(jnp.where(eid == eidx_ref[kk:kk + 1, :], start, 0.0), axis=0, keepdims=True)
        slot = base.astype(I32) + rank_ref[kk:kk + 1, :]
        for j in range(tm // tr):
            o_ref[j, kk:kk + 1, :] = slot[:, tr * j:tr * (j + 1)]


def assignment_slots(eidx_t, rank_t, pad_start):
    t = eidx_t.shape[1]
    tm = TOKEN_TILE
    tr = ROUTE_TILE
    col = lambda i: (0, i)
    out = pl.pallas_call(
        _slot_kernel,
        out_shape=jax.ShapeDtypeStruct((t // tr, TOP_K, tr), I32),
        grid=(t // tm,),
        in_specs=[pl.BlockSpec((TOP_K, tm), col), pl.BlockSpec((TOP_K, tm), col),
                  pl.BlockSpec((N_EXPERTS, 1), lambda i: (0, 0))],
        out_specs=pl.BlockSpec((tm // tr, TOP_K, tr), lambda i: (i, 0, 0)),
        compiler_params=_params(("parallel",)),
        name="assignment_slots",
    )(eidx_t, rank_t, pad_start.astype(F32).reshape(-1, 1))
    return out.reshape(t // tr, TOP_K * tr)


def _routing_tables(counts, n_blocks):
    rb = EXPERT_BLOCK
    counts = counts.astype(I32)
    padded = (counts + rb - 1) // rb * rb
    pad_end = jnp.cumsum(padded)
    pad_start = pad_end - padded
    n_used = pad_end[-1] // rb
    blocks = jnp.arange(n_blocks, dtype=I32)
    block_expert = jnp.sum((pad_end[None, :] <= (blocks * rb)[:, None]).astype(I32), axis=1)
    last_used = jnp.sum((pad_end <= (n_used - 1) * rb).astype(I32))
    block_expert = jnp.minimum(jnp.where(blocks < n_used, block_expert, last_used), N_EXPERTS - 1)
    zero_blocks = jnp.where(counts % rb != 0, pad_end // rb - 1, -1)
    zero_blocks = jnp.concatenate([zero_blocks, jnp.where(n_used % 2 == 1, n_used, -1).reshape(1)]).astype(I32)
    return pad_start, block_expert, n_used.reshape(1).astype(I32), zero_blocks


def moe_layer(h, x1, eidx_t, w_t, rank_t, counts, mod_route, wg, wu, wd, layer, swg_bf, swu_bf, swd_bf,
              final_g, final):
    t = h.shape[0]
    rb = EXPERT_BLOCK
    tr = ROUTE_TILE
    n_blocks = -(-(t * TOP_K) // rb) + N_EXPERTS
    n_blocks += n_blocks % 2
    pad_start, block_expert, n_used, zero_blocks = _routing_tables(counts, n_blocks)
    dest_tiles = assignment_slots(eidx_t, rank_t, pad_start)
    xs = moe_dispatch(h, dest_tiles, zero_blocks, n_blocks * rb)
    ys = moe_experts(xs, block_expert, n_used, wg, wu, wd, layer)
    return moe_combine(h, swg_bf, swu_bf, swd_bf, x1, mod_route, w_t.T, final_g, dest_tiles, ys, final)


def _rope_tables(n_batch, seq_len, n_seq, n_new):
    half = SWA_HEAD_DIM // 2
    inv = ROPE_THETA ** (-jnp.arange(half, dtype=F32) / half)
    pos = jnp.concatenate([jnp.tile(jnp.arange(seq_len, dtype=F32), n_batch),
                           jnp.tile(PAST_LEN + jnp.arange(n_new, dtype=F32), n_seq)])
    ang = pos[:, None] * inv[None, :]
    cos = jnp.tile(jnp.cos(ang), (1, 128 // half))
    sin = jnp.sin(ang)
    sin = jnp.tile(jnp.concatenate([-sin, sin], axis=1), (1, 128 // SWA_HEAD_DIM))
    return cos, sin


def kernel(x_prompt, x_sample, c_prompt, c_sample, cache_swa_k, cache_swa_v, state_gla, norm_mix_g, norm_ffn_g,
           final_g, ada_w, ada_b, swa_wqkv, swa_sinks, swa_wo, gla_win, gla_wa1, gla_wa2, gla_ba, gla_norm_g,
           gla_wo, moe_router, moe_bias, moe_wg, moe_wu, moe_wd, shared_wg, shared_wu, shared_wd):
    n_batch, seq_len, d = x_prompt.shape
    n_seq, n_new, _ = x_sample.shape
    depth = ada_w.shape[0]
    tp = n_batch * seq_len
    ts = n_seq * n_new
    t = tp + ts
    tm = TOKEN_TILE
    tr = ROUTE_TILE

    x = jnp.concatenate([x_prompt.reshape(tp, d), x_sample.reshape(ts, d)], axis=0)
    c_all = jnp.concatenate([jnp.repeat(c_sample, n_new, axis=0), c_prompt], axis=0)
    mod = ada_modulation(c_all, ada_w, ada_b)
    cos_tab, sin_tab = _rope_tables(n_batch, seq_len, n_seq, n_new)
    tri = jnp.triu(jnp.ones((tm, tm), BF16), k=1)

    new_k, new_v, new_s = [], [], []
    new_k_s, new_v_s, new_s_s = [], [], []
    for layer in range(depth):
        mod_tok = _Mod(mod, layer, n_batch, seq_len, tm)
        mod_route = _Mod(mod, layer, n_batch, seq_len, tr)
        m = layer // 2
        if layer % 2 == 0:
            q, k, v, k_dup, v_dup = swa_qkv(x, norm_mix_g[layer], mod_tok, swa_wqkv[m].astype(BF16), cos_tab, sin_tab)
            a_p = swa_prompt_attention(q, k_dup, v_dup, swa_sinks[m], n_batch, seq_len)
            nk = SWA_KV_HEADS * SWA_HEAD_DIM
            a_s, ck, cv = swa_sample_attention(q, k, v, cache_swa_k[m].reshape(n_seq, WINDOW, nk),
                                             cache_swa_v[m].reshape(n_seq, WINDOW, nk), swa_sinks[m], tp, n_new)
            kv_shape = (n_batch, WINDOW, SWA_KV_HEADS, SWA_HEAD_DIM)
            tails = [slice((b + 1) * seq_len - WINDOW, (b + 1) * seq_len) for b in range(n_batch)]
            new_k.append(jnp.stack([k[rows] for rows in tails]).reshape(kv_shape))
            new_v.append(jnp.stack([v[rows] for rows in tails]).reshape(kv_shape))
            new_k_s.append(ck.reshape(n_seq, WINDOW, SWA_KV_HEADS, SWA_HEAD_DIM))
            new_v_s.append(cv.reshape(n_seq, WINDOW, SWA_KV_HEADS, SWA_HEAD_DIM))
            wo = swa_wo[m]
        else:
            pad = jnp.zeros((d, 128 - GLA_GATE_RANK), F32)
            win_ext = jnp.concatenate([gla_win[m], gla_wa1[m], pad], axis=1).astype(BF16)
            wa2_pad = jnp.concatenate([gla_wa2[m], jnp.zeros((128 - GLA_GATE_RANK, GLA_KEY_DIM), F32)],
                                      axis=0).astype(BF16)
            q, k, v, r, gate = gla_project(x, norm_mix_g[layer], mod_tok, win_ext, wa2_pad, gla_ba[m])
            a_p, s_prompt = gla_prompt(q, k, gate, v, r, gla_norm_g[m], n_batch, seq_len)
            a_s, s_sample = gla_sample(q, k, gate, v, r, gla_norm_g[m], state_gla[m], tp, n_new)
            new_s.append(s_prompt)
            new_s_s.append(s_sample)
            wo = gla_wo[m]
        x1, h, eidx_t, w_t, rank_t, cnt = post_mixer(a_p, a_s, wo.astype(BF16), x, mod_tok, norm_ffn_g[layer],
                                                     moe_router[layer].T, moe_bias[layer], tri)
        x = moe_layer(h, x1, eidx_t, w_t, rank_t, cnt[:, 0], mod_route,
                      moe_wg, moe_wu, moe_wd, layer,
                      shared_wg[layer].astype(BF16), shared_wu[layer].astype(BF16), shared_wd[layer].astype(BF16),
                      final_g, layer == depth - 1)

    y_prompt = x[:tp].reshape(n_batch, seq_len, d)
    y_sample = x[tp:].reshape(n_seq, n_new, d)
    return (y_prompt, y_sample, jnp.stack(new_k), jnp.stack(new_v), jnp.stack(new_k_s), jnp.stack(new_v_s),
            jnp.stack(new_s), jnp.stack(new_s_s))
```

```python
import functools

import jax
import jax.numpy as jnp
from jax import lax
from jax.experimental import pallas as pl
from jax.experimental.pallas import tpu as pltpu

F32 = jnp.float32
BF16 = jnp.bfloat16
I32 = jnp.int32
U32 = jnp.uint32

D_MODEL = 1024
PAST_LEN = 8192
SWA_HEAD_DIM = 64
SWA_HEADS = 16
SWA_KV_HEADS = 4
SWA_GROUP = 4
WINDOW = 128
ROPE_THETA = 10000.0
GLA_HEADS = 4
GLA_DK = 128
GLA_DV = 256
GLA_KEY_DIM = 512
GLA_VAL_DIM = 1024
GLA_GATE_RANK = 16
GLA_GATE_NORMALIZER = 16.0
GLA_CHUNK = 64
N_EXPERTS = 64
TOP_K = 8
N_GROUPS = 8
TOPK_GROUPS = 4
EXPERT_FF = 256
ROUTED_SCALE = 2.5
NORM_EPS = 1e-6

TOKEN_TILE = 512
ROUTE_TILE = 256
EXPERT_BLOCK = 256
ADA_TILE = 512
SAMPLE_SEQS = 8
GLA_PROMPT_BATCH = 2
VMEM_LIMIT = 48 * 1024 * 1024

NT_DIMS = (((1,), (1,)), ((), ()))
TN_DIMS = (((0,), (0,)), ((), ()))


def _params(semantics):
    return pltpu.CompilerParams(dimension_semantics=semantics, vmem_limit_bytes=VMEM_LIMIT)


def _silu(x):
    return x * jax.nn.sigmoid(x)


def _norm_mod(x, g, sc, sh):
    ms = jnp.mean(x * x, axis=-1, keepdims=True)
    return (x * lax.rsqrt(ms + NORM_EPS) * g) * (1.0 + sc) + sh


def _pack_bf16_pairs(x):
    half = x.shape[1] // 2
    xb = x.astype(BF16).astype(F32)
    lo = lax.bitcast_convert_type(xb[:, :half], U32) >> 16
    hi = lax.bitcast_convert_type(xb[:, half:], U32) & jnp.uint32(0xFFFF0000)
    return lo | hi


def _unpack_bf16_pairs(u):
    lo = lax.bitcast_convert_type(u << 16, F32)
    hi = lax.bitcast_convert_type(u & jnp.uint32(0xFFFF0000), F32)
    return jnp.concatenate([lo, hi], axis=1).astype(BF16)


ROW_CHUNKS = D_MODEL // 2 // 128


def _store_row_slabs(ref, rows, words):
    for c in range(ROW_CHUNKS):
        ref[rows, c, :] = words[:, 128 * c:128 * (c + 1)]


def _load_row_slabs(ref, rows):
    return jnp.concatenate([ref[rows, c, :] for c in range(ROW_CHUNKS)], axis=1)


def _split3(x):
    x1 = x.astype(BF16)
    r1 = x - x1.astype(F32)
    x2 = r1.astype(BF16)
    x3 = (r1 - x2.astype(F32)).astype(BF16)
    return x1, x2, x3


def _ada_kernel(c_ref, w_ref, b_ref, o_ref):
    s = _silu(c_ref[...]).astype(BF16)
    o_ref[...] = jnp.dot(s, w_ref[...].astype(BF16), preferred_element_type=F32) + b_ref[...]


def ada_modulation(c_all, ada_w, ada_b):
    depth, d, n = ada_w.shape
    rows = c_all.shape[0]
    return pl.pallas_call(
        _ada_kernel,
        out_shape=jax.ShapeDtypeStruct((depth, rows, n), F32),
        grid=(depth, n // ADA_TILE),
        in_specs=[
            pl.BlockSpec((rows, d), lambda l, j: (0, 0)),
            pl.BlockSpec((None, d, ADA_TILE), lambda l, j: (l, 0, j)),
            pl.BlockSpec((None, 1, ADA_TILE), lambda l, j: (l, 0, j)),
        ],
        out_specs=pl.BlockSpec((None, rows, ADA_TILE), lambda l, j: (l, 0, j)),
        compiler_params=_params(("parallel", "parallel")),
        name="ada_modulation",
    )(c_all, ada_w, ada_b.reshape(depth, 1, n))


class _Mod:
    def __init__(self, mod, layer, n_batch, seq_len, tile):
        depth, rows, n = mod.shape
        self.tile = tile
        self.layer = layer
        self.npt = n_batch * seq_len // tile
        self.mod_tok = mod
        self.mod_seq = mod[:, rows - n_batch:].reshape(depth, n_batch, 1, n)
        self.tiles_per_seq = seq_len // tile

    def operands(self, chunk):
        del chunk
        return [self.mod_seq, self.mod_tok]

    def specs(self, chunk):
        l, npt, tps = self.layer, self.npt, self.tiles_per_seq
        n_seq = self.mod_seq.shape[1]
        seq_spec = pl.BlockSpec((None, None, 1, D_MODEL),
                                lambda i, *_: (l, jnp.minimum(i // tps, n_seq - 1), 0, chunk))
        tok_spec = pl.BlockSpec((None, self.tile, D_MODEL),
                                lambda i, *_: (l, jnp.maximum(i - npt, 0), chunk))
        return [seq_spec, tok_spec]


def _pick(is_sample, seq_ref, tok_ref):
    return jnp.where(is_sample, tok_ref[...], seq_ref[...])


def _swa_qkv_kernel(npt, x_ref, g_ref, scs_ref, sct_ref, shs_ref, sht_ref, w_ref, cos_ref, sin_ref,
                    q_ref, k_ref, v_ref, kd_ref, vd_ref):
    is_s = pl.program_id(0) >= npt
    h = _norm_mod(x_ref[...], g_ref[...], _pick(is_s, scs_ref, sct_ref), _pick(is_s, shs_ref, sht_ref))
    qkv = jnp.dot(h.astype(BF16), w_ref[...], preferred_element_type=F32)
    cos = cos_ref[...]
    sin = sin_ref[...]
    lane = lax.broadcasted_iota(I32, cos.shape, 1)
    first_half = (lane % SWA_HEAD_DIM) < (SWA_HEAD_DIM // 2)

    def rope(xc):
        rot = jnp.where(first_half, pltpu.roll(xc, 128 - SWA_HEAD_DIM // 2, 1), pltpu.roll(xc, SWA_HEAD_DIM // 2, 1))
        return xc * cos + rot * sin

    nq = SWA_HEADS * SWA_HEAD_DIM
    nk = SWA_KV_HEADS * SWA_HEAD_DIM
    for c in range(nq // 128):
        q_ref[:, 128 * c:128 * (c + 1)] = (rope(qkv[:, 128 * c:128 * (c + 1)]) * (SWA_HEAD_DIM ** -0.5)).astype(BF16)
    low = lane < SWA_HEAD_DIM

    def spread(chunk):
        rolled = pltpu.roll(chunk, SWA_HEAD_DIM, 1)
        return jnp.where(low, chunk, rolled).astype(BF16), jnp.where(low, rolled, chunk).astype(BF16)

    for c in range(nk // 128):
        kc = rope(qkv[:, nq + 128 * c:nq + 128 * (c + 1)])
        vc = qkv[:, nq + nk + 128 * c:nq + nk + 128 * (c + 1)]
        k_ref[:, 128 * c:128 * (c + 1)] = kc
        v_ref[:, 128 * c:128 * (c + 1)] = vc
        kd_ref[:, 256 * c:256 * c + 128], kd_ref[:, 256 * c + 128:256 * (c + 1)] = spread(kc)
        vd_ref[:, 256 * c:256 * c + 128], vd_ref[:, 256 * c + 128:256 * (c + 1)] = spread(vc)


def swa_qkv(x, g, mod, w_bf, cos_tab, sin_tab):
    t = x.shape[0]
    tm = mod.tile
    nq = SWA_HEADS * SWA_HEAD_DIM
    nk = SWA_KV_HEADS * SWA_HEAD_DIM
    row = lambda i: (i, 0)
    return pl.pallas_call(
        functools.partial(_swa_qkv_kernel, mod.npt),
        out_shape=(jax.ShapeDtypeStruct((t, nq), BF16), jax.ShapeDtypeStruct((t, nk), F32),
                   jax.ShapeDtypeStruct((t, nk), F32), jax.ShapeDtypeStruct((t, 2 * nk), BF16),
                   jax.ShapeDtypeStruct((t, 2 * nk), BF16)),
        grid=(t // tm,),
        in_specs=[pl.BlockSpec((tm, D_MODEL), row), pl.BlockSpec((1, D_MODEL), lambda i: (0, 0))]
        + mod.specs(1) + mod.specs(0)
        + [pl.BlockSpec(w_bf.shape, lambda i: (0, 0)), pl.BlockSpec((tm, 128), row), pl.BlockSpec((tm, 128), row)],
        out_specs=(pl.BlockSpec((tm, nq), row), pl.BlockSpec((tm, nk), row), pl.BlockSpec((tm, nk), row),
                   pl.BlockSpec((tm, 2 * nk), row), pl.BlockSpec((tm, 2 * nk), row)),
        compiler_params=_params(("parallel",)),
        name="swa_qkv",
    )(x, g.reshape(1, -1), *mod.operands(1), *mod.operands(0), w_bf, cos_tab, sin_tab)


def _sink_softmax(s, sink_col):
    m = jnp.maximum(jnp.max(s, axis=-1, keepdims=True), sink_col)
    e = jnp.exp(s - m)
    den = jnp.sum(e, axis=-1, keepdims=True) + jnp.exp(sink_col - m)
    return e * (1.0 / den)


def _pair_attention(sink_ref, q_ref, kblks, vblks, mask, o_ref):
    rows = q_ref.shape[0]
    half = mask.shape[1]
    per_group = SWA_GROUP // 2
    scores = []
    for g in range(SWA_KV_HEADS):
        q2 = jnp.concatenate([q_ref[:, 128 * c:128 * (c + 1)] for c in range(per_group * g, per_group * (g + 1))],
                             axis=0)
        s = lax.dot_general(q2, kblks[g], NT_DIMS, preferred_element_type=F32)
        for j in range(per_group):
            scores += [s[rows * j:rows * (j + 1), :half], s[rows * j:rows * (j + 1), half:]]
    s_all = jnp.where(mask, jnp.stack(scores), -jnp.inf)
    sinks = jnp.stack([jnp.full((1, 1), sink_ref[h], F32) for h in range(SWA_HEADS)])
    p_all = _sink_softmax(s_all, sinks).astype(BF16)
    for g in range(SWA_KV_HEADS):
        chunks = range(per_group * g, per_group * (g + 1))
        p2 = jnp.concatenate([jnp.concatenate([p_all[2 * c], p_all[2 * c + 1]], axis=1) for c in chunks], axis=0)
        o = jnp.dot(p2, vblks[g], preferred_element_type=F32).astype(BF16)
        for j, c in enumerate(chunks):
            o_ref[:, 128 * c:128 * (c + 1)] = o[rows * j:rows * (j + 1)]


def _swa_prompt_kernel(sink_ref, q_ref, kc_ref, kp_ref, vc_ref, vp_ref, o_ref):
    j = pl.program_id(1)
    blk = q_ref.shape[0]
    qi = lax.broadcasted_iota(I32, (blk, 2 * blk), 0)
    sj = lax.broadcasted_iota(I32, (blk, 2 * blk), 1)
    rel = qi + blk - sj
    mask = (rel >= 0) & (rel <= WINDOW) & ((sj >= blk) | (j > 0))
    low = lax.broadcasted_iota(I32, (2 * blk, 128), 1) < SWA_HEAD_DIM
    zero = jnp.zeros((2 * blk, 128), BF16)
    kblks, vblks = [], []
    for g in range(SWA_KV_HEADS):
        cs = slice(128 * g, 128 * (g + 1))
        kcat = jnp.concatenate([kp_ref[:, cs], kc_ref[:, cs]], axis=0)
        vcat = jnp.concatenate([vp_ref[:, cs], vc_ref[:, cs]], axis=0)
        kblks.append(jnp.concatenate([jnp.where(low, kcat, zero), jnp.where(low, zero, kcat)], axis=0))
        vblks.append(jnp.concatenate([jnp.where(low, vcat, zero), jnp.where(low, zero, vcat)], axis=0))
    _pair_attention(sink_ref, q_ref, kblks, vblks, mask, o_ref)


def swa_prompt_attention(q, k, v, sinks, n_batch, seq_len):
    blk = WINDOW
    nb = seq_len // blk
    nq = q.shape[1]
    nk = k.shape[1]
    cur = lambda b, j: (b * nb + j, 0)
    prev = lambda b, j: (b * nb + jnp.maximum(j - 1, 0), 0)
    return pl.pallas_call(
        _swa_prompt_kernel,
        out_shape=jax.ShapeDtypeStruct((n_batch * seq_len, nq), BF16),
        grid=(n_batch, nb),
        in_specs=[pl.BlockSpec(memory_space=pltpu.SMEM),
                  pl.BlockSpec((blk, nq), cur),
                  pl.BlockSpec((blk, nk), cur), pl.BlockSpec((blk, nk), prev),
                  pl.BlockSpec((blk, nk), cur), pl.BlockSpec((blk, nk), prev)],
        out_specs=pl.BlockSpec((blk, nq), cur),
        compiler_params=_params(("parallel", "parallel")),
        name="swa_prompt_attention",
    )(sinks, q, k, k, v, v)


def _swa_sample_kernel(n_new, sink_ref, q_ref, kn_ref, vn_ref, ck_ref, cv_ref, o_ref, nk_ref, nv_ref):
    n_sb, win, _ = ck_ref.shape
    per_seq = win + n_new
    rows = n_sb * n_new
    cols = n_sb * per_seq
    keys, vals = [], []
    for sb in range(n_sb):
        r0 = sb * n_new
        kc = ck_ref[sb]
        vc = cv_ref[sb]
        kn = kn_ref[r0:r0 + n_new, :]
        vn = vn_ref[r0:r0 + n_new, :]
        nk_ref[sb, 0:win - n_new, :] = kc[n_new:]
        nk_ref[sb, win - n_new:win, :] = kn
        nv_ref[sb, 0:win - n_new, :] = vc[n_new:]
        nv_ref[sb, win - n_new:win, :] = vn
        keys += [kc, kn]
        vals += [vc, vn]
    keys = jnp.concatenate(keys, axis=0)
    vals = jnp.concatenate(vals, axis=0)
    ri = lax.broadcasted_iota(I32, (rows, cols), 0)
    ci = lax.broadcasted_iota(I32, (rows, cols), 1)
    ti = ri % n_new
    si = ci % per_seq
    mask = (ri // n_new == ci // per_seq) & (si >= ti) & (si <= ti + WINDOW)
    low = lax.broadcasted_iota(I32, (cols, 128), 1) < SWA_HEAD_DIM
    zero = jnp.zeros((cols, 128), BF16)

    def block_diag(chunk, first):
        rolled = pltpu.roll(chunk, SWA_HEAD_DIM, 1)
        both = (jnp.where(low, chunk, rolled) if first else jnp.where(low, rolled, chunk)).astype(BF16)
        return jnp.concatenate([jnp.where(low, both, zero), jnp.where(low, zero, both)], axis=0)

    chunks = [slice(128 * (g // 2), 128 * (g // 2 + 1)) for g in range(SWA_KV_HEADS)]
    kblks = [block_diag(keys[:, cs], g % 2 == 0) for g, cs in enumerate(chunks)]
    vblks = [block_diag(vals[:, cs], g % 2 == 0) for g, cs in enumerate(chunks)]
    _pair_attention(sink_ref, q_ref, kblks, vblks, mask, o_ref)


def swa_sample_attention(q, k, v, cache_k, cache_v, sinks, n_prompt_rows, n_new):
    n_seq, win, nk = cache_k.shape
    sb = SAMPLE_SEQS
    rows = sb * n_new
    nq = q.shape[1]
    base = n_prompt_rows // rows
    tok = lambda i: (base + i, 0)
    seq = lambda i: (i, 0, 0)
    return pl.pallas_call(
        functools.partial(_swa_sample_kernel, n_new),
        out_shape=(jax.ShapeDtypeStruct((n_seq * n_new, nq), BF16),
                   jax.ShapeDtypeStruct(cache_k.shape, F32), jax.ShapeDtypeStruct(cache_v.shape, F32)),
        grid=(n_seq // sb,),
        in_specs=[pl.BlockSpec(memory_space=pltpu.SMEM),
                  pl.BlockSpec((rows, nq), tok), pl.BlockSpec((rows, nk), tok), pl.BlockSpec((rows, nk), tok),
                  pl.BlockSpec((sb, win, nk), seq), pl.BlockSpec((sb, win, nk), seq)],
        out_specs=(pl.BlockSpec((rows, nq), lambda i: (i, 0)), pl.BlockSpec((sb, win, nk), seq),
                   pl.BlockSpec((sb, win, nk), seq)),
        compiler_params=_params(("parallel",)),
        name="swa_sample_attention",
    )(sinks, q, k, v, cache_k, cache_v)


def _gla_proj_kernel(npt, x_ref, g_ref, scs_ref, sct_ref, shs_ref, sht_ref, w_ref, wa2_ref, ba_ref,
                     q_ref, k_ref, v_ref, r_ref, gate_ref):
    is_s = pl.program_id(0) >= npt
    h = _norm_mod(x_ref[...], g_ref[...], _pick(is_s, scs_ref, sct_ref), _pick(is_s, shs_ref, sht_ref))
    proj = jnp.dot(h.astype(BF16), w_ref[...], preferred_element_type=F32)
    kd = GLA_KEY_DIM
    vd = GLA_VAL_DIM
    q_ref[...] = proj[:, :kd] * (GLA_DK ** -0.5)
    k_ref[...] = proj[:, kd:2 * kd]
    v_ref[...] = proj[:, 2 * kd:2 * kd + vd].astype(BF16)
    r_ref[...] = proj[:, 2 * kd + vd:2 * kd + 2 * vd]
    low = proj[:, 2 * kd + 2 * vd:].astype(BF16)
    z = jnp.dot(low, wa2_ref[...], preferred_element_type=F32) + ba_ref[...]
    log_sig = jnp.minimum(z, 0.0) - jnp.log1p(jnp.exp(-jnp.abs(z)))
    gate_ref[...] = log_sig / GLA_GATE_NORMALIZER


def gla_project(x, g, mod, win_ext, wa2_pad, ba):
    t = x.shape[0]
    tm = mod.tile
    kd, vd = GLA_KEY_DIM, GLA_VAL_DIM
    row = lambda i: (i, 0)
    const = lambda i: (0, 0)
    return pl.pallas_call(
        functools.partial(_gla_proj_kernel, mod.npt),
        out_shape=(jax.ShapeDtypeStruct((t, kd), F32), jax.ShapeDtypeStruct((t, kd), F32),
                   jax.ShapeDtypeStruct((t, vd), BF16), jax.ShapeDtypeStruct((t, vd), F32),
                   jax.ShapeDtypeStruct((t, kd), F32)),
        grid=(t // tm,),
        in_specs=[pl.BlockSpec((tm, D_MODEL), row), pl.BlockSpec((1, D_MODEL), const)]
        + mod.specs(1) + mod.specs(0)
        + [pl.BlockSpec(win_ext.shape, const), pl.BlockSpec(wa2_pad.shape, const), pl.BlockSpec((1, kd), const)],
        out_specs=(pl.BlockSpec((tm, kd), row), pl.BlockSpec((tm, kd), row), pl.BlockSpec((tm, vd), row),
                   pl.BlockSpec((tm, vd), row), pl.BlockSpec((tm, kd), row)),
        compiler_params=_params(("parallel",)),
        name="gla_project",
    )(x, g.reshape(1, -1), *mod.operands(1), *mod.operands(0), win_ext, wa2_pad, ba.reshape(1, -1))


def _cumsum_rows(tri, g):
    n = g.shape[1]
    s = jnp.dot(tri, jnp.concatenate(_split3(g), axis=1), preferred_element_type=F32)
    return s[:, :n] + s[:, n:2 * n] + s[:, 2 * n:]


def _diag_attention(q, k, b, n):
    ng = n // 8
    dk = q.shape[1]
    q3 = q.reshape(ng, 8, dk)
    k3 = k.reshape(ng, 8, dk)
    b3 = b.reshape(ng, 8, dk)
    sub = lax.broadcasted_iota(I32, (ng, 8, dk), 1)
    ti = lax.broadcasted_iota(I32, (n, n), 0)
    si = lax.broadcasted_iota(I32, (n, n), 1)
    attn = jnp.zeros((n, n), F32)
    for j in range(8):
        bj = jnp.broadcast_to(b3[:, j:j + 1, :], b3.shape)
        kj = jnp.broadcast_to(k3[:, j:j + 1, :], k3.shape)
        e = jnp.exp(jnp.minimum(b3 - bj, 0.0))
        m = jnp.where(sub >= j, q3 * e * kj, 0.0)
        col = jnp.sum(m, axis=-1, keepdims=True).reshape(n, 1)
        attn = attn + jnp.where(si == (ti // 8) * 8 + j, col, 0.0)
    return attn


def _cross_attention(q, k, b, n):
    ti = lax.broadcasted_iota(I32, (n, n), 0)
    si = lax.broadcasted_iota(I32, (n, n), 1)
    row = lax.broadcasted_iota(I32, b.shape, 0)
    attn = jnp.zeros((n, n), F32)
    m = n // 2
    while m >= 8:
        nblk = n // m
        refq = jnp.concatenate(
            [jnp.broadcast_to(b[i * m - 1:i * m], (m, b.shape[1])) if i % 2 else b[i * m:(i + 1) * m]
             for i in range(nblk)], axis=0)
        refk = jnp.concatenate(
            [b[i * m:(i + 1) * m] if i % 2 else jnp.broadcast_to(b[(i + 1) * m - 1:(i + 1) * m], (m, b.shape[1]))
             for i in range(nblk)], axis=0)
        odd = ((row // m) % 2) == 1
        qt = jnp.where(odd, q * jnp.exp(jnp.minimum(b - refq, 0.0)), 0.0).astype(BF16)
        kt = jnp.where(odd, 0.0, k * jnp.exp(jnp.minimum(refk - b, 0.0))).astype(BF16)
        a = lax.dot_general(qt, kt, NT_DIMS, preferred_element_type=F32)
        keep = (((ti // m) % 2) == 1) & ((si // m) == (ti // m) - 1)
        attn = attn + jnp.where(keep, a, 0.0)
        m //= 2
    return attn


def _gla_epilogue(o, r, ng):
    ms = jnp.mean(o * o, axis=-1, keepdims=True)
    return (o * lax.rsqrt(ms + NORM_EPS) * ng * _silu(r)).astype(BF16)


def _gla_prompt_kernel(*refs):
    nb = GLA_PROMPT_BATCH
    ins, (ng_ref, o_ref, so_ref, st_ref) = refs[:5 * nb], refs[5 * nb:]
    c = pl.program_id(1)
    n = ins[0].shape[0]

    @pl.when(c == 0)
    def _():
        st_ref[...] = jnp.zeros(st_ref.shape, F32)

    ti = lax.broadcasted_iota(I32, (n, n), 0)
    si = lax.broadcasted_iota(I32, (n, n), 1)
    tri = jnp.where(ti >= si, 1.0, 0.0).astype(BF16)
    for i in range(nb):
        q_ref, k_ref, g_ref, v_ref, r_ref = ins[5 * i:5 * (i + 1)]
        b_all = _cumsum_rows(tri, g_ref[...])
        for h in range(GLA_HEADS):
            ks = slice(GLA_DK * h, GLA_DK * (h + 1))
            vs = slice(GLA_DV * h, GLA_DV * (h + 1))
            q = q_ref[:, ks]
            k = k_ref[:, ks]
            v = v_ref[:, vs]
            b = b_all[:, ks]
            s_t = st_ref[i, h]
            o = lax.dot_general((q * jnp.exp(b)).astype(BF16), s_t.astype(BF16), NT_DIMS,
                                preferred_element_type=F32)
            attn = _cross_attention(q, k, b, n) + _diag_attention(q, k, b, n)
            o = o + jnp.dot(attn.astype(BF16), v, preferred_element_type=F32)
            bl = b[n - 1:n, :]
            kd = (k * jnp.exp(bl - b)).astype(BF16)
            s_new = s_t * jnp.exp(bl) + lax.dot_general(v, kd, TN_DIMS, preferred_element_type=F32)
            st_ref[i, h] = s_new
            o_ref[i, :, vs] = _gla_epilogue(o, r_ref[:, vs], ng_ref[...])

    @pl.when(c == pl.num_programs(1) - 1)
    def _():
        for i in range(nb):
            for h in range(GLA_HEADS):
                so_ref[i, h] = st_ref[i, h].T


def gla_prompt(q, k, g, v, r, norm_g, n_batch, seq_len):
    n = GLA_CHUNK
    nb = GLA_PROMPT_BATCH
    nc = seq_len // n
    kd, vd = GLA_KEY_DIM, GLA_VAL_DIM
    in_specs, operands = [], []
    for i in range(nb):
        row = lambda b, c, i=i: ((nb * b + i) * nc + c, 0)
        in_specs += [pl.BlockSpec((n, kd), row), pl.BlockSpec((n, kd), row), pl.BlockSpec((n, kd), row),
                     pl.BlockSpec((n, vd), row), pl.BlockSpec((n, vd), row)]
        operands += [q, k, g, v, r]
    o, state = pl.pallas_call(
        _gla_prompt_kernel,
        out_shape=(jax.ShapeDtypeStruct((n_batch, seq_len, vd), BF16),
                   jax.ShapeDtypeStruct((n_batch, GLA_HEADS, GLA_DK, GLA_DV), F32)),
        grid=(n_batch // nb, nc),
        in_specs=in_specs + [pl.BlockSpec((1, GLA_DV), lambda b, c: (0, 0))],
        out_specs=(pl.BlockSpec((nb, n, vd), lambda b, c: (b, c, 0)),
                   pl.BlockSpec((nb, GLA_HEADS, GLA_DK, GLA_DV), lambda b, c: (b, 0, 0, 0))),
        scratch_shapes=[pltpu.VMEM((nb, GLA_HEADS, GLA_DV, GLA_DK), F32)],
        compiler_params=_params(("parallel", "arbitrary")),
        name="gla_prompt",
    )(*operands, norm_g.reshape(1, -1))
    return o.reshape(n_batch * seq_len, vd), state


def _gla_sample_kernel(n_new, q_ref, k_ref, g_ref, v_ref, r_ref, ng_ref, si_ref, o_ref, so_ref):
    n = q_ref.shape[0]
    ti = lax.broadcasted_iota(I32, (n, n), 0)
    si = lax.broadcasted_iota(I32, (n, n), 1)
    tri = jnp.where((ti >= si) & (ti // n_new == si // n_new), 1.0, 0.0).astype(BF16)
    b_all = _cumsum_rows(tri, g_ref[...])
    for h in range(GLA_HEADS):
        ks = slice(GLA_DK * h, GLA_DK * (h + 1))
        vs = slice(GLA_DV * h, GLA_DV * (h + 1))
        q = q_ref[:, ks]
        k = k_ref[:, ks]
        v = v_ref[:, vs]
        b = b_all[:, ks]
        attn = _diag_attention(q, k, b, n)
        o_intra = jnp.dot(attn.astype(BF16), v, preferred_element_type=F32)
        qe = (q * jnp.exp(b)).astype(BF16)
        n_sb = n // n_new
        last = [b[n_new * (sb + 1) - 1:n_new * (sb + 1), :] for sb in range(n_sb)]
        bl_rows = jnp.concatenate([jnp.broadcast_to(bl, (n_new, GLA_DK)) for bl in last], axis=0)
        kd = (k * jnp.exp(bl_rows - b)).astype(BF16)
        seq_of_row = lax.broadcasted_iota(I32, (n, GLA_DV), 0) // n_new
        o = o_intra
        for sb in range(n_sb):
            mine = seq_of_row == sb
            s_t = si_ref[sb, h].T
            o_sb = lax.dot_general(qe, s_t.astype(BF16), NT_DIMS, preferred_element_type=F32)
            o = o + jnp.where(mine, o_sb, 0.0)
            v_sb = jnp.where(mine, v, jnp.zeros_like(v))
            upd = lax.dot_general(v_sb, kd, TN_DIMS, preferred_element_type=F32)
            so_ref[sb, h] = (s_t * jnp.exp(last[sb]) + upd).T
        o_ref[:, vs] = _gla_epilogue(o, r_ref[:, vs], ng_ref[...])


def gla_sample(q, k, g, v, r, norm_g, state, n_prompt_rows, n_new):
    n_seq = state.shape[0]
    sb = SAMPLE_SEQS
    rows = sb * n_new
    kd, vd = GLA_KEY_DIM, GLA_VAL_DIM
    base = n_prompt_rows // rows
    tok = lambda i: (base + i, 0)
    seq = lambda i: (i, 0, 0, 0)
    sblock = (sb, GLA_HEADS, GLA_DK, GLA_DV)
    return pl.pallas_call(
        functools.partial(_gla_sample_kernel, n_new),
        out_shape=(jax.ShapeDtypeStruct((n_seq * n_new, vd), BF16), jax.ShapeDtypeStruct(state.shape, F32)),
        grid=(n_seq // sb,),
        in_specs=[pl.BlockSpec((rows, kd), tok), pl.BlockSpec((rows, kd), tok), pl.BlockSpec((rows, kd), tok),
                  pl.BlockSpec((rows, vd), tok), pl.BlockSpec((rows, vd), tok),
                  pl.BlockSpec((1, GLA_DV), lambda i: (0, 0)),
                  pl.BlockSpec(sblock, seq)],
        out_specs=(pl.BlockSpec((rows, vd), lambda i: (i, 0)), pl.BlockSpec(sblock, seq)),
        compiler_params=_params(("parallel",)),
        name="gla_sample",
    )(q, k, g, v, r, norm_g.reshape(1, -1), state)


def _post_mixer_kernel(npt, ap_ref, as_ref, wo_ref, x_ref, g1s_ref, g1t_ref, gf_ref, scs_ref, sct_ref, shs_ref, sht_ref,
                       rw_ref, rb_ref, tri_ref,
                       x1_ref, h_ref, eidx_ref, w_ref, rank_ref, cnt_ref, carry_ref):
    i = pl.program_id(0)
    is_s = i >= npt

    @pl.when(i == 0)
    def _():
        carry_ref[...] = jnp.zeros(carry_ref.shape, F32)

    a = jnp.where(is_s, as_ref[...], ap_ref[...])
    x1 = x_ref[...] + _pick(is_s, g1s_ref, g1t_ref) * jnp.dot(a, wo_ref[...], preferred_element_type=F32)
    x1_ref[...] = x1
    h = _norm_mod(x1, gf_ref[...], _pick(is_s, scs_ref, sct_ref), _pick(is_s, shs_ref, sht_ref))
    h_ref[...] = _pack_bf16_pairs(h)

    h1, h2, _ = _split3(h)
    r1, r2, _ = _split3(rw_ref[...])
    logits = (lax.dot_general(r1, h1, NT_DIMS, preferred_element_type=F32)
              + lax.dot_general(r1, h2, NT_DIMS, preferred_element_type=F32)
              + lax.dot_general(r2, h1, NT_DIMS, preferred_element_type=F32))
    scores = jax.nn.sigmoid(logits)
    sel = scores + rb_ref[...]
    tm = sel.shape[1]
    gsz = N_EXPERTS // N_GROUPS

    sub = lax.broadcasted_iota(I32, (gsz, tm), 0)
    blocks, gscore = [], []
    for g in range(N_GROUPS):
        blk = sel[gsz * g:gsz * (g + 1)]
        m1 = jnp.max(blk, axis=0, keepdims=True)
        first = jnp.min(jnp.where(blk == m1, sub, gsz), axis=0, keepdims=True)
        m2 = jnp.max(jnp.where(sub == first, -jnp.inf, blk), axis=0, keepdims=True)
        blocks.append(blk)
        gscore.append(m1 + m2)
    masked = []
    for g in range(N_GROUPS):
        beaten = jnp.zeros((1, tm), I32)
        for o in range(N_GROUPS):
            if o == g:
                continue
            wins = (gscore[o] > gscore[g]) | ((gscore[o] == gscore[g]) & (o < g))
            beaten = beaten + wins.astype(I32)
        masked.append(jnp.where(beaten < TOPK_GROUPS, blocks[g], -jnp.inf))
    cur = jnp.concatenate(masked, axis=0)

    eid = lax.broadcasted_iota(I32, (N_EXPERTS, tm), 0)
    picked, weights = [], []
    onehot = jnp.zeros((N_EXPERTS, tm), F32)
    for _ in range(TOP_K):
        m = jnp.max(cur, axis=0, keepdims=True)
        idx = jnp.min(jnp.where(cur == m, eid, N_EXPERTS), axis=0, keepdims=True)
        hit = eid == idx
        picked.append(idx)
        weights.append(jnp.sum(jnp.where(hit, scores, 0.0), axis=0, keepdims=True))
        onehot = jnp.where(hit, 1.0, onehot)
        cur = jnp.where(hit, -jnp.inf, cur)
    wsum = weights[0]
    for wk in weights[1:]:
        wsum = wsum + wk
    scale = ROUTED_SCALE / wsum

    before = jnp.dot(onehot.astype(BF16), tri_ref[...], preferred_element_type=F32) + carry_ref[...]
    carry = carry_ref[...] + jnp.sum(onehot, axis=1, keepdims=True)
    carry_ref[...] = carry
    cnt_ref[...] = jnp.broadcast_to(carry, cnt_ref.shape)
    for kk in range(TOP_K):
        eidx_ref[kk:kk + 1, :] = picked[kk]
        w_ref[kk:kk + 1, :] = weights[kk] * scale
        rank_ref[kk:kk + 1, :] = jnp.sum(jnp.where(eid == picked[kk], before, 0.0), axis=0, keepdims=True).astype(I32)


def post_mixer(a_prompt, a_sample, wo_bf, x, mod, gffn, router_t, router_b, tri):
    t = x.shape[0]
    tm = mod.tile
    npt = mod.npt
    row = lambda i: (i, 0)
    col = lambda i: (0, i)
    const = lambda i: (0, 0)
    return pl.pallas_call(
        functools.partial(_post_mixer_kernel, mod.npt),
        out_shape=(jax.ShapeDtypeStruct((t, D_MODEL), F32), jax.ShapeDtypeStruct((t, D_MODEL // 2), U32),
                   jax.ShapeDtypeStruct((TOP_K, t), I32), jax.ShapeDtypeStruct((TOP_K, t), F32),
                   jax.ShapeDtypeStruct((TOP_K, t), I32), jax.ShapeDtypeStruct((N_EXPERTS, 128), F32)),
        grid=(t // tm,),
        in_specs=[pl.BlockSpec((tm, D_MODEL), lambda i: (jnp.minimum(i, npt - 1), 0)),
                  pl.BlockSpec((tm, D_MODEL), lambda i: (jnp.maximum(i - npt, 0), 0)),
                  pl.BlockSpec((D_MODEL, D_MODEL), const), pl.BlockSpec((tm, D_MODEL), row)]
        + mod.specs(2) + [pl.BlockSpec((1, D_MODEL), const)] + mod.specs(4) + mod.specs(3)
        + [pl.BlockSpec((N_EXPERTS, D_MODEL), const), pl.BlockSpec((N_EXPERTS, 1), const),
           pl.BlockSpec((tm, tm), const)],
        out_specs=(pl.BlockSpec((tm, D_MODEL), row), pl.BlockSpec((tm, D_MODEL // 2), row),
                   pl.BlockSpec((TOP_K, tm), col), pl.BlockSpec((TOP_K, tm), col), pl.BlockSpec((TOP_K, tm), col),
                   pl.BlockSpec((N_EXPERTS, 128), const)),
        scratch_shapes=[pltpu.VMEM((N_EXPERTS, 1), F32)],
        compiler_params=_params(("arbitrary",)),
        name="post_mixer",
    )(a_prompt, a_sample, wo_bf, x, *mod.operands(2), gffn.reshape(1, -1), *mod.operands(4), *mod.operands(3),
      router_t, router_b.reshape(-1, 1), tri)


def _row_copy(src, src_row, dst, dst_row, sem):
    return pltpu.make_async_copy(src.at[src_row], dst.at[dst_row], sem)


def _by_parity(i, fn):
    @pl.when(i % 2 == 0)
    def _():
        fn(0)

    @pl.when(i % 2 == 1)
    def _():
        fn(1)


def _dispatch_kernel(zb_ref, h_ref, dest_hbm, xs_hbm, idx_a, idx_b, zero_ref, slab_ref, sem_idx, sem_zero, sem_rows):
    i = pl.program_id(0)
    te = h_ref.shape[0]
    _store_row_slabs(slab_ref, slice(None), h_ref[...])
    idx_bufs = (idx_a, idx_b)

    def idx_copy(tile, p):
        return pltpu.make_async_copy(dest_hbm.at[tile], idx_bufs[p], sem_idx.at[p])

    def zero_copy(e):
        return pltpu.make_async_copy(zero_ref, xs_hbm.at[pl.ds(zb_ref[e] * EXPERT_BLOCK, EXPERT_BLOCK)], sem_zero)

    @pl.when(i == 0)
    def _():
        idx_copy(0, 0).start()
        zero_ref[...] = jnp.zeros(zero_ref.shape, U32)

        def start(e, carry):
            @pl.when(zb_ref[e] >= 0)
            def _():
                zero_copy(e).start()
            return carry

        def wait(e, carry):
            @pl.when(zb_ref[e] >= 0)
            def _():
                zero_copy(e).wait()
            return carry

        lax.fori_loop(0, zb_ref.shape[0], start, 0)
        lax.fori_loop(0, zb_ref.shape[0], wait, 0)

    def step(p):
        @pl.when(i + 1 < pl.num_programs(0))
        def _():
            idx_copy(i + 1, 1 - p).start()

        idx_copy(i, p).wait()
        idx = idx_bufs[p]

        def issue(t, carry):
            for kk in range(TOP_K):
                _row_copy(slab_ref, t, xs_hbm, idx[kk * te + t], sem_rows).start(priority=kk % 2)
            return carry

        def drain(t, carry):
            for kk in range(TOP_K):
                _row_copy(slab_ref, t, xs_hbm, idx[kk * te + t], sem_rows).wait()
            return carry

        lax.fori_loop(0, te, issue, 0)
        lax.fori_loop(0, te, drain, 0)

    _by_parity(i, step)


def moe_dispatch(h, dest_tiles, zero_blocks, n_slots):
    t = h.shape[0]
    te = ROUTE_TILE
    return pl.pallas_call(
        _dispatch_kernel,
        out_shape=jax.ShapeDtypeStruct((n_slots, ROW_CHUNKS, 128), U32),
        grid_spec=pltpu.PrefetchScalarGridSpec(
            num_scalar_prefetch=1,
            grid=(t // te,),
            in_specs=[pl.BlockSpec((te, h.shape[1]), lambda i, zb: (i, 0)), pl.BlockSpec(memory_space=pl.ANY)],
            out_specs=pl.BlockSpec(memory_space=pl.ANY),
            scratch_shapes=[pltpu.SMEM((te * TOP_K,), I32), pltpu.SMEM((te * TOP_K,), I32),
                            pltpu.VMEM((EXPERT_BLOCK, ROW_CHUNKS, 128), U32), pltpu.VMEM((te, ROW_CHUNKS, 128), U32),
                            pltpu.SemaphoreType.DMA((2,)), pltpu.SemaphoreType.DMA, pltpu.SemaphoreType.DMA],
        ),
        compiler_params=_params(("arbitrary",)),
        name="moe_dispatch",
    )(zero_blocks, h, dest_tiles)


def _expert_kernel(be_ref, nu_ref, xs_ref, wg0_ref, wu0_ref, wd0_ref, wg1_ref, wu1_ref, wd1_ref, ys_ref,
                   wg_bf, wu_bf, wd_bf):
    b = pl.program_id(0)
    rb = EXPERT_BLOCK
    used = 2 * b < nu_ref[0]
    for s, (wg, wu, wd) in enumerate(((wg0_ref, wu0_ref, wd0_ref), (wg1_ref, wu1_ref, wd1_ref))):
        j = 2 * b + s
        fresh = (b == 0) | (be_ref[j] != be_ref[jnp.maximum(j - 2, 0)])

        @pl.when(used & fresh)
        def _():
            wg_bf[s] = wg[...].astype(BF16)
            wu_bf[s] = wu[...].astype(BF16)
            wd_bf[s] = wd[...].astype(BF16)

    @pl.when(used)
    def _():
        for s in range(2):
            rows = slice(rb * s, rb * (s + 1))
            x = _unpack_bf16_pairs(_load_row_slabs(xs_ref, rows))
            hg = jnp.dot(x, wg_bf[s], preferred_element_type=F32)
            hu = jnp.dot(x, wu_bf[s], preferred_element_type=F32)
            y = jnp.dot((_silu(hg) * hu).astype(BF16), wd_bf[s], preferred_element_type=F32)
            ys_ref[rows, :] = _pack_bf16_pairs(y)


def moe_experts(xs, block_expert, n_used, wg, wu, wd, layer):
    n_slots = xs.shape[0]
    rb = EXPERT_BLOCK
    ff = wg.shape[3]
    rows_in = lambda b, be, nu: (jnp.minimum(b, (nu[0] - 1) // 2), 0, 0)
    rows_out = lambda b, be, nu: (jnp.minimum(b, (nu[0] - 1) // 2), 0)
    w_in = lambda s: (lambda b, be, nu: (layer, be[2 * b + s], 0, 0))
    return pl.pallas_call(
        _expert_kernel,
        out_shape=jax.ShapeDtypeStruct((n_slots, D_MODEL // 2), U32),
        grid_spec=pltpu.PrefetchScalarGridSpec(
            num_scalar_prefetch=2,
            grid=(n_slots // (2 * rb),),
            in_specs=[pl.BlockSpec((2 * rb,) + xs.shape[1:], rows_in)]
            + [pl.BlockSpec((None, None, D_MODEL, ff), w_in(0)), pl.BlockSpec((None, None, D_MODEL, ff), w_in(0)),
               pl.BlockSpec((None, None, ff, D_MODEL), w_in(0)),
               pl.BlockSpec((None, None, D_MODEL, ff), w_in(1)), pl.BlockSpec((None, None, D_MODEL, ff), w_in(1)),
               pl.BlockSpec((None, None, ff, D_MODEL), w_in(1))],
            out_specs=pl.BlockSpec((2 * rb, D_MODEL // 2), rows_out),
            scratch_shapes=[pltpu.VMEM((2, D_MODEL, ff), BF16), pltpu.VMEM((2, D_MODEL, ff), BF16),
                            pltpu.VMEM((2, ff, D_MODEL), BF16)],
        ),
        compiler_params=_params(("arbitrary",)),
        name="moe_experts",
    )(block_expert, n_used, xs, wg, wu, wd, wg, wu, wd)


def _combine_kernel(npt, final, h_ref, swg_ref, swu_ref, swd_ref, x1_ref, g2s_ref, g2t_ref, w_ref, fg_ref,
                    dest_hbm, ys_hbm, o_ref, idx_a, idx_b, ybuf_ref, sem_idx, sem_rows):
    i = pl.program_id(0)
    n = pl.num_programs(0)
    tg = h_ref.shape[0]
    idx_bufs = (idx_a, idx_b)

    def idx_copy(tile, p):
        return pltpu.make_async_copy(dest_hbm.at[tile], idx_bufs[p], sem_idx.at[p])

    def gather_rows(p, wait):
        idx = idx_bufs[p]

        def body(t, carry):
            for kk in range(TOP_K):
                src = 0 if wait else idx[kk * tg + t]
                cp = pltpu.make_async_copy(ys_hbm.at[pl.ds(src, 1)], ybuf_ref.at[p, kk, pl.ds(t, 1)], sem_rows.at[p])
                cp.wait() if wait else cp.start(priority=kk % 2)
            return carry

        lax.fori_loop(0, tg, body, 0)

    @pl.when(i == 0)
    def _():
        idx_copy(0, 0).start()
        idx_copy(0, 0).wait()
        gather_rows(0, wait=False)

        @pl.when(n > 1)
        def _():
            idx_copy(1, 1).start()

    def prefetch(p):
        @pl.when(i + 1 < n)
        def _():
            idx_copy(i + 1, 1 - p).wait()
            gather_rows(1 - p, wait=False)

        @pl.when(i + 2 < n)
        def _():
            idx_copy(i + 2, p).start()

    _by_parity(i, prefetch)

    hb = _unpack_bf16_pairs(h_ref[...])
    hid = _silu(jnp.dot(hb, swg_ref[...], preferred_element_type=F32)) * jnp.dot(hb, swu_ref[...],
                                                                                 preferred_element_type=F32)
    acc = jnp.dot(hid.astype(BF16), swd_ref[...], preferred_element_type=F32)
    w = w_ref[...]
    gate = _pick(i >= npt, g2s_ref, g2t_ref)

    def finish(p):
        gather_rows(p, wait=True)
        half = D_MODEL // 2
        lo = jnp.zeros((tg, half), F32)
        hi = jnp.zeros((tg, half), F32)
        for kk in range(TOP_K):
            u = ybuf_ref[p, kk]
            wk = w[:, kk:kk + 1]
            lo = lo + lax.bitcast_convert_type(u << 16, F32) * wk
            hi = hi + lax.bitcast_convert_type(u & jnp.uint32(0xFFFF0000), F32) * wk
        routed = jnp.concatenate([lo, hi], axis=1)
        x2 = x1_ref[...] + gate * (routed + acc)
        if final:
            ms = jnp.mean(x2 * x2, axis=-1, keepdims=True)
            x2 = x2 * lax.rsqrt(ms + NORM_EPS) * fg_ref[...]
        o_ref[...] = x2

    _by_parity(i, finish)


def moe_combine(h, swg_bf, swu_bf, swd_bf, x1, mod, w_tok, final_g, dest_tiles, ys, final):
    t = h.shape[0]
    tg = mod.tile
    row = lambda i: (i, 0)
    const = lambda i: (0, 0)
    return pl.pallas_call(
        functools.partial(_combine_kernel, mod.npt, final),
        out_shape=jax.ShapeDtypeStruct((t, D_MODEL), F32),
        grid=(t // tg,),
        in_specs=[pl.BlockSpec((tg, h.shape[1]), row), pl.BlockSpec(swg_bf.shape, const),
                  pl.BlockSpec(swu_bf.shape, const),
                  pl.BlockSpec(swd_bf.shape, const), pl.BlockSpec((tg, D_MODEL), row)]
        + mod.specs(5)
        + [pl.BlockSpec((tg, TOP_K), row), pl.BlockSpec((1, D_MODEL), const),
           pl.BlockSpec(memory_space=pl.ANY), pl.BlockSpec(memory_space=pl.ANY)],
        out_specs=pl.BlockSpec((tg, D_MODEL), row),
        scratch_shapes=[pltpu.SMEM((tg * TOP_K,), I32), pltpu.SMEM((tg * TOP_K,), I32),
                        pltpu.VMEM((2, TOP_K, tg, D_MODEL // 2), U32),
                        pltpu.SemaphoreType.DMA((2,)), pltpu.SemaphoreType.DMA((2,))],
        compiler_params=_params(("arbitrary",)),
        name="moe_combine",
    )(h, swg_bf, swu_bf, swd_bf, x1, *mod.operands(5), w_tok, final_g.reshape(1, -1), dest_tiles, ys)


def _slot_kernel(eidx_ref, rank_ref, start_ref, o_ref):
    tm = eidx_ref.shape[1]
    tr = o_ref.shape[2]
    eid = lax.broadcasted_iota(I32, (N_EXPERTS, tm), 0)
    start = start_ref[...]
    for kk in range(TOP_K):
        base = jnp.sum(jnp.where(eid == eidx_ref[kk:kk + 1, :], start, 0.0), axis=0, keepdims=True)
        slot = base.astype(I32) + rank_ref[kk:kk + 1, :]
        for j in range(tm // tr):
            o_ref[j, kk:kk + 1, :] = slot[:, tr * j:tr * (j + 1)]


def assignment_slots(eidx_t, rank_t, pad_start):
    t = eidx_t.shape[1]
    tm = TOKEN_TILE
    tr = ROUTE_TILE
    col = lambda i: (0, i)
    out = pl.pallas_call(
        _slot_kernel,
        out_shape=jax.ShapeDtypeStruct((t // tr, TOP_K, tr), I32),
        grid=(t // tm,),
        in_specs=[pl.BlockSpec((TOP_K, tm), col), pl.BlockSpec((TOP_K, tm), col),
                  pl.BlockSpec((N_EXPERTS, 1), lambda i: (0, 0))],
        out_specs=pl.BlockSpec((tm // tr, TOP_K, tr), lambda i: (i, 0, 0)),
        compiler_params=_params(("parallel",)),
        name="assignment_slots",
    )(eidx_t, rank_t, pad_start.astype(F32).reshape(-1, 1))
    return out.reshape(t // tr, TOP_K * tr)


def _routing_tables(counts, n_blocks):
    rb = EXPERT_BLOCK
    counts = counts.astype(I32)
    padded = (counts + rb - 1) // rb * rb
    pad_end = jnp.cumsum(padded)
    pad_start = pad_end - padded
    n_used = pad_end[-1] // rb
    blocks = jnp.arange(n_blocks, dtype=I32)
    block_expert = jnp.sum((pad_end[None, :] <= (blocks * rb)[:, None]).astype(I32), axis=1)
    last_used = jnp.sum((pad_end <= (n_used - 1) * rb).astype(I32))
    block_expert = jnp.minimum(jnp.where(blocks < n_used, block_expert, last_used), N_EXPERTS - 1)
    zero_blocks = jnp.where(counts % rb != 0, pad_end // rb - 1, -1)
    zero_blocks = jnp.concatenate([zero_blocks, jnp.where(n_used % 2 == 1, n_used, -1).reshape(1)]).astype(I32)
    return pad_start, block_expert, n_used.reshape(1).astype(I32), zero_blocks


def moe_layer(h, x1, eidx_t, w_t, rank_t, counts, mod_route, wg, wu, wd, layer, swg_bf, swu_bf, swd_bf,
              final_g, final):
    t = h.shape[0]
    rb = EXPERT_BLOCK
    tr = ROUTE_TILE
    n_blocks = -(-(t * TOP_K) // rb) + N_EXPERTS
    n_blocks += n_blocks % 2
    pad_start, block_expert, n_used, zero_blocks = _routing_tables(counts, n_blocks)
    dest_tiles = assignment_slots(eidx_t, rank_t, pad_start)
    xs = moe_dispatch(h, dest_tiles, zero_blocks, n_blocks * rb)
    ys = moe_experts(xs, block_expert, n_used, wg, wu, wd, layer)
    return moe_combine(h, swg_bf, swu_bf, swd_bf, x1, mod_route, w_t.T, final_g, dest_tiles, ys, final)


def _rope_tables(n_batch, seq_len, n_seq, n_new):
    half = SWA_HEAD_DIM // 2
    inv = ROPE_THETA ** (-jnp.arange(half, dtype=F32) / half)
    pos = jnp.concatenate([jnp.tile(jnp.arange(seq_len, dtype=F32), n_batch),
                           jnp.tile(PAST_LEN + jnp.arange(n_new, dtype=F32), n_seq)])
    ang = pos[:, None] * inv[None, :]
    cos = jnp.tile(jnp.cos(ang), (1, 128 // half))
    sin = jnp.sin(ang)
    sin = jnp.tile(jnp.concatenate([-sin, sin], axis=1), (1, 128 // SWA_HEAD_DIM))
    return cos, sin


def kernel(x_prompt, x_sample, c_prompt, c_sample, cache_swa_k, cache_swa_v, state_gla, norm_mix_g, norm_ffn_g,
           final_g, ada_w, ada_b, swa_wqkv, swa_sinks, swa_wo, gla_win, gla_wa1, gla_wa2, gla_ba, gla_norm_g,
           gla_wo, moe_router, moe_bias, moe_wg, moe_wu, moe_wd, shared_wg, shared_wu, shared_wd):
    n_batch, seq_len, d = x_prompt.shape
    n_seq, n_new, _ = x_sample.shape
    depth = ada_w.shape[0]
    tp = n_batch * seq_len
    ts = n_seq * n_new
    t = tp + ts
    tm = TOKEN_TILE
    tr = ROUTE_TILE

    x = jnp.concatenate([x_prompt.reshape(tp, d), x_sample.reshape(ts, d)], axis=0)
    c_all = jnp.concatenate([jnp.repeat(c_sample, n_new, axis=0), c_prompt], axis=0)
    mod = ada_modulation(c_all, ada_w, ada_b)
    cos_tab, sin_tab = _rope_tables(n_batch, seq_len, n_seq, n_new)
    tri = jnp.triu(jnp.ones((tm, tm), BF16), k=1)

    new_k, new_v, new_s = [], [], []
    new_k_s, new_v_s, new_s_s = [], [], []
    for layer in range(depth):
        mod_tok = _Mod(mod, layer, n_batch, seq_len, tm)
        mod_route = _Mod(mod, layer, n_batch, seq_len, tr)
        m = layer // 2
        if layer % 2 == 0:
            q, k, v, k_dup, v_dup = swa_qkv(x, norm_mix_g[layer], mod_tok, swa_wqkv[m].astype(BF16), cos_tab, sin_tab)
            a_p = swa_prompt_attention(q, k_dup, v_dup, swa_sinks[m], n_batch, seq_len)
            nk = SWA_KV_HEADS * SWA_HEAD_DIM
            a_s, ck, cv = swa_sample_attention(q, k, v, cache_swa_k[m].reshape(n_seq, WINDOW, nk),
                                             cache_swa_v[m].reshape(n_seq, WINDOW, nk), swa_sinks[m], tp, n_new)
            kv_shape = (n_batch, WINDOW, SWA_KV_HEADS, SWA_HEAD_DIM)
            tails = [slice((b + 1) * seq_len - WINDOW, (b + 1) * seq_len) for b in range(n_batch)]
            new_k.append(jnp.stack([k[rows] for rows in tails]).reshape(kv_shape))
            new_v.append(jnp.stack([v[rows] for rows in tails]).reshape(kv_shape))
            new_k_s.append(ck.reshape(n_seq, WINDOW, SWA_KV_HEADS, SWA_HEAD_DIM))
            new_v_s.append(cv.reshape(n_seq, WINDOW, SWA_KV_HEADS, SWA_HEAD_DIM))
            wo = swa_wo[m]
        else:
            pad = jnp.zeros((d, 128 - GLA_GATE_RANK), F32)
            win_ext = jnp.concatenate([gla_win[m], gla_wa1[m], pad], axis=1).astype(BF16)
            wa2_pad = jnp.concatenate([gla_wa2[m], jnp.zeros((128 - GLA_GATE_RANK, GLA_KEY_DIM), F32)],
                                      axis=0).astype(BF16)
            q, k, v, r, gate = gla_project(x, norm_mix_g[layer], mod_tok, win_ext, wa2_pad, gla_ba[m])
            a_p, s_prompt = gla_prompt(q, k, gate, v, r, gla_norm_g[m], n_batch, seq_len)
            a_s, s_sample = gla_sample(q, k, gate, v, r, gla_norm_g[m], state_gla[m], tp, n_new)
            new_s.append(s_prompt)
            new_s_s.append(s_sample)
            wo = gla_wo[m]
        x1, h, eidx_t, w_t, rank_t, cnt = post_mixer(a_p, a_s, wo.astype(BF16), x, mod_tok, norm_ffn_g[layer],
                                                     moe_router[layer].T, moe_bias[layer], tri)
        x = moe_layer(h, x1, eidx_t, w_t, rank_t, cnt[:, 0], mod_route,
                      moe_wg, moe_wu, moe_wd, layer,
                      shared_wg[layer].astype(BF16), shared_wu[layer].astype(BF16), shared_wd[layer].astype(BF16),
                      final_g, layer == depth - 1)

    y_prompt = x[:tp].reshape(n_batch, seq_len, d)
    y_sample = x[tp:].reshape(n_seq, n_new, d)
    return (y_prompt, y_sample, jnp.stack(new_k), jnp.stack(new_v), jnp.stack(new_k_s), jnp.stack(new_v_s),
            jnp.stack(new_s), jnp.stack(new_s_s))
```

```python
import functools

import jax
import jax.numpy as jnp
from jax import lax
from jax.experimental import pallas as pl
from jax.experimental.pallas import tpu as pltpu

F32 = jnp.float32
BF16 = jnp.bfloat16
I32 = jnp.int32
U32 = jnp.uint32

D_MODEL = 1024
PAST_LEN = 8192
SWA_HEAD_DIM = 64
SWA_HEADS = 16
SWA_KV_HEADS = 4
SWA_GROUP = 4
WINDOW = 128
ROPE_THETA = 10000.0
GLA_HEADS = 4
GLA_DK = 128
GLA_DV = 256
GLA_KEY_DIM = 512
GLA_VAL_DIM = 1024
GLA_GATE_RANK = 16
GLA_GATE_NORMALIZER = 16.0
GLA_CHUNK = 64
N_EXPERTS = 64
TOP_K = 8
N_GROUPS = 8
TOPK_GROUPS = 4
EXPERT_FF = 256
ROUTED_SCALE = 2.5
NORM_EPS = 1e-6

TOKEN_TILE = 512
ROUTE_TILE = 256
EXPERT_BLOCK = 256
ADA_TILE = 512
SAMPLE_SEQS = 8
GLA_PROMPT_BATCH = 2
VMEM_LIMIT = 48 * 1024 * 1024

NT_DIMS = (((1,), (1,)), ((), ()))
TN_DIMS = (((0,), (0,)), ((), ()))


def _params(semantics):
    return pltpu.CompilerParams(dimension_semantics=semantics, vmem_limit_bytes=VMEM_LIMIT)


def _silu(x):
    return x * jax.nn.sigmoid(x)


def _norm_mod(x, g, sc, sh):
    ms = jnp.mean(x * x, axis=-1, keepdims=True)
    return (x * lax.rsqrt(ms + NORM_EPS) * g) * (1.0 + sc) + sh


def _pack_bf16_pairs(x):
    half = x.shape[1] // 2
    xb = x.astype(BF16).astype(F32)
    lo = lax.bitcast_convert_type(xb[:, :half], U32) >> 16
    hi = lax.bitcast_convert_type(xb[:, half:], U32) & jnp.uint32(0xFFFF0000)
    return lo | hi


def _unpack_bf16_pairs(u):
    lo = lax.bitcast_convert_type(u << 16, F32)
    hi = lax.bitcast_convert_type(u & jnp.uint32(0xFFFF0000), F32)
    return jnp.concatenate([lo, hi], axis=1).astype(BF16)


ROW_CHUNKS = D_MODEL // 2 // 128


def _store_row_slabs(ref, rows, words):
    for c in range(ROW_CHUNKS):
        ref[rows, c, :] = words[:, 128 * c:128 * (c + 1)]


def _load_row_slabs(ref, rows):
    return jnp.concatenate([ref[rows, c, :] for c in range(ROW_CHUNKS)], axis=1)


def _split3(x):
    x1 = x.astype(BF16)
    r1 = x - x1.astype(F32)
    x2 = r1.astype(BF16)
    x3 = (r1 - x2.astype(F32)).astype(BF16)
    return x1, x2, x3


def _ada_kernel(c_ref, w_ref, b_ref, o_ref):
    s = _silu(c_ref[...]).astype(BF16)
    o_ref[...] = jnp.dot(s, w_ref[...].astype(BF16), preferred_element_type=F32) + b_ref[...]


def ada_modulation(c_all, ada_w, ada_b):
    depth, d, n = ada_w.shape
    rows = c_all.shape[0]
    return pl.pallas_call(
        _ada_kernel,
        out_shape=jax.ShapeDtypeStruct((depth, rows, n), F32),
        grid=(depth, n // ADA_TILE),
        in_specs=[
            pl.BlockSpec((rows, d), lambda l, j: (0, 0)),
            pl.BlockSpec((None, d, ADA_TILE), lambda l, j: (l, 0, j)),
            pl.BlockSpec((None, 1, ADA_TILE), lambda l, j: (l, 0, j)),
        ],
        out_specs=pl.BlockSpec((None, rows, ADA_TILE), lambda l, j: (l, 0, j)),
        compiler_params=_params(("parallel", "parallel")),
        name="ada_modulation",
    )(c_all, ada_w, ada_b.reshape(depth, 1, n))


class _Mod:
    def __init__(self, mod, layer, n_batch, seq_len, tile):
        depth, rows, n = mod.shape
        self.tile = tile
        self.layer = layer
        self.npt = n_batch * seq_len // tile
        self.mod_tok = mod
        self.mod_seq = mod[:, rows - n_batch:].reshape(depth, n_batch, 1, n)
        self.tiles_per_seq = seq_len // tile

    def operands(self, chunk):
        del chunk
        return [self.mod_seq, self.mod_tok]

    def specs(self, chunk):
        l, npt, tps = self.layer, self.npt, self.tiles_per_seq
        n_seq = self.mod_seq.shape[1]
        seq_spec = pl.BlockSpec((None, None, 1, D_MODEL),
                                lambda i, *_: (l, jnp.minimum(i // tps, n_seq - 1), 0, chunk))
        tok_spec = pl.BlockSpec((None, self.tile, D_MODEL),
                                lambda i, *_: (l, jnp.maximum(i - npt, 0), chunk))
        return [seq_spec, tok_spec]


def _pick(is_sample, seq_ref, tok_ref):
    return jnp.where(is_sample, tok_ref[...], seq_ref[...])


def _swa_qkv_kernel(npt, x_ref, g_ref, scs_ref, sct_ref, shs_ref, sht_ref, w_ref, cos_ref, sin_ref,
                    q_ref, k_ref, v_ref, kd_ref, vd_ref):
    is_s = pl.program_id(0) >= npt
    h = _norm_mod(x_ref[...], g_ref[...], _pick(is_s, scs_ref, sct_ref), _pick(is_s, shs_ref, sht_ref))
    qkv = jnp.dot(h.astype(BF16), w_ref[...], preferred_element_type=F32)
    cos = cos_ref[...]
    sin = sin_ref[...]
    lane = lax.broadcasted_iota(I32, cos.shape, 1)
    first_half = (lane % SWA_HEAD_DIM) < (SWA_HEAD_DIM // 2)

    def rope(xc):
        rot = jnp.where(first_half, pltpu.roll(xc, 128 - SWA_HEAD_DIM // 2, 1), pltpu.roll(xc, SWA_HEAD_DIM // 2, 1))
        return xc * cos + rot * sin

    nq = SWA_HEADS * SWA_HEAD_DIM
    nk = SWA_KV_HEADS * SWA_HEAD_DIM
    for c in range(nq // 128):
        q_ref[:, 128 * c:128 * (c + 1)] = (rope(qkv[:, 128 * c:128 * (c + 1)]) * (SWA_HEAD_DIM ** -0.5)).astype(BF16)
    low = lane < SWA_HEAD_DIM

    def spread(chunk):
        rolled = pltpu.roll(chunk, SWA_HEAD_DIM, 1)
        return jnp.where(low, chunk, rolled).astype(BF16), jnp.where(low, rolled, chunk).astype(BF16)

    for c in range(nk // 128):
        kc = rope(qkv[:, nq + 128 * c:nq + 128 * (c + 1)])
        vc = qkv[:, nq + nk + 128 * c:nq + nk + 128 * (c + 1)]
        k_ref[:, 128 * c:128 * (c + 1)] = kc
        v_ref[:, 128 * c:128 * (c + 1)] = vc
        kd_ref[:, 256 * c:256 * c + 128], kd_ref[:, 256 * c + 128:256 * (c + 1)] = spread(kc)
        vd_ref[:, 256 * c:256 * c + 128], vd_ref[:, 256 * c + 128:256 * (c + 1)] = spread(vc)


def swa_qkv(x, g, mod, w_bf, cos_tab, sin_tab):
    t = x.shape[0]
    tm = mod.tile
    nq = SWA_HEADS * SWA_HEAD_DIM
    nk = SWA_KV_HEADS * SWA_HEAD_DIM
    row = lambda i: (i, 0)
    return pl.pallas_call(
        functools.partial(_swa_qkv_kernel, mod.npt),
        out_shape=(jax.ShapeDtypeStruct((t, nq), BF16), jax.ShapeDtypeStruct((t, nk), F32),
                   jax.ShapeDtypeStruct((t, nk), F32), jax.ShapeDtypeStruct((t, 2 * nk), BF16),
                   jax.ShapeDtypeStruct((t, 2 * nk), BF16)),
        grid=(t // tm,),
        in_specs=[pl.BlockSpec((tm, D_MODEL), row), pl.BlockSpec((1, D_MODEL), lambda i: (0, 0))]
        + mod.specs(1) + mod.specs(0)
        + [pl.BlockSpec(w_bf.shape, lambda i: (0, 0)), pl.BlockSpec((tm, 128), row), pl.BlockSpec((tm, 128), row)],
        out_specs=(pl.BlockSpec((tm, nq), row), pl.BlockSpec((tm, nk), row), pl.BlockSpec((tm, nk), row),
                   pl.BlockSpec((tm, 2 * nk), row), pl.BlockSpec((tm, 2 * nk), row)),
        compiler_params=_params(("parallel",)),
        name="swa_qkv",
    )(x, g.reshape(1, -1), *mod.operands(1), *mod.operands(0), w_bf, cos_tab, sin_tab)


def _sink_softmax(s, sink_col):
    m = jnp.maximum(jnp.max(s, axis=-1, keepdims=True), sink_col)
    e = jnp.exp(s - m)
    den = jnp.sum(e, axis=-1, keepdims=True) + jnp.exp(sink_col - m)
    return e * (1.0 / den)


def _pair_attention(sink_ref, q_ref, kblks, vblks, mask, o_ref):
    rows = q_ref.shape[0]
    half = mask.shape[1]
    per_group = SWA_GROUP // 2
    scores = []
    for g in range(SWA_KV_HEADS):
        q2 = jnp.concatenate([q_ref[:, 128 * c:128 * (c + 1)] for c in range(per_group * g, per_group * (g + 1))],
                             axis=0)
        s = lax.dot_general(q2, kblks[g], NT_DIMS, preferred_element_type=F32)
        for j in range(per_group):
            scores += [s[rows * j:rows * (j + 1), :half], s[rows * j:rows * (j + 1), half:]]
    s_all = jnp.where(mask, jnp.stack(scores), -jnp.inf)
    sinks = jnp.stack([jnp.full((1, 1), sink_ref[h], F32) for h in range(SWA_HEADS)])
    p_all = _sink_softmax(s_all, sinks).astype(BF16)
    for g in range(SWA_KV_HEADS):
        chunks = range(per_group * g, per_group * (g + 1))
        p2 = jnp.concatenate([jnp.concatenate([p_all[2 * c], p_all[2 * c + 1]], axis=1) for c in chunks], axis=0)
        o = jnp.dot(p2, vblks[g], preferred_element_type=F32).astype(BF16)
        for j, c in enumerate(chunks):
            o_ref[:, 128 * c:128 * (c + 1)] = o[rows * j:rows * (j + 1)]


def _swa_prompt_kernel(sink_ref, q_ref, kc_ref, kp_ref, vc_ref, vp_ref, o_ref):
    j = pl.program_id(1)
    blk = q_ref.shape[0]
    qi = lax.broadcasted_iota(I32, (blk, 2 * blk), 0)
    sj = lax.broadcasted_iota(I32, (blk, 2 * blk), 1)
    rel = qi + blk - sj
    mask = (rel >= 0) & (rel <= WINDOW) & ((sj >= blk) | (j > 0))
    low = lax.broadcasted_iota(I32, (2 * blk, 128), 1) < SWA_HEAD_DIM
    zero = jnp.zeros((2 * blk, 128), BF16)
    kblks, vblks = [], []
    for g in range(SWA_KV_HEADS):
        cs = slice(128 * g, 128 * (g + 1))
        kcat = jnp.concatenate([kp_ref[:, cs], kc_ref[:, cs]], axis=0)
        vcat = jnp.concatenate([vp_ref[:, cs], vc_ref[:, cs]], axis=0)
        kblks.append(jnp.concatenate([jnp.where(low, kcat, zero), jnp.where(low, zero, kcat)], axis=0))
        vblks.append(jnp.concatenate([jnp.where(low, vcat, zero), jnp.where(low, zero, vcat)], axis=0))
    _pair_attention(sink_ref, q_ref, kblks, vblks, mask, o_ref)


def swa_prompt_attention(q, k, v, sinks, n_batch, seq_len):
    blk = WINDOW
    nb = seq_len // blk
    nq = q.shape[1]
    nk = k.shape[1]
    cur = lambda b, j: (b * nb + j, 0)
    prev = lambda b, j: (b * nb + jnp.maximum(j - 1, 0), 0)
    return pl.pallas_call(
        _swa_prompt_kernel,
        out_shape=jax.ShapeDtypeStruct((n_batch * seq_len, nq), BF16),
        grid=(n_batch, nb),
        in_specs=[pl.BlockSpec(memory_space=pltpu.SMEM),
                  pl.BlockSpec((blk, nq), cur),
                  pl.BlockSpec((blk, nk), cur), pl.BlockSpec((blk, nk), prev),
                  pl.BlockSpec((blk, nk), cur), pl.BlockSpec((blk, nk), prev)],
        out_specs=pl.BlockSpec((blk, nq), cur),
        compiler_params=_params(("parallel", "parallel")),
        name="swa_prompt_attention",
    )(sinks, q, k, k, v, v)


def _swa_sample_kernel(n_new, sink_ref, q_ref, kn_ref, vn_ref, ck_ref, cv_ref, o_ref, nk_ref, nv_ref):
    n_sb, win, _ = ck_ref.shape
    per_seq = win + n_new
    rows = n_sb * n_new
    cols = n_sb * per_seq
    keys, vals = [], []
    for sb in range(n_sb):
        r0 = sb * n_new
        kc = ck_ref[sb]
        vc = cv_ref[sb]
        kn = kn_ref[r0:r0 + n_new, :]
        vn = vn_ref[r0:r0 + n_new, :]
        nk_ref[sb, 0:win - n_new, :] = kc[n_new:]
        nk_ref[sb, win - n_new:win, :] = kn
        nv_ref[sb, 0:win - n_new, :] = vc[n_new:]
        nv_ref[sb, win - n_new:win, :] = vn
        keys += [kc, kn]
        vals += [vc, vn]
    keys = jnp.concatenate(keys, axis=0)
    vals = jnp.concatenate(vals, axis=0)
    ri = lax.broadcasted_iota(I32, (rows, cols), 0)
    ci = lax.broadcasted_iota(I32, (rows, cols), 1)
    ti = ri % n_new
    si = ci % per_seq
    mask = (ri // n_new == ci // per_seq) & (si >= ti) & (si <= ti + WINDOW)
    low = lax.broadcasted_iota(I32, (cols, 128), 1) < SWA_HEAD_DIM
    zero = jnp.zeros((cols, 128), BF16)

    def block_diag(chunk, first):
        rolled = pltpu.roll(chunk, SWA_HEAD_DIM, 1)
        both = (jnp.where(low, chunk, rolled) if first else jnp.where(low, rolled, chunk)).astype(BF16)
        return jnp.concatenate([jnp.where(low, both, zero), jnp.where(low, zero, both)], axis=0)

    chunks = [slice(128 * (g // 2), 128 * (g // 2 + 1)) for g in range(SWA_KV_HEADS)]
    kblks = [block_diag(keys[:, cs], g % 2 == 0) for g, cs in enumerate(chunks)]
    vblks = [block_diag(vals[:, cs], g % 2 == 0) for g, cs in enumerate(chunks)]
    _pair_attention(sink_ref, q_ref, kblks, vblks, mask, o_ref)


def swa_sample_attention(q, k, v, cache_k, cache_v, sinks, n_prompt_rows, n_new):
    n_seq, win, nk = cache_k.shape
    sb = SAMPLE_SEQS
    rows = sb * n_new
    nq = q.shape[1]
    base = n_prompt_rows // rows
    tok = lambda i: (base + i, 0)
    seq = lambda i: (i, 0, 0)
    return pl.pallas_call(
        functools.partial(_swa_sample_kernel, n_new),
        out_shape=(jax.ShapeDtypeStruct((n_seq * n_new, nq), BF16),
                   jax.ShapeDtypeStruct(cache_k.shape, F32), jax.ShapeDtypeStruct(cache_v.shape, F32)),
        grid=(n_seq // sb,),
        in_specs=[pl.BlockSpec(memory_space=pltpu.SMEM),
                  pl.BlockSpec((rows, nq), tok), pl.BlockSpec((rows, nk), tok), pl.BlockSpec((rows, nk), tok),
                  pl.BlockSpec((sb, win, nk), seq), pl.BlockSpec((sb, win, nk), seq)],
        out_specs=(pl.BlockSpec((rows, nq), lambda i: (i, 0)), pl.BlockSpec((sb, win, nk), seq),
                   pl.BlockSpec((sb, win, nk), seq)),
        compiler_params=_params(("parallel",)),
        name="swa_sample_attention",
    )(sinks, q, k, v, cache_k, cache_v)


def _gla_proj_kernel(npt, x_ref, g_ref, scs_ref, sct_ref, shs_ref, sht_ref, w_ref, wa2_ref, ba_ref,
                     q_ref, k_ref, v_ref, r_ref, gate_ref):
    is_s = pl.program_id(0) >= npt
    h = _norm_mod(x_ref[...], g_ref[...], _pick(is_s, scs_ref, sct_ref), _pick(is_s, shs_ref, sht_ref))
    proj = jnp.dot(h.astype(BF16), w_ref[...], preferred_element_type=F32)
    kd = GLA_KEY_DIM
    vd = GLA_VAL_DIM
    q_ref[...] = proj[:, :kd] * (GLA_DK ** -0.5)
    k_ref[...] = proj[:, kd:2 * kd]
    v_ref[...] = proj[:, 2 * kd:2 * kd + vd].astype(BF16)
    r_ref[...] = proj[:, 2 * kd + vd:2 * kd + 2 * vd]
    low = proj[:, 2 * kd + 2 * vd:].astype(BF16)
    z = jnp.dot(low, wa2_ref[...], preferred_element_type=F32) + ba_ref[...]
    log_sig = jnp.minimum(z, 0.0) - jnp.log1p(jnp.exp(-jnp.abs(z)))
    gate_ref[...] = log_sig / GLA_GATE_NORMALIZER


def gla_project(x, g, mod, win_ext, wa2_pad, ba):
    t = x.shape[0]
    tm = mod.tile
    kd, vd = GLA_KEY_DIM, GLA_VAL_DIM
    row = lambda i: (i, 0)
    const = lambda i: (0, 0)
    return pl.pallas_call(
        functools.partial(_gla_proj_kernel, mod.npt),
        out_shape=(jax.ShapeDtypeStruct((t, kd), F32), jax.ShapeDtypeStruct((t, kd), F32),
                   jax.ShapeDtypeStruct((t, vd), BF16), jax.ShapeDtypeStruct((t, vd), F32),
                   jax.ShapeDtypeStruct((t, kd), F32)),
        grid=(t // tm,),
        in_specs=[pl.BlockSpec((tm, D_MODEL), row), pl.BlockSpec((1, D_MODEL), const)]
        + mod.specs(1) + mod.specs(0)
        + [pl.BlockSpec(win_ext.shape, const), pl.BlockSpec(wa2_pad.shape, const), pl.BlockSpec((1, kd), const)],
        out_specs=(pl.BlockSpec((tm, kd), row), pl.BlockSpec((tm, kd), row), pl.BlockSpec((tm, vd), row),
                   pl.BlockSpec((tm, vd), row), pl.BlockSpec((tm, kd), row)),
        compiler_params=_params(("parallel",)),
        name="gla_project",
    )(x, g.reshape(1, -1), *mod.operands(1), *mod.operands(0), win_ext, wa2_pad, ba.reshape(1, -1))


def _cumsum_rows(tri, g):
    n = g.shape[1]
    s = jnp.dot(tri, jnp.concatenate(_split3(g), axis=1), preferred_element_type=F32)
    return s[:, :n] + s[:, n:2 * n] + s[:, 2 * n:]


def _diag_attention(q, k, b, n):
    ng = n // 8
    dk = q.shape[1]
    q3 = q.reshape(ng, 8, dk)
    k3 = k.reshape(ng, 8, dk)
    b3 = b.reshape(ng, 8, dk)
    sub = lax.broadcasted_iota(I32, (ng, 8, dk), 1)
    ti = lax.broadcasted_iota(I32, (n, n), 0)
    si = lax.broadcasted_iota(I32, (n, n), 1)
    attn = jnp.zeros((n, n), F32)
    for j in range(8):
        bj = jnp.broadcast_to(b3[:, j:j + 1, :], b3.shape)
        kj = jnp.broadcast_to(k3[:, j:j + 1, :], k3.shape)
        e = jnp.exp(jnp.minimum(b3 - bj, 0.0))
        m = jnp.where(sub >= j, q3 * e * kj, 0.0)
        col = jnp.sum(m, axis=-1, keepdims=True).reshape(n, 1)
        attn = attn + jnp.where(si == (ti // 8) * 8 + j, col, 0.0)
    return attn


def _cross_attention(q, k, b, n):
    ti = lax.broadcasted_iota(I32, (n, n), 0)
    si = lax.broadcasted_iota(I32, (n, n), 1)
    row = lax.broadcasted_iota(I32, b.shape, 0)
    attn = jnp.zeros((n, n), F32)
    m = n // 2
    while m >= 8:
        nblk = n // m
        refq = jnp.concatenate(
            [jnp.broadcast_to(b[i * m - 1:i * m], (m, b.shape[1])) if i % 2 else b[i * m:(i + 1) * m]
             for i in range(nblk)], axis=0)
        refk = jnp.concatenate(
            [b[i * m:(i + 1) * m] if i % 2 else jnp.broadcast_to(b[(i + 1) * m - 1:(i + 1) * m], (m, b.shape[1]))
             for i in range(nblk)], axis=0)
        odd = ((row // m) % 2) == 1
        qt = jnp.where(odd, q * jnp.exp(jnp.minimum(b - refq, 0.0)), 0.0).astype(BF16)
        kt = jnp.where(odd, 0.0, k * jnp.exp(jnp.minimum(refk - b, 0.0))).astype(BF16)
        a = lax.dot_general(qt, kt, NT_DIMS, preferred_element_type=F32)
        keep = (((ti // m) % 2) == 1) & ((si // m) == (ti // m) - 1)
        attn = attn + jnp.where(keep, a, 0.0)
        m //= 2
    return attn


def _gla_epilogue(o, r, ng):
    ms = jnp.mean(o * o, axis=-1, keepdims=True)
    return (o * lax.rsqrt(ms + NORM_EPS) * ng * _silu(r)).astype(BF16)


def _gla_prompt_kernel(*refs):
    nb = GLA_PROMPT_BATCH
    ins, (ng_ref, o_ref, so_ref, st_ref) = refs[:5 * nb], refs[5 * nb:]
    c = pl.program_id(1)
    n = ins[0].shape[0]

    @pl.when(c == 0)
    def _():
        st_ref[...] = jnp.zeros(st_ref.shape, F32)

    ti = lax.broadcasted_iota(I32, (n, n), 0)
    si = lax.broadcasted_iota(I32, (n, n), 1)
    tri = jnp.where(ti >= si, 1.0, 0.0).astype(BF16)
    for i in range(nb):
        q_ref, k_ref, g_ref, v_ref, r_ref = ins[5 * i:5 * (i + 1)]
        b_all = _cumsum_rows(tri, g_ref[...])
        for h in range(GLA_HEADS):
            ks = slice(GLA_DK * h, GLA_DK * (h + 1))
            vs = slice(GLA_DV * h, GLA_DV * (h + 1))
            q = q_ref[:, ks]
            k = k_ref[:, ks]
            v = v_ref[:, vs]
            b = b_all[:, ks]
            s_t = st_ref[i, h]
            o = lax.dot_general((q * jnp.exp(b)).astype(BF16), s_t.astype(BF16), NT_DIMS,
                                preferred_element_type=F32)
            attn = _cross_attention(q, k, b, n) + _diag_attention(q, k, b, n)
            o = o + jnp.dot(attn.astype(BF16), v, preferred_element_type=F32)
            bl = b[n - 1:n, :]
            kd = (k * jnp.exp(bl - b)).astype(BF16)
            s_new = s_t * jnp.exp(bl) + lax.dot_general(v, kd, TN_DIMS, preferred_element_type=F32)
            st_ref[i, h] = s_new
            o_ref[i, :, vs] = _gla_epilogue(o, r_ref[:, vs], ng_ref[...])

    @pl.when(c == pl.num_programs(1) - 1)
    def _():
        for i in range(nb):
            for h in range(GLA_HEADS):
                so_ref[i, h] = st_ref[i, h].T


def gla_prompt(q, k, g, v, r, norm_g, n_batch, seq_len):
    n = GLA_CHUNK
    nb = GLA_PROMPT_BATCH
    nc = seq_len // n
    kd, vd = GLA_KEY_DIM, GLA_VAL_DIM
    in_specs, operands = [], []
    for i in range(nb):
        row = lambda b, c, i=i: ((nb * b + i) * nc + c, 0)
        in_specs += [pl.BlockSpec((n, kd), row), pl.BlockSpec((n, kd), row), pl.BlockSpec((n, kd), row),
                     pl.BlockSpec((n, vd), row), pl.BlockSpec((n, vd), row)]
        operands += [q, k, g, v, r]
    o, state = pl.pallas_call(
        _gla_prompt_kernel,
        out_shape=(jax.ShapeDtypeStruct((n_batch, seq_len, vd), BF16),
                   jax.ShapeDtypeStruct((n_batch, GLA_HEADS, GLA_DK, GLA_DV), F32)),
        grid=(n_batch // nb, nc),
        in_specs=in_specs + [pl.BlockSpec((1, GLA_DV), lambda b, c: (0, 0))],
        out_specs=(pl.BlockSpec((nb, n, vd), lambda b, c: (b, c, 0)),
                   pl.BlockSpec((nb, GLA_HEADS, GLA_DK, GLA_DV), lambda b, c: (b, 0, 0, 0))),
        scratch_shapes=[pltpu.VMEM((nb, GLA_HEADS, GLA_DV, GLA_DK), F32)],
        compiler_params=_params(("parallel", "arbitrary")),
        name="gla_prompt",
    )(*operands, norm_g.reshape(1, -1))
    return o.reshape(n_batch * seq_len, vd), state


def _gla_sample_kernel(n_new, q_ref, k_ref, g_ref, v_ref, r_ref, ng_ref, si_ref, o_ref, so_ref):
    n = q_ref.shape[0]
    ti = lax.broadcasted_iota(I32, (n, n), 0)
    si = lax.broadcasted_iota(I32, (n, n), 1)
    tri = jnp.where((ti >= si) & (ti // n_new == si // n_new), 1.0, 0.0).astype(BF16)
    b_all = _cumsum_rows(tri, g_ref[...])
    for h in range(GLA_HEADS):
        ks = slice(GLA_DK * h, GLA_DK * (h + 1))
        vs = slice(GLA_DV * h, GLA_DV * (h + 1))
        q = q_ref[:, ks]
        k = k_ref[:, ks]
        v = v_ref[:, vs]
        b = b_all[:, ks]
        attn = _diag_attention(q, k, b, n)
        o_intra = jnp.dot(attn.astype(BF16), v, preferred_element_type=F32)
        qe = (q * jnp.exp(b)).astype(BF16)
        n_sb = n // n_new
        last = [b[n_new * (sb + 1) - 1:n_new * (sb + 1), :] for sb in range(n_sb)]
        bl_rows = jnp.concatenate([jnp.broadcast_to(bl, (n_new, GLA_DK)) for bl in last], axis=0)
        kd = (k * jnp.exp(bl_rows - b)).astype(BF16)
        seq_of_vrow = lax.broadcasted_iota(I32, (n, GLA_DV), 0) // n_new
        seq_of_krow = lax.broadcasted_iota(I32, (n, GLA_DK), 0) // n_new
        o = o_intra
        for sb in range(n_sb):
            s0 = si_ref[sb, h]
            o_sb = jnp.dot(qe, s0.astype(BF16), preferred_element_type=F32)
            o = o + jnp.where(seq_of_vrow == sb, o_sb, 0.0)
            kd_sb = jnp.where(seq_of_krow == sb, kd, jnp.zeros_like(kd))
            upd = lax.dot_general(kd_sb, v, TN_DIMS, preferred_element_type=F32)
            decay_col = jnp.transpose(jnp.broadcast_to(jnp.exp(last[sb]), (8, GLA_DK)))[:, 0:1]
            so_ref[sb, h] = s0 * decay_col + upd
        o_ref[:, vs] = _gla_epilogue(o, r_ref[:, vs], ng_ref[...])


def gla_sample(q, k, g, v, r, norm_g, state, n_prompt_rows, n_new):
    n_seq = state.shape[0]
    sb = SAMPLE_SEQS
    rows = sb * n_new
    kd, vd = GLA_KEY_DIM, GLA_VAL_DIM
    base = n_prompt_rows // rows
    tok = lambda i: (base + i, 0)
    seq = lambda i: (i, 0, 0, 0)
    sblock = (sb, GLA_HEADS, GLA_DK, GLA_DV)
    return pl.pallas_call(
        functools.partial(_gla_sample_kernel, n_new),
        out_shape=(jax.ShapeDtypeStruct((n_seq * n_new, vd), BF16), jax.ShapeDtypeStruct(state.shape, F32)),
        grid=(n_seq // sb,),
        in_specs=[pl.BlockSpec((rows, kd), tok), pl.BlockSpec((rows, kd), tok), pl.BlockSpec((rows, kd), tok),
                  pl.BlockSpec((rows, vd), tok), pl.BlockSpec((rows, vd), tok),
                  pl.BlockSpec((1, GLA_DV), lambda i: (0, 0)),
                  pl.BlockSpec(sblock, seq)],
        out_specs=(pl.BlockSpec((rows, vd), lambda i: (i, 0)), pl.BlockSpec(sblock, seq)),
        compiler_params=_params(("parallel",)),
        name="gla_sample",
    )(q, k, g, v, r, norm_g.reshape(1, -1), state)


def _post_mixer_kernel(npt, ap_ref, as_ref, wo_ref, x_ref, g1s_ref, g1t_ref, gf_ref, scs_ref, sct_ref, shs_ref, sht_ref,
                       rw_ref, rb_ref, tri_ref,
                       x1_ref, h_ref, eidx_ref, w_ref, rank_ref, cnt_ref, carry_ref):
    i = pl.program_id(0)
    is_s = i >= npt

    @pl.when(i == 0)
    def _():
        carry_ref[...] = jnp.zeros(carry_ref.shape, F32)

    a = jnp.where(is_s, as_ref[...], ap_ref[...])
    x1 = x_ref[...] + _pick(is_s, g1s_ref, g1t_ref) * jnp.dot(a, wo_ref[...], preferred_element_type=F32)
    x1_ref[...] = x1
    h = _norm_mod(x1, gf_ref[...], _pick(is_s, scs_ref, sct_ref), _pick(is_s, shs_ref, sht_ref))
    h_ref[...] = _pack_bf16_pairs(h)

    h1, h2, _ = _split3(h)
    r1, r2, _ = _split3(rw_ref[...])
    logits = (lax.dot_general(r1, h1, NT_DIMS, preferred_element_type=F32)
              + lax.dot_general(r1, h2, NT_DIMS, preferred_element_type=F32)
              + lax.dot_general(r2, h1, NT_DIMS, preferred_element_type=F32))
    scores = jax.nn.sigmoid(logits)
    sel = scores + rb_ref[...]
    tm = sel.shape[1]
    gsz = N_EXPERTS // N_GROUPS

    sub = lax.broadcasted_iota(I32, (gsz, tm), 0)
    blocks, gscore = [], []
    for g in range(N_GROUPS):
        blk = sel[gsz * g:gsz * (g + 1)]
        m1 = jnp.max(blk, axis=0, keepdims=True)
        first = jnp.min(jnp.where(blk == m1, sub, gsz), axis=0, keepdims=True)
        m2 = jnp.max(jnp.where(sub == first, -jnp.inf, blk), axis=0, keepdims=True)
        blocks.append(blk)
        gscore.append(m1 + m2)
    masked = []
    for g in range(N_GROUPS):
        beaten = jnp.zeros((1, tm), I32)
        for o in range(N_GROUPS):
            if o == g:
                continue
            wins = (gscore[o] > gscore[g]) | ((gscore[o] == gscore[g]) & (o < g))
            beaten = beaten + wins.astype(I32)
        masked.append(jnp.where(beaten < TOPK_GROUPS, blocks[g], -jnp.inf))
    cur = jnp.concatenate(masked, axis=0)

    eid = lax.broadcasted_iota(I32, (N_EXPERTS, tm), 0)
    picked, weights = [], []
    onehot = jnp.zeros((N_EXPERTS, tm), F32)
    for _ in range(TOP_K):
        m = jnp.max(cur, axis=0, keepdims=True)
        idx = jnp.min(jnp.where(cur == m, eid, N_EXPERTS), axis=0, keepdims=True)
        hit = eid == idx
        picked.append(idx)
        weights.append(jnp.sum(jnp.where(hit, scores, 0.0), axis=0, keepdims=True))
        onehot = jnp.where(hit, 1.0, onehot)
        cur = jnp.where(hit, -jnp.inf, cur)
    wsum = weights[0]
    for wk in weights[1:]:
        wsum = wsum + wk
    scale = ROUTED_SCALE / wsum

    before = jnp.dot(onehot.astype(BF16), tri_ref[...], preferred_element_type=F32) + carry_ref[...]
    carry = carry_ref[...] + jnp.sum(onehot, axis=1, keepdims=True)
    carry_ref[...] = carry
    cnt_ref[...] = jnp.broadcast_to(carry, cnt_ref.shape)
    for kk in range(TOP_K):
        eidx_ref[kk:kk + 1, :] = picked[kk]
        w_ref[kk:kk + 1, :] = weights[kk] * scale
        rank_ref[kk:kk + 1, :] = jnp.sum(jnp.where(eid == picked[kk], before, 0.0), axis=0, keepdims=True).astype(I32)


def post_mixer(a_prompt, a_sample, wo_bf, x, mod, gffn, router_t, router_b, tri):
    t = x.shape[0]
    tm = mod.tile
    npt = mod.npt
    row = lambda i: (i, 0)
    col = lambda i: (0, i)
    const = lambda i: (0, 0)
    return pl.pallas_call(
        functools.partial(_post_mixer_kernel, mod.npt),
        out_shape=(jax.ShapeDtypeStruct((t, D_MODEL), F32), jax.ShapeDtypeStruct((t, D_MODEL // 2), U32),
                   jax.ShapeDtypeStruct((TOP_K, t), I32), jax.ShapeDtypeStruct((TOP_K, t), F32),
                   jax.ShapeDtypeStruct((TOP_K, t), I32), jax.ShapeDtypeStruct((N_EXPERTS, 128), F32)),
        grid=(t // tm,),
        in_specs=[pl.BlockSpec((tm, D_MODEL), lambda i: (jnp.minimum(i, npt - 1), 0)),
                  pl.BlockSpec((tm, D_MODEL), lambda i: (jnp.maximum(i - npt, 0), 0)),
                  pl.BlockSpec((D_MODEL, D_MODEL), const), pl.BlockSpec((tm, D_MODEL), row)]
        + mod.specs(2) + [pl.BlockSpec((1, D_MODEL), const)] + mod.specs(4) + mod.specs(3)
        + [pl.BlockSpec((N_EXPERTS, D_MODEL), const), pl.BlockSpec((N_EXPERTS, 1), const),
           pl.BlockSpec((tm, tm), const)],
        out_specs=(pl.BlockSpec((tm, D_MODEL), row), pl.BlockSpec((tm, D_MODEL // 2), row),
                   pl.BlockSpec((TOP_K, tm), col), pl.BlockSpec((TOP_K, tm), col), pl.BlockSpec((TOP_K, tm), col),
                   pl.BlockSpec((N_EXPERTS, 128), const)),
        scratch_shapes=[pltpu.VMEM((N_EXPERTS, 1), F32)],
        compiler_params=_params(("arbitrary",)),
        name="post_mixer",
    )(a_prompt, a_sample, wo_bf, x, *mod.operands(2), gffn.reshape(1, -1), *mod.operands(4), *mod.operands(3),
      router_t, router_b.reshape(-1, 1), tri)


def _row_copy(src, src_row, dst, dst_row, sem):
    return pltpu.make_async_copy(src.at[src_row], dst.at[dst_row], sem)


def _by_parity(i, fn):
    @pl.when(i % 2 == 0)
    def _():
        fn(0)

    @pl.when(i % 2 == 1)
    def _():
        fn(1)


def _dispatch_kernel(zb_ref, h_ref, dest_hbm, xs_hbm, idx_a, idx_b, zero_ref, slab_ref, sem_idx, sem_zero, sem_rows):
    i = pl.program_id(0)
    te = h_ref.shape[0]
    _store_row_slabs(slab_ref, slice(None), h_ref[...])
    idx_bufs = (idx_a, idx_b)

    def idx_copy(tile, p):
        return pltpu.make_async_copy(dest_hbm.at[tile], idx_bufs[p], sem_idx.at[p])

    def zero_copy(e):
        return pltpu.make_async_copy(zero_ref, xs_hbm.at[pl.ds(zb_ref[e] * EXPERT_BLOCK, EXPERT_BLOCK)], sem_zero)

    @pl.when(i == 0)
    def _():
        idx_copy(0, 0).start()
        zero_ref[...] = jnp.zeros(zero_ref.shape, U32)

        def start(e, carry):
            @pl.when(zb_ref[e] >= 0)
            def _():
                zero_copy(e).start()
            return carry

        def wait(e, carry):
            @pl.when(zb_ref[e] >= 0)
            def _():
                zero_copy(e).wait()
            return carry

        lax.fori_loop(0, zb_ref.shape[0], start, 0)
        lax.fori_loop(0, zb_ref.shape[0], wait, 0)

    def step(p):
        @pl.when(i + 1 < pl.num_programs(0))
        def _():
            idx_copy(i + 1, 1 - p).start()

        idx_copy(i, p).wait()
        idx = idx_bufs[p]

        def issue(t, carry):
            for kk in range(TOP_K):
                _row_copy(slab_ref, t, xs_hbm, idx[kk * te + t], sem_rows).start(priority=kk % 2)
            return carry

        def drain(t, carry):
            for kk in range(TOP_K):
                _row_copy(slab_ref, t, xs_hbm, idx[kk * te + t], sem_rows).wait()
            return carry

        lax.fori_loop(0, te, issue, 0)
        lax.fori_loop(0, te, drain, 0)

    _by_parity(i, step)


def moe_dispatch(h, dest_tiles, zero_blocks, n_slots):
    t = h.shape[0]
    te = ROUTE_TILE
    return pl.pallas_call(
        _dispatch_kernel,
        out_shape=jax.ShapeDtypeStruct((n_slots, ROW_CHUNKS, 128), U32),
        grid_spec=pltpu.PrefetchScalarGridSpec(
            num_scalar_prefetch=1,
            grid=(t // te,),
            in_specs=[pl.BlockSpec((te, h.shape[1]), lambda i, zb: (i, 0)), pl.BlockSpec(memory_space=pl.ANY)],
            out_specs=pl.BlockSpec(memory_space=pl.ANY),
            scratch_shapes=[pltpu.SMEM((te * TOP_K,), I32), pltpu.SMEM((te * TOP_K,), I32),
                            pltpu.VMEM((EXPERT_BLOCK, ROW_CHUNKS, 128), U32), pltpu.VMEM((te, ROW_CHUNKS, 128), U32),
                            pltpu.SemaphoreType.DMA((2,)), pltpu.SemaphoreType.DMA, pltpu.SemaphoreType.DMA],
        ),
        compiler_params=_params(("arbitrary",)),
        name="moe_dispatch",
    )(zero_blocks, h, dest_tiles)


def _expert_kernel(be_ref, nu_ref, xs_ref, wg0_ref, wu0_ref, wd0_ref, wg1_ref, wu1_ref, wd1_ref, ys_ref,
                   wg_bf, wu_bf, wd_bf):
    b = pl.program_id(0)
    rb = EXPERT_BLOCK
    used = 2 * b < nu_ref[0]
    for s, (wg, wu, wd) in enumerate(((wg0_ref, wu0_ref, wd0_ref), (wg1_ref, wu1_ref, wd1_ref))):
        j = 2 * b + s
        fresh = (b == 0) | (be_ref[j] != be_ref[jnp.maximum(j - 2, 0)])

        @pl.when(used & fresh)
        def _():
            wg_bf[s] = wg[...].astype(BF16)
            wu_bf[s] = wu[...].astype(BF16)
            wd_bf[s] = wd[...].astype(BF16)

    @pl.when(used)
    def _():
        for s in range(2):
            rows = slice(rb * s, rb * (s + 1))
            x = _unpack_bf16_pairs(_load_row_slabs(xs_ref, rows))
            hg = jnp.dot(x, wg_bf[s], preferred_element_type=F32)
            hu = jnp.dot(x, wu_bf[s], preferred_element_type=F32)
            y = jnp.dot((_silu(hg) * hu).astype(BF16), wd_bf[s], preferred_element_type=F32)
            ys_ref[rows, :] = _pack_bf16_pairs(y)


def moe_experts(xs, block_expert, n_used, wg, wu, wd, layer):
    n_slots = xs.shape[0]
    rb = EXPERT_BLOCK
    ff = wg.shape[3]
    rows_in = lambda b, be, nu: (jnp.minimum(b, (nu[0] - 1) // 2), 0, 0)
    rows_out = lambda b, be, nu: (jnp.minimum(b, (nu[0] - 1) // 2), 0)
    w_in = lambda s: (lambda b, be, nu: (layer, be[2 * b + s], 0, 0))
    return pl.pallas_call(
        _expert_kernel,
        out_shape=jax.ShapeDtypeStruct((n_slots, D_MODEL // 2), U32),
        grid_spec=pltpu.PrefetchScalarGridSpec(
            num_scalar_prefetch=2,
            grid=(n_slots // (2 * rb),),
            in_specs=[pl.BlockSpec((2 * rb,) + xs.shape[1:], rows_in)]
            + [pl.BlockSpec((None, None, D_MODEL, ff), w_in(0)), pl.BlockSpec((None, None, D_MODEL, ff), w_in(0)),
               pl.BlockSpec((None, None, ff, D_MODEL), w_in(0)),
               pl.BlockSpec((None, None, D_MODEL, ff), w_in(1)), pl.BlockSpec((None, None, D_MODEL, ff), w_in(1)),
               pl.BlockSpec((None, None, ff, D_MODEL), w_in(1))],
            out_specs=pl.BlockSpec((2 * rb, D_MODEL // 2), rows_out),
            scratch_shapes=[pltpu.VMEM((2, D_MODEL, ff), BF16), pltpu.VMEM((2, D_MODEL, ff), BF16),
                            pltpu.VMEM((2, ff, D_MODEL), BF16)],
        ),
        compiler_params=_params(("arbitrary",)),
        name="moe_experts",
    )(block_expert, n_used, xs, wg, wu, wd, wg, wu, wd)


def _combine_kernel(npt, final, h_ref, swg_ref, swu_ref, swd_ref, x1_ref, g2s_ref, g2t_ref, w_ref, fg_ref,
                    dest_hbm, ys_hbm, *rest):
    out_refs, (idx_a, idx_b, ybuf_ref, sem_idx, sem_rows) = rest[:-5], rest[-5:]
    i = pl.program_id(0)
    n = pl.num_programs(0)
    tg = h_ref.shape[0]
    idx_bufs = (idx_a, idx_b)

    def idx_copy(tile, p):
        return pltpu.make_async_copy(dest_hbm.at[tile], idx_bufs[p], sem_idx.at[p])

    def gather_rows(p, wait):
        idx = idx_bufs[p]

        def body(t, carry):
            for kk in range(TOP_K):
                src = 0 if wait else idx[kk * tg + t]
                dst = ybuf_ref.at[p, kk]
                cp = pltpu.make_async_copy(ys_hbm.at[pl.ds(src, 1)], dst.at[pl.ds(t, 1)], sem_rows.at[p])
                cp.wait() if wait else cp.start(priority=kk % 2)
            return carry

        lax.fori_loop(0, tg, body, 0)

    @pl.when(i == 0)
    def _():
        idx_copy(0, 0).start()
        idx_copy(0, 0).wait()
        gather_rows(0, wait=False)

        @pl.when(n > 1)
        def _():
            idx_copy(1, 1).start()

    def prefetch(p):
        @pl.when(i + 1 < n)
        def _():
            idx_copy(i + 1, 1 - p).wait()
            gather_rows(1 - p, wait=False)

        @pl.when(i + 2 < n)
        def _():
            idx_copy(i + 2, p).start()

    _by_parity(i, prefetch)

    hb = _unpack_bf16_pairs(h_ref[...])
    hid = _silu(jnp.dot(hb, swg_ref[...], preferred_element_type=F32)) * jnp.dot(hb, swu_ref[...],
                                                                                 preferred_element_type=F32)
    acc = jnp.dot(hid.astype(BF16), swd_ref[...], preferred_element_type=F32)
    w = w_ref[...]
    gate = _pick(i >= npt, g2s_ref, g2t_ref)

    def finish(p):
        gather_rows(p, wait=True)
        half = D_MODEL // 2
        lo = jnp.zeros((tg, half), F32)
        hi = jnp.zeros((tg, half), F32)
        for kk in range(TOP_K):
            u = ybuf_ref[p, kk]
            wk = w[:, kk:kk + 1]
            lo = lo + lax.bitcast_convert_type(u << 16, F32) * wk
            hi = hi + lax.bitcast_convert_type(u & jnp.uint32(0xFFFF0000), F32) * wk
        routed = jnp.concatenate([lo, hi], axis=1)
        x2 = x1_ref[...] + gate * (routed + acc)
        if final:
            ms = jnp.mean(x2 * x2, axis=-1, keepdims=True)
            x2 = x2 * lax.rsqrt(ms + NORM_EPS) * fg_ref[...]

            @pl.when(i < npt)
            def _():
                out_refs[0][...] = x2

            @pl.when(i >= npt)
            def _():
                out_refs[1][...] = x2
        else:
            out_refs[0][...] = x2

    _by_parity(i, finish)


def moe_combine(h, swg_bf, swu_bf, swd_bf, x1, mod, w_tok, final_g, dest_tiles, ys, final):
    t = h.shape[0]
    tg = mod.tile
    npt = mod.npt
    row = lambda i: (i, 0)
    const = lambda i: (0, 0)
    if final:
        out_shape = (jax.ShapeDtypeStruct((npt * tg, D_MODEL), F32), jax.ShapeDtypeStruct((t - npt * tg, D_MODEL), F32))
        out_specs = (pl.BlockSpec((tg, D_MODEL), lambda i: (jnp.minimum(i, npt - 1), 0)),
                     pl.BlockSpec((tg, D_MODEL), lambda i: (jnp.maximum(i - npt, 0), 0)))
    else:
        out_shape = jax.ShapeDtypeStruct((t, D_MODEL), F32)
        out_specs = pl.BlockSpec((tg, D_MODEL), row)
    return pl.pallas_call(
        functools.partial(_combine_kernel, mod.npt, final),
        out_shape=out_shape,
        grid=(t // tg,),
        in_specs=[pl.BlockSpec((tg, h.shape[1]), row), pl.BlockSpec(swg_bf.shape, const),
                  pl.BlockSpec(swu_bf.shape, const),
                  pl.BlockSpec(swd_bf.shape, const), pl.BlockSpec((tg, D_MODEL), row)]
        + mod.specs(5)
        + [pl.BlockSpec((tg, TOP_K), row), pl.BlockSpec((1, D_MODEL), const),
           pl.BlockSpec(memory_space=pl.ANY), pl.BlockSpec(memory_space=pl.ANY)],
        out_specs=out_specs,
        scratch_shapes=[pltpu.SMEM((tg * TOP_K,), I32), pltpu.SMEM((tg * TOP_K,), I32),
                        pltpu.VMEM((2, TOP_K, tg, D_MODEL // 2), U32),
                        pltpu.SemaphoreType.DMA((2,)), pltpu.SemaphoreType.DMA((2,))],
        compiler_params=_params(("arbitrary",)),
        name="moe_combine",
    )(h, swg_bf, swu_bf, swd_bf, x1, *mod.operands(5), w_tok, final_g.reshape(1, -1), dest_tiles, ys)


def _slot_kernel(eidx_ref, rank_ref, start_ref, o_ref):
    tm = eidx_ref.shape[1]
    tr = o_ref.shape[2]
    eid = lax.broadcasted_iota(I32, (N_EXPERTS, tm), 0)
    start = start_ref[...]
    for kk in range(TOP_K):
        base = jnp.sum(jnp.where(eid == eidx_ref[kk:kk + 1, :], start, 0.0), axis=0, keepdims=True)
        slot = base.astype(I32) + rank_ref[kk:kk + 1, :]
        for j in range(tm // tr):
            o_ref[j, kk:kk + 1, :] = slot[:, tr * j:tr * (j + 1)]


def assignment_slots(eidx_t, rank_t, pad_start):
    t = eidx_t.shape[1]
    tm = TOKEN_TILE
    tr = ROUTE_TILE
    col = lambda i: (0, i)
    out = pl.pallas_call(
        _slot_kernel,
        out_shape=jax.ShapeDtypeStruct((t // tr, TOP_K, tr), I32),
        grid=(t // tm,),
        in_specs=[pl.BlockSpec((TOP_K, tm), col), pl.BlockSpec((TOP_K, tm), col),
                  pl.BlockSpec((N_EXPERTS, 1), lambda i: (0, 0))],
        out_specs=pl.BlockSpec((tm // tr, TOP_K, tr), lambda i: (i, 0, 0)),
        compiler_params=_params(("parallel",)),
        name="assignment_slots",
    )(eidx_t, rank_t, pad_start.astype(F32).reshape(-1, 1))
    return out.reshape(t // tr, TOP_K * tr)


def _routing_tables(counts, n_blocks):
    rb = EXPERT_BLOCK
    counts = counts.astype(I32)
    padded = (counts + rb - 1) // rb * rb
    pad_end = jnp.cumsum(padded)
    pad_start = pad_end - padded
    n_used = pad_end[-1] // rb
    blocks = jnp.arange(n_blocks, dtype=I32)
    block_expert = jnp.sum((pad_end[None, :] <= (blocks * rb)[:, None]).astype(I32), axis=1)
    last_used = jnp.sum((pad_end <= (n_used - 1) * rb).astype(I32))
    block_expert = jnp.minimum(jnp.where(blocks < n_used, block_expert, last_used), N_EXPERTS - 1)
    zero_blocks = jnp.where(counts % rb != 0, pad_end // rb - 1, -1)
    zero_blocks = jnp.concatenate([zero_blocks, jnp.where(n_used % 2 == 1, n_used, -1).reshape(1)]).astype(I32)
    return pad_start, block_expert, n_used.reshape(1).astype(I32), zero_blocks


def moe_layer(h, x1, eidx_t, w_t, rank_t, counts, mod_route, wg, wu, wd, layer, swg_bf, swu_bf, swd_bf,
              final_g, final):
    t = h.shape[0]
    rb = EXPERT_BLOCK
    tr = ROUTE_TILE
    n_blocks = -(-(t * TOP_K) // rb) + N_EXPERTS
    n_blocks += n_blocks % 2
    pad_start, block_expert, n_used, zero_blocks = _routing_tables(counts, n_blocks)
    dest_tiles = assignment_slots(eidx_t, rank_t, pad_start)
    xs = moe_dispatch(h, dest_tiles, zero_blocks, n_blocks * rb)
    ys = moe_experts(xs, block_expert, n_used, wg, wu, wd, layer)
    return moe_combine(h, swg_bf, swu_bf, swd_bf, x1, mod_route, w_t.T, final_g, dest_tiles, ys, final)


def _rope_tables(n_batch, seq_len, n_seq, n_new):
    half = SWA_HEAD_DIM // 2
    inv = ROPE_THETA ** (-jnp.arange(half, dtype=F32) / half)
    pos = jnp.concatenate([jnp.tile(jnp.arange(seq_len, dtype=F32), n_batch),
                           jnp.tile(PAST_LEN + jnp.arange(n_new, dtype=F32), n_seq)])
    ang = pos[:, None] * inv[None, :]
    cos = jnp.tile(jnp.cos(ang), (1, 128 // half))
    sin = jnp.sin(ang)
    sin = jnp.tile(jnp.concatenate([-sin, sin], axis=1), (1, 128 // SWA_HEAD_DIM))
    return cos, sin


def kernel(x_prompt, x_sample, c_prompt, c_sample, cache_swa_k, cache_swa_v, state_gla, norm_mix_g, norm_ffn_g,
           final_g, ada_w, ada_b, swa_wqkv, swa_sinks, swa_wo, gla_win, gla_wa1, gla_wa2, gla_ba, gla_norm_g,
           gla_wo, moe_router, moe_bias, moe_wg, moe_wu, moe_wd, shared_wg, shared_wu, shared_wd):
    n_batch, seq_len, d = x_prompt.shape
    n_seq, n_new, _ = x_sample.shape
    depth = ada_w.shape[0]
    tp = n_batch * seq_len
    ts = n_seq * n_new
    t = tp + ts
    tm = TOKEN_TILE
    tr = ROUTE_TILE

    x = jnp.concatenate([x_prompt.reshape(tp, d), x_sample.reshape(ts, d)], axis=0)
    c_all = jnp.concatenate([jnp.repeat(c_sample, n_new, axis=0), c_prompt], axis=0)
    mod = ada_modulation(c_all, ada_w, ada_b)
    cos_tab, sin_tab = _rope_tables(n_batch, seq_len, n_seq, n_new)
    tri = jnp.triu(jnp.ones((tm, tm), BF16), k=1)

    new_k, new_v, new_s = [], [], []
    new_k_s, new_v_s, new_s_s = [], [], []
    for layer in range(depth):
        mod_tok = _Mod(mod, layer, n_batch, seq_len, tm)
        mod_route = _Mod(mod, layer, n_batch, seq_len, tr)
        m = layer // 2
        if layer % 2 == 0:
            q, k, v, k_dup, v_dup = swa_qkv(x, norm_mix_g[layer], mod_tok, swa_wqkv[m].astype(BF16), cos_tab, sin_tab)
            a_p = swa_prompt_attention(q, k_dup, v_dup, swa_sinks[m], n_batch, seq_len)
            nk = SWA_KV_HEADS * SWA_HEAD_DIM
            a_s, ck, cv = swa_sample_attention(q, k, v, cache_swa_k[m].reshape(n_seq, WINDOW, nk),
                                             cache_swa_v[m].reshape(n_seq, WINDOW, nk), swa_sinks[m], tp, n_new)
            kv_shape = (n_batch, WINDOW, SWA_KV_HEADS, SWA_HEAD_DIM)
            tails = [slice((b + 1) * seq_len - WINDOW, (b + 1) * seq_len) for b in range(n_batch)]
            new_k.append(jnp.stack([k[rows] for rows in tails]).reshape(kv_shape))
            new_v.append(jnp.stack([v[rows] for rows in tails]).reshape(kv_shape))
            new_k_s.append(ck.reshape(n_seq, WINDOW, SWA_KV_HEADS, SWA_HEAD_DIM))
            new_v_s.append(cv.reshape(n_seq, WINDOW, SWA_KV_HEADS, SWA_HEAD_DIM))
            wo = swa_wo[m]
        else:
            pad = jnp.zeros((d, 128 - GLA_GATE_RANK), F32)
            win_ext = jnp.concatenate([gla_win[m], gla_wa1[m], pad], axis=1).astype(BF16)
            wa2_pad = jnp.concatenate([gla_wa2[m], jnp.zeros((128 - GLA_GATE_RANK, GLA_KEY_DIM), F32)],
                                      axis=0).astype(BF16)
            q, k, v, r, gate = gla_project(x, norm_mix_g[layer], mod_tok, win_ext, wa2_pad, gla_ba[m])
            a_p, s_prompt = gla_prompt(q, k, gate, v, r, gla_norm_g[m], n_batch, seq_len)
            a_s, s_sample = gla_sample(q, k, gate, v, r, gla_norm_g[m], state_gla[m], tp, n_new)
            new_s.append(s_prompt)
            new_s_s.append(s_sample)
            wo = gla_wo[m]
        x1, h, eidx_t, w_t, rank_t, cnt = post_mixer(a_p, a_s, wo.astype(BF16), x, mod_tok, norm_ffn_g[layer],
                                                     moe_router[layer].T, moe_bias[layer], tri)
        x = moe_layer(h, x1, eidx_t, w_t, rank_t, cnt[:, 0], mod_route,
                      moe_wg, moe_wu, moe_wd, layer,
                      shared_wg[layer].astype(BF16), shared_wu[layer].astype(BF16), shared_wd[layer].astype(BF16),
                      final_g, layer == depth - 1)

    y_prompt = x[0].reshape(n_batch, seq_len, d)
    y_sample = x[1].reshape(n_seq, n_new, d)
    return (y_prompt, y_sample, jnp.stack(new_k), jnp.stack(new_v), jnp.stack(new_k_s), jnp.stack(new_v_s),
            jnp.stack(new_s), jnp.stack(new_s_s))
```

```python
import functools

import jax
import jax.numpy as jnp
from jax import lax
from jax.experimental import pallas as pl
from jax.experimental.pallas import tpu as pltpu

F32 = jnp.float32
BF16 = jnp.bfloat16
I32 = jnp.int32
U32 = jnp.uint32

D_MODEL = 1024
PAST_LEN = 8192
SWA_HEAD_DIM = 64
SWA_HEADS = 16
SWA_KV_HEADS = 4
SWA_GROUP = 4
WINDOW = 128
ROPE_THETA = 10000.0
GLA_HEADS = 4
GLA_DK = 128
GLA_DV = 256
GLA_KEY_DIM = 512
GLA_VAL_DIM = 1024
GLA_GATE_RANK = 16
GLA_GATE_NORMALIZER = 16.0
GLA_CHUNK = 64
N_EXPERTS = 64
TOP_K = 8
N_GROUPS = 8
TOPK_GROUPS = 4
EXPERT_FF = 256
ROUTED_SCALE = 2.5
NORM_EPS = 1e-6

TOKEN_TILE = 512
ROUTE_TILE = 512
EXPERT_BLOCK = 256
ADA_TILE = 512
SAMPLE_SEQS = 8
GLA_PROMPT_BATCH = 4
VMEM_LIMIT = 48 * 1024 * 1024

NT_DIMS = (((1,), (1,)), ((), ()))
TN_DIMS = (((0,), (0,)), ((), ()))


def _params(semantics):
    return pltpu.CompilerParams(dimension_semantics=semantics, vmem_limit_bytes=VMEM_LIMIT)


def _silu(x):
    return x * jax.nn.sigmoid(x)


def _norm_mod(x, g, sc, sh):
    ms = jnp.mean(x * x, axis=-1, keepdims=True)
    return (x * lax.rsqrt(ms + NORM_EPS) * g) * (1.0 + sc) + sh


def _pack_bf16_pairs(x):
    half = x.shape[1] // 2
    xb = x.astype(BF16).astype(F32)
    lo = lax.bitcast_convert_type(xb[:, :half], U32) >> 16
    hi = lax.bitcast_convert_type(xb[:, half:], U32) & jnp.uint32(0xFFFF0000)
    return lo | hi


def _unpack_bf16_pairs(u):
    lo = lax.bitcast_convert_type(u << 16, F32)
    hi = lax.bitcast_convert_type(u & jnp.uint32(0xFFFF0000), F32)
    return jnp.concatenate([lo, hi], axis=1).astype(BF16)


ROW_CHUNKS = D_MODEL // 2 // 128


def _store_row_slabs(ref, rows, words):
    for c in range(ROW_CHUNKS):
        ref[rows, c, :] = words[:, 128 * c:128 * (c + 1)]


def _load_row_slabs(ref, rows):
    return jnp.concatenate([ref[rows, c, :] for c in range(ROW_CHUNKS)], axis=1)


def _split3(x):
    x1 = x.astype(BF16)
    r1 = x - x1.astype(F32)
    x2 = r1.astype(BF16)
    x3 = (r1 - x2.astype(F32)).astype(BF16)
    return x1, x2, x3


def _ada_kernel(c_ref, w_ref, b_ref, o_ref):
    s = _silu(c_ref[...]).astype(BF16)
    o_ref[...] = jnp.dot(s, w_ref[...].astype(BF16), preferred_element_type=F32) + b_ref[...]


def ada_modulation(c_all, ada_w, ada_b):
    depth, d, n = ada_w.shape
    rows = c_all.shape[0]
    return pl.pallas_call(
        _ada_kernel,
        out_shape=jax.ShapeDtypeStruct((depth, rows, n), F32),
        grid=(depth, n // ADA_TILE),
        in_specs=[
            pl.BlockSpec((rows, d), lambda l, j: (0, 0)),
            pl.BlockSpec((None, d, ADA_TILE), lambda l, j: (l, 0, j)),
            pl.BlockSpec((None, 1, ADA_TILE), lambda l, j: (l, 0, j)),
        ],
        out_specs=pl.BlockSpec((None, rows, ADA_TILE), lambda l, j: (l, 0, j)),
        compiler_params=_params(("parallel", "parallel")),
        name="ada_modulation",
    )(c_all, ada_w, ada_b.reshape(depth, 1, n))


class _Mod:
    def __init__(self, mod, layer, n_batch, seq_len, tile):
        depth, rows, n = mod.shape
        self.tile = tile
        self.layer = layer
        self.npt = n_batch * seq_len // tile
        self.mod_tok = mod
        self.mod_seq = mod[:, rows - n_batch:].reshape(depth, n_batch, 1, n)
        self.tiles_per_seq = seq_len // tile

    def operands(self, chunk):
        del chunk
        return [self.mod_seq, self.mod_tok]

    def specs(self, chunk):
        l, npt, tps = self.layer, self.npt, self.tiles_per_seq
        n_seq = self.mod_seq.shape[1]
        seq_spec = pl.BlockSpec((None, None, 1, D_MODEL),
                                lambda i, *_: (l, jnp.minimum(i // tps, n_seq - 1), 0, chunk))
        tok_spec = pl.BlockSpec((None, self.tile, D_MODEL),
                                lambda i, *_: (l, jnp.maximum(i - npt, 0), chunk))
        return [seq_spec, tok_spec]


def _pick(is_sample, seq_ref, tok_ref):
    return jnp.where(is_sample, tok_ref[...], seq_ref[...])


def _stream_specs(x_pair, tile, npt):
    del x_pair
    return [pl.BlockSpec((tile, D_MODEL), lambda i, *_: (jnp.minimum(i, npt - 1), 0)),
            pl.BlockSpec((tile, D_MODEL), lambda i, *_: (jnp.maximum(i - npt, 0), 0))]


def _swa_qkv_kernel(npt, xp_ref, xs_ref, g_ref, scs_ref, sct_ref, shs_ref, sht_ref, w_ref, cos_ref, sin_ref,
                    q_ref, k_ref, v_ref, kd_ref, vd_ref):
    is_s = pl.program_id(0) >= npt
    h = _norm_mod(_pick(is_s, xp_ref, xs_ref), g_ref[...], _pick(is_s, scs_ref, sct_ref), _pick(is_s, shs_ref, sht_ref))
    qkv = jnp.dot(h.astype(BF16), w_ref[...], preferred_element_type=F32)
    cos = cos_ref[...]
    sin = sin_ref[...]
    lane = lax.broadcasted_iota(I32, cos.shape, 1)
    first_half = (lane % SWA_HEAD_DIM) < (SWA_HEAD_DIM // 2)

    def rope(xc):
        rot = jnp.where(first_half, pltpu.roll(xc, 128 - SWA_HEAD_DIM // 2, 1), pltpu.roll(xc, SWA_HEAD_DIM // 2, 1))
        return xc * cos + rot * sin

    nq = SWA_HEADS * SWA_HEAD_DIM
    nk = SWA_KV_HEADS * SWA_HEAD_DIM
    for c in range(nq // 128):
        q_ref[:, 128 * c:128 * (c + 1)] = (rope(qkv[:, 128 * c:128 * (c + 1)]) * (SWA_HEAD_DIM ** -0.5)).astype(BF16)
    low = lane < SWA_HEAD_DIM

    def spread(chunk):
        rolled = pltpu.roll(chunk, SWA_HEAD_DIM, 1)
        return jnp.where(low, chunk, rolled).astype(BF16), jnp.where(low, rolled, chunk).astype(BF16)

    for c in range(nk // 128):
        kc = rope(qkv[:, nq + 128 * c:nq + 128 * (c + 1)])
        vc = qkv[:, nq + nk + 128 * c:nq + nk + 128 * (c + 1)]
        k_ref[:, 128 * c:128 * (c + 1)] = kc
        v_ref[:, 128 * c:128 * (c + 1)] = vc
        kd_ref[:, 256 * c:256 * c + 128], kd_ref[:, 256 * c + 128:256 * (c + 1)] = spread(kc)
        vd_ref[:, 256 * c:256 * c + 128], vd_ref[:, 256 * c + 128:256 * (c + 1)] = spread(vc)


def swa_qkv(x, g, mod, w_bf, cos_tab, sin_tab):
    t = x[0].shape[0] + x[1].shape[0]
    tm = mod.tile
    nq = SWA_HEADS * SWA_HEAD_DIM
    nk = SWA_KV_HEADS * SWA_HEAD_DIM
    row = lambda i: (i, 0)
    return pl.pallas_call(
        functools.partial(_swa_qkv_kernel, mod.npt),
        out_shape=(jax.ShapeDtypeStruct((t, nq), BF16), jax.ShapeDtypeStruct((t, nk), F32),
                   jax.ShapeDtypeStruct((t, nk), F32), jax.ShapeDtypeStruct((t, 2 * nk), BF16),
                   jax.ShapeDtypeStruct((t, 2 * nk), BF16)),
        grid=(t // tm,),
        in_specs=_stream_specs(x, tm, mod.npt) + [pl.BlockSpec((1, D_MODEL), lambda i: (0, 0))]
        + mod.specs(1) + mod.specs(0)
        + [pl.BlockSpec(w_bf.shape, lambda i: (0, 0)), pl.BlockSpec((tm, 128), row), pl.BlockSpec((tm, 128), row)],
        out_specs=(pl.BlockSpec((tm, nq), row), pl.BlockSpec((tm, nk), row), pl.BlockSpec((tm, nk), row),
                   pl.BlockSpec((tm, 2 * nk), row), pl.BlockSpec((tm, 2 * nk), row)),
        compiler_params=_params(("parallel",)),
        name="swa_qkv",
    )(*x, g.reshape(1, -1), *mod.operands(1), *mod.operands(0), w_bf, cos_tab, sin_tab)


def _sink_softmax(s, sink_col):
    m = jnp.maximum(jnp.max(s, axis=-1, keepdims=True), sink_col)
    e = jnp.exp(s - m)
    den = jnp.sum(e, axis=-1, keepdims=True) + jnp.exp(sink_col - m)
    return e * (1.0 / den)


def _pair_attention(sink_ref, q_ref, kblks, vblks, mask, o_ref):
    rows = q_ref.shape[0]
    half = mask.shape[1]
    per_group = SWA_GROUP // 2
    scores = []
    for g in range(SWA_KV_HEADS):
        q2 = jnp.concatenate([q_ref[:, 128 * c:128 * (c + 1)] for c in range(per_group * g, per_group * (g + 1))],
                             axis=0)
        s = lax.dot_general(q2, kblks[g], NT_DIMS, preferred_element_type=F32)
        for j in range(per_group):
            scores += [s[rows * j:rows * (j + 1), :half], s[rows * j:rows * (j + 1), half:]]
    s_all = jnp.where(mask, jnp.stack(scores), -jnp.inf)
    sinks = jnp.stack([jnp.full((1, 1), sink_ref[h], F32) for h in range(SWA_HEADS)])
    p_all = _sink_softmax(s_all, sinks).astype(BF16)
    for g in range(SWA_KV_HEADS):
        chunks = range(per_group * g, per_group * (g + 1))
        p2 = jnp.concatenate([jnp.concatenate([p_all[2 * c], p_all[2 * c + 1]], axis=1) for c in chunks], axis=0)
        o = jnp.dot(p2, vblks[g], preferred_element_type=F32).astype(BF16)
        for j, c in enumerate(chunks):
            o_ref[:, 128 * c:128 * (c + 1)] = o[rows * j:rows * (j + 1)]


def _swa_prompt_kernel(sink_ref, q_ref, kc_ref, kp_ref, vc_ref, vp_ref, o_ref):
    j = pl.program_id(1)
    blk = q_ref.shape[0]
    qi = lax.broadcasted_iota(I32, (blk, 2 * blk), 0)
    sj = lax.broadcasted_iota(I32, (blk, 2 * blk), 1)
    rel = qi + blk - sj
    mask = (rel >= 0) & (rel <= WINDOW) & ((sj >= blk) | (j > 0))
    low = lax.broadcasted_iota(I32, (2 * blk, 128), 1) < SWA_HEAD_DIM
    zero = jnp.zeros((2 * blk, 128), BF16)
    kblks, vblks = [], []
    for g in range(SWA_KV_HEADS):
        cs = slice(128 * g, 128 * (g + 1))
        kcat = jnp.concatenate([kp_ref[:, cs], kc_ref[:, cs]], axis=0)
        vcat = jnp.concatenate([vp_ref[:, cs], vc_ref[:, cs]], axis=0)
        kblks.append(jnp.concatenate([jnp.where(low, kcat, zero), jnp.where(low, zero, kcat)], axis=0))
        vblks.append(jnp.concatenate([jnp.where(low, vcat, zero), jnp.where(low, zero, vcat)], axis=0))
    _pair_attention(sink_ref, q_ref, kblks, vblks, mask, o_ref)


def swa_prompt_attention(q, k, v, sinks, n_batch, seq_len):
    blk = WINDOW
    nb = seq_len // blk
    nq = q.shape[1]
    nk = k.shape[1]
    cur = lambda b, j: (b * nb + j, 0)
    prev = lambda b, j: (b * nb + jnp.maximum(j - 1, 0), 0)
    return pl.pallas_call(
        _swa_prompt_kernel,
        out_shape=jax.ShapeDtypeStruct((n_batch * seq_len, nq), BF16),
        grid=(n_batch, nb),
        in_specs=[pl.BlockSpec(memory_space=pltpu.SMEM),
                  pl.BlockSpec((blk, nq), cur),
                  pl.BlockSpec((blk, nk), cur), pl.BlockSpec((blk, nk), prev),
                  pl.BlockSpec((blk, nk), cur), pl.BlockSpec((blk, nk), prev)],
        out_specs=pl.BlockSpec((blk, nq), cur),
        compiler_params=_params(("parallel", "parallel")),
        name="swa_prompt_attention",
    )(sinks, q, k, k, v, v)


def _swa_sample_kernel(n_new, sink_ref, q_ref, kn_ref, vn_ref, ck_ref, cv_ref, o_ref, nk_ref, nv_ref):
    n_sb, win, _ = ck_ref.shape
    per_seq = win + n_new
    rows = n_sb * n_new
    cols = n_sb * per_seq
    keys, vals = [], []
    for sb in range(n_sb):
        r0 = sb * n_new
        kc = ck_ref[sb]
        vc = cv_ref[sb]
        kn = kn_ref[r0:r0 + n_new, :]
        vn = vn_ref[r0:r0 + n_new, :]
        nk_ref[sb, 0:win - n_new, :] = kc[n_new:]
        nk_ref[sb, win - n_new:win, :] = kn
        nv_ref[sb, 0:win - n_new, :] = vc[n_new:]
        nv_ref[sb, win - n_new:win, :] = vn
        keys += [kc, kn]
        vals += [vc, vn]
    keys = jnp.concatenate(keys, axis=0)
    vals = jnp.concatenate(vals, axis=0)
    ri = lax.broadcasted_iota(I32, (rows, cols), 0)
    ci = lax.broadcasted_iota(I32, (rows, cols), 1)
    ti = ri % n_new
    si = ci % per_seq
    mask = (ri // n_new == ci // per_seq) & (si >= ti) & (si <= ti + WINDOW)
    low = lax.broadcasted_iota(I32, (cols, 128), 1) < SWA_HEAD_DIM
    zero = jnp.zeros((cols, 128), BF16)

    def block_diag(chunk, first):
        rolled = pltpu.roll(chunk, SWA_HEAD_DIM, 1)
        both = (jnp.where(low, chunk, rolled) if first else jnp.where(low, rolled, chunk)).astype(BF16)
        return jnp.concatenate([jnp.where(low, both, zero), jnp.where(low, zero, both)], axis=0)

    chunks = [slice(128 * (g // 2), 128 * (g // 2 + 1)) for g in range(SWA_KV_HEADS)]
    kblks = [block_diag(keys[:, cs], g % 2 == 0) for g, cs in enumerate(chunks)]
    vblks = [block_diag(vals[:, cs], g % 2 == 0) for g, cs in enumerate(chunks)]
    _pair_attention(sink_ref, q_ref, kblks, vblks, mask, o_ref)


def swa_sample_attention(q, k, v, cache_k, cache_v, sinks, n_prompt_rows, n_new):
    n_seq, win, nk = cache_k.shape
    sb = SAMPLE_SEQS
    rows = sb * n_new
    nq = q.shape[1]
    base = n_prompt_rows // rows
    tok = lambda i: (base + i, 0)
    seq = lambda i: (i, 0, 0)
    return pl.pallas_call(
        functools.partial(_swa_sample_kernel, n_new),
        out_shape=(jax.ShapeDtypeStruct((n_seq * n_new, nq), BF16),
                   jax.ShapeDtypeStruct(cache_k.shape, F32), jax.ShapeDtypeStruct(cache_v.shape, F32)),
        grid=(n_seq // sb,),
        in_specs=[pl.BlockSpec(memory_space=pltpu.SMEM),
                  pl.BlockSpec((rows, nq), tok), pl.BlockSpec((rows, nk), tok), pl.BlockSpec((rows, nk), tok),
                  pl.BlockSpec((sb, win, nk), seq), pl.BlockSpec((sb, win, nk), seq)],
        out_specs=(pl.BlockSpec((rows, nq), lambda i: (i, 0)), pl.BlockSpec((sb, win, nk), seq),
                   pl.BlockSpec((sb, win, nk), seq)),
        compiler_params=_params(("parallel",)),
        name="swa_sample_attention",
    )(sinks, q, k, v, cache_k, cache_v)


def _gla_proj_kernel(npt, xp_ref, xs_ref, g_ref, scs_ref, sct_ref, shs_ref, sht_ref, w_ref, wa2_ref, ba_ref,
                     q_ref, k_ref, v_ref, r_ref, gate_ref):
    is_s = pl.program_id(0) >= npt
    h = _norm_mod(_pick(is_s, xp_ref, xs_ref), g_ref[...], _pick(is_s, scs_ref, sct_ref), _pick(is_s, shs_ref, sht_ref))
    proj = jnp.dot(h.astype(BF16), w_ref[...], preferred_element_type=F32)
    kd = GLA_KEY_DIM
    vd = GLA_VAL_DIM
    q_ref[...] = proj[:, :kd] * (GLA_DK ** -0.5)
    k_ref[...] = proj[:, kd:2 * kd]
    v_ref[...] = proj[:, 2 * kd:2 * kd + vd].astype(BF16)
    r_ref[...] = proj[:, 2 * kd + vd:2 * kd + 2 * vd]
    low = proj[:, 2 * kd + 2 * vd:].astype(BF16)
    z = jnp.dot(low, wa2_ref[...], preferred_element_type=F32) + ba_ref[...]
    log_sig = jnp.minimum(z, 0.0) - jnp.log1p(jnp.exp(-jnp.abs(z)))
    gate_ref[...] = log_sig / GLA_GATE_NORMALIZER


def gla_project(x, g, mod, win_ext, wa2_pad, ba):
    t = x[0].shape[0] + x[1].shape[0]
    tm = mod.tile
    kd, vd = GLA_KEY_DIM, GLA_VAL_DIM
    row = lambda i: (i, 0)
    const = lambda i: (0, 0)
    return pl.pallas_call(
        functools.partial(_gla_proj_kernel, mod.npt),
        out_shape=(jax.ShapeDtypeStruct((t, kd), F32), jax.ShapeDtypeStruct((t, kd), F32),
                   jax.ShapeDtypeStruct((t, vd), BF16), jax.ShapeDtypeStruct((t, vd), F32),
                   jax.ShapeDtypeStruct((t, kd), F32)),
        grid=(t // tm,),
        in_specs=_stream_specs(x, tm, mod.npt) + [pl.BlockSpec((1, D_MODEL), const)]
        + mod.specs(1) + mod.specs(0)
        + [pl.BlockSpec(win_ext.shape, const), pl.BlockSpec(wa2_pad.shape, const), pl.BlockSpec((1, kd), const)],
        out_specs=(pl.BlockSpec((tm, kd), row), pl.BlockSpec((tm, kd), row), pl.BlockSpec((tm, vd), row),
                   pl.BlockSpec((tm, vd), row), pl.BlockSpec((tm, kd), row)),
        compiler_params=_params(("parallel",)),
        name="gla_project",
    )(*x, g.reshape(1, -1), *mod.operands(1), *mod.operands(0), win_ext, wa2_pad, ba.reshape(1, -1))


def _cumsum_rows(tri, g):
    n = g.shape[1]
    s = jnp.dot(tri, jnp.concatenate(_split3(g), axis=1), preferred_element_type=F32)
    return s[:, :n] + s[:, n:2 * n] + s[:, 2 * n:]


def _diag_attention(q, k, b, n):
    ng = n // 8
    dk = q.shape[1]
    q3 = q.reshape(ng, 8, dk)
    k3 = k.reshape(ng, 8, dk)
    b3 = b.reshape(ng, 8, dk)
    sub = lax.broadcasted_iota(I32, (ng, 8, dk), 1)
    ti = lax.broadcasted_iota(I32, (n, n), 0)
    si = lax.broadcasted_iota(I32, (n, n), 1)
    attn = jnp.zeros((n, n), F32)
    for j in range(8):
        bj = jnp.broadcast_to(b3[:, j:j + 1, :], b3.shape)
        kj = jnp.broadcast_to(k3[:, j:j + 1, :], k3.shape)
        e = jnp.exp(jnp.minimum(b3 - bj, 0.0))
        m = jnp.where(sub >= j, q3 * e * kj, 0.0)
        col = jnp.sum(m, axis=-1, keepdims=True).reshape(n, 1)
        attn = attn + jnp.where(si == (ti // 8) * 8 + j, col, 0.0)
    return attn


def _cross_attention(q, k, b, n):
    ti = lax.broadcasted_iota(I32, (n, n), 0)
    si = lax.broadcasted_iota(I32, (n, n), 1)
    row = lax.broadcasted_iota(I32, b.shape, 0)
    attn = jnp.zeros((n, n), F32)
    m = n // 2
    while m >= 8:
        nblk = n // m
        refq = jnp.concatenate(
            [jnp.broadcast_to(b[i * m - 1:i * m], (m, b.shape[1])) if i % 2 else b[i * m:(i + 1) * m]
             for i in range(nblk)], axis=0)
        refk = jnp.concatenate(
            [b[i * m:(i + 1) * m] if i % 2 else jnp.broadcast_to(b[(i + 1) * m - 1:(i + 1) * m], (m, b.shape[1]))
             for i in range(nblk)], axis=0)
        odd = ((row // m) % 2) == 1
        qt = jnp.where(odd, q * jnp.exp(jnp.minimum(b - refq, 0.0)), 0.0).astype(BF16)
        kt = jnp.where(odd, 0.0, k * jnp.exp(jnp.minimum(refk - b, 0.0))).astype(BF16)
        a = lax.dot_general(qt, kt, NT_DIMS, preferred_element_type=F32)
        keep = (((ti // m) % 2) == 1) & ((si // m) == (ti // m) - 1)
        attn = attn + jnp.where(keep, a, 0.0)
        m //= 2
    return attn


def _gla_epilogue(o, r, ng):
    ms = jnp.mean(o * o, axis=-1, keepdims=True)
    return (o * lax.rsqrt(ms + NORM_EPS) * ng * _silu(r)).astype(BF16)


def _gla_prompt_kernel(*refs):
    nb = GLA_PROMPT_BATCH
    ins, (ng_ref, o_ref, so_ref, st_ref) = refs[:5 * nb], refs[5 * nb:]
    c = pl.program_id(1)
    n = ins[0].shape[0]

    @pl.when(c == 0)
    def _():
        st_ref[...] = jnp.zeros(st_ref.shape, F32)

    ti = lax.broadcasted_iota(I32, (n, n), 0)
    si = lax.broadcasted_iota(I32, (n, n), 1)
    tri = jnp.where(ti >= si, 1.0, 0.0).astype(BF16)
    for i in range(nb):
        q_ref, k_ref, g_ref, v_ref, r_ref = ins[5 * i:5 * (i + 1)]
        b_all = _cumsum_rows(tri, g_ref[...])
        for h in range(GLA_HEADS):
            ks = slice(GLA_DK * h, GLA_DK * (h + 1))
            vs = slice(GLA_DV * h, GLA_DV * (h + 1))
            q = q_ref[:, ks]
            k = k_ref[:, ks]
            v = v_ref[:, vs]
            b = b_all[:, ks]
            s_t = st_ref[i, h]
            o = lax.dot_general((q * jnp.exp(b)).astype(BF16), s_t.astype(BF16), NT_DIMS,
                                preferred_element_type=F32)
            attn = _cross_attention(q, k, b, n) + _diag_attention(q, k, b, n)
            o = o + jnp.dot(attn.astype(BF16), v, preferred_element_type=F32)
            bl = b[n - 1:n, :]
            kd = (k * jnp.exp(bl - b)).astype(BF16)
            s_new = s_t * jnp.exp(bl) + lax.dot_general(v, kd, TN_DIMS, preferred_element_type=F32)
            st_ref[i, h] = s_new
            o_ref[i, :, vs] = _gla_epilogue(o, r_ref[:, vs], ng_ref[...])

    @pl.when(c == pl.num_programs(1) - 1)
    def _():
        for i in range(nb):
            for h in range(GLA_HEADS):
                so_ref[i, h] = st_ref[i, h].T


def gla_prompt(q, k, g, v, r, norm_g, n_batch, seq_len):
    n = GLA_CHUNK
    nb = GLA_PROMPT_BATCH
    nc = seq_len // n
    kd, vd = GLA_KEY_DIM, GLA_VAL_DIM
    in_specs, operands = [], []
    for i in range(nb):
        row = lambda b, c, i=i: ((nb * b + i) * nc + c, 0)
        in_specs += [pl.BlockSpec((n, kd), row), pl.BlockSpec((n, kd), row), pl.BlockSpec((n, kd), row),
                     pl.BlockSpec((n, vd), row), pl.BlockSpec((n, vd), row)]
        operands += [q, k, g, v, r]
    o, state = pl.pallas_call(
        _gla_prompt_kernel,
        out_shape=(jax.ShapeDtypeStruct((n_batch, seq_len, vd), BF16),
                   jax.ShapeDtypeStruct((n_batch, GLA_HEADS, GLA_DK, GLA_DV), F32)),
        grid=(n_batch // nb, nc),
        in_specs=in_specs + [pl.BlockSpec((1, GLA_DV), lambda b, c: (0, 0))],
        out_specs=(pl.BlockSpec((nb, n, vd), lambda b, c: (b, c, 0)),
                   pl.BlockSpec((nb, GLA_HEADS, GLA_DK, GLA_DV), lambda b, c: (b, 0, 0, 0))),
        scratch_shapes=[pltpu.VMEM((nb, GLA_HEADS, GLA_DV, GLA_DK), F32)],
        compiler_params=_params(("parallel", "arbitrary")),
        name="gla_prompt",
    )(*operands, norm_g.reshape(1, -1))
    return o.reshape(n_batch * seq_len, vd), state


def _gla_sample_kernel(n_new, q_ref, k_ref, g_ref, v_ref, r_ref, ng_ref, si_ref, o_ref, so_ref):
    n = q_ref.shape[0]
    ti = lax.broadcasted_iota(I32, (n, n), 0)
    si = lax.broadcasted_iota(I32, (n, n), 1)
    tri = jnp.where((ti >= si) & (ti // n_new == si // n_new), 1.0, 0.0).astype(BF16)
    b_all = _cumsum_rows(tri, g_ref[...])
    for h in range(GLA_HEADS):
        ks = slice(GLA_DK * h, GLA_DK * (h + 1))
        vs = slice(GLA_DV * h, GLA_DV * (h + 1))
        q = q_ref[:, ks]
        k = k_ref[:, ks]
        v = v_ref[:, vs]
        b = b_all[:, ks]
        attn = _diag_attention(q, k, b, n)
        o_intra = jnp.dot(attn.astype(BF16), v, preferred_element_type=F32)
        qe = (q * jnp.exp(b)).astype(BF16)
        n_sb = n // n_new
        last = [b[n_new * (sb + 1) - 1:n_new * (sb + 1), :] for sb in range(n_sb)]
        bl_rows = jnp.concatenate([jnp.broadcast_to(bl, (n_new, GLA_DK)) for bl in last], axis=0)
        kd = (k * jnp.exp(bl_rows - b)).astype(BF16)
        seq_of_vrow = lax.broadcasted_iota(I32, (n, GLA_DV), 0) // n_new
        seq_of_krow = lax.broadcasted_iota(I32, (n, GLA_DK), 0) // n_new
        o = o_intra
        for sb in range(n_sb):
            s0 = si_ref[sb, h]
            o_sb = jnp.dot(qe, s0.astype(BF16), preferred_element_type=F32)
            o = o + jnp.where(seq_of_vrow == sb, o_sb, 0.0)
            kd_sb = jnp.where(seq_of_krow == sb, kd, jnp.zeros_like(kd))
            upd = lax.dot_general(kd_sb, v, TN_DIMS, preferred_element_type=F32)
            decay_col = jnp.transpose(jnp.broadcast_to(jnp.exp(last[sb]), (8, GLA_DK)))[:, 0:1]
            so_ref[sb, h] = s0 * decay_col + upd
        o_ref[:, vs] = _gla_epilogue(o, r_ref[:, vs], ng_ref[...])


def gla_sample(q, k, g, v, r, norm_g, state, n_prompt_rows, n_new):
    n_seq = state.shape[0]
    sb = SAMPLE_SEQS
    rows = sb * n_new
    kd, vd = GLA_KEY_DIM, GLA_VAL_DIM
    base = n_prompt_rows // rows
    tok = lambda i: (base + i, 0)
    seq = lambda i: (i, 0, 0, 0)
    sblock = (sb, GLA_HEADS, GLA_DK, GLA_DV)
    return pl.pallas_call(
        functools.partial(_gla_sample_kernel, n_new),
        out_shape=(jax.ShapeDtypeStruct((n_seq * n_new, vd), BF16), jax.ShapeDtypeStruct(state.shape, F32)),
        grid=(n_seq // sb,),
        in_specs=[pl.BlockSpec((rows, kd), tok), pl.BlockSpec((rows, kd), tok), pl.BlockSpec((rows, kd), tok),
                  pl.BlockSpec((rows, vd), tok), pl.BlockSpec((rows, vd), tok),
                  pl.BlockSpec((1, GLA_DV), lambda i: (0, 0)),
                  pl.BlockSpec(sblock, seq)],
        out_specs=(pl.BlockSpec((rows, vd), lambda i: (i, 0)), pl.BlockSpec(sblock, seq)),
        compiler_params=_params(("parallel",)),
        name="gla_sample",
    )(q, k, g, v, r, norm_g.reshape(1, -1), state)


def _post_mixer_kernel(npt, ap_ref, as_ref, wo_ref, xp_ref, xs_ref, g1s_ref, g1t_ref, gf_ref, scs_ref, sct_ref, shs_ref, sht_ref,
                       rw_ref, rb_ref, tri_ref,
                       x1_ref, h_ref, eidx_ref, w_ref, rank_ref, cnt_ref, carry_ref):
    i = pl.program_id(0)
    is_s = i >= npt

    @pl.when(i == 0)
    def _():
        carry_ref[...] = jnp.zeros(carry_ref.shape, F32)

    a = jnp.where(is_s, as_ref[...], ap_ref[...])
    x1 = _pick(is_s, xp_ref, xs_ref) + _pick(is_s, g1s_ref, g1t_ref) * jnp.dot(a, wo_ref[...],
                                                                               preferred_element_type=F32)
    x1_ref[...] = x1
    h = _norm_mod(x1, gf_ref[...], _pick(is_s, scs_ref, sct_ref), _pick(is_s, shs_ref, sht_ref))
    h_ref[...] = _pack_bf16_pairs(h)

    h1, h2, _ = _split3(h)
    r1, r2, _ = _split3(rw_ref[...])
    logits = (lax.dot_general(r1, h1, NT_DIMS, preferred_element_type=F32)
              + lax.dot_general(r1, h2, NT_DIMS, preferred_element_type=F32)
              + lax.dot_general(r2, h1, NT_DIMS, preferred_element_type=F32))
    scores = jax.nn.sigmoid(logits)
    sel = scores + rb_ref[...]
    tm = sel.shape[1]
    gsz = N_EXPERTS // N_GROUPS

    sub = lax.broadcasted_iota(I32, (gsz, tm), 0)
    blocks, gscore = [], []
    for g in range(N_GROUPS):
        blk = sel[gsz * g:gsz * (g + 1)]
        m1 = jnp.max(blk, axis=0, keepdims=True)
        first = jnp.min(jnp.where(blk == m1, sub, gsz), axis=0, keepdims=True)
        m2 = jnp.max(jnp.where(sub == first, -jnp.inf, blk), axis=0, keepdims=True)
        blocks.append(blk)
        gscore.append(m1 + m2)
    masked = []
    for g in range(N_GROUPS):
        beaten = jnp.zeros((1, tm), I32)
        for o in range(N_GROUPS):
            if o == g:
                continue
            wins = (gscore[o] > gscore[g]) | ((gscore[o] == gscore[g]) & (o < g))
            beaten = beaten + wins.astype(I32)
        masked.append(jnp.where(beaten < TOPK_GROUPS, blocks[g], -jnp.inf))
    cur = jnp.concatenate(masked, axis=0)

    eid = lax.broadcasted_iota(I32, (N_EXPERTS, tm), 0)
    picked, weights = [], []
    onehot = jnp.zeros((N_EXPERTS, tm), F32)
    for _ in range(TOP_K):
        m = jnp.max(cur, axis=0, keepdims=True)
        idx = jnp.min(jnp.where(cur == m, eid, N_EXPERTS), axis=0, keepdims=True)
        hit = eid == idx
        picked.append(idx)
        weights.append(jnp.sum(jnp.where(hit, scores, 0.0), axis=0, keepdims=True))
        onehot = jnp.where(hit, 1.0, onehot)
        cur = jnp.where(hit, -jnp.inf, cur)
    wsum = weights[0]
    for wk in weights[1:]:
        wsum = wsum + wk
    scale = ROUTED_SCALE / wsum

    before = jnp.dot(onehot.astype(BF16), tri_ref[...], preferred_element_type=F32) + carry_ref[...]
    carry = carry_ref[...] + jnp.sum(onehot, axis=1, keepdims=True)
    carry_ref[...] = carry
    cnt_ref[...] = jnp.broadcast_to(carry, cnt_ref.shape)
    for kk in range(TOP_K):
        eidx_ref[kk:kk + 1, :] = picked[kk]
        w_ref[kk:kk + 1, :] = weights[kk] * scale
        rank_ref[kk:kk + 1, :] = jnp.sum(jnp.where(eid == picked[kk], before, 0.0), axis=0, keepdims=True).astype(I32)


def post_mixer(a_prompt, a_sample, wo_bf, x, mod, gffn, router_t, router_b, tri):
    t = x[0].shape[0] + x[1].shape[0]
    tm = mod.tile
    npt = mod.npt
    row = lambda i: (i, 0)
    col = lambda i: (0, i)
    const = lambda i: (0, 0)
    return pl.pallas_call(
        functools.partial(_post_mixer_kernel, mod.npt),
        out_shape=(jax.ShapeDtypeStruct((t, D_MODEL), F32), jax.ShapeDtypeStruct((t, D_MODEL // 2), U32),
                   jax.ShapeDtypeStruct((TOP_K, t), I32), jax.ShapeDtypeStruct((TOP_K, t), F32),
                   jax.ShapeDtypeStruct((TOP_K, t), I32), jax.ShapeDtypeStruct((N_EXPERTS, 128), F32)),
        grid=(t // tm,),
        in_specs=[pl.BlockSpec((tm, D_MODEL), lambda i: (jnp.minimum(i, npt - 1), 0)),
                  pl.BlockSpec((tm, D_MODEL), lambda i: (jnp.maximum(i - npt, 0), 0)),
                  pl.BlockSpec((D_MODEL, D_MODEL), const)] + _stream_specs(x, tm, npt)
        + mod.specs(2) + [pl.BlockSpec((1, D_MODEL), const)] + mod.specs(4) + mod.specs(3)
        + [pl.BlockSpec((N_EXPERTS, D_MODEL), const), pl.BlockSpec((N_EXPERTS, 1), const),
           pl.BlockSpec((tm, tm), const)],
        out_specs=(pl.BlockSpec((tm, D_MODEL), row), pl.BlockSpec((tm, D_MODEL // 2), row),
                   pl.BlockSpec((TOP_K, tm), col), pl.BlockSpec((TOP_K, tm), col), pl.BlockSpec((TOP_K, tm), col),
                   pl.BlockSpec((N_EXPERTS, 128), const)),
        scratch_shapes=[pltpu.VMEM((N_EXPERTS, 1), F32)],
        compiler_params=_params(("arbitrary",)),
        name="post_mixer",
    )(a_prompt, a_sample, wo_bf, *x, *mod.operands(2), gffn.reshape(1, -1), *mod.operands(4), *mod.operands(3),
      router_t, router_b.reshape(-1, 1), tri)


def _row_copy(src, src_row, dst, dst_row, sem):
    return pltpu.make_async_copy(src.at[src_row], dst.at[dst_row], sem)


def _by_parity(i, fn):
    @pl.when(i % 2 == 0)
    def _():
        fn(0)

    @pl.when(i % 2 == 1)
    def _():
        fn(1)


def _dispatch_kernel(zb_ref, h_ref, dest_hbm, xs_hbm, idx_a, idx_b, zero_ref, slab_ref, sem_idx, sem_zero, sem_rows):
    i = pl.program_id(0)
    te = h_ref.shape[0]
    _store_row_slabs(slab_ref, slice(None), h_ref[...])
    idx_bufs = (idx_a, idx_b)

    def idx_copy(tile, p):
        return pltpu.make_async_copy(dest_hbm.at[tile], idx_bufs[p], sem_idx.at[p])

    def zero_copy(e):
        return pltpu.make_async_copy(zero_ref, xs_hbm.at[pl.ds(zb_ref[e] * EXPERT_BLOCK, EXPERT_BLOCK)], sem_zero)

    @pl.when(i == 0)
    def _():
        idx_copy(0, 0).start()
        zero_ref[...] = jnp.zeros(zero_ref.shape, U32)

        def start(e, carry):
            @pl.when(zb_ref[e] >= 0)
            def _():
                zero_copy(e).start()
            return carry

        def wait(e, carry):
            @pl.when(zb_ref[e] >= 0)
            def _():
                zero_copy(e).wait()
            return carry

        lax.fori_loop(0, zb_ref.shape[0], start, 0)
        lax.fori_loop(0, zb_ref.shape[0], wait, 0)

    def step(p):
        @pl.when(i + 1 < pl.num_programs(0))
        def _():
            idx_copy(i + 1, 1 - p).start()

        idx_copy(i, p).wait()
        idx = idx_bufs[p]

        def issue(t, carry):
            for kk in range(TOP_K):
                _row_copy(slab_ref, t, xs_hbm, idx[kk * te + t], sem_rows).start(priority=kk % 2)
            return carry

        def drain(t, carry):
            for kk in range(TOP_K):
                _row_copy(slab_ref, t, xs_hbm, idx[kk * te + t], sem_rows).wait()
            return carry

        lax.fori_loop(0, te, issue, 0)
        lax.fori_loop(0, te, drain, 0)

    _by_parity(i, step)


def moe_dispatch(h, dest_tiles, zero_blocks, n_slots):
    t = h.shape[0]
    te = ROUTE_TILE
    return pl.pallas_call(
        _dispatch_kernel,
        out_shape=jax.ShapeDtypeStruct((n_slots, ROW_CHUNKS, 128), U32),
        grid_spec=pltpu.PrefetchScalarGridSpec(
            num_scalar_prefetch=1,
            grid=(t // te,),
            in_specs=[pl.BlockSpec((te, h.shape[1]), lambda i, zb: (i, 0)), pl.BlockSpec(memory_space=pl.ANY)],
            out_specs=pl.BlockSpec(memory_space=pl.ANY),
            scratch_shapes=[pltpu.SMEM((te * TOP_K,), I32), pltpu.SMEM((te * TOP_K,), I32),
                            pltpu.VMEM((EXPERT_BLOCK, ROW_CHUNKS, 128), U32), pltpu.VMEM((te, ROW_CHUNKS, 128), U32),
                            pltpu.SemaphoreType.DMA((2,)), pltpu.SemaphoreType.DMA, pltpu.SemaphoreType.DMA],
        ),
        compiler_params=_params(("arbitrary",)),
        name="moe_dispatch",
    )(zero_blocks, h, dest_tiles)


def _expert_kernel(be_ref, nu_ref, xs_ref, wg0_ref, wu0_ref, wd0_ref, wg1_ref, wu1_ref, wd1_ref, ys_ref,
                   wg_bf, wu_bf, wd_bf):
    b = pl.program_id(0)
    rb = EXPERT_BLOCK
    used = 2 * b < nu_ref[0]
    for s, (wg, wu, wd) in enumerate(((wg0_ref, wu0_ref, wd0_ref), (wg1_ref, wu1_ref, wd1_ref))):
        j = 2 * b + s
        fresh = (b == 0) | (be_ref[j] != be_ref[jnp.maximum(j - 2, 0)])

        @pl.when(used & fresh)
        def _():
            wg_bf[s] = wg[...].astype(BF16)
            wu_bf[s] = wu[...].astype(BF16)
            wd_bf[s] = wd[...].astype(BF16)

    @pl.when(used)
    def _():
        for s in range(2):
            rows = slice(rb * s, rb * (s + 1))
            x = _unpack_bf16_pairs(_load_row_slabs(xs_ref, rows))
            hg = jnp.dot(x, wg_bf[s], preferred_element_type=F32)
            hu = jnp.dot(x, wu_bf[s], preferred_element_type=F32)
            y = jnp.dot((_silu(hg) * hu).astype(BF16), wd_bf[s], preferred_element_type=F32)
            ys_ref[rows, :] = _pack_bf16_pairs(y)


def moe_experts(xs, block_expert, n_used, wg, wu, wd, layer):
    n_slots = xs.shape[0]
    rb = EXPERT_BLOCK
    ff = wg.shape[3]
    rows_in = lambda b, be, nu: (jnp.minimum(b, (nu[0] - 1) // 2), 0, 0)
    rows_out = lambda b, be, nu: (jnp.minimum(b, (nu[0] - 1) // 2), 0)
    w_in = lambda s: (lambda b, be, nu: (layer, be[2 * b + s], 0, 0))
    return pl.pallas_call(
        _expert_kernel,
        out_shape=jax.ShapeDtypeStruct((n_slots, D_MODEL // 2), U32),
        grid_spec=pltpu.PrefetchScalarGridSpec(
            num_scalar_prefetch=2,
            grid=(n_slots // (2 * rb),),
            in_specs=[pl.BlockSpec((2 * rb,) + xs.shape[1:], rows_in)]
            + [pl.BlockSpec((None, None, D_MODEL, ff), w_in(0)), pl.BlockSpec((None, None, D_MODEL, ff), w_in(0)),
               pl.BlockSpec((None, None, ff, D_MODEL), w_in(0)),
               pl.BlockSpec((None, None, D_MODEL, ff), w_in(1)), pl.BlockSpec((None, None, D_MODEL, ff), w_in(1)),
               pl.BlockSpec((None, None, ff, D_MODEL), w_in(1))],
            out_specs=pl.BlockSpec((2 * rb, D_MODEL // 2), rows_out),
            scratch_shapes=[pltpu.VMEM((2, D_MODEL, ff), BF16), pltpu.VMEM((2, D_MODEL, ff), BF16),
                            pltpu.VMEM((2, ff, D_MODEL), BF16)],
        ),
        compiler_params=_params(("arbitrary",)),
        name="moe_experts",
    )(block_expert, n_used, xs, wg, wu, wd, wg, wu, wd)


def _combine_kernel(npt, final, h_ref, swg_ref, swu_ref, swd_ref, x1_ref, g2s_ref, g2t_ref, w_ref, fg_ref,
                    dest_hbm, ys_hbm, *rest):
    out_refs, (idx_a, idx_b, ybuf_ref, sem_idx, sem_rows) = rest[:-5], rest[-5:]
    i = pl.program_id(0)
    n = pl.num_programs(0)
    tg = h_ref.shape[0]
    idx_bufs = (idx_a, idx_b)

    def idx_copy(tile, p):
        return pltpu.make_async_copy(dest_hbm.at[tile], idx_bufs[p], sem_idx.at[p])

    def gather_rows(p, wait):
        idx = idx_bufs[p]

        def body(t, carry):
            for kk in range(TOP_K):
                src = 0 if wait else idx[kk * tg + t]
                dst = ybuf_ref.at[p, kk]
                cp = pltpu.make_async_copy(ys_hbm.at[pl.ds(src, 1)], dst.at[pl.ds(t, 1)], sem_rows.at[p])
                cp.wait() if wait else cp.start(priority=kk % 2)
            return carry

        lax.fori_loop(0, tg, body, 0)

    @pl.when(i == 0)
    def _():
        idx_copy(0, 0).start()
        idx_copy(0, 0).wait()
        gather_rows(0, wait=False)

        @pl.when(n > 1)
        def _():
            idx_copy(1, 1).start()

    def prefetch(p):
        @pl.when(i + 1 < n)
        def _():
            idx_copy(i + 1, 1 - p).wait()
            gather_rows(1 - p, wait=False)

        @pl.when(i + 2 < n)
        def _():
            idx_copy(i + 2, p).start()

    _by_parity(i, prefetch)

    hb = _unpack_bf16_pairs(h_ref[...])
    hid = _silu(jnp.dot(hb, swg_ref[...], preferred_element_type=F32)) * jnp.dot(hb, swu_ref[...],
                                                                                 preferred_element_type=F32)
    acc = jnp.dot(hid.astype(BF16), swd_ref[...], preferred_element_type=F32)
    w = w_ref[...]
    gate = _pick(i >= npt, g2s_ref, g2t_ref)

    def finish(p):
        gather_rows(p, wait=True)
        half = D_MODEL // 2
        lo = jnp.zeros((tg, half), F32)
        hi = jnp.zeros((tg, half), F32)
        for kk in range(TOP_K):
            u = ybuf_ref[p, kk]
            wk = w[:, kk:kk + 1]
            lo = lo + lax.bitcast_convert_type(u << 16, F32) * wk
            hi = hi + lax.bitcast_convert_type(u & jnp.uint32(0xFFFF0000), F32) * wk
        routed = jnp.concatenate([lo, hi], axis=1)
        x2 = x1_ref[...] + gate * (routed + acc)
        if final:
            ms = jnp.mean(x2 * x2, axis=-1, keepdims=True)
            x2 = x2 * lax.rsqrt(ms + NORM_EPS) * fg_ref[...]

        @pl.when(i < npt)
        def _():
            out_refs[0][...] = x2

        @pl.when(i >= npt)
        def _():
            out_refs[1][...] = x2

    _by_parity(i, finish)


def moe_combine(h, swg_bf, swu_bf, swd_bf, x1, mod, w_tok, final_g, dest_tiles, ys, final):
    t = h.shape[0]
    tg = mod.tile
    npt = mod.npt
    row = lambda i: (i, 0)
    const = lambda i: (0, 0)
    out_shape = (jax.ShapeDtypeStruct((npt * tg, D_MODEL), F32), jax.ShapeDtypeStruct((t - npt * tg, D_MODEL), F32))
    out_specs = tuple(_stream_specs(None, tg, npt))
    return pl.pallas_call(
        functools.partial(_combine_kernel, mod.npt, final),
        out_shape=out_shape,
        grid=(t // tg,),
        in_specs=[pl.BlockSpec((tg, h.shape[1]), row), pl.BlockSpec(swg_bf.shape, const),
                  pl.BlockSpec(swu_bf.shape, const),
                  pl.BlockSpec(swd_bf.shape, const), pl.BlockSpec((tg, D_MODEL), row)]
        + mod.specs(5)
        + [pl.BlockSpec((tg, TOP_K), row), pl.BlockSpec((1, D_MODEL), const),
           pl.BlockSpec(memory_space=pl.ANY), pl.BlockSpec(memory_space=pl.ANY)],
        out_specs=out_specs,
        scratch_shapes=[pltpu.SMEM((tg * TOP_K,), I32), pltpu.SMEM((tg * TOP_K,), I32),
                        pltpu.VMEM((2, TOP_K, tg, D_MODEL // 2), U32),
                        pltpu.SemaphoreType.DMA((2,)), pltpu.SemaphoreType.DMA((2,))],
        compiler_params=_params(("arbitrary",)),
        name="moe_combine",
    )(h, swg_bf, swu_bf, swd_bf, x1, *mod.operands(5), w_tok, final_g.reshape(1, -1), dest_tiles, ys)


def _slot_kernel(eidx_ref, rank_ref, start_ref, o_ref):
    tm = eidx_ref.shape[1]
    tr = o_ref.shape[2]
    eid = lax.broadcasted_iota(I32, (N_EXPERTS, tm), 0)
    start = start_ref[...]
    for kk in range(TOP_K):
        base = jnp.sum(jnp.where(eid == eidx_ref[kk:kk + 1, :], start, 0.0), axis=0, keepdims=True)
        slot = base.astype(I32) + rank_ref[kk:kk + 1, :]
        for j in range(tm // tr):
            o_ref[j, kk:kk + 1, :] = slot[:, tr * j:tr * (j + 1)]


def assignment_slots(eidx_t, rank_t, pad_start):
    t = eidx_t.shape[1]
    tm = TOKEN_TILE
    tr = ROUTE_TILE
    col = lambda i: (0, i)
    out = pl.pallas_call(
        _slot_kernel,
        out_shape=jax.ShapeDtypeStruct((t // tr, TOP_K, tr), I32),
        grid=(t // tm,),
        in_specs=[pl.BlockSpec((TOP_K, tm), col), pl.BlockSpec((TOP_K, tm), col),
                  pl.BlockSpec((N_EXPERTS, 1), lambda i: (0, 0))],
        out_specs=pl.BlockSpec((tm // tr, TOP_K, tr), lambda i: (i, 0, 0)),
        compiler_params=_params(("parallel",)),
        name="assignment_slots",
    )(eidx_t, rank_t, pad_start.astype(F32).reshape(-1, 1))
    return out.reshape(t // tr, TOP_K * tr)


def _routing_tables(counts, n_blocks):
    rb = EXPERT_BLOCK
    counts = counts.astype(I32)
    padded = (counts + rb - 1) // rb * rb
    pad_end = jnp.cumsum(padded)
    pad_start = pad_end - padded
    n_used = pad_end[-1] // rb
    blocks = jnp.arange(n_blocks, dtype=I32)
    block_expert = jnp.sum((pad_end[None, :] <= (blocks * rb)[:, None]).astype(I32), axis=1)
    last_used = jnp.sum((pad_end <= (n_used - 1) * rb).astype(I32))
    block_expert = jnp.minimum(jnp.where(blocks < n_used, block_expert, last_used), N_EXPERTS - 1)
    zero_blocks = jnp.where(counts % rb != 0, pad_end // rb - 1, -1)
    zero_blocks = jnp.concatenate([zero_blocks, jnp.where(n_used % 2 == 1, n_used, -1).reshape(1)]).astype(I32)
    return pad_start, block_expert, n_used.reshape(1).astype(I32), zero_blocks


def moe_layer(h, x1, eidx_t, w_t, rank_t, counts, mod_route, wg, wu, wd, layer, swg_bf, swu_bf, swd_bf,
              final_g, final):
    t = h.shape[0]
    rb = EXPERT_BLOCK
    tr = ROUTE_TILE
    n_blocks = -(-(t * TOP_K) // rb) + N_EXPERTS
    n_blocks += n_blocks % 2
    pad_start, block_expert, n_used, zero_blocks = _routing_tables(counts, n_blocks)
    dest_tiles = assignment_slots(eidx_t, rank_t, pad_start)
    xs = moe_dispatch(h, dest_tiles, zero_blocks, n_blocks * rb)
    ys = moe_experts(xs, block_expert, n_used, wg, wu, wd, layer)
    return moe_combine(h, swg_bf, swu_bf, swd_bf, x1, mod_route, w_t.T, final_g, dest_tiles, ys, final)


def _rope_tables(n_batch, seq_len, n_seq, n_new):
    half = SWA_HEAD_DIM // 2
    inv = ROPE_THETA ** (-jnp.arange(half, dtype=F32) / half)
    pos = jnp.concatenate([jnp.tile(jnp.arange(seq_len, dtype=F32), n_batch),
                           jnp.tile(PAST_LEN + jnp.arange(n_new, dtype=F32), n_seq)])
    ang = pos[:, None] * inv[None, :]
    cos = jnp.tile(jnp.cos(ang), (1, 128 // half))
    sin = jnp.sin(ang)
    sin = jnp.tile(jnp.concatenate([-sin, sin], axis=1), (1, 128 // SWA_HEAD_DIM))
    return cos, sin


def kernel(x_prompt, x_sample, c_prompt, c_sample, cache_swa_k, cache_swa_v, state_gla, norm_mix_g, norm_ffn_g,
           final_g, ada_w, ada_b, swa_wqkv, swa_sinks, swa_wo, gla_win, gla_wa1, gla_wa2, gla_ba, gla_norm_g,
           gla_wo, moe_router, moe_bias, moe_wg, moe_wu, moe_wd, shared_wg, shared_wu, shared_wd):
    n_batch, seq_len, d = x_prompt.shape
    n_seq, n_new, _ = x_sample.shape
    depth = ada_w.shape[0]
    tp = n_batch * seq_len
    ts = n_seq * n_new
    t = tp + ts
    tm = TOKEN_TILE
    tr = ROUTE_TILE

    x = (x_prompt.reshape(tp, d), x_sample.reshape(ts, d))
    c_all = jnp.concatenate([jnp.repeat(c_sample, n_new, axis=0), c_prompt], axis=0)
    mod = ada_modulation(c_all, ada_w, ada_b)
    cos_tab, sin_tab = _rope_tables(n_batch, seq_len, n_seq, n_new)
    tri = jnp.triu(jnp.ones((tm, tm), BF16), k=1)

    new_k, new_v, new_s = [], [], []
    new_k_s, new_v_s, new_s_s = [], [], []
    for layer in range(depth):
        mod_tok = _Mod(mod, layer, n_batch, seq_len, tm)
        mod_route = _Mod(mod, layer, n_batch, seq_len, tr)
        m = layer // 2
        if layer % 2 == 0:
            q, k, v, k_dup, v_dup = swa_qkv(x, norm_mix_g[layer], mod_tok, swa_wqkv[m].astype(BF16), cos_tab, sin_tab)
            a_p = swa_prompt_attention(q, k_dup, v_dup, swa_sinks[m], n_batch, seq_len)
            nk = SWA_KV_HEADS * SWA_HEAD_DIM
            a_s, ck, cv = swa_sample_attention(q, k, v, cache_swa_k[m].reshape(n_seq, WINDOW, nk),
                                             cache_swa_v[m].reshape(n_seq, WINDOW, nk), swa_sinks[m], tp, n_new)
            kv_shape = (n_batch, WINDOW, SWA_KV_HEADS, SWA_HEAD_DIM)
            tails = [slice((b + 1) * seq_len - WINDOW, (b + 1) * seq_len) for b in range(n_batch)]
            new_k.append(jnp.stack([k[rows] for rows in tails]).reshape(kv_shape))
            new_v.append(jnp.stack([v[rows] for rows in tails]).reshape(kv_shape))
            new_k_s.append(ck.reshape(n_seq, WINDOW, SWA_KV_HEADS, SWA_HEAD_DIM))
            new_v_s.append(cv.reshape(n_seq, WINDOW, SWA_KV_HEADS, SWA_HEAD_DIM))
            wo = swa_wo[m]
        else:
            pad = jnp.zeros((d, 128 - GLA_GATE_RANK), F32)
            win_ext = jnp.concatenate([gla_win[m], gla_wa1[m], pad], axis=1).astype(BF16)
            wa2_pad = jnp.concatenate([gla_wa2[m], jnp.zeros((128 - GLA_GATE_RANK, GLA_KEY_DIM), F32)],
                                      axis=0).astype(BF16)
            q, k, v, r, gate = gla_project(x, norm_mix_g[layer], mod_tok, win_ext, wa2_pad, gla_ba[m])
            a_p, s_prompt = gla_prompt(q, k, gate, v, r, gla_norm_g[m], n_batch, seq_len)
            a_s, s_sample = gla_sample(q, k, gate, v, r, gla_norm_g[m], state_gla[m], tp, n_new)
            new_s.append(s_prompt)
            new_s_s.append(s_sample)
            wo = gla_wo[m]
        x1, h, eidx_t, w_t, rank_t, cnt = post_mixer(a_p, a_s, wo.astype(BF16), x, mod_tok, norm_ffn_g[layer],
                                                     moe_router[layer].T, moe_bias[layer], tri)
        x = moe_layer(h, x1, eidx_t, w_t, rank_t, cnt[:, 0], mod_route,
                      moe_wg, moe_wu, moe_wd, layer,
                      shared_wg[layer].astype(BF16), shared_wu[layer].astype(BF16), shared_wd[layer].astype(BF16),
                      final_g, layer == depth - 1)

    y_prompt = x[0].reshape(n_batch, seq_len, d)
    y_sample = x[1].reshape(n_seq, n_new, d)
    return (y_prompt, y_sample, jnp.stack(new_k), jnp.stack(new_v), jnp.stack(new_k_s), jnp.stack(new_v_s),
            jnp.stack(new_s), jnp.stack(new_s_s))
```

```python
import functools

import jax
import jax.numpy as jnp
from jax import lax
from jax.experimental import pallas as pl
from jax.experimental.pallas import tpu as pltpu

F32 = jnp.float32
BF16 = jnp.bfloat16
I32 = jnp.int32
U32 = jnp.uint32

D_MODEL = 1024
PAST_LEN = 8192
SWA_HEAD_DIM = 64
SWA_HEADS = 16
SWA_KV_HEADS = 4
SWA_GROUP = 4
WINDOW = 128
ROPE_THETA = 10000.0
GLA_HEADS = 4
GLA_DK = 128
GLA_DV = 256
GLA_KEY_DIM = 512
GLA_VAL_DIM = 1024
GLA_GATE_RANK = 16
GLA_GATE_NORMALIZER = 16.0
GLA_CHUNK = 64
N_EXPERTS = 64
TOP_K = 8
N_GROUPS = 8
TOPK_GROUPS = 4
EXPERT_FF = 256
ROUTED_SCALE = 2.5
NORM_EPS = 1e-6

TOKEN_TILE = 512
ROUTE_TILE = 512
EXPERT_BLOCK = 256
ADA_TILE = 512
SAMPLE_SEQS = 8
GLA_PROMPT_BATCH = 4
SWA_PROMPT_BATCH = 4
VMEM_LIMIT = 48 * 1024 * 1024

NT_DIMS = (((1,), (1,)), ((), ()))
TN_DIMS = (((0,), (0,)), ((), ()))


def _params(semantics):
    return pltpu.CompilerParams(dimension_semantics=semantics, vmem_limit_bytes=VMEM_LIMIT)


def _silu(x):
    return x * jax.nn.sigmoid(x)


def _norm_mod(x, g, sc, sh):
    ms = jnp.mean(x * x, axis=-1, keepdims=True)
    return (x * lax.rsqrt(ms + NORM_EPS) * g) * (1.0 + sc) + sh


def _pack_bf16_pairs(x):
    half = x.shape[1] // 2
    xb = x.astype(BF16).astype(F32)
    lo = lax.bitcast_convert_type(xb[:, :half], U32) >> 16
    hi = lax.bitcast_convert_type(xb[:, half:], U32) & jnp.uint32(0xFFFF0000)
    return lo | hi


def _unpack_bf16_pairs(u):
    lo = lax.bitcast_convert_type(u << 16, F32)
    hi = lax.bitcast_convert_type(u & jnp.uint32(0xFFFF0000), F32)
    return jnp.concatenate([lo, hi], axis=1).astype(BF16)


ROW_CHUNKS = D_MODEL // 2 // 128


def _store_row_slabs(ref, rows, words):
    for c in range(ROW_CHUNKS):
        ref[rows, c, :] = words[:, 128 * c:128 * (c + 1)]


def _load_row_slabs(ref, rows):
    return jnp.concatenate([ref[rows, c, :] for c in range(ROW_CHUNKS)], axis=1)


def _split3(x):
    x1 = x.astype(BF16)
    r1 = x - x1.astype(F32)
    x2 = r1.astype(BF16)
    x3 = (r1 - x2.astype(F32)).astype(BF16)
    return x1, x2, x3


def _ada_kernel(c_ref, w_ref, b_ref, o_ref):
    s = _silu(c_ref[...]).astype(BF16)
    o_ref[...] = jnp.dot(s, w_ref[...].astype(BF16), preferred_element_type=F32) + b_ref[...]


def ada_modulation(c_all, ada_w, ada_b):
    depth, d, n = ada_w.shape
    rows = c_all.shape[0]
    return pl.pallas_call(
        _ada_kernel,
        out_shape=jax.ShapeDtypeStruct((depth, rows, n), F32),
        grid=(depth, n // ADA_TILE),
        in_specs=[
            pl.BlockSpec((rows, d), lambda l, j: (0, 0)),
            pl.BlockSpec((None, d, ADA_TILE), lambda l, j: (l, 0, j)),
            pl.BlockSpec((None, 1, ADA_TILE), lambda l, j: (l, 0, j)),
        ],
        out_specs=pl.BlockSpec((None, rows, ADA_TILE), lambda l, j: (l, 0, j)),
        compiler_params=_params(("parallel", "parallel")),
        name="ada_modulation",
    )(c_all, ada_w, ada_b.reshape(depth, 1, n))


class _Mod:
    def __init__(self, mod, layer, n_batch, seq_len, tile):
        depth, rows, n = mod.shape
        self.tile = tile
        self.layer = layer
        self.npt = n_batch * seq_len // tile
        self.mod_tok = mod
        self.mod_seq = mod[:, rows - n_batch:].reshape(depth, n_batch, 1, n)
        self.tiles_per_seq = seq_len // tile

    def operands(self, chunk):
        del chunk
        return [self.mod_seq, self.mod_tok]

    def specs(self, chunk):
        l, npt, tps = self.layer, self.npt, self.tiles_per_seq
        n_seq = self.mod_seq.shape[1]
        seq_spec = pl.BlockSpec((None, None, 1, D_MODEL),
                                lambda i, *_: (l, jnp.minimum(i // tps, n_seq - 1), 0, chunk))
        tok_spec = pl.BlockSpec((None, self.tile, D_MODEL),
                                lambda i, *_: (l, jnp.maximum(i - npt, 0), chunk))
        return [seq_spec, tok_spec]


def _pick(is_sample, seq_ref, tok_ref):
    return jnp.where(is_sample, tok_ref[...], seq_ref[...])


def _stream_specs(x_pair, tile, npt):
    del x_pair
    return [pl.BlockSpec((tile, D_MODEL), lambda i, *_: (jnp.minimum(i, npt - 1), 0)),
            pl.BlockSpec((tile, D_MODEL), lambda i, *_: (jnp.maximum(i - npt, 0), 0))]


def _swa_qkv_kernel(npt, xp_ref, xs_ref, g_ref, scs_ref, sct_ref, shs_ref, sht_ref, w_ref, cos_ref, sin_ref,
                    q_ref, k_ref, v_ref, kd_ref, vd_ref):
    is_s = pl.program_id(0) >= npt
    h = _norm_mod(_pick(is_s, xp_ref, xs_ref), g_ref[...], _pick(is_s, scs_ref, sct_ref), _pick(is_s, shs_ref, sht_ref))
    qkv = jnp.dot(h.astype(BF16), w_ref[...], preferred_element_type=F32)
    cos = cos_ref[...]
    sin = sin_ref[...]
    lane = lax.broadcasted_iota(I32, cos.shape, 1)
    first_half = (lane % SWA_HEAD_DIM) < (SWA_HEAD_DIM // 2)

    def rope(xc):
        rot = jnp.where(first_half, pltpu.roll(xc, 128 - SWA_HEAD_DIM // 2, 1), pltpu.roll(xc, SWA_HEAD_DIM // 2, 1))
        return xc * cos + rot * sin

    nq = SWA_HEADS * SWA_HEAD_DIM
    nk = SWA_KV_HEADS * SWA_HEAD_DIM
    for c in range(nq // 128):
        q_ref[:, 128 * c:128 * (c + 1)] = (rope(qkv[:, 128 * c:128 * (c + 1)]) * (SWA_HEAD_DIM ** -0.5)).astype(BF16)
    low = lane < SWA_HEAD_DIM

    def spread(chunk):
        rolled = pltpu.roll(chunk, SWA_HEAD_DIM, 1)
        return jnp.where(low, chunk, rolled).astype(BF16), jnp.where(low, rolled, chunk).astype(BF16)

    for c in range(nk // 128):
        kc = rope(qkv[:, nq + 128 * c:nq + 128 * (c + 1)])
        vc = qkv[:, nq + nk + 128 * c:nq + nk + 128 * (c + 1)]
        k_ref[:, 128 * c:128 * (c + 1)] = kc
        v_ref[:, 128 * c:128 * (c + 1)] = vc
        kd_ref[:, 256 * c:256 * c + 128], kd_ref[:, 256 * c + 128:256 * (c + 1)] = spread(kc)
        vd_ref[:, 256 * c:256 * c + 128], vd_ref[:, 256 * c + 128:256 * (c + 1)] = spread(vc)


def swa_qkv(x, g, mod, w_bf, cos_tab, sin_tab):
    t = x[0].shape[0] + x[1].shape[0]
    tm = mod.tile
    nq = SWA_HEADS * SWA_HEAD_DIM
    nk = SWA_KV_HEADS * SWA_HEAD_DIM
    row = lambda i: (i, 0)
    return pl.pallas_call(
        functools.partial(_swa_qkv_kernel, mod.npt),
        out_shape=(jax.ShapeDtypeStruct((t, nq), BF16), jax.ShapeDtypeStruct((t, nk), F32),
                   jax.ShapeDtypeStruct((t, nk), F32), jax.ShapeDtypeStruct((t, 2 * nk), BF16),
                   jax.ShapeDtypeStruct((t, 2 * nk), BF16)),
        grid=(t // tm,),
        in_specs=_stream_specs(x, tm, mod.npt) + [pl.BlockSpec((1, D_MODEL), lambda i: (0, 0))]
        + mod.specs(1) + mod.specs(0)
        + [pl.BlockSpec(w_bf.shape, lambda i: (0, 0)), pl.BlockSpec((tm, 128), row), pl.BlockSpec((tm, 128), row)],
        out_specs=(pl.BlockSpec((tm, nq), row), pl.BlockSpec((tm, nk), row), pl.BlockSpec((tm, nk), row),
                   pl.BlockSpec((tm, 2 * nk), row), pl.BlockSpec((tm, 2 * nk), row)),
        compiler_params=_params(("parallel",)),
        name="swa_qkv",
    )(*x, g.reshape(1, -1), *mod.operands(1), *mod.operands(0), w_bf, cos_tab, sin_tab)


def _sink_softmax(s, sink_col):
    m = jnp.maximum(jnp.max(s, axis=-1, keepdims=True), sink_col)
    e = jnp.exp(s - m)
    den = jnp.sum(e, axis=-1, keepdims=True) + jnp.exp(sink_col - m)
    return e * (1.0 / den)


def _pair_attention(sink_ref, q_ref, kblks, vblks, mask, o_ref):
    rows = q_ref.shape[0]
    half = mask.shape[1]
    per_group = SWA_GROUP // 2
    scores = []
    for g in range(SWA_KV_HEADS):
        q2 = jnp.concatenate([q_ref[:, 128 * c:128 * (c + 1)] for c in range(per_group * g, per_group * (g + 1))],
                             axis=0)
        s = lax.dot_general(q2, kblks[g], NT_DIMS, preferred_element_type=F32)
        for j in range(per_group):
            scores += [s[rows * j:rows * (j + 1), :half], s[rows * j:rows * (j + 1), half:]]
    s_all = jnp.where(mask, jnp.stack(scores), -jnp.inf)
    sinks = jnp.stack([jnp.full((1, 1), sink_ref[h], F32) for h in range(SWA_HEADS)])
    p_all = _sink_softmax(s_all, sinks).astype(BF16)
    for g in range(SWA_KV_HEADS):
        chunks = range(per_group * g, per_group * (g + 1))
        p2 = jnp.concatenate([jnp.concatenate([p_all[2 * c], p_all[2 * c + 1]], axis=1) for c in chunks], axis=0)
        o = jnp.dot(p2, vblks[g], preferred_element_type=F32).astype(BF16)
        for j, c in enumerate(chunks):
            o_ref[:, 128 * c:128 * (c + 1)] = o[rows * j:rows * (j + 1)]


def _swa_prompt_kernel(sink_ref, *refs):
    o_ref = refs[-1]
    j = pl.program_id(1)
    blk = refs[0].shape[0]
    qi = lax.broadcasted_iota(I32, (blk, 2 * blk), 0)
    sj = lax.broadcasted_iota(I32, (blk, 2 * blk), 1)
    rel = qi + blk - sj
    mask = (rel >= 0) & (rel <= WINDOW) & ((sj >= blk) | (j > 0))
    low = lax.broadcasted_iota(I32, (2 * blk, 128), 1) < SWA_HEAD_DIM
    zero = jnp.zeros((2 * blk, 128), BF16)
    for i in range(SWA_PROMPT_BATCH):
        q_ref, kc_ref, kp_ref, vc_ref, vp_ref = refs[5 * i:5 * (i + 1)]
        kblks, vblks = [], []
        for g in range(SWA_KV_HEADS):
            cs = slice(128 * g, 128 * (g + 1))
            kcat = jnp.concatenate([kp_ref[:, cs], kc_ref[:, cs]], axis=0)
            vcat = jnp.concatenate([vp_ref[:, cs], vc_ref[:, cs]], axis=0)
            kblks.append(jnp.concatenate([jnp.where(low, kcat, zero), jnp.where(low, zero, kcat)], axis=0))
            vblks.append(jnp.concatenate([jnp.where(low, vcat, zero), jnp.where(low, zero, vcat)], axis=0))
        _pair_attention(sink_ref, q_ref, kblks, vblks, mask, o_ref.at[i])


def swa_prompt_attention(q, k, v, sinks, n_batch, seq_len):
    blk = WINDOW
    nsb = SWA_PROMPT_BATCH
    nb = seq_len // blk
    nq = q.shape[1]
    nk = k.shape[1]
    in_specs, operands = [pl.BlockSpec(memory_space=pltpu.SMEM)], [sinks]
    for i in range(nsb):
        cur = lambda b, j, i=i: ((nsb * b + i) * nb + j, 0)
        prev = lambda b, j, i=i: ((nsb * b + i) * nb + jnp.maximum(j - 1, 0), 0)
        in_specs += [pl.BlockSpec((blk, nq), cur), pl.BlockSpec((blk, nk), cur), pl.BlockSpec((blk, nk), prev),
                     pl.BlockSpec((blk, nk), cur), pl.BlockSpec((blk, nk), prev)]
        operands += [q, k, k, v, v]
    out = pl.pallas_call(
        _swa_prompt_kernel,
        out_shape=jax.ShapeDtypeStruct((n_batch, seq_len, nq), BF16),
        grid=(n_batch // nsb, nb),
        in_specs=in_specs,
        out_specs=pl.BlockSpec((nsb, blk, nq), lambda b, j: (b, j, 0)),
        compiler_params=_params(("parallel", "parallel")),
        name="swa_prompt_attention",
    )(*operands)
    return out.reshape(n_batch * seq_len, nq)


def _swa_sample_kernel(n_new, sink_ref, q_ref, kn_ref, vn_ref, ck_ref, cv_ref, o_ref, nk_ref, nv_ref):
    n_sb, win, _ = ck_ref.shape
    per_seq = win + n_new
    rows = n_sb * n_new
    cols = n_sb * per_seq
    keys, vals = [], []
    for sb in range(n_sb):
        r0 = sb * n_new
        kc = ck_ref[sb]
        vc = cv_ref[sb]
        kn = kn_ref[r0:r0 + n_new, :]
        vn = vn_ref[r0:r0 + n_new, :]
        nk_ref[sb, 0:win - n_new, :] = kc[n_new:]
        nk_ref[sb, win - n_new:win, :] = kn
        nv_ref[sb, 0:win - n_new, :] = vc[n_new:]
        nv_ref[sb, win - n_new:win, :] = vn
        keys += [kc, kn]
        vals += [vc, vn]
    keys = jnp.concatenate(keys, axis=0)
    vals = jnp.concatenate(vals, axis=0)
    ri = lax.broadcasted_iota(I32, (rows, cols), 0)
    ci = lax.broadcasted_iota(I32, (rows, cols), 1)
    ti = ri % n_new
    si = ci % per_seq
    mask = (ri // n_new == ci // per_seq) & (si >= ti) & (si <= ti + WINDOW)
    low = lax.broadcasted_iota(I32, (cols, 128), 1) < SWA_HEAD_DIM
    zero = jnp.zeros((cols, 128), BF16)

    def block_diag(chunk, first):
        rolled = pltpu.roll(chunk, SWA_HEAD_DIM, 1)
        both = (jnp.where(low, chunk, rolled) if first else jnp.where(low, rolled, chunk)).astype(BF16)
        return jnp.concatenate([jnp.where(low, both, zero), jnp.where(low, zero, both)], axis=0)

    chunks = [slice(128 * (g // 2), 128 * (g // 2 + 1)) for g in range(SWA_KV_HEADS)]
    kblks = [block_diag(keys[:, cs], g % 2 == 0) for g, cs in enumerate(chunks)]
    vblks = [block_diag(vals[:, cs], g % 2 == 0) for g, cs in enumerate(chunks)]
    _pair_attention(sink_ref, q_ref, kblks, vblks, mask, o_ref)


def swa_sample_attention(q, k, v, cache_k, cache_v, sinks, n_prompt_rows, n_new):
    n_seq, win, nk = cache_k.shape
    sb = SAMPLE_SEQS
    rows = sb * n_new
    nq = q.shape[1]
    base = n_prompt_rows // rows
    tok = lambda i: (base + i, 0)
    seq = lambda i: (i, 0, 0)
    return pl.pallas_call(
        functools.partial(_swa_sample_kernel, n_new),
        out_shape=(jax.ShapeDtypeStruct((n_seq * n_new, nq), BF16),
                   jax.ShapeDtypeStruct(cache_k.shape, F32), jax.ShapeDtypeStruct(cache_v.shape, F32)),
        grid=(n_seq // sb,),
        in_specs=[pl.BlockSpec(memory_space=pltpu.SMEM),
                  pl.BlockSpec((rows, nq), tok), pl.BlockSpec((rows, nk), tok), pl.BlockSpec((rows, nk), tok),
                  pl.BlockSpec((sb, win, nk), seq), pl.BlockSpec((sb, win, nk), seq)],
        out_specs=(pl.BlockSpec((rows, nq), lambda i: (i, 0)), pl.BlockSpec((sb, win, nk), seq),
                   pl.BlockSpec((sb, win, nk), seq)),
        compiler_params=_params(("parallel",)),
        name="swa_sample_attention",
    )(sinks, q, k, v, cache_k, cache_v)


def _gla_proj_kernel(npt, xp_ref, xs_ref, g_ref, scs_ref, sct_ref, shs_ref, sht_ref, w_ref, wa2_ref, ba_ref,
                     q_ref, k_ref, v_ref, r_ref, gate_ref):
    is_s = pl.program_id(0) >= npt
    h = _norm_mod(_pick(is_s, xp_ref, xs_ref), g_ref[...], _pick(is_s, scs_ref, sct_ref), _pick(is_s, shs_ref, sht_ref))
    proj = jnp.dot(h.astype(BF16), w_ref[...], preferred_element_type=F32)
    kd = GLA_KEY_DIM
    vd = GLA_VAL_DIM
    q_ref[...] = proj[:, :kd] * (GLA_DK ** -0.5)
    k_ref[...] = proj[:, kd:2 * kd]
    v_ref[...] = proj[:, 2 * kd:2 * kd + vd].astype(BF16)
    r_ref[...] = proj[:, 2 * kd + vd:2 * kd + 2 * vd]
    low = proj[:, 2 * kd + 2 * vd:].astype(BF16)
    z = jnp.dot(low, wa2_ref[...], preferred_element_type=F32) + ba_ref[...]
    log_sig = jnp.minimum(z, 0.0) - jnp.log1p(jnp.exp(-jnp.abs(z)))
    gate_ref[...] = log_sig / GLA_GATE_NORMALIZER


def gla_project(x, g, mod, win_ext, wa2_pad, ba):
    t = x[0].shape[0] + x[1].shape[0]
    tm = mod.tile
    kd, vd = GLA_KEY_DIM, GLA_VAL_DIM
    row = lambda i: (i, 0)
    const = lambda i: (0, 0)
    return pl.pallas_call(
        functools.partial(_gla_proj_kernel, mod.npt),
        out_shape=(jax.ShapeDtypeStruct((t, kd), F32), jax.ShapeDtypeStruct((t, kd), F32),
                   jax.ShapeDtypeStruct((t, vd), BF16), jax.ShapeDtypeStruct((t, vd), F32),
                   jax.ShapeDtypeStruct((t, kd), F32)),
        grid=(t // tm,),
        in_specs=_stream_specs(x, tm, mod.npt) + [pl.BlockSpec((1, D_MODEL), const)]
        + mod.specs(1) + mod.specs(0)
        + [pl.BlockSpec(win_ext.shape, const), pl.BlockSpec(wa2_pad.shape, const), pl.BlockSpec((1, kd), const)],
        out_specs=(pl.BlockSpec((tm, kd), row), pl.BlockSpec((tm, kd), row), pl.BlockSpec((tm, vd), row),
                   pl.BlockSpec((tm, vd), row), pl.BlockSpec((tm, kd), row)),
        compiler_params=_params(("parallel",)),
        name="gla_project",
    )(*x, g.reshape(1, -1), *mod.operands(1), *mod.operands(0), win_ext, wa2_pad, ba.reshape(1, -1))


def _cumsum_rows(tri, g):
    n = g.shape[1]
    s = jnp.dot(tri, jnp.concatenate(_split3(g), axis=1), preferred_element_type=F32)
    return s[:, :n] + s[:, n:2 * n] + s[:, 2 * n:]


def _diag_attention(q, k, b, n):
    ng = n // 8
    dk = q.shape[1]
    q3 = q.reshape(ng, 8, dk)
    k3 = k.reshape(ng, 8, dk)
    b3 = b.reshape(ng, 8, dk)
    sub = lax.broadcasted_iota(I32, (ng, 8, dk), 1)
    ti = lax.broadcasted_iota(I32, (n, n), 0)
    si = lax.broadcasted_iota(I32, (n, n), 1)
    attn = jnp.zeros((n, n), F32)
    for j in range(8):
        bj = jnp.broadcast_to(b3[:, j:j + 1, :], b3.shape)
        kj = jnp.broadcast_to(k3[:, j:j + 1, :], k3.shape)
        e = jnp.exp(jnp.minimum(b3 - bj, 0.0))
        m = jnp.where(sub >= j, q3 * e * kj, 0.0)
        col = jnp.sum(m, axis=-1, keepdims=True).reshape(n, 1)
        attn = attn + jnp.where(si == (ti // 8) * 8 + j, col, 0.0)
    return attn


def _cross_attention(q, k, b, n):
    ti = lax.broadcasted_iota(I32, (n, n), 0)
    si = lax.broadcasted_iota(I32, (n, n), 1)
    row = lax.broadcasted_iota(I32, b.shape, 0)
    attn = jnp.zeros((n, n), F32)
    m = n // 2
    while m >= 8:
        nblk = n // m
        refq = jnp.concatenate(
            [jnp.broadcast_to(b[i * m - 1:i * m], (m, b.shape[1])) if i % 2 else b[i * m:(i + 1) * m]
             for i in range(nblk)], axis=0)
        refk = jnp.concatenate(
            [b[i * m:(i + 1) * m] if i % 2 else jnp.broadcast_to(b[(i + 1) * m - 1:(i + 1) * m], (m, b.shape[1]))
             for i in range(nblk)], axis=0)
        odd = ((row // m) % 2) == 1
        qt = jnp.where(odd, q * jnp.exp(jnp.minimum(b - refq, 0.0)), 0.0).astype(BF16)
        kt = jnp.where(odd, 0.0, k * jnp.exp(jnp.minimum(refk - b, 0.0))).astype(BF16)
        a = lax.dot_general(qt, kt, NT_DIMS, preferred_element_type=F32)
        keep = (((ti // m) % 2) == 1) & ((si // m) == (ti // m) - 1)
        attn = attn + jnp.where(keep, a, 0.0)
        m //= 2
    return attn


def _gla_epilogue(o, r, ng):
    ms = jnp.mean(o * o, axis=-1, keepdims=True)
    return (o * lax.rsqrt(ms + NORM_EPS) * ng * _silu(r)).astype(BF16)


def _gla_prompt_kernel(*refs):
    nb = GLA_PROMPT_BATCH
    ins, (ng_ref, o_ref, so_ref, st_ref) = refs[:5 * nb], refs[5 * nb:]
    c = pl.program_id(1)
    n = ins[0].shape[0]

    @pl.when(c == 0)
    def _():
        st_ref[...] = jnp.zeros(st_ref.shape, F32)

    ti = lax.broadcasted_iota(I32, (n, n), 0)
    si = lax.broadcasted_iota(I32, (n, n), 1)
    tri = jnp.where(ti >= si, 1.0, 0.0).astype(BF16)
    for i in range(nb):
        q_ref, k_ref, g_ref, v_ref, r_ref = ins[5 * i:5 * (i + 1)]
        b_all = _cumsum_rows(tri, g_ref[...])
        for h in range(GLA_HEADS):
            ks = slice(GLA_DK * h, GLA_DK * (h + 1))
            vs = slice(GLA_DV * h, GLA_DV * (h + 1))
            q = q_ref[:, ks]
            k = k_ref[:, ks]
            v = v_ref[:, vs]
            b = b_all[:, ks]
            s_t = st_ref[i, h]
            o = lax.dot_general((q * jnp.exp(b)).astype(BF16), s_t.astype(BF16), NT_DIMS,
                                preferred_element_type=F32)
            attn = _cross_attention(q, k, b, n) + _diag_attention(q, k, b, n)
            o = o + jnp.dot(attn.astype(BF16), v, preferred_element_type=F32)
            bl = b[n - 1:n, :]
            kd = (k * jnp.exp(bl - b)).astype(BF16)
            s_new = s_t * jnp.exp(bl) + lax.dot_general(v, kd, TN_DIMS, preferred_element_type=F32)
            st_ref[i, h] = s_new
            o_ref[i, :, vs] = _gla_epilogue(o, r_ref[:, vs], ng_ref[...])

    @pl.when(c == pl.num_programs(1) - 1)
    def _():
        for i in range(nb):
            for h in range(GLA_HEADS):
                so_ref[i, h] = st_ref[i, h].T


def gla_prompt(q, k, g, v, r, norm_g, n_batch, seq_len):
    n = GLA_CHUNK
    nb = GLA_PROMPT_BATCH
    nc = seq_len // n
    kd, vd = GLA_KEY_DIM, GLA_VAL_DIM
    in_specs, operands = [], []
    for i in range(nb):
        row = lambda b, c, i=i: ((nb * b + i) * nc + c, 0)
        in_specs += [pl.BlockSpec((n, kd), row), pl.BlockSpec((n, kd), row), pl.BlockSpec((n, kd), row),
                     pl.BlockSpec((n, vd), row), pl.BlockSpec((n, vd), row)]
        operands += [q, k, g, v, r]
    o, state = pl.pallas_call(
        _gla_prompt_kernel,
        out_shape=(jax.ShapeDtypeStruct((n_batch, seq_len, vd), BF16),
                   jax.ShapeDtypeStruct((n_batch, GLA_HEADS, GLA_DK, GLA_DV), F32)),
        grid=(n_batch // nb, nc),
        in_specs=in_specs + [pl.BlockSpec((1, GLA_DV), lambda b, c: (0, 0))],
        out_specs=(pl.BlockSpec((nb, n, vd), lambda b, c: (b, c, 0)),
                   pl.BlockSpec((nb, GLA_HEADS, GLA_DK, GLA_DV), lambda b, c: (b, 0, 0, 0))),
        scratch_shapes=[pltpu.VMEM((nb, GLA_HEADS, GLA_DV, GLA_DK), F32)],
        compiler_params=_params(("parallel", "arbitrary")),
        name="gla_prompt",
    )(*operands, norm_g.reshape(1, -1))
    return o.reshape(n_batch * seq_len, vd), state


def _gla_sample_kernel(n_new, q_ref, k_ref, g_ref, v_ref, r_ref, ng_ref, si_ref, o_ref, so_ref):
    n = q_ref.shape[0]
    ti = lax.broadcasted_iota(I32, (n, n), 0)
    si = lax.broadcasted_iota(I32, (n, n), 1)
    tri = jnp.where((ti >= si) & (ti // n_new == si // n_new), 1.0, 0.0).astype(BF16)
    b_all = _cumsum_rows(tri, g_ref[...])
    for h in range(GLA_HEADS):
        ks = slice(GLA_DK * h, GLA_DK * (h + 1))
        vs = slice(GLA_DV * h, GLA_DV * (h + 1))
        q = q_ref[:, ks]
        k = k_ref[:, ks]
        v = v_ref[:, vs]
        b = b_all[:, ks]
        attn = _diag_attention(q, k, b, n)
        o_intra = jnp.dot(attn.astype(BF16), v, preferred_element_type=F32)
        qe = (q * jnp.exp(b)).astype(BF16)
        n_sb = n // n_new
        last = [b[n_new * (sb + 1) - 1:n_new * (sb + 1), :] for sb in range(n_sb)]
        bl_rows = jnp.concatenate([jnp.broadcast_to(bl, (n_new, GLA_DK)) for bl in last], axis=0)
        kd = (k * jnp.exp(bl_rows - b)).astype(BF16)
        seq_of_vrow = lax.broadcasted_iota(I32, (n, GLA_DV), 0) // n_new
        seq_of_krow = lax.broadcasted_iota(I32, (n, GLA_DK), 0) // n_new
        o = o_intra
        for sb in range(n_sb):
            s0 = si_ref[sb, h]
            o_sb = jnp.dot(qe, s0.astype(BF16), preferred_element_type=F32)
            o = o + jnp.where(seq_of_vrow == sb, o_sb, 0.0)
            kd_sb = jnp.where(seq_of_krow == sb, kd, jnp.zeros_like(kd))
            upd = lax.dot_general(kd_sb, v, TN_DIMS, preferred_element_type=F32)
            decay_col = jnp.transpose(jnp.broadcast_to(jnp.exp(last[sb]), (8, GLA_DK)))[:, 0:1]
            so_ref[sb, h] = s0 * decay_col + upd
        o_ref[:, vs] = _gla_epilogue(o, r_ref[:, vs], ng_ref[...])


def gla_sample(q, k, g, v, r, norm_g, state, n_prompt_rows, n_new):
    n_seq = state.shape[0]
    sb = SAMPLE_SEQS
    rows = sb * n_new
    kd, vd = GLA_KEY_DIM, GLA_VAL_DIM
    base = n_prompt_rows // rows
    tok = lambda i: (base + i, 0)
    seq = lambda i: (i, 0, 0, 0)
    sblock = (sb, GLA_HEADS, GLA_DK, GLA_DV)
    return pl.pallas_call(
        functools.partial(_gla_sample_kernel, n_new),
        out_shape=(jax.ShapeDtypeStruct((n_seq * n_new, vd), BF16), jax.ShapeDtypeStruct(state.shape, F32)),
        grid=(n_seq // sb,),
        in_specs=[pl.BlockSpec((rows, kd), tok), pl.BlockSpec((rows, kd), tok), pl.BlockSpec((rows, kd), tok),
                  pl.BlockSpec((rows, vd), tok), pl.BlockSpec((rows, vd), tok),
                  pl.BlockSpec((1, GLA_DV), lambda i: (0, 0)),
                  pl.BlockSpec(sblock, seq)],
        out_specs=(pl.BlockSpec((rows, vd), lambda i: (i, 0)), pl.BlockSpec(sblock, seq)),
        compiler_params=_params(("parallel",)),
        name="gla_sample",
    )(q, k, g, v, r, norm_g.reshape(1, -1), state)


def _post_mixer_kernel(npt, ap_ref, as_ref, wo_ref, xp_ref, xs_ref, g1s_ref, g1t_ref, gf_ref, scs_ref, sct_ref, shs_ref, sht_ref,
                       rw_ref, rb_ref, tri_ref,
                       x1_ref, h_ref, eidx_ref, w_ref, rank_ref, cnt_ref, carry_ref):
    i = pl.program_id(0)
    is_s = i >= npt

    @pl.when(i == 0)
    def _():
        carry_ref[...] = jnp.zeros(carry_ref.shape, F32)

    a = jnp.where(is_s, as_ref[...], ap_ref[...])
    x1 = _pick(is_s, xp_ref, xs_ref) + _pick(is_s, g1s_ref, g1t_ref) * jnp.dot(a, wo_ref[...],
                                                                               preferred_element_type=F32)
    x1_ref[...] = x1
    h = _norm_mod(x1, gf_ref[...], _pick(is_s, scs_ref, sct_ref), _pick(is_s, shs_ref, sht_ref))
    h_ref[...] = _pack_bf16_pairs(h)

    h1, h2, _ = _split3(h)
    r1, r2, _ = _split3(rw_ref[...])
    logits = (lax.dot_general(r1, h1, NT_DIMS, preferred_element_type=F32)
              + lax.dot_general(r1, h2, NT_DIMS, preferred_element_type=F32)
              + lax.dot_general(r2, h1, NT_DIMS, preferred_element_type=F32))
    scores = jax.nn.sigmoid(logits)
    sel = scores + rb_ref[...]
    tm = sel.shape[1]
    gsz = N_EXPERTS // N_GROUPS

    sub = lax.broadcasted_iota(I32, (gsz, tm), 0)
    blocks, gscore = [], []
    for g in range(N_GROUPS):
        blk = sel[gsz * g:gsz * (g + 1)]
        m1 = jnp.max(blk, axis=0, keepdims=True)
        first = jnp.min(jnp.where(blk == m1, sub, gsz), axis=0, keepdims=True)
        m2 = jnp.max(jnp.where(sub == first, -jnp.inf, blk), axis=0, keepdims=True)
        blocks.append(blk)
        gscore.append(m1 + m2)
    masked = []
    for g in range(N_GROUPS):
        beaten = jnp.zeros((1, tm), I32)
        for o in range(N_GROUPS):
            if o == g:
                continue
            wins = (gscore[o] > gscore[g]) | ((gscore[o] == gscore[g]) & (o < g))
            beaten = beaten + wins.astype(I32)
        masked.append(jnp.where(beaten < TOPK_GROUPS, blocks[g], -jnp.inf))
    cur = jnp.concatenate(masked, axis=0)

    eid = lax.broadcasted_iota(I32, (N_EXPERTS, tm), 0)
    picked, weights = [], []
    onehot = jnp.zeros((N_EXPERTS, tm), F32)
    for _ in range(TOP_K):
        m = jnp.max(cur, axis=0, keepdims=True)
        idx = jnp.min(jnp.where(cur == m, eid, N_EXPERTS), axis=0, keepdims=True)
        hit = eid == idx
        picked.append(idx)
        weights.append(jnp.sum(jnp.where(hit, scores, 0.0), axis=0, keepdims=True))
        onehot = jnp.where(hit, 1.0, onehot)
        cur = jnp.where(hit, -jnp.inf, cur)
    wsum = weights[0]
    for wk in weights[1:]:
        wsum = wsum + wk
    scale = ROUTED_SCALE / wsum

    before = jnp.dot(onehot.astype(BF16), tri_ref[...], preferred_element_type=F32) + carry_ref[...]
    carry = carry_ref[...] + jnp.sum(onehot, axis=1, keepdims=True)
    carry_ref[...] = carry
    cnt_ref[...] = jnp.broadcast_to(carry, cnt_ref.shape)
    for kk in range(TOP_K):
        eidx_ref[kk:kk + 1, :] = picked[kk]
        w_ref[kk:kk + 1, :] = weights[kk] * scale
        rank_ref[kk:kk + 1, :] = jnp.sum(jnp.where(eid == picked[kk], before, 0.0), axis=0, keepdims=True).astype(I32)


def post_mixer(a_prompt, a_sample, wo_bf, x, mod, gffn, router_t, router_b, tri):
    t = x[0].shape[0] + x[1].shape[0]
    tm = mod.tile
    npt = mod.npt
    row = lambda i: (i, 0)
    col = lambda i: (0, i)
    const = lambda i: (0, 0)
    return pl.pallas_call(
        functools.partial(_post_mixer_kernel, mod.npt),
        out_shape=(jax.ShapeDtypeStruct((t, D_MODEL), F32), jax.ShapeDtypeStruct((t, D_MODEL // 2), U32),
                   jax.ShapeDtypeStruct((TOP_K, t), I32), jax.ShapeDtypeStruct((TOP_K, t), F32),
                   jax.ShapeDtypeStruct((TOP_K, t), I32), jax.ShapeDtypeStruct((N_EXPERTS, 128), F32)),
        grid=(t // tm,),
        in_specs=[pl.BlockSpec((tm, D_MODEL), lambda i: (jnp.minimum(i, npt - 1), 0)),
                  pl.BlockSpec((tm, D_MODEL), lambda i: (jnp.maximum(i - npt, 0), 0)),
                  pl.BlockSpec((D_MODEL, D_MODEL), const)] + _stream_specs(x, tm, npt)
        + mod.specs(2) + [pl.BlockSpec((1, D_MODEL), const)] + mod.specs(4) + mod.specs(3)
        + [pl.BlockSpec((N_EXPERTS, D_MODEL), const), pl.BlockSpec((N_EXPERTS, 1), const),
           pl.BlockSpec((tm, tm), const)],
        out_specs=(pl.BlockSpec((tm, D_MODEL), row), pl.BlockSpec((tm, D_MODEL // 2), row),
                   pl.BlockSpec((TOP_K, tm), col), pl.BlockSpec((TOP_K, tm), col), pl.BlockSpec((TOP_K, tm), col),
                   pl.BlockSpec((N_EXPERTS, 128), const)),
        scratch_shapes=[pltpu.VMEM((N_EXPERTS, 1), F32)],
        compiler_params=_params(("arbitrary",)),
        name="post_mixer",
    )(a_prompt, a_sample, wo_bf, *x, *mod.operands(2), gffn.reshape(1, -1), *mod.operands(4), *mod.operands(3),
      router_t, router_b.reshape(-1, 1), tri)


def _row_copy(src, src_row, dst, dst_row, sem):
    return pltpu.make_async_copy(src.at[src_row], dst.at[dst_row], sem)


def _by_parity(i, fn):
    @pl.when(i % 2 == 0)
    def _():
        fn(0)

    @pl.when(i % 2 == 1)
    def _():
        fn(1)


def _dispatch_kernel(zb_ref, h_ref, dest_hbm, xs_hbm, idx_a, idx_b, zero_ref, slab_ref, sem_idx, sem_zero, sem_rows):
    i = pl.program_id(0)
    te = h_ref.shape[0]
    _store_row_slabs(slab_ref, slice(None), h_ref[...])
    idx_bufs = (idx_a, idx_b)

    def idx_copy(tile, p):
        return pltpu.make_async_copy(dest_hbm.at[tile], idx_bufs[p], sem_idx.at[p])

    def zero_copy(e):
        return pltpu.make_async_copy(zero_ref, xs_hbm.at[pl.ds(zb_ref[e] * EXPERT_BLOCK, EXPERT_BLOCK)], sem_zero)

    @pl.when(i == 0)
    def _():
        idx_copy(0, 0).start()
        zero_ref[...] = jnp.zeros(zero_ref.shape, U32)

        def start(e, carry):
            @pl.when(zb_ref[e] >= 0)
            def _():
                zero_copy(e).start()
            return carry

        def wait(e, carry):
            @pl.when(zb_ref[e] >= 0)
            def _():
                zero_copy(e).wait()
            return carry

        lax.fori_loop(0, zb_ref.shape[0], start, 0)
        lax.fori_loop(0, zb_ref.shape[0], wait, 0)

    def step(p):
        @pl.when(i + 1 < pl.num_programs(0))
        def _():
            idx_copy(i + 1, 1 - p).start()

        idx_copy(i, p).wait()
        idx = idx_bufs[p]

        def issue(t, carry):
            for kk in range(TOP_K):
                _row_copy(slab_ref, t, xs_hbm, idx[kk * te + t], sem_rows).start(priority=kk % 2)
            return carry

        def drain(t, carry):
            for kk in range(TOP_K):
                _row_copy(slab_ref, t, xs_hbm, idx[kk * te + t], sem_rows).wait()
            return carry

        lax.fori_loop(0, te, issue, 0)
        lax.fori_loop(0, te, drain, 0)

    _by_parity(i, step)


def moe_dispatch(h, dest_tiles, zero_blocks, n_slots):
    t = h.shape[0]
    te = ROUTE_TILE
    return pl.pallas_call(
        _dispatch_kernel,
        out_shape=jax.ShapeDtypeStruct((n_slots, ROW_CHUNKS, 128), U32),
        grid_spec=pltpu.PrefetchScalarGridSpec(
            num_scalar_prefetch=1,
            grid=(t // te,),
            in_specs=[pl.BlockSpec((te, h.shape[1]), lambda i, zb: (i, 0)), pl.BlockSpec(memory_space=pl.ANY)],
            out_specs=pl.BlockSpec(memory_space=pl.ANY),
            scratch_shapes=[pltpu.SMEM((te * TOP_K,), I32), pltpu.SMEM((te * TOP_K,), I32),
                            pltpu.VMEM((EXPERT_BLOCK, ROW_CHUNKS, 128), U32), pltpu.VMEM((te, ROW_CHUNKS, 128), U32),
                            pltpu.SemaphoreType.DMA((2,)), pltpu.SemaphoreType.DMA, pltpu.SemaphoreType.DMA],
        ),
        compiler_params=_params(("arbitrary",)),
        name="moe_dispatch",
    )(zero_blocks, h, dest_tiles)


def _expert_kernel(be_ref, nu_ref, xs_ref, wg0_ref, wu0_ref, wd0_ref, wg1_ref, wu1_ref, wd1_ref, ys_ref,
                   wg_bf, wu_bf, wd_bf):
    b = pl.program_id(0)
    rb = EXPERT_BLOCK
    used = 2 * b < nu_ref[0]
    for s, (wg, wu, wd) in enumerate(((wg0_ref, wu0_ref, wd0_ref), (wg1_ref, wu1_ref, wd1_ref))):
        j = 2 * b + s
        fresh = (b == 0) | (be_ref[j] != be_ref[jnp.maximum(j - 2, 0)])

        @pl.when(used & fresh)
        def _():
            wg_bf[s] = wg[...].astype(BF16)
            wu_bf[s] = wu[...].astype(BF16)
            wd_bf[s] = wd[...].astype(BF16)

    @pl.when(used)
    def _():
        for s in range(2):
            rows = slice(rb * s, rb * (s + 1))
            x = _unpack_bf16_pairs(_load_row_slabs(xs_ref, rows))
            hg = jnp.dot(x, wg_bf[s], preferred_element_type=F32)
            hu = jnp.dot(x, wu_bf[s], preferred_element_type=F32)
            y = jnp.dot((_silu(hg) * hu).astype(BF16), wd_bf[s], preferred_element_type=F32)
            ys_ref[rows, :] = _pack_bf16_pairs(y)


def moe_experts(xs, block_expert, n_used, wg, wu, wd, layer):
    n_slots = xs.shape[0]
    rb = EXPERT_BLOCK
    ff = wg.shape[3]
    rows_in = lambda b, be, nu: (jnp.minimum(b, (nu[0] - 1) // 2), 0, 0)
    rows_out = lambda b, be, nu: (jnp.minimum(b, (nu[0] - 1) // 2), 0)
    w_in = lambda s: (lambda b, be, nu: (layer, be[2 * b + s], 0, 0))
    return pl.pallas_call(
        _expert_kernel,
        out_shape=jax.ShapeDtypeStruct((n_slots, D_MODEL // 2), U32),
        grid_spec=pltpu.PrefetchScalarGridSpec(
            num_scalar_prefetch=2,
            grid=(n_slots // (2 * rb),),
            in_specs=[pl.BlockSpec((2 * rb,) + xs.shape[1:], rows_in)]
            + [pl.BlockSpec((None, None, D_MODEL, ff), w_in(0)), pl.BlockSpec((None, None, D_MODEL, ff), w_in(0)),
               pl.BlockSpec((None, None, ff, D_MODEL), w_in(0)),
               pl.BlockSpec((None, None, D_MODEL, ff), w_in(1)), pl.BlockSpec((None, None, D_MODEL, ff), w_in(1)),
               pl.BlockSpec((None, None, ff, D_MODEL), w_in(1))],
            out_specs=pl.BlockSpec((2 * rb, D_MODEL // 2), rows_out),
            scratch_shapes=[pltpu.VMEM((2, D_MODEL, ff), BF16), pltpu.VMEM((2, D_MODEL, ff), BF16),
                            pltpu.VMEM((2, ff, D_MODEL), BF16)],
        ),
        compiler_params=_params(("arbitrary",)),
        name="moe_experts",
    )(block_expert, n_used, xs, wg, wu, wd, wg, wu, wd)


def _combine_kernel(npt, final, h_ref, swg_ref, swu_ref, swd_ref, x1_ref, g2s_ref, g2t_ref, w_ref, fg_ref,
                    dest_hbm, ys_hbm, *rest):
    out_refs, (idx_a, idx_b, ybuf_ref, sem_idx, sem_rows) = rest[:-5], rest[-5:]
    i = pl.program_id(0)
    n = pl.num_programs(0)
    tg = h_ref.shape[0]
    idx_bufs = (idx_a, idx_b)

    def idx_copy(tile, p):
        return pltpu.make_async_copy(dest_hbm.at[tile], idx_bufs[p], sem_idx.at[p])

    def gather_rows(p, wait):
        idx = idx_bufs[p]

        def body(t, carry):
            for kk in range(TOP_K):
                src = 0 if wait else idx[kk * tg + t]
                dst = ybuf_ref.at[p, kk]
                cp = pltpu.make_async_copy(ys_hbm.at[pl.ds(src, 1)], dst.at[pl.ds(t, 1)], sem_rows.at[p])
                cp.wait() if wait else cp.start(priority=kk % 2)
            return carry

        lax.fori_loop(0, tg, body, 0)

    @pl.when(i == 0)
    def _():
        idx_copy(0, 0).start()
        idx_copy(0, 0).wait()
        gather_rows(0, wait=False)

        @pl.when(n > 1)
        def _():
            idx_copy(1, 1).start()

    def prefetch(p):
        @pl.when(i + 1 < n)
        def _():
            idx_copy(i + 1, 1 - p).wait()
            gather_rows(1 - p, wait=False)

        @pl.when(i + 2 < n)
        def _():
            idx_copy(i + 2, p).start()

    _by_parity(i, prefetch)

    hb = _unpack_bf16_pairs(h_ref[...])
    hid = _silu(jnp.dot(hb, swg_ref[...], preferred_element_type=F32)) * jnp.dot(hb, swu_ref[...],
                                                                                 preferred_element_type=F32)
    acc = jnp.dot(hid.astype(BF16), swd_ref[...], preferred_element_type=F32)
    w = w_ref[...]
    gate = _pick(i >= npt, g2s_ref, g2t_ref)

    def finish(p):
        gather_rows(p, wait=True)
        half = D_MODEL // 2
        lo = jnp.zeros((tg, half), F32)
        hi = jnp.zeros((tg, half), F32)
        for kk in range(TOP_K):
            u = ybuf_ref[p, kk]
            wk = w[:, kk:kk + 1]
            lo = lo + lax.bitcast_convert_type(u << 16, F32) * wk
            hi = hi + lax.bitcast_convert_type(u & jnp.uint32(0xFFFF0000), F32) * wk
        routed = jnp.concatenate([lo, hi], axis=1)
        x2 = x1_ref[...] + gate * (routed + acc)
        if final:
            ms = jnp.mean(x2 * x2, axis=-1, keepdims=True)
            x2 = x2 * lax.rsqrt(ms + NORM_EPS) * fg_ref[...]

        @pl.when(i < npt)
        def _():
            out_refs[0][...] = x2

        @pl.when(i >= npt)
        def _():
            out_refs[1][...] = x2

    _by_parity(i, finish)


def moe_combine(h, swg_bf, swu_bf, swd_bf, x1, mod, w_tok, final_g, dest_tiles, ys, final):
    t = h.shape[0]
    tg = mod.tile
    npt = mod.npt
    row = lambda i: (i, 0)
    const = lambda i: (0, 0)
    out_shape = (jax.ShapeDtypeStruct((npt * tg, D_MODEL), F32), jax.ShapeDtypeStruct((t - npt * tg, D_MODEL), F32))
    out_specs = tuple(_stream_specs(None, tg, npt))
    return pl.pallas_call(
        functools.partial(_combine_kernel, mod.npt, final),
        out_shape=out_shape,
        grid=(t // tg,),
        in_specs=[pl.BlockSpec((tg, h.shape[1]), row), pl.BlockSpec(swg_bf.shape, const),
                  pl.BlockSpec(swu_bf.shape, const),
                  pl.BlockSpec(swd_bf.shape, const), pl.BlockSpec((tg, D_MODEL), row)]
        + mod.specs(5)
        + [pl.BlockSpec((tg, TOP_K), row), pl.BlockSpec((1, D_MODEL), const),
           pl.BlockSpec(memory_space=pl.ANY), pl.BlockSpec(memory_space=pl.ANY)],
        out_specs=out_specs,
        scratch_shapes=[pltpu.SMEM((tg * TOP_K,), I32), pltpu.SMEM((tg * TOP_K,), I32),
                        pltpu.VMEM((2, TOP_K, tg, D_MODEL // 2), U32),
                        pltpu.SemaphoreType.DMA((2,)), pltpu.SemaphoreType.DMA((2,))],
        compiler_params=_params(("arbitrary",)),
        name="moe_combine",
    )(h, swg_bf, swu_bf, swd_bf, x1, *mod.operands(5), w_tok, final_g.reshape(1, -1), dest_tiles, ys)


def _slot_kernel(eidx_ref, rank_ref, start_ref, o_ref):
    tm = eidx_ref.shape[1]
    tr = o_ref.shape[2]
    eid = lax.broadcasted_iota(I32, (N_EXPERTS, tm), 0)
    start = start_ref[...]
    for kk in range(TOP_K):
        base = jnp.sum(jnp.where(eid == eidx_ref[kk:kk + 1, :], start, 0.0), axis=0, keepdims=True)
        slot = base.astype(I32) + rank_ref[kk:kk + 1, :]
        for j in range(tm // tr):
            o_ref[j, kk:kk + 1, :] = slot[:, tr * j:tr * (j + 1)]


def assignment_slots(eidx_t, rank_t, pad_start):
    t = eidx_t.shape[1]
    tm = TOKEN_TILE
    tr = ROUTE_TILE
    col = lambda i: (0, i)
    out = pl.pallas_call(
        _slot_kernel,
        out_shape=jax.ShapeDtypeStruct((t // tr, TOP_K, tr), I32),
        grid=(t // tm,),
        in_specs=[pl.BlockSpec((TOP_K, tm), col), pl.BlockSpec((TOP_K, tm), col),
                  pl.BlockSpec((N_EXPERTS, 1), lambda i: (0, 0))],
        out_specs=pl.BlockSpec((tm // tr, TOP_K, tr), lambda i: (i, 0, 0)),
        compiler_params=_params(("parallel",)),
        name="assignment_slots",
    )(eidx_t, rank_t, pad_start.astype(F32).reshape(-1, 1))
    return out.reshape(t // tr, TOP_K * tr)


def _routing_tables(counts, n_blocks):
    rb = EXPERT_BLOCK
    counts = counts.astype(I32)
    padded = (counts + rb - 1) // rb * rb
    pad_end = jnp.cumsum(padded)
    pad_start = pad_end - padded
    n_used = pad_end[-1] // rb
    blocks = jnp.arange(n_blocks, dtype=I32)
    block_expert = jnp.sum((pad_end[None, :] <= (blocks * rb)[:, None]).astype(I32), axis=1)
    last_used = jnp.sum((pad_end <= (n_used - 1) * rb).astype(I32))
    block_expert = jnp.minimum(jnp.where(blocks < n_used, block_expert, last_used), N_EXPERTS - 1)
    zero_blocks = jnp.where(counts % rb != 0, pad_end // rb - 1, -1)
    zero_blocks = jnp.concatenate([zero_blocks, jnp.where(n_used % 2 == 1, n_used, -1).reshape(1)]).astype(I32)
    return pad_start, block_expert, n_used.reshape(1).astype(I32), zero_blocks


def moe_layer(h, x1, eidx_t, w_t, rank_t, counts, mod_route, wg, wu, wd, layer, swg_bf, swu_bf, swd_bf,
              final_g, final):
    t = h.shape[0]
    rb = EXPERT_BLOCK
    tr = ROUTE_TILE
    n_blocks = -(-(t * TOP_K) // rb) + N_EXPERTS
    n_blocks += n_blocks % 2
    pad_start, block_expert, n_used, zero_blocks = _routing_tables(counts, n_blocks)
    dest_tiles = assignment_slots(eidx_t, rank_t, pad_start)
    xs = moe_dispatch(h, dest_tiles, zero_blocks, n_blocks * rb)
    ys = moe_experts(xs, block_expert, n_used, wg, wu, wd, layer)
    return moe_combine(h, swg_bf, swu_bf, swd_bf, x1, mod_route, w_t.T, final_g, dest_tiles, ys, final)


def _rope_tables(n_batch, seq_len, n_seq, n_new):
    half = SWA_HEAD_DIM // 2
    inv = ROPE_THETA ** (-jnp.arange(half, dtype=F32) / half)
    pos = jnp.concatenate([jnp.tile(jnp.arange(seq_len, dtype=F32), n_batch),
                           jnp.tile(PAST_LEN + jnp.arange(n_new, dtype=F32), n_seq)])
    ang = pos[:, None] * inv[None, :]
    cos = jnp.tile(jnp.cos(ang), (1, 128 // half))
    sin = jnp.sin(ang)
    sin = jnp.tile(jnp.concatenate([-sin, sin], axis=1), (1, 128 // SWA_HEAD_DIM))
    return cos, sin


def kernel(x_prompt, x_sample, c_prompt, c_sample, cache_swa_k, cache_swa_v, state_gla, norm_mix_g, norm_ffn_g,
           final_g, ada_w, ada_b, swa_wqkv, swa_sinks, swa_wo, gla_win, gla_wa1, gla_wa2, gla_ba, gla_norm_g,
           gla_wo, moe_router, moe_bias, moe_wg, moe_wu, moe_wd, shared_wg, shared_wu, shared_wd):
    n_batch, seq_len, d = x_prompt.shape
    n_seq, n_new, _ = x_sample.shape
    depth = ada_w.shape[0]
    tp = n_batch * seq_len
    ts = n_seq * n_new
    t = tp + ts
    tm = TOKEN_TILE
    tr = ROUTE_TILE

    x = (x_prompt.reshape(tp, d), x_sample.reshape(ts, d))
    c_all = jnp.concatenate([jnp.repeat(c_sample, n_new, axis=0), c_prompt], axis=0)
    mod = ada_modulation(c_all, ada_w, ada_b)
    cos_tab, sin_tab = _rope_tables(n_batch, seq_len, n_seq, n_new)
    tri = jnp.triu(jnp.ones((tm, tm), BF16), k=1)

    new_k, new_v, new_s = [], [], []
    new_k_s, new_v_s, new_s_s = [], [], []
    for layer in range(depth):
        mod_tok = _Mod(mod, layer, n_batch, seq_len, tm)
        mod_route = _Mod(mod, layer, n_batch, seq_len, tr)
        m = layer // 2
        if layer % 2 == 0:
            q, k, v, k_dup, v_dup = swa_qkv(x, norm_mix_g[layer], mod_tok, swa_wqkv[m].astype(BF16), cos_tab, sin_tab)
            a_p = swa_prompt_attention(q, k_dup, v_dup, swa_sinks[m], n_batch, seq_len)
            nk = SWA_KV_HEADS * SWA_HEAD_DIM
            a_s, ck, cv = swa_sample_attention(q, k, v, cache_swa_k[m].reshape(n_seq, WINDOW, nk),
                                             cache_swa_v[m].reshape(n_seq, WINDOW, nk), swa_sinks[m], tp, n_new)
            kv_shape = (n_batch, WINDOW, SWA_KV_HEADS, SWA_HEAD_DIM)
            tails = [slice((b + 1) * seq_len - WINDOW, (b + 1) * seq_len) for b in range(n_batch)]
            new_k.append(jnp.stack([k[rows] for rows in tails]).reshape(kv_shape))
            new_v.append(jnp.stack([v[rows] for rows in tails]).reshape(kv_shape))
            new_k_s.append(ck.reshape(n_seq, WINDOW, SWA_KV_HEADS, SWA_HEAD_DIM))
            new_v_s.append(cv.reshape(n_seq, WINDOW, SWA_KV_HEADS, SWA_HEAD_DIM))
            wo = swa_wo[m]
        else:
            pad = jnp.zeros((d, 128 - GLA_GATE_RANK), F32)
            win_ext = jnp.concatenate([gla_win[m], gla_wa1[m], pad], axis=1).astype(BF16)
            wa2_pad = jnp.concatenate([gla_wa2[m], jnp.zeros((128 - GLA_GATE_RANK, GLA_KEY_DIM), F32)],
                                      axis=0).astype(BF16)
            q, k, v, r, gate = gla_project(x, norm_mix_g[layer], mod_tok, win_ext, wa2_pad, gla_ba[m])
            a_p, s_prompt = gla_prompt(q, k, gate, v, r, gla_norm_g[m], n_batch, seq_len)
            a_s, s_sample = gla_sample(q, k, gate, v, r, gla_norm_g[m], state_gla[m], tp, n_new)
            new_s.append(s_prompt)
            new_s_s.append(s_sample)
            wo = gla_wo[m]
        x1, h, eidx_t, w_t, rank_t, cnt = post_mixer(a_p, a_s, wo.astype(BF16), x, mod_tok, norm_ffn_g[layer],
                                                     moe_router[layer].T, moe_bias[layer], tri)
        x = moe_layer(h, x1, eidx_t, w_t, rank_t, cnt[:, 0], mod_route,
                      moe_wg, moe_wu, moe_wd, layer,
                      shared_wg[layer].astype(BF16), shared_wu[layer].astype(BF16), shared_wd[layer].astype(BF16),
                      final_g, layer == depth - 1)

    y_prompt = x[0].reshape(n_batch, seq_len, d)
    y_sample = x[1].reshape(n_seq, n_new, d)
    return (y_prompt, y_sample, jnp.stack(new_k), jnp.stack(new_v), jnp.stack(new_k_s), jnp.stack(new_v_s),
            jnp.stack(new_s), jnp.stack(new_s_s))
```

```python
import functools

import jax
import jax.numpy as jnp
from jax import lax
from jax.experimental import pallas as pl
from jax.experimental.pallas import tpu as pltpu

F32 = jnp.float32
BF16 = jnp.bfloat16
I32 = jnp.int32
U32 = jnp.uint32

D_MODEL = 1024
PAST_LEN = 8192
SWA_HEAD_DIM = 64
SWA_HEADS = 16
SWA_KV_HEADS = 4
SWA_GROUP = 4
WINDOW = 128
ROPE_THETA = 10000.0
GLA_HEADS = 4
GLA_DK = 128
GLA_DV = 256
GLA_KEY_DIM = 512
GLA_VAL_DIM = 1024
GLA_GATE_RANK = 16
GLA_GATE_NORMALIZER = 16.0
GLA_CHUNK = 64
N_EXPERTS = 64
TOP_K = 8
N_GROUPS = 8
TOPK_GROUPS = 4
EXPERT_FF = 256
ROUTED_SCALE = 2.5
NORM_EPS = 1e-6

TOKEN_TILE = 512
ROUTE_TILE = 512
EXPERT_BLOCK = 256
ADA_TILE = 512
SAMPLE_SEQS = 8
GLA_PROMPT_BATCH = 4
ROW_COPY_UNROLL = 4
SWA_PROMPT_BATCH = 4
VMEM_LIMIT = 48 * 1024 * 1024

NT_DIMS = (((1,), (1,)), ((), ()))
TN_DIMS = (((0,), (0,)), ((), ()))


def _params(semantics):
    return pltpu.CompilerParams(dimension_semantics=semantics, vmem_limit_bytes=VMEM_LIMIT)


def _silu(x):
    return x * jax.nn.sigmoid(x)


def _norm_mod(x, g, sc, sh):
    ms = jnp.mean(x * x, axis=-1, keepdims=True)
    return (x * lax.rsqrt(ms + NORM_EPS) * g) * (1.0 + sc) + sh


def _pack_bf16_pairs(x):
    half = x.shape[1] // 2
    xb = x.astype(BF16).astype(F32)
    lo = lax.bitcast_convert_type(xb[:, :half], U32) >> 16
    hi = lax.bitcast_convert_type(xb[:, half:], U32) & jnp.uint32(0xFFFF0000)
    return lo | hi


def _unpack_bf16_pairs(u):
    lo = lax.bitcast_convert_type(u << 16, F32)
    hi = lax.bitcast_convert_type(u & jnp.uint32(0xFFFF0000), F32)
    return jnp.concatenate([lo, hi], axis=1).astype(BF16)


ROW_CHUNKS = D_MODEL // 2 // 128


def _store_row_slabs(ref, rows, words):
    for c in range(ROW_CHUNKS):
        ref[rows, c, :] = words[:, 128 * c:128 * (c + 1)]


def _load_row_slabs(ref, rows):
    return jnp.concatenate([ref[rows, c, :] for c in range(ROW_CHUNKS)], axis=1)


def _split3(x):
    x1 = x.astype(BF16)
    r1 = x - x1.astype(F32)
    x2 = r1.astype(BF16)
    x3 = (r1 - x2.astype(F32)).astype(BF16)
    return x1, x2, x3


def _ada_kernel(c_ref, w_ref, b_ref, o_ref):
    s = _silu(c_ref[...]).astype(BF16)
    o_ref[...] = jnp.dot(s, w_ref[...].astype(BF16), preferred_element_type=F32) + b_ref[...]


def ada_modulation(c_all, ada_w, ada_b):
    depth, d, n = ada_w.shape
    rows = c_all.shape[0]
    return pl.pallas_call(
        _ada_kernel,
        out_shape=jax.ShapeDtypeStruct((depth, rows, n), F32),
        grid=(depth, n // ADA_TILE),
        in_specs=[
            pl.BlockSpec((rows, d), lambda l, j: (0, 0)),
            pl.BlockSpec((None, d, ADA_TILE), lambda l, j: (l, 0, j)),
            pl.BlockSpec((None, 1, ADA_TILE), lambda l, j: (l, 0, j)),
        ],
        out_specs=pl.BlockSpec((None, rows, ADA_TILE), lambda l, j: (l, 0, j)),
        compiler_params=_params(("parallel", "parallel")),
        name="ada_modulation",
    )(c_all, ada_w, ada_b.reshape(depth, 1, n))


class _Mod:
    def __init__(self, mod, layer, n_batch, seq_len, tile):
        depth, rows, n = mod.shape
        self.tile = tile
        self.layer = layer
        self.npt = n_batch * seq_len // tile
        self.mod_tok = mod
        self.mod_seq = mod[:, rows - n_batch:].reshape(depth, n_batch, 1, n)
        self.tiles_per_seq = seq_len // tile

    def operands(self, chunk):
        del chunk
        return [self.mod_seq, self.mod_tok]

    def specs(self, chunk):
        l, npt, tps = self.layer, self.npt, self.tiles_per_seq
        n_seq = self.mod_seq.shape[1]
        seq_spec = pl.BlockSpec((None, None, 1, D_MODEL),
                                lambda i, *_: (l, jnp.minimum(i // tps, n_seq - 1), 0, chunk))
        tok_spec = pl.BlockSpec((None, self.tile, D_MODEL),
                                lambda i, *_: (l, jnp.maximum(i - npt, 0), chunk))
        return [seq_spec, tok_spec]


def _pick(is_sample, seq_ref, tok_ref):
    return jnp.where(is_sample, tok_ref[...], seq_ref[...])


def _stream_specs(x_pair, tile, npt):
    del x_pair
    return [pl.BlockSpec((tile, D_MODEL), lambda i, *_: (jnp.minimum(i, npt - 1), 0)),
            pl.BlockSpec((tile, D_MODEL), lambda i, *_: (jnp.maximum(i - npt, 0), 0))]


def _swa_qkv_kernel(npt, xp_ref, xs_ref, g_ref, scs_ref, sct_ref, shs_ref, sht_ref, w_ref, cos_ref, sin_ref,
                    q_ref, k_ref, v_ref, kd_ref, vd_ref):
    is_s = pl.program_id(0) >= npt
    h = _norm_mod(_pick(is_s, xp_ref, xs_ref), g_ref[...], _pick(is_s, scs_ref, sct_ref), _pick(is_s, shs_ref, sht_ref))
    qkv = jnp.dot(h.astype(BF16), w_ref[...], preferred_element_type=F32)
    cos = cos_ref[...]
    sin = sin_ref[...]
    lane = lax.broadcasted_iota(I32, cos.shape, 1)
    first_half = (lane % SWA_HEAD_DIM) < (SWA_HEAD_DIM // 2)

    def rope(xc):
        rot = jnp.where(first_half, pltpu.roll(xc, 128 - SWA_HEAD_DIM // 2, 1), pltpu.roll(xc, SWA_HEAD_DIM // 2, 1))
        return xc * cos + rot * sin

    nq = SWA_HEADS * SWA_HEAD_DIM
    nk = SWA_KV_HEADS * SWA_HEAD_DIM
    for c in range(nq // 128):
        q_ref[:, 128 * c:128 * (c + 1)] = (rope(qkv[:, 128 * c:128 * (c + 1)]) * (SWA_HEAD_DIM ** -0.5)).astype(BF16)
    low = lane < SWA_HEAD_DIM

    def spread(chunk):
        rolled = pltpu.roll(chunk, SWA_HEAD_DIM, 1)
        return jnp.where(low, chunk, rolled).astype(BF16), jnp.where(low, rolled, chunk).astype(BF16)

    for c in range(nk // 128):
        kc = rope(qkv[:, nq + 128 * c:nq + 128 * (c + 1)])
        vc = qkv[:, nq + nk + 128 * c:nq + nk + 128 * (c + 1)]
        k_ref[:, 128 * c:128 * (c + 1)] = kc
        v_ref[:, 128 * c:128 * (c + 1)] = vc
        kd_ref[:, 256 * c:256 * c + 128], kd_ref[:, 256 * c + 128:256 * (c + 1)] = spread(kc)
        vd_ref[:, 256 * c:256 * c + 128], vd_ref[:, 256 * c + 128:256 * (c + 1)] = spread(vc)


def swa_qkv(x, g, mod, w_bf, cos_tab, sin_tab):
    t = x[0].shape[0] + x[1].shape[0]
    tm = mod.tile
    nq = SWA_HEADS * SWA_HEAD_DIM
    nk = SWA_KV_HEADS * SWA_HEAD_DIM
    row = lambda i: (i, 0)
    return pl.pallas_call(
        functools.partial(_swa_qkv_kernel, mod.npt),
        out_shape=(jax.ShapeDtypeStruct((t, nq), BF16), jax.ShapeDtypeStruct((t, nk), F32),
                   jax.ShapeDtypeStruct((t, nk), F32), jax.ShapeDtypeStruct((t, 2 * nk), BF16),
                   jax.ShapeDtypeStruct((t, 2 * nk), BF16)),
        grid=(t // tm,),
        in_specs=_stream_specs(x, tm, mod.npt) + [pl.BlockSpec((1, D_MODEL), lambda i: (0, 0))]
        + mod.specs(1) + mod.specs(0)
        + [pl.BlockSpec(w_bf.shape, lambda i: (0, 0)), pl.BlockSpec((tm, 128), row), pl.BlockSpec((tm, 128), row)],
        out_specs=(pl.BlockSpec((tm, nq), row), pl.BlockSpec((tm, nk), row), pl.BlockSpec((tm, nk), row),
                   pl.BlockSpec((tm, 2 * nk), row), pl.BlockSpec((tm, 2 * nk), row)),
        compiler_params=_params(("parallel",)),
        name="swa_qkv",
    )(*x, g.reshape(1, -1), *mod.operands(1), *mod.operands(0), w_bf, cos_tab, sin_tab)


def _sink_softmax(s, sink_col):
    m = jnp.maximum(jnp.max(s, axis=-1, keepdims=True), sink_col)
    e = jnp.exp(s - m)
    den = jnp.sum(e, axis=-1, keepdims=True) + jnp.exp(sink_col - m)
    return e * (1.0 / den)


def _pair_attention(sink_ref, q_ref, kblks, vblks, mask, o_ref):
    rows = q_ref.shape[0]
    half = mask.shape[1]
    per_group = SWA_GROUP // 2
    scores = []
    for g in range(SWA_KV_HEADS):
        q2 = jnp.concatenate([q_ref[:, 128 * c:128 * (c + 1)] for c in range(per_group * g, per_group * (g + 1))],
                             axis=0)
        s = lax.dot_general(q2, kblks[g], NT_DIMS, preferred_element_type=F32)
        for j in range(per_group):
            scores += [s[rows * j:rows * (j + 1), :half], s[rows * j:rows * (j + 1), half:]]
    s_all = jnp.where(mask, jnp.stack(scores), -jnp.inf)
    sinks = jnp.stack([jnp.full((1, 1), sink_ref[h], F32) for h in range(SWA_HEADS)])
    p_all = _sink_softmax(s_all, sinks).astype(BF16)
    for g in range(SWA_KV_HEADS):
        chunks = range(per_group * g, per_group * (g + 1))
        p2 = jnp.concatenate([jnp.concatenate([p_all[2 * c], p_all[2 * c + 1]], axis=1) for c in chunks], axis=0)
        o = jnp.dot(p2, vblks[g], preferred_element_type=F32).astype(BF16)
        for j, c in enumerate(chunks):
            o_ref[:, 128 * c:128 * (c + 1)] = o[rows * j:rows * (j + 1)]


def _swa_prompt_kernel(sink_ref, *refs):
    o_ref = refs[-1]
    j = pl.program_id(1)
    blk = refs[0].shape[0]
    qi = lax.broadcasted_iota(I32, (blk, 2 * blk), 0)
    sj = lax.broadcasted_iota(I32, (blk, 2 * blk), 1)
    rel = qi + blk - sj
    mask = (rel >= 0) & (rel <= WINDOW) & ((sj >= blk) | (j > 0))
    low = lax.broadcasted_iota(I32, (2 * blk, 128), 1) < SWA_HEAD_DIM
    zero = jnp.zeros((2 * blk, 128), BF16)
    for i in range(SWA_PROMPT_BATCH):
        q_ref, kc_ref, kp_ref, vc_ref, vp_ref = refs[5 * i:5 * (i + 1)]
        kblks, vblks = [], []
        for g in range(SWA_KV_HEADS):
            cs = slice(128 * g, 128 * (g + 1))
            kcat = jnp.concatenate([kp_ref[:, cs], kc_ref[:, cs]], axis=0)
            vcat = jnp.concatenate([vp_ref[:, cs], vc_ref[:, cs]], axis=0)
            kblks.append(jnp.concatenate([jnp.where(low, kcat, zero), jnp.where(low, zero, kcat)], axis=0))
            vblks.append(jnp.concatenate([jnp.where(low, vcat, zero), jnp.where(low, zero, vcat)], axis=0))
        _pair_attention(sink_ref, q_ref, kblks, vblks, mask, o_ref.at[i])


def swa_prompt_attention(q, k, v, sinks, n_batch, seq_len):
    blk = WINDOW
    nsb = SWA_PROMPT_BATCH
    nb = seq_len // blk
    nq = q.shape[1]
    nk = k.shape[1]
    in_specs, operands = [pl.BlockSpec(memory_space=pltpu.SMEM)], [sinks]
    for i in range(nsb):
        cur = lambda b, j, i=i: ((nsb * b + i) * nb + j, 0)
        prev = lambda b, j, i=i: ((nsb * b + i) * nb + jnp.maximum(j - 1, 0), 0)
        in_specs += [pl.BlockSpec((blk, nq), cur), pl.BlockSpec((blk, nk), cur), pl.BlockSpec((blk, nk), prev),
                     pl.BlockSpec((blk, nk), cur), pl.BlockSpec((blk, nk), prev)]
        operands += [q, k, k, v, v]
    out = pl.pallas_call(
        _swa_prompt_kernel,
        out_shape=jax.ShapeDtypeStruct((n_batch, seq_len, nq), BF16),
        grid=(n_batch // nsb, nb),
        in_specs=in_specs,
        out_specs=pl.BlockSpec((nsb, blk, nq), lambda b, j: (b, j, 0)),
        compiler_params=_params(("parallel", "parallel")),
        name="swa_prompt_attention",
    )(*operands)
    return out.reshape(n_batch * seq_len, nq)


def _swa_sample_kernel(n_new, sink_ref, q_ref, kn_ref, vn_ref, ck_ref, cv_ref, o_ref, nk_ref, nv_ref):
    n_sb, win, _ = ck_ref.shape
    per_seq = win + n_new
    rows = n_sb * n_new
    cols = n_sb * per_seq
    keys, vals = [], []
    for sb in range(n_sb):
        r0 = sb * n_new
        kc = ck_ref[sb]
        vc = cv_ref[sb]
        kn = kn_ref[r0:r0 + n_new, :]
        vn = vn_ref[r0:r0 + n_new, :]
        nk_ref[sb, 0:win - n_new, :] = kc[n_new:]
        nk_ref[sb, win - n_new:win, :] = kn
        nv_ref[sb, 0:win - n_new, :] = vc[n_new:]
        nv_ref[sb, win - n_new:win, :] = vn
        keys += [kc, kn]
        vals += [vc, vn]
    keys = jnp.concatenate(keys, axis=0)
    vals = jnp.concatenate(vals, axis=0)
    ri = lax.broadcasted_iota(I32, (rows, cols), 0)
    ci = lax.broadcasted_iota(I32, (rows, cols), 1)
    ti = ri % n_new
    si = ci % per_seq
    mask = (ri // n_new == ci // per_seq) & (si >= ti) & (si <= ti + WINDOW)
    low = lax.broadcasted_iota(I32, (cols, 128), 1) < SWA_HEAD_DIM
    zero = jnp.zeros((cols, 128), BF16)

    def block_diag(chunk, first):
        rolled = pltpu.roll(chunk, SWA_HEAD_DIM, 1)
        both = (jnp.where(low, chunk, rolled) if first else jnp.where(low, rolled, chunk)).astype(BF16)
        return jnp.concatenate([jnp.where(low, both, zero), jnp.where(low, zero, both)], axis=0)

    chunks = [slice(128 * (g // 2), 128 * (g // 2 + 1)) for g in range(SWA_KV_HEADS)]
    kblks = [block_diag(keys[:, cs], g % 2 == 0) for g, cs in enumerate(chunks)]
    vblks = [block_diag(vals[:, cs], g % 2 == 0) for g, cs in enumerate(chunks)]
    _pair_attention(sink_ref, q_ref, kblks, vblks, mask, o_ref)


def swa_sample_attention(q, k, v, cache_k, cache_v, sinks, n_prompt_rows, n_new):
    n_seq, win, nk = cache_k.shape
    sb = SAMPLE_SEQS
    rows = sb * n_new
    nq = q.shape[1]
    base = n_prompt_rows // rows
    tok = lambda i: (base + i, 0)
    seq = lambda i: (i, 0, 0)
    return pl.pallas_call(
        functools.partial(_swa_sample_kernel, n_new),
        out_shape=(jax.ShapeDtypeStruct((n_seq * n_new, nq), BF16),
                   jax.ShapeDtypeStruct(cache_k.shape, F32), jax.ShapeDtypeStruct(cache_v.shape, F32)),
        grid=(n_seq // sb,),
        in_specs=[pl.BlockSpec(memory_space=pltpu.SMEM),
                  pl.BlockSpec((rows, nq), tok), pl.BlockSpec((rows, nk), tok), pl.BlockSpec((rows, nk), tok),
                  pl.BlockSpec((sb, win, nk), seq), pl.BlockSpec((sb, win, nk), seq)],
        out_specs=(pl.BlockSpec((rows, nq), lambda i: (i, 0)), pl.BlockSpec((sb, win, nk), seq),
                   pl.BlockSpec((sb, win, nk), seq)),
        compiler_params=_params(("parallel",)),
        name="swa_sample_attention",
    )(sinks, q, k, v, cache_k, cache_v)


def _gla_proj_kernel(npt, xp_ref, xs_ref, g_ref, scs_ref, sct_ref, shs_ref, sht_ref, w_ref, wa2_ref, ba_ref,
                     q_ref, k_ref, v_ref, r_ref, gate_ref):
    is_s = pl.program_id(0) >= npt
    h = _norm_mod(_pick(is_s, xp_ref, xs_ref), g_ref[...], _pick(is_s, scs_ref, sct_ref), _pick(is_s, shs_ref, sht_ref))
    proj = jnp.dot(h.astype(BF16), w_ref[...], preferred_element_type=F32)
    kd = GLA_KEY_DIM
    vd = GLA_VAL_DIM
    q_ref[...] = proj[:, :kd] * (GLA_DK ** -0.5)
    k_ref[...] = proj[:, kd:2 * kd]
    v_ref[...] = proj[:, 2 * kd:2 * kd + vd].astype(BF16)
    r_ref[...] = proj[:, 2 * kd + vd:2 * kd + 2 * vd]
    low = proj[:, 2 * kd + 2 * vd:].astype(BF16)
    z = jnp.dot(low, wa2_ref[...], preferred_element_type=F32) + ba_ref[...]
    log_sig = jnp.minimum(z, 0.0) - jnp.log1p(jnp.exp(-jnp.abs(z)))
    gate_ref[...] = log_sig / GLA_GATE_NORMALIZER


def gla_project(x, g, mod, win_ext, wa2_pad, ba):
    t = x[0].shape[0] + x[1].shape[0]
    tm = mod.tile
    kd, vd = GLA_KEY_DIM, GLA_VAL_DIM
    row = lambda i: (i, 0)
    const = lambda i: (0, 0)
    return pl.pallas_call(
        functools.partial(_gla_proj_kernel, mod.npt),
        out_shape=(jax.ShapeDtypeStruct((t, kd), F32), jax.ShapeDtypeStruct((t, kd), F32),
                   jax.ShapeDtypeStruct((t, vd), BF16), jax.ShapeDtypeStruct((t, vd), F32),
                   jax.ShapeDtypeStruct((t, kd), F32)),
        grid=(t // tm,),
        in_specs=_stream_specs(x, tm, mod.npt) + [pl.BlockSpec((1, D_MODEL), const)]
        + mod.specs(1) + mod.specs(0)
        + [pl.BlockSpec(win_ext.shape, const), pl.BlockSpec(wa2_pad.shape, const), pl.BlockSpec((1, kd), const)],
        out_specs=(pl.BlockSpec((tm, kd), row), pl.BlockSpec((tm, kd), row), pl.BlockSpec((tm, vd), row),
                   pl.BlockSpec((tm, vd), row), pl.BlockSpec((tm, kd), row)),
        compiler_params=_params(("parallel",)),
        name="gla_project",
    )(*x, g.reshape(1, -1), *mod.operands(1), *mod.operands(0), win_ext, wa2_pad, ba.reshape(1, -1))


def _cumsum_rows(tri, g):
    n = g.shape[1]
    s = jnp.dot(tri, jnp.concatenate(_split3(g), axis=1), preferred_element_type=F32)
    return s[:, :n] + s[:, n:2 * n] + s[:, 2 * n:]


def _diag_attention(q, k, b, n):
    ng = n // 8
    dk = q.shape[1]
    q3 = q.reshape(ng, 8, dk)
    k3 = k.reshape(ng, 8, dk)
    b3 = b.reshape(ng, 8, dk)
    sub = lax.broadcasted_iota(I32, (ng, 8, dk), 1)
    ti = lax.broadcasted_iota(I32, (n, n), 0)
    si = lax.broadcasted_iota(I32, (n, n), 1)
    attn = jnp.zeros((n, n), F32)
    for j in range(8):
        bj = jnp.broadcast_to(b3[:, j:j + 1, :], b3.shape)
        kj = jnp.broadcast_to(k3[:, j:j + 1, :], k3.shape)
        e = jnp.exp(jnp.minimum(b3 - bj, 0.0))
        m = jnp.where(sub >= j, q3 * e * kj, 0.0)
        col = jnp.sum(m, axis=-1, keepdims=True).reshape(n, 1)
        attn = attn + jnp.where(si == (ti // 8) * 8 + j, col, 0.0)
    return attn


def _cross_attention(q, k, b, n):
    ti = lax.broadcasted_iota(I32, (n, n), 0)
    si = lax.broadcasted_iota(I32, (n, n), 1)
    row = lax.broadcasted_iota(I32, b.shape, 0)
    attn = jnp.zeros((n, n), F32)
    m = n // 2
    while m >= 8:
        nblk = n // m
        refq = jnp.concatenate(
            [jnp.broadcast_to(b[i * m - 1:i * m], (m, b.shape[1])) if i % 2 else b[i * m:(i + 1) * m]
             for i in range(nblk)], axis=0)
        refk = jnp.concatenate(
            [b[i * m:(i + 1) * m] if i % 2 else jnp.broadcast_to(b[(i + 1) * m - 1:(i + 1) * m], (m, b.shape[1]))
             for i in range(nblk)], axis=0)
        odd = ((row // m) % 2) == 1
        qt = jnp.where(odd, q * jnp.exp(jnp.minimum(b - refq, 0.0)), 0.0).astype(BF16)
        kt = jnp.where(odd, 0.0, k * jnp.exp(jnp.minimum(refk - b, 0.0))).astype(BF16)
        a = lax.dot_general(qt, kt, NT_DIMS, preferred_element_type=F32)
        keep = (((ti // m) % 2) == 1) & ((si // m) == (ti // m) - 1)
        attn = attn + jnp.where(keep, a, 0.0)
        m //= 2
    return attn


def _gla_epilogue(o, r, ng):
    ms = jnp.mean(o * o, axis=-1, keepdims=True)
    return (o * lax.rsqrt(ms + NORM_EPS) * ng * _silu(r)).astype(BF16)


def _gla_prompt_kernel(*refs):
    nb = GLA_PROMPT_BATCH
    ins, (ng_ref, o_ref, so_ref, st_ref) = refs[:5 * nb], refs[5 * nb:]
    c = pl.program_id(1)
    n = ins[0].shape[0]

    @pl.when(c == 0)
    def _():
        st_ref[...] = jnp.zeros(st_ref.shape, F32)

    ti = lax.broadcasted_iota(I32, (n, n), 0)
    si = lax.broadcasted_iota(I32, (n, n), 1)
    tri = jnp.where(ti >= si, 1.0, 0.0).astype(BF16)
    for i in range(nb):
        q_ref, k_ref, g_ref, v_ref, r_ref = ins[5 * i:5 * (i + 1)]
        b_all = _cumsum_rows(tri, g_ref[...])
        for h in range(GLA_HEADS):
            ks = slice(GLA_DK * h, GLA_DK * (h + 1))
            vs = slice(GLA_DV * h, GLA_DV * (h + 1))
            q = q_ref[:, ks]
            k = k_ref[:, ks]
            v = v_ref[:, vs]
            b = b_all[:, ks]
            s_t = st_ref[i, h]
            o = lax.dot_general((q * jnp.exp(b)).astype(BF16), s_t.astype(BF16), NT_DIMS,
                                preferred_element_type=F32)
            attn = _cross_attention(q, k, b, n) + _diag_attention(q, k, b, n)
            o = o + jnp.dot(attn.astype(BF16), v, preferred_element_type=F32)
            bl = b[n - 1:n, :]
            kd = (k * jnp.exp(bl - b)).astype(BF16)
            s_new = s_t * jnp.exp(bl) + lax.dot_general(v, kd, TN_DIMS, preferred_element_type=F32)
            st_ref[i, h] = s_new
            o_ref[i, :, vs] = _gla_epilogue(o, r_ref[:, vs], ng_ref[...])

    @pl.when(c == pl.num_programs(1) - 1)
    def _():
        for i in range(nb):
            for h in range(GLA_HEADS):
                so_ref[i, h] = st_ref[i, h].T


def gla_prompt(q, k, g, v, r, norm_g, n_batch, seq_len):
    n = GLA_CHUNK
    nb = GLA_PROMPT_BATCH
    nc = seq_len // n
    kd, vd = GLA_KEY_DIM, GLA_VAL_DIM
    in_specs, operands = [], []
    for i in range(nb):
        row = lambda b, c, i=i: ((nb * b + i) * nc + c, 0)
        in_specs += [pl.BlockSpec((n, kd), row), pl.BlockSpec((n, kd), row), pl.BlockSpec((n, kd), row),
                     pl.BlockSpec((n, vd), row), pl.BlockSpec((n, vd), row)]
        operands += [q, k, g, v, r]
    o, state = pl.pallas_call(
        _gla_prompt_kernel,
        out_shape=(jax.ShapeDtypeStruct((n_batch, seq_len, vd), BF16),
                   jax.ShapeDtypeStruct((n_batch, GLA_HEADS, GLA_DK, GLA_DV), F32)),
        grid=(n_batch // nb, nc),
        in_specs=in_specs + [pl.BlockSpec((1, GLA_DV), lambda b, c: (0, 0))],
        out_specs=(pl.BlockSpec((nb, n, vd), lambda b, c: (b, c, 0)),
                   pl.BlockSpec((nb, GLA_HEADS, GLA_DK, GLA_DV), lambda b, c: (b, 0, 0, 0))),
        scratch_shapes=[pltpu.VMEM((nb, GLA_HEADS, GLA_DV, GLA_DK), F32)],
        compiler_params=_params(("parallel", "arbitrary")),
        name="gla_prompt",
    )(*operands, norm_g.reshape(1, -1))
    return o.reshape(n_batch * seq_len, vd), state


def _gla_sample_kernel(n_new, q_ref, k_ref, g_ref, v_ref, r_ref, ng_ref, si_ref, o_ref, so_ref):
    n = q_ref.shape[0]
    ti = lax.broadcasted_iota(I32, (n, n), 0)
    si = lax.broadcasted_iota(I32, (n, n), 1)
    tri = jnp.where((ti >= si) & (ti // n_new == si // n_new), 1.0, 0.0).astype(BF16)
    b_all = _cumsum_rows(tri, g_ref[...])
    for h in range(GLA_HEADS):
        ks = slice(GLA_DK * h, GLA_DK * (h + 1))
        vs = slice(GLA_DV * h, GLA_DV * (h + 1))
        q = q_ref[:, ks]
        k = k_ref[:, ks]
        v = v_ref[:, vs]
        b = b_all[:, ks]
        attn = _diag_attention(q, k, b, n)
        o_intra = jnp.dot(attn.astype(BF16), v, preferred_element_type=F32)
        qe = (q * jnp.exp(b)).astype(BF16)
        n_sb = n // n_new
        last = [b[n_new * (sb + 1) - 1:n_new * (sb + 1), :] for sb in range(n_sb)]
        bl_rows = jnp.concatenate([jnp.broadcast_to(bl, (n_new, GLA_DK)) for bl in last], axis=0)
        kd = (k * jnp.exp(bl_rows - b)).astype(BF16)
        seq_of_vrow = lax.broadcasted_iota(I32, (n, GLA_DV), 0) // n_new
        seq_of_krow = lax.broadcasted_iota(I32, (n, GLA_DK), 0) // n_new
        o = o_intra
        for sb in range(n_sb):
            s0 = si_ref[sb, h]
            o_sb = jnp.dot(qe, s0.astype(BF16), preferred_element_type=F32)
            o = o + jnp.where(seq_of_vrow == sb, o_sb, 0.0)
            kd_sb = jnp.where(seq_of_krow == sb, kd, jnp.zeros_like(kd))
            upd = lax.dot_general(kd_sb, v, TN_DIMS, preferred_element_type=F32)
            decay_col = jnp.transpose(jnp.broadcast_to(jnp.exp(last[sb]), (8, GLA_DK)))[:, 0:1]
            so_ref[sb, h] = s0 * decay_col + upd
        o_ref[:, vs] = _gla_epilogue(o, r_ref[:, vs], ng_ref[...])


def gla_sample(q, k, g, v, r, norm_g, state, n_prompt_rows, n_new):
    n_seq = state.shape[0]
    sb = SAMPLE_SEQS
    rows = sb * n_new
    kd, vd = GLA_KEY_DIM, GLA_VAL_DIM
    base = n_prompt_rows // rows
    tok = lambda i: (base + i, 0)
    seq = lambda i: (i, 0, 0, 0)
    sblock = (sb, GLA_HEADS, GLA_DK, GLA_DV)
    return pl.pallas_call(
        functools.partial(_gla_sample_kernel, n_new),
        out_shape=(jax.ShapeDtypeStruct((n_seq * n_new, vd), BF16), jax.ShapeDtypeStruct(state.shape, F32)),
        grid=(n_seq // sb,),
        in_specs=[pl.BlockSpec((rows, kd), tok), pl.BlockSpec((rows, kd), tok), pl.BlockSpec((rows, kd), tok),
                  pl.BlockSpec((rows, vd), tok), pl.BlockSpec((rows, vd), tok),
                  pl.BlockSpec((1, GLA_DV), lambda i: (0, 0)),
                  pl.BlockSpec(sblock, seq)],
        out_specs=(pl.BlockSpec((rows, vd), lambda i: (i, 0)), pl.BlockSpec(sblock, seq)),
        compiler_params=_params(("parallel",)),
        name="gla_sample",
    )(q, k, g, v, r, norm_g.reshape(1, -1), state)


def _post_mixer_kernel(npt, ap_ref, as_ref, wo_ref, xp_ref, xs_ref, g1s_ref, g1t_ref, gf_ref, scs_ref, sct_ref, shs_ref, sht_ref,
                       rw_ref, rb_ref, tri_ref,
                       x1_ref, h_ref, eidx_ref, w_ref, rank_ref, cnt_ref, carry_ref):
    i = pl.program_id(0)
    is_s = i >= npt

    @pl.when(i == 0)
    def _():
        carry_ref[...] = jnp.zeros(carry_ref.shape, F32)

    a = jnp.where(is_s, as_ref[...], ap_ref[...])
    x1 = _pick(is_s, xp_ref, xs_ref) + _pick(is_s, g1s_ref, g1t_ref) * jnp.dot(a, wo_ref[...],
                                                                               preferred_element_type=F32)
    x1_ref[...] = x1
    h = _norm_mod(x1, gf_ref[...], _pick(is_s, scs_ref, sct_ref), _pick(is_s, shs_ref, sht_ref))
    h_ref[...] = _pack_bf16_pairs(h)

    h1, h2, _ = _split3(h)
    r1, r2, _ = _split3(rw_ref[...])
    logits = (lax.dot_general(r1, h1, NT_DIMS, preferred_element_type=F32)
              + lax.dot_general(r1, h2, NT_DIMS, preferred_element_type=F32)
              + lax.dot_general(r2, h1, NT_DIMS, preferred_element_type=F32))
    scores = jax.nn.sigmoid(logits)
    sel = scores + rb_ref[...]
    tm = sel.shape[1]
    gsz = N_EXPERTS // N_GROUPS

    sub = lax.broadcasted_iota(I32, (gsz, tm), 0)
    blocks, gscore = [], []
    for g in range(N_GROUPS):
        blk = sel[gsz * g:gsz * (g + 1)]
        m1 = jnp.max(blk, axis=0, keepdims=True)
        first = jnp.min(jnp.where(blk == m1, sub, gsz), axis=0, keepdims=True)
        m2 = jnp.max(jnp.where(sub == first, -jnp.inf, blk), axis=0, keepdims=True)
        blocks.append(blk)
        gscore.append(m1 + m2)
    masked = []
    for g in range(N_GROUPS):
        beaten = jnp.zeros((1, tm), I32)
        for o in range(N_GROUPS):
            if o == g:
                continue
            wins = (gscore[o] > gscore[g]) | ((gscore[o] == gscore[g]) & (o < g))
            beaten = beaten + wins.astype(I32)
        masked.append(jnp.where(beaten < TOPK_GROUPS, blocks[g], -jnp.inf))
    cur = jnp.concatenate(masked, axis=0)

    eid = lax.broadcasted_iota(I32, (N_EXPERTS, tm), 0)
    picked, weights = [], []
    onehot = jnp.zeros((N_EXPERTS, tm), F32)
    for _ in range(TOP_K):
        m = jnp.max(cur, axis=0, keepdims=True)
        idx = jnp.min(jnp.where(cur == m, eid, N_EXPERTS), axis=0, keepdims=True)
        hit = eid == idx
        picked.append(idx)
        weights.append(jnp.sum(jnp.where(hit, scores, 0.0), axis=0, keepdims=True))
        onehot = jnp.where(hit, 1.0, onehot)
        cur = jnp.where(hit, -jnp.inf, cur)
    wsum = weights[0]
    for wk in weights[1:]:
        wsum = wsum + wk
    scale = ROUTED_SCALE / wsum

    before = jnp.dot(onehot.astype(BF16), tri_ref[...], preferred_element_type=F32) + carry_ref[...]
    carry = carry_ref[...] + jnp.sum(onehot, axis=1, keepdims=True)
    carry_ref[...] = carry
    cnt_ref[...] = jnp.broadcast_to(carry, cnt_ref.shape)
    for kk in range(TOP_K):
        eidx_ref[kk:kk + 1, :] = picked[kk]
        w_ref[kk:kk + 1, :] = weights[kk] * scale
        rank_ref[kk:kk + 1, :] = jnp.sum(jnp.where(eid == picked[kk], before, 0.0), axis=0, keepdims=True).astype(I32)


def post_mixer(a_prompt, a_sample, wo_bf, x, mod, gffn, router_t, router_b, tri):
    t = x[0].shape[0] + x[1].shape[0]
    tm = mod.tile
    npt = mod.npt
    row = lambda i: (i, 0)
    col = lambda i: (0, i)
    const = lambda i: (0, 0)
    return pl.pallas_call(
        functools.partial(_post_mixer_kernel, mod.npt),
        out_shape=(jax.ShapeDtypeStruct((t, D_MODEL), F32), jax.ShapeDtypeStruct((t, D_MODEL // 2), U32),
                   jax.ShapeDtypeStruct((TOP_K, t), I32), jax.ShapeDtypeStruct((TOP_K, t), F32),
                   jax.ShapeDtypeStruct((TOP_K, t), I32), jax.ShapeDtypeStruct((N_EXPERTS, 128), F32)),
        grid=(t // tm,),
        in_specs=[pl.BlockSpec((tm, D_MODEL), lambda i: (jnp.minimum(i, npt - 1), 0)),
                  pl.BlockSpec((tm, D_MODEL), lambda i: (jnp.maximum(i - npt, 0), 0)),
                  pl.BlockSpec((D_MODEL, D_MODEL), const)] + _stream_specs(x, tm, npt)
        + mod.specs(2) + [pl.BlockSpec((1, D_MODEL), const)] + mod.specs(4) + mod.specs(3)
        + [pl.BlockSpec((N_EXPERTS, D_MODEL), const), pl.BlockSpec((N_EXPERTS, 1), const),
           pl.BlockSpec((tm, tm), const)],
        out_specs=(pl.BlockSpec((tm, D_MODEL), row), pl.BlockSpec((tm, D_MODEL // 2), row),
                   pl.BlockSpec((TOP_K, tm), col), pl.BlockSpec((TOP_K, tm), col), pl.BlockSpec((TOP_K, tm), col),
                   pl.BlockSpec((N_EXPERTS, 128), const)),
        scratch_shapes=[pltpu.VMEM((N_EXPERTS, 1), F32)],
        compiler_params=_params(("arbitrary",)),
        name="post_mixer",
    )(a_prompt, a_sample, wo_bf, *x, *mod.operands(2), gffn.reshape(1, -1), *mod.operands(4), *mod.operands(3),
      router_t, router_b.reshape(-1, 1), tri)


def _row_copy(src, src_row, dst, dst_row, sem):
    return pltpu.make_async_copy(src.at[src_row], dst.at[dst_row], sem)


def _by_parity(i, fn):
    @pl.when(i % 2 == 0)
    def _():
        fn(0)

    @pl.when(i % 2 == 1)
    def _():
        fn(1)


def _dispatch_kernel(zb_ref, h_ref, dest_hbm, xs_hbm, idx_a, idx_b, zero_ref, slab_ref, sem_idx, sem_zero, sem_rows):
    i = pl.program_id(0)
    te = h_ref.shape[0]
    _store_row_slabs(slab_ref, slice(None), h_ref[...])
    idx_bufs = (idx_a, idx_b)

    def idx_copy(tile, p):
        return pltpu.make_async_copy(dest_hbm.at[tile], idx_bufs[p], sem_idx.at[p])

    def zero_copy(e):
        return pltpu.make_async_copy(zero_ref, xs_hbm.at[pl.ds(zb_ref[e] * EXPERT_BLOCK, EXPERT_BLOCK)], sem_zero)

    @pl.when(i == 0)
    def _():
        idx_copy(0, 0).start()
        zero_ref[...] = jnp.zeros(zero_ref.shape, U32)

        def start(e, carry):
            @pl.when(zb_ref[e] >= 0)
            def _():
                zero_copy(e).start()
            return carry

        def wait(e, carry):
            @pl.when(zb_ref[e] >= 0)
            def _():
                zero_copy(e).wait()
            return carry

        lax.fori_loop(0, zb_ref.shape[0], start, 0)
        lax.fori_loop(0, zb_ref.shape[0], wait, 0)

    def step(p):
        @pl.when(i + 1 < pl.num_programs(0))
        def _():
            idx_copy(i + 1, 1 - p).start()

        idx_copy(i, p).wait()
        idx = idx_bufs[p]

        def issue(t, carry):
            for kk in range(TOP_K):
                _row_copy(slab_ref, t, xs_hbm, idx[kk * te + t], sem_rows).start(priority=kk % 2)
            return carry

        def drain(t, carry):
            for kk in range(TOP_K):
                _row_copy(slab_ref, t, xs_hbm, idx[kk * te + t], sem_rows).wait()
            return carry

        lax.fori_loop(0, te, issue, 0, unroll=ROW_COPY_UNROLL)
        lax.fori_loop(0, te, drain, 0, unroll=ROW_COPY_UNROLL)

    _by_parity(i, step)


def moe_dispatch(h, dest_tiles, zero_blocks, n_slots):
    t = h.shape[0]
    te = ROUTE_TILE
    return pl.pallas_call(
        _dispatch_kernel,
        out_shape=jax.ShapeDtypeStruct((n_slots, ROW_CHUNKS, 128), U32),
        grid_spec=pltpu.PrefetchScalarGridSpec(
            num_scalar_prefetch=1,
            grid=(t // te,),
            in_specs=[pl.BlockSpec((te, h.shape[1]), lambda i, zb: (i, 0)), pl.BlockSpec(memory_space=pl.ANY)],
            out_specs=pl.BlockSpec(memory_space=pl.ANY),
            scratch_shapes=[pltpu.SMEM((te * TOP_K,), I32), pltpu.SMEM((te * TOP_K,), I32),
                            pltpu.VMEM((EXPERT_BLOCK, ROW_CHUNKS, 128), U32), pltpu.VMEM((te, ROW_CHUNKS, 128), U32),
                            pltpu.SemaphoreType.DMA((2,)), pltpu.SemaphoreType.DMA, pltpu.SemaphoreType.DMA],
        ),
        compiler_params=_params(("arbitrary",)),
        name="moe_dispatch",
    )(zero_blocks, h, dest_tiles)


def _expert_kernel(be_ref, nu_ref, xs_ref, wg0_ref, wu0_ref, wd0_ref, wg1_ref, wu1_ref, wd1_ref, ys_ref,
                   wg_bf, wu_bf, wd_bf):
    b = pl.program_id(0)
    rb = EXPERT_BLOCK
    used = 2 * b < nu_ref[0]
    for s, (wg, wu, wd) in enumerate(((wg0_ref, wu0_ref, wd0_ref), (wg1_ref, wu1_ref, wd1_ref))):
        j = 2 * b + s
        fresh = (b == 0) | (be_ref[j] != be_ref[jnp.maximum(j - 2, 0)])

        @pl.when(used & fresh)
        def _():
            wg_bf[s] = wg[...].astype(BF16)
            wu_bf[s] = wu[...].astype(BF16)
            wd_bf[s] = wd[...].astype(BF16)

    @pl.when(used)
    def _():
        for s in range(2):
            rows = slice(rb * s, rb * (s + 1))
            x = _unpack_bf16_pairs(_load_row_slabs(xs_ref, rows))
            hg = jnp.dot(x, wg_bf[s], preferred_element_type=F32)
            hu = jnp.dot(x, wu_bf[s], preferred_element_type=F32)
            y = jnp.dot((_silu(hg) * hu).astype(BF16), wd_bf[s], preferred_element_type=F32)
            ys_ref[rows, :] = _pack_bf16_pairs(y)


def moe_experts(xs, block_expert, n_used, wg, wu, wd, layer):
    n_slots = xs.shape[0]
    rb = EXPERT_BLOCK
    ff = wg.shape[3]
    rows_in = lambda b, be, nu: (jnp.minimum(b, (nu[0] - 1) // 2), 0, 0)
    rows_out = lambda b, be, nu: (jnp.minimum(b, (nu[0] - 1) // 2), 0)
    w_in = lambda s: (lambda b, be, nu: (layer, be[2 * b + s], 0, 0))
    return pl.pallas_call(
        _expert_kernel,
        out_shape=jax.ShapeDtypeStruct((n_slots, D_MODEL // 2), U32),
        grid_spec=pltpu.PrefetchScalarGridSpec(
            num_scalar_prefetch=2,
            grid=(n_slots // (2 * rb),),
            in_specs=[pl.BlockSpec((2 * rb,) + xs.shape[1:], rows_in)]
            + [pl.BlockSpec((None, None, D_MODEL, ff), w_in(0)), pl.BlockSpec((None, None, D_MODEL, ff), w_in(0)),
               pl.BlockSpec((None, None, ff, D_MODEL), w_in(0)),
               pl.BlockSpec((None, None, D_MODEL, ff), w_in(1)), pl.BlockSpec((None, None, D_MODEL, ff), w_in(1)),
               pl.BlockSpec((None, None, ff, D_MODEL), w_in(1))],
            out_specs=pl.BlockSpec((2 * rb, D_MODEL // 2), rows_out),
            scratch_shapes=[pltpu.VMEM((2, D_MODEL, ff), BF16), pltpu.VMEM((2, D_MODEL, ff), BF16),
                            pltpu.VMEM((2, ff, D_MODEL), BF16)],
        ),
        compiler_params=_params(("arbitrary",)),
        name="moe_experts",
    )(block_expert, n_used, xs, wg, wu, wd, wg, wu, wd)


def _combine_kernel(npt, final, h_ref, swg_ref, swu_ref, swd_ref, x1_ref, g2s_ref, g2t_ref, w_ref, fg_ref,
                    dest_hbm, ys_hbm, *rest):
    out_refs, (idx_a, idx_b, ybuf_ref, sem_idx, sem_rows) = rest[:-5], rest[-5:]
    i = pl.program_id(0)
    n = pl.num_programs(0)
    tg = h_ref.shape[0]
    idx_bufs = (idx_a, idx_b)

    def idx_copy(tile, p):
        return pltpu.make_async_copy(dest_hbm.at[tile], idx_bufs[p], sem_idx.at[p])

    def gather_rows(p, wait):
        idx = idx_bufs[p]

        def body(t, carry):
            for kk in range(TOP_K):
                src = 0 if wait else idx[kk * tg + t]
                dst = ybuf_ref.at[p, kk]
                cp = pltpu.make_async_copy(ys_hbm.at[pl.ds(src, 1)], dst.at[pl.ds(t, 1)], sem_rows.at[p])
                cp.wait() if wait else cp.start(priority=kk % 2)
            return carry

        lax.fori_loop(0, tg, body, 0, unroll=ROW_COPY_UNROLL)

    @pl.when(i == 0)
    def _():
        idx_copy(0, 0).start()
        idx_copy(0, 0).wait()
        gather_rows(0, wait=False)

        @pl.when(n > 1)
        def _():
            idx_copy(1, 1).start()

    def prefetch(p):
        @pl.when(i + 1 < n)
        def _():
            idx_copy(i + 1, 1 - p).wait()
            gather_rows(1 - p, wait=False)

        @pl.when(i + 2 < n)
        def _():
            idx_copy(i + 2, p).start()

    _by_parity(i, prefetch)

    hb = _unpack_bf16_pairs(h_ref[...])
    hid = _silu(jnp.dot(hb, swg_ref[...], preferred_element_type=F32)) * jnp.dot(hb, swu_ref[...],
                                                                                 preferred_element_type=F32)
    acc = jnp.dot(hid.astype(BF16), swd_ref[...], preferred_element_type=F32)
    w = w_ref[...]
    gate = _pick(i >= npt, g2s_ref, g2t_ref)

    def finish(p):
        gather_rows(p, wait=True)
        half = D_MODEL // 2
        lo = jnp.zeros((tg, half), F32)
        hi = jnp.zeros((tg, half), F32)
        for kk in range(TOP_K):
            u = ybuf_ref[p, kk]
            wk = w[:, kk:kk + 1]
            lo = lo + lax.bitcast_convert_type(u << 16, F32) * wk
            hi = hi + lax.bitcast_convert_type(u & jnp.uint32(0xFFFF0000), F32) * wk
        routed = jnp.concatenate([lo, hi], axis=1)
        x2 = x1_ref[...] + gate * (routed + acc)
        if final:
            ms = jnp.mean(x2 * x2, axis=-1, keepdims=True)
            x2 = x2 * lax.rsqrt(ms + NORM_EPS) * fg_ref[...]

        @pl.when(i < npt)
        def _():
            out_refs[0][...] = x2

        @pl.when(i >= npt)
        def _():
            out_refs[1][...] = x2

    _by_parity(i, finish)


def moe_combine(h, swg_bf, swu_bf, swd_bf, x1, mod, w_tok, final_g, dest_tiles, ys, final):
    t = h.shape[0]
    tg = mod.tile
    npt = mod.npt
    row = lambda i: (i, 0)
    const = lambda i: (0, 0)
    out_shape = (jax.ShapeDtypeStruct((npt * tg, D_MODEL), F32), jax.ShapeDtypeStruct((t - npt * tg, D_MODEL), F32))
    out_specs = tuple(_stream_specs(None, tg, npt))
    return pl.pallas_call(
        functools.partial(_combine_kernel, mod.npt, final),
        out_shape=out_shape,
        grid=(t // tg,),
        in_specs=[pl.BlockSpec((tg, h.shape[1]), row), pl.BlockSpec(swg_bf.shape, const),
                  pl.BlockSpec(swu_bf.shape, const),
                  pl.BlockSpec(swd_bf.shape, const), pl.BlockSpec((tg, D_MODEL), row)]
        + mod.specs(5)
        + [pl.BlockSpec((tg, TOP_K), row), pl.BlockSpec((1, D_MODEL), const),
           pl.BlockSpec(memory_space=pl.ANY), pl.BlockSpec(memory_space=pl.ANY)],
        out_specs=out_specs,
        scratch_shapes=[pltpu.SMEM((tg * TOP_K,), I32), pltpu.SMEM((tg * TOP_K,), I32),
                        pltpu.VMEM((2, TOP_K, tg, D_MODEL // 2), U32),
                        pltpu.SemaphoreType.DMA((2,)), pltpu.SemaphoreType.DMA((2,))],
        compiler_params=_params(("arbitrary",)),
        name="moe_combine",
    )(h, swg_bf, swu_bf, swd_bf, x1, *mod.operands(5), w_tok, final_g.reshape(1, -1), dest_tiles, ys)


def _slot_kernel(eidx_ref, rank_ref, start_ref, o_ref):
    tm = eidx_ref.shape[1]
    tr = o_ref.shape[2]
    eid = lax.broadcasted_iota(I32, (N_EXPERTS, tm), 0)
    start = start_ref[...]
    for kk in range(TOP_K):
        base = jnp.sum(jnp.where(eid == eidx_ref[kk:kk + 1, :], start, 0.0), axis=0, keepdims=True)
        slot = base.astype(I32) + rank_ref[kk:kk + 1, :]
        for j in range(tm // tr):
            o_ref[j, kk:kk + 1, :] = slot[:, tr * j:tr * (j + 1)]


def assignment_slots(eidx_t, rank_t, pad_start):
    t = eidx_t.shape[1]
    tm = TOKEN_TILE
    tr = ROUTE_TILE
    col = lambda i: (0, i)
    out = pl.pallas_call(
        _slot_kernel,
        out_shape=jax.ShapeDtypeStruct((t // tr, TOP_K, tr), I32),
        grid=(t // tm,),
        in_specs=[pl.BlockSpec((TOP_K, tm), col), pl.BlockSpec((TOP_K, tm), col),
                  pl.BlockSpec((N_EXPERTS, 1), lambda i: (0, 0))],
        out_specs=pl.BlockSpec((tm // tr, TOP_K, tr), lambda i: (i, 0, 0)),
        compiler_params=_params(("parallel",)),
        name="assignment_slots",
    )(eidx_t, rank_t, pad_start.astype(F32).reshape(-1, 1))
    return out.reshape(t // tr, TOP_K * tr)


def _routing_tables(counts, n_blocks):
    rb = EXPERT_BLOCK
    counts = counts.astype(I32)
    padded = (counts + rb - 1) // rb * rb
    pad_end = jnp.cumsum(padded)
    pad_start = pad_end - padded
    n_used = pad_end[-1] // rb
    blocks = jnp.arange(n_blocks, dtype=I32)
    block_expert = jnp.sum((pad_end[None, :] <= (blocks * rb)[:, None]).astype(I32), axis=1)
    last_used = jnp.sum((pad_end <= (n_used - 1) * rb).astype(I32))
    block_expert = jnp.minimum(jnp.where(blocks < n_used, block_expert, last_used), N_EXPERTS - 1)
    zero_blocks = jnp.where(counts % rb != 0, pad_end // rb - 1, -1)
    zero_blocks = jnp.concatenate([zero_blocks, jnp.where(n_used % 2 == 1, n_used, -1).reshape(1)]).astype(I32)
    return pad_start, block_expert, n_used.reshape(1).astype(I32), zero_blocks


def moe_layer(h, x1, eidx_t, w_t, rank_t, counts, mod_route, wg, wu, wd, layer, swg_bf, swu_bf, swd_bf,
              final_g, final):
    t = h.shape[0]
    rb = EXPERT_BLOCK
    tr = ROUTE_TILE
    n_blocks = -(-(t * TOP_K) // rb) + N_EXPERTS
    n_blocks += n_blocks % 2
    pad_start, block_expert, n_used, zero_blocks = _routing_tables(counts, n_blocks)
    dest_tiles = assignment_slots(eidx_t, rank_t, pad_start)
    xs = moe_dispatch(h, dest_tiles, zero_blocks, n_blocks * rb)
    ys = moe_experts(xs, block_expert, n_used, wg, wu, wd, layer)
    return moe_combine(h, swg_bf, swu_bf, swd_bf, x1, mod_route, w_t.T, final_g, dest_tiles, ys, final)


def _rope_tables(n_batch, seq_len, n_seq, n_new):
    half = SWA_HEAD_DIM // 2
    inv = ROPE_THETA ** (-jnp.arange(half, dtype=F32) / half)
    pos = jnp.concatenate([jnp.tile(jnp.arange(seq_len, dtype=F32), n_batch),
                           jnp.tile(PAST_LEN + jnp.arange(n_new, dtype=F32), n_seq)])
    ang = pos[:, None] * inv[None, :]
    cos = jnp.tile(jnp.cos(ang), (1, 128 // half))
    sin = jnp.sin(ang)
    sin = jnp.tile(jnp.concatenate([-sin, sin], axis=1), (1, 128 // SWA_HEAD_DIM))
    return cos, sin


def kernel(x_prompt, x_sample, c_prompt, c_sample, cache_swa_k, cache_swa_v, state_gla, norm_mix_g, norm_ffn_g,
           final_g, ada_w, ada_b, swa_wqkv, swa_sinks, swa_wo, gla_win, gla_wa1, gla_wa2, gla_ba, gla_norm_g,
           gla_wo, moe_router, moe_bias, moe_wg, moe_wu, moe_wd, shared_wg, shared_wu, shared_wd):
    n_batch, seq_len, d = x_prompt.shape
    n_seq, n_new, _ = x_sample.shape
    depth = ada_w.shape[0]
    tp = n_batch * seq_len
    ts = n_seq * n_new
    t = tp + ts
    tm = TOKEN_TILE
    tr = ROUTE_TILE

    x = (x_prompt.reshape(tp, d), x_sample.reshape(ts, d))
    c_all = jnp.concatenate([jnp.repeat(c_sample, n_new, axis=0), c_prompt], axis=0)
    mod = ada_modulation(c_all, ada_w, ada_b)
    cos_tab, sin_tab = _rope_tables(n_batch, seq_len, n_seq, n_new)
    tri = jnp.triu(jnp.ones((tm, tm), BF16), k=1)

    new_k, new_v, new_s = [], [], []
    new_k_s, new_v_s, new_s_s = [], [], []
    for layer in range(depth):
        mod_tok = _Mod(mod, layer, n_batch, seq_len, tm)
        mod_route = _Mod(mod, layer, n_batch, seq_len, tr)
        m = layer // 2
        if layer % 2 == 0:
            q, k, v, k_dup, v_dup = swa_qkv(x, norm_mix_g[layer], mod_tok, swa_wqkv[m].astype(BF16), cos_tab, sin_tab)
            a_p = swa_prompt_attention(q, k_dup, v_dup, swa_sinks[m], n_batch, seq_len)
            nk = SWA_KV_HEADS * SWA_HEAD_DIM
            a_s, ck, cv = swa_sample_attention(q, k, v, cache_swa_k[m].reshape(n_seq, WINDOW, nk),
                                             cache_swa_v[m].reshape(n_seq, WINDOW, nk), swa_sinks[m], tp, n_new)
            kv_shape = (n_batch, WINDOW, SWA_KV_HEADS, SWA_HEAD_DIM)
            tails = [slice((b + 1) * seq_len - WINDOW, (b + 1) * seq_len) for b in range(n_batch)]
            new_k.append(jnp.stack([k[rows] for rows in tails]).reshape(kv_shape))
            new_v.append(jnp.stack([v[rows] for rows in tails]).reshape(kv_shape))
            new_k_s.append(ck.reshape(n_seq, WINDOW, SWA_KV_HEADS, SWA_HEAD_DIM))
            new_v_s.append(cv.reshape(n_seq, WINDOW, SWA_KV_HEADS, SWA_HEAD_DIM))
            wo = swa_wo[m]
        else:
            pad = jnp.zeros((d, 128 - GLA_GATE_RANK), F32)
            win_ext = jnp.concatenate([gla_win[m], gla_wa1[m], pad], axis=1).astype(BF16)
            wa2_pad = jnp.concatenate([gla_wa2[m], jnp.zeros((128 - GLA_GATE_RANK, GLA_KEY_DIM), F32)],
                                      axis=0).astype(BF16)
            q, k, v, r, gate = gla_project(x, norm_mix_g[layer], mod_tok, win_ext, wa2_pad, gla_ba[m])
            a_p, s_prompt = gla_prompt(q, k, gate, v, r, gla_norm_g[m], n_batch, seq_len)
            a_s, s_sample = gla_sample(q, k, gate, v, r, gla_norm_g[m], state_gla[m], tp, n_new)
            new_s.append(s_prompt)
            new_s_s.append(s_sample)
            wo = gla_wo[m]
        x1, h, eidx_t, w_t, rank_t, cnt = post_mixer(a_p, a_s, wo.astype(BF16), x, mod_tok, norm_ffn_g[layer],
                                                     moe_router[layer].T, moe_bias[layer], tri)
        x = moe_layer(h, x1, eidx_t, w_t, rank_t, cnt[:, 0], mod_route,
                      moe_wg, moe_wu, moe_wd, layer,
                      shared_wg[layer].astype(BF16), shared_wu[layer].astype(BF16), shared_wd[layer].astype(BF16),
                      final_g, layer == depth - 1)

    y_prompt = x[0].reshape(n_batch, seq_len, d)
    y_sample = x[1].reshape(n_seq, n_new, d)
    return (y_prompt, y_sample, jnp.stack(new_k), jnp.stack(new_v), jnp.stack(new_k_s), jnp.stack(new_v_s),
            jnp.stack(new_s), jnp.stack(new_s_s))
```

```python
import functools

import jax
import jax.numpy as jnp
from jax import lax
from jax.experimental import pallas as pl
from jax.experimental.pallas import tpu as pltpu

F32 = jnp.float32
BF16 = jnp.bfloat16
I32 = jnp.int32
U32 = jnp.uint32

D_MODEL = 1024
PAST_LEN = 8192
SWA_HEAD_DIM = 64
SWA_HEADS = 16
SWA_KV_HEADS = 4
SWA_GROUP = 4
WINDOW = 128
ROPE_THETA = 10000.0
GLA_HEADS = 4
GLA_DK = 128
GLA_DV = 256
GLA_KEY_DIM = 512
GLA_VAL_DIM = 1024
GLA_GATE_RANK = 16
GLA_GATE_NORMALIZER = 16.0
GLA_CHUNK = 64
N_EXPERTS = 64
TOP_K = 8
N_GROUPS = 8
TOPK_GROUPS = 4
EXPERT_FF = 256
ROUTED_SCALE = 2.5
NORM_EPS = 1e-6

TOKEN_TILE = 512
ROUTE_TILE = 512
EXPERT_BLOCK = 256
ADA_TILE = 512
SAMPLE_SEQS = 8
GLA_PROMPT_BATCH = 4
ROW_COPY_UNROLL = 8
SWA_PROMPT_BATCH = 4
VMEM_LIMIT = 48 * 1024 * 1024

NT_DIMS = (((1,), (1,)), ((), ()))
TN_DIMS = (((0,), (0,)), ((), ()))


def _params(semantics):
    return pltpu.CompilerParams(dimension_semantics=semantics, vmem_limit_bytes=VMEM_LIMIT)


def _silu(x):
    return x * jax.nn.sigmoid(x)


def _norm_mod(x, g, sc, sh):
    ms = jnp.mean(x * x, axis=-1, keepdims=True)
    return (x * lax.rsqrt(ms + NORM_EPS) * g) * (1.0 + sc) + sh


def _pack_bf16_pairs(x):
    half = x.shape[1] // 2
    xb = x.astype(BF16).astype(F32)
    lo = lax.bitcast_convert_type(xb[:, :half], U32) >> 16
    hi = lax.bitcast_convert_type(xb[:, half:], U32) & jnp.uint32(0xFFFF0000)
    return lo | hi


def _unpack_bf16_pairs(u):
    lo = lax.bitcast_convert_type(u << 16, F32)
    hi = lax.bitcast_convert_type(u & jnp.uint32(0xFFFF0000), F32)
    return jnp.concatenate([lo, hi], axis=1).astype(BF16)


ROW_CHUNKS = D_MODEL // 2 // 128


def _store_row_slabs(ref, rows, words):
    for c in range(ROW_CHUNKS):
        ref[rows, c, :] = words[:, 128 * c:128 * (c + 1)]


def _load_row_slabs(ref, rows):
    return jnp.concatenate([ref[rows, c, :] for c in range(ROW_CHUNKS)], axis=1)


def _split3(x):
    x1 = x.astype(BF16)
    r1 = x - x1.astype(F32)
    x2 = r1.astype(BF16)
    x3 = (r1 - x2.astype(F32)).astype(BF16)
    return x1, x2, x3


def _ada_kernel(c_ref, w_ref, b_ref, o_ref):
    s = _silu(c_ref[...]).astype(BF16)
    o_ref[...] = jnp.dot(s, w_ref[...].astype(BF16), preferred_element_type=F32) + b_ref[...]


def ada_modulation(c_all, ada_w, ada_b):
    depth, d, n = ada_w.shape
    rows = c_all.shape[0]
    return pl.pallas_call(
        _ada_kernel,
        out_shape=jax.ShapeDtypeStruct((depth, rows, n), F32),
        grid=(depth, n // ADA_TILE),
        in_specs=[
            pl.BlockSpec((rows, d), lambda l, j: (0, 0)),
            pl.BlockSpec((None, d, ADA_TILE), lambda l, j: (l, 0, j)),
            pl.BlockSpec((None, 1, ADA_TILE), lambda l, j: (l, 0, j)),
        ],
        out_specs=pl.BlockSpec((None, rows, ADA_TILE), lambda l, j: (l, 0, j)),
        compiler_params=_params(("parallel", "parallel")),
        name="ada_modulation",
    )(c_all, ada_w, ada_b.reshape(depth, 1, n))


class _Mod:
    def __init__(self, mod, layer, n_batch, seq_len, tile):
        depth, rows, n = mod.shape
        self.tile = tile
        self.layer = layer
        self.npt = n_batch * seq_len // tile
        self.mod_tok = mod
        self.mod_seq = mod[:, rows - n_batch:].reshape(depth, n_batch, 1, n)
        self.tiles_per_seq = seq_len // tile

    def operands(self, chunk):
        del chunk
        return [self.mod_seq, self.mod_tok]

    def specs(self, chunk):
        l, npt, tps = self.layer, self.npt, self.tiles_per_seq
        n_seq = self.mod_seq.shape[1]
        seq_spec = pl.BlockSpec((None, None, 1, D_MODEL),
                                lambda i, *_: (l, jnp.minimum(i // tps, n_seq - 1), 0, chunk))
        tok_spec = pl.BlockSpec((None, self.tile, D_MODEL),
                                lambda i, *_: (l, jnp.maximum(i - npt, 0), chunk))
        return [seq_spec, tok_spec]


def _pick(is_sample, seq_ref, tok_ref):
    return jnp.where(is_sample, tok_ref[...], seq_ref[...])


def _stream_specs(x_pair, tile, npt):
    del x_pair
    return [pl.BlockSpec((tile, D_MODEL), lambda i, *_: (jnp.minimum(i, npt - 1), 0)),
            pl.BlockSpec((tile, D_MODEL), lambda i, *_: (jnp.maximum(i - npt, 0), 0))]


def _swa_qkv_kernel(npt, xp_ref, xs_ref, g_ref, scs_ref, sct_ref, shs_ref, sht_ref, w_ref, cos_ref, sin_ref,
                    q_ref, k_ref, v_ref, kd_ref, vd_ref):
    is_s = pl.program_id(0) >= npt
    h = _norm_mod(_pick(is_s, xp_ref, xs_ref), g_ref[...], _pick(is_s, scs_ref, sct_ref), _pick(is_s, shs_ref, sht_ref))
    qkv = jnp.dot(h.astype(BF16), w_ref[...], preferred_element_type=F32)
    cos = cos_ref[...]
    sin = sin_ref[...]
    lane = lax.broadcasted_iota(I32, cos.shape, 1)
    first_half = (lane % SWA_HEAD_DIM) < (SWA_HEAD_DIM // 2)

    def rope(xc):
        rot = jnp.where(first_half, pltpu.roll(xc, 128 - SWA_HEAD_DIM // 2, 1), pltpu.roll(xc, SWA_HEAD_DIM // 2, 1))
        return xc * cos + rot * sin

    nq = SWA_HEADS * SWA_HEAD_DIM
    nk = SWA_KV_HEADS * SWA_HEAD_DIM
    for c in range(nq // 128):
        q_ref[:, 128 * c:128 * (c + 1)] = (rope(qkv[:, 128 * c:128 * (c + 1)]) * (SWA_HEAD_DIM ** -0.5)).astype(BF16)
    low = lane < SWA_HEAD_DIM

    def spread(chunk):
        rolled = pltpu.roll(chunk, SWA_HEAD_DIM, 1)
        return jnp.where(low, chunk, rolled).astype(BF16), jnp.where(low, rolled, chunk).astype(BF16)

    for c in range(nk // 128):
        kc = rope(qkv[:, nq + 128 * c:nq + 128 * (c + 1)])
        vc = qkv[:, nq + nk + 128 * c:nq + nk + 128 * (c + 1)]
        k_ref[:, 128 * c:128 * (c + 1)] = kc
        v_ref[:, 128 * c:128 * (c + 1)] = vc
        kd_ref[:, 256 * c:256 * c + 128], kd_ref[:, 256 * c + 128:256 * (c + 1)] = spread(kc)
        vd_ref[:, 256 * c:256 * c + 128], vd_ref[:, 256 * c + 128:256 * (c + 1)] = spread(vc)


def swa_qkv(x, g, mod, w_bf, cos_tab, sin_tab):
    t = x[0].shape[0] + x[1].shape[0]
    tm = mod.tile
    nq = SWA_HEADS * SWA_HEAD_DIM
    nk = SWA_KV_HEADS * SWA_HEAD_DIM
    row = lambda i: (i, 0)
    return pl.pallas_call(
        functools.partial(_swa_qkv_kernel, mod.npt),
        out_shape=(jax.ShapeDtypeStruct((t, nq), BF16), jax.ShapeDtypeStruct((t, nk), F32),
                   jax.ShapeDtypeStruct((t, nk), F32), jax.ShapeDtypeStruct((t, 2 * nk), BF16),
                   jax.ShapeDtypeStruct((t, 2 * nk), BF16)),
        grid=(t // tm,),
        in_specs=_stream_specs(x, tm, mod.npt) + [pl.BlockSpec((1, D_MODEL), lambda i: (0, 0))]
        + mod.specs(1) + mod.specs(0)
        + [pl.BlockSpec(w_bf.shape, lambda i: (0, 0)), pl.BlockSpec((tm, 128), row), pl.BlockSpec((tm, 128), row)],
        out_specs=(pl.BlockSpec((tm, nq), row), pl.BlockSpec((tm, nk), row), pl.BlockSpec((tm, nk), row),
                   pl.BlockSpec((tm, 2 * nk), row), pl.BlockSpec((tm, 2 * nk), row)),
        compiler_params=_params(("parallel",)),
        name="swa_qkv",
    )(*x, g.reshape(1, -1), *mod.operands(1), *mod.operands(0), w_bf, cos_tab, sin_tab)


def _sink_softmax(s, sink_col):
    m = jnp.maximum(jnp.max(s, axis=-1, keepdims=True), sink_col)
    e = jnp.exp(s - m)
    den = jnp.sum(e, axis=-1, keepdims=True) + jnp.exp(sink_col - m)
    return e * (1.0 / den)


def _pair_attention(sink_ref, q_ref, kblks, vblks, mask, o_ref):
    rows = q_ref.shape[0]
    half = mask.shape[1]
    per_group = SWA_GROUP // 2
    scores = []
    for g in range(SWA_KV_HEADS):
        q2 = jnp.concatenate([q_ref[:, 128 * c:128 * (c + 1)] for c in range(per_group * g, per_group * (g + 1))],
                             axis=0)
        s = lax.dot_general(q2, kblks[g], NT_DIMS, preferred_element_type=F32)
        for j in range(per_group):
            scores += [s[rows * j:rows * (j + 1), :half], s[rows * j:rows * (j + 1), half:]]
    s_all = jnp.where(mask, jnp.stack(scores), -jnp.inf)
    sinks = jnp.stack([jnp.full((1, 1), sink_ref[h], F32) for h in range(SWA_HEADS)])
    p_all = _sink_softmax(s_all, sinks).astype(BF16)
    for g in range(SWA_KV_HEADS):
        chunks = range(per_group * g, per_group * (g + 1))
        p2 = jnp.concatenate([jnp.concatenate([p_all[2 * c], p_all[2 * c + 1]], axis=1) for c in chunks], axis=0)
        o = jnp.dot(p2, vblks[g], preferred_element_type=F32).astype(BF16)
        for j, c in enumerate(chunks):
            o_ref[:, 128 * c:128 * (c + 1)] = o[rows * j:rows * (j + 1)]


def _swa_prompt_kernel(sink_ref, *refs):
    o_ref = refs[-1]
    j = pl.program_id(1)
    blk = refs[0].shape[0]
    qi = lax.broadcasted_iota(I32, (blk, 2 * blk), 0)
    sj = lax.broadcasted_iota(I32, (blk, 2 * blk), 1)
    rel = qi + blk - sj
    mask = (rel >= 0) & (rel <= WINDOW) & ((sj >= blk) | (j > 0))
    low = lax.broadcasted_iota(I32, (2 * blk, 128), 1) < SWA_HEAD_DIM
    zero = jnp.zeros((2 * blk, 128), BF16)
    for i in range(SWA_PROMPT_BATCH):
        q_ref, kc_ref, kp_ref, vc_ref, vp_ref = refs[5 * i:5 * (i + 1)]
        kblks, vblks = [], []
        for g in range(SWA_KV_HEADS):
            cs = slice(128 * g, 128 * (g + 1))
            kcat = jnp.concatenate([kp_ref[:, cs], kc_ref[:, cs]], axis=0)
            vcat = jnp.concatenate([vp_ref[:, cs], vc_ref[:, cs]], axis=0)
            kblks.append(jnp.concatenate([jnp.where(low, kcat, zero), jnp.where(low, zero, kcat)], axis=0))
            vblks.append(jnp.concatenate([jnp.where(low, vcat, zero), jnp.where(low, zero, vcat)], axis=0))
        _pair_attention(sink_ref, q_ref, kblks, vblks, mask, o_ref.at[i])


def swa_prompt_attention(q, k, v, sinks, n_batch, seq_len):
    blk = WINDOW
    nsb = SWA_PROMPT_BATCH
    nb = seq_len // blk
    nq = q.shape[1]
    nk = k.shape[1]
    in_specs, operands = [pl.BlockSpec(memory_space=pltpu.SMEM)], [sinks]
    for i in range(nsb):
        cur = lambda b, j, i=i: ((nsb * b + i) * nb + j, 0)
        prev = lambda b, j, i=i: ((nsb * b + i) * nb + jnp.maximum(j - 1, 0), 0)
        in_specs += [pl.BlockSpec((blk, nq), cur), pl.BlockSpec((blk, nk), cur), pl.BlockSpec((blk, nk), prev),
                     pl.BlockSpec((blk, nk), cur), pl.BlockSpec((blk, nk), prev)]
        operands += [q, k, k, v, v]
    out = pl.pallas_call(
        _swa_prompt_kernel,
        out_shape=jax.ShapeDtypeStruct((n_batch, seq_len, nq), BF16),
        grid=(n_batch // nsb, nb),
        in_specs=in_specs,
        out_specs=pl.BlockSpec((nsb, blk, nq), lambda b, j: (b, j, 0)),
        compiler_params=_params(("parallel", "parallel")),
        name="swa_prompt_attention",
    )(*operands)
    return out.reshape(n_batch * seq_len, nq)


def _swa_sample_kernel(n_new, sink_ref, q_ref, kn_ref, vn_ref, ck_ref, cv_ref, o_ref, nk_ref, nv_ref):
    n_sb, win, _ = ck_ref.shape
    per_seq = win + n_new
    rows = n_sb * n_new
    cols = n_sb * per_seq
    keys, vals = [], []
    for sb in range(n_sb):
        r0 = sb * n_new
        kc = ck_ref[sb]
        vc = cv_ref[sb]
        kn = kn_ref[r0:r0 + n_new, :]
        vn = vn_ref[r0:r0 + n_new, :]
        nk_ref[sb, 0:win - n_new, :] = kc[n_new:]
        nk_ref[sb, win - n_new:win, :] = kn
        nv_ref[sb, 0:win - n_new, :] = vc[n_new:]
        nv_ref[sb, win - n_new:win, :] = vn
        keys += [kc, kn]
        vals += [vc, vn]
    keys = jnp.concatenate(keys, axis=0)
    vals = jnp.concatenate(vals, axis=0)
    ri = lax.broadcasted_iota(I32, (rows, cols), 0)
    ci = lax.broadcasted_iota(I32, (rows, cols), 1)
    ti = ri % n_new
    si = ci % per_seq
    mask = (ri // n_new == ci // per_seq) & (si >= ti) & (si <= ti + WINDOW)
    low = lax.broadcasted_iota(I32, (cols, 128), 1) < SWA_HEAD_DIM
    zero = jnp.zeros((cols, 128), BF16)

    def block_diag(chunk, first):
        rolled = pltpu.roll(chunk, SWA_HEAD_DIM, 1)
        both = (jnp.where(low, chunk, rolled) if first else jnp.where(low, rolled, chunk)).astype(BF16)
        return jnp.concatenate([jnp.where(low, both, zero), jnp.where(low, zero, both)], axis=0)

    chunks = [slice(128 * (g // 2), 128 * (g // 2 + 1)) for g in range(SWA_KV_HEADS)]
    kblks = [block_diag(keys[:, cs], g % 2 == 0) for g, cs in enumerate(chunks)]
    vblks = [block_diag(vals[:, cs], g % 2 == 0) for g, cs in enumerate(chunks)]
    _pair_attention(sink_ref, q_ref, kblks, vblks, mask, o_ref)


def swa_sample_attention(q, k, v, cache_k, cache_v, sinks, n_prompt_rows, n_new):
    n_seq, win, nk = cache_k.shape
    sb = SAMPLE_SEQS
    rows = sb * n_new
    nq = q.shape[1]
    base = n_prompt_rows // rows
    tok = lambda i: (base + i, 0)
    seq = lambda i: (i, 0, 0)
    return pl.pallas_call(
        functools.partial(_swa_sample_kernel, n_new),
        out_shape=(jax.ShapeDtypeStruct((n_seq * n_new, nq), BF16),
                   jax.ShapeDtypeStruct(cache_k.shape, F32), jax.ShapeDtypeStruct(cache_v.shape, F32)),
        grid=(n_seq // sb,),
        in_specs=[pl.BlockSpec(memory_space=pltpu.SMEM),
                  pl.BlockSpec((rows, nq), tok), pl.BlockSpec((rows, nk), tok), pl.BlockSpec((rows, nk), tok),
                  pl.BlockSpec((sb, win, nk), seq), pl.BlockSpec((sb, win, nk), seq)],
        out_specs=(pl.BlockSpec((rows, nq), lambda i: (i, 0)), pl.BlockSpec((sb, win, nk), seq),
                   pl.BlockSpec((sb, win, nk), seq)),
        compiler_params=_params(("parallel",)),
        name="swa_sample_attention",
    )(sinks, q, k, v, cache_k, cache_v)


def _gla_proj_kernel(npt, xp_ref, xs_ref, g_ref, scs_ref, sct_ref, shs_ref, sht_ref, w_ref, wa2_ref, ba_ref,
                     q_ref, k_ref, v_ref, r_ref, gate_ref):
    is_s = pl.program_id(0) >= npt
    h = _norm_mod(_pick(is_s, xp_ref, xs_ref), g_ref[...], _pick(is_s, scs_ref, sct_ref), _pick(is_s, shs_ref, sht_ref))
    proj = jnp.dot(h.astype(BF16), w_ref[...], preferred_element_type=F32)
    kd = GLA_KEY_DIM
    vd = GLA_VAL_DIM
    q_ref[...] = proj[:, :kd] * (GLA_DK ** -0.5)
    k_ref[...] = proj[:, kd:2 * kd]
    v_ref[...] = proj[:, 2 * kd:2 * kd + vd].astype(BF16)
    r_ref[...] = proj[:, 2 * kd + vd:2 * kd + 2 * vd]
    low = proj[:, 2 * kd + 2 * vd:].astype(BF16)
    z = jnp.dot(low, wa2_ref[...], preferred_element_type=F32) + ba_ref[...]
    log_sig = jnp.minimum(z, 0.0) - jnp.log1p(jnp.exp(-jnp.abs(z)))
    gate_ref[...] = log_sig / GLA_GATE_NORMALIZER


def gla_project(x, g, mod, win_ext, wa2_pad, ba):
    t = x[0].shape[0] + x[1].shape[0]
    tm = mod.tile
    kd, vd = GLA_KEY_DIM, GLA_VAL_DIM
    row = lambda i: (i, 0)
    const = lambda i: (0, 0)
    return pl.pallas_call(
        functools.partial(_gla_proj_kernel, mod.npt),
        out_shape=(jax.ShapeDtypeStruct((t, kd), F32), jax.ShapeDtypeStruct((t, kd), F32),
                   jax.ShapeDtypeStruct((t, vd), BF16), jax.ShapeDtypeStruct((t, vd), F32),
                   jax.ShapeDtypeStruct((t, kd), F32)),
        grid=(t // tm,),
        in_specs=_stream_specs(x, tm, mod.npt) + [pl.BlockSpec((1, D_MODEL), const)]
        + mod.specs(1) + mod.specs(0)
        + [pl.BlockSpec(win_ext.shape, const), pl.BlockSpec(wa2_pad.shape, const), pl.BlockSpec((1, kd), const)],
        out_specs=(pl.BlockSpec((tm, kd), row), pl.BlockSpec((tm, kd), row), pl.BlockSpec((tm, vd), row),
                   pl.BlockSpec((tm, vd), row), pl.BlockSpec((tm, kd), row)),
        compiler_params=_params(("parallel",)),
        name="gla_project",
    )(*x, g.reshape(1, -1), *mod.operands(1), *mod.operands(0), win_ext, wa2_pad, ba.reshape(1, -1))


def _cumsum_rows(tri, g):
    n = g.shape[1]
    s = jnp.dot(tri, jnp.concatenate(_split3(g), axis=1), preferred_element_type=F32)
    return s[:, :n] + s[:, n:2 * n] + s[:, 2 * n:]


def _diag_attention(q, k, b, n):
    ng = n // 8
    dk = q.shape[1]
    q3 = q.reshape(ng, 8, dk)
    k3 = k.reshape(ng, 8, dk)
    b3 = b.reshape(ng, 8, dk)
    sub = lax.broadcasted_iota(I32, (ng, 8, dk), 1)
    ti = lax.broadcasted_iota(I32, (n, n), 0)
    si = lax.broadcasted_iota(I32, (n, n), 1)
    attn = jnp.zeros((n, n), F32)
    for j in range(8):
        bj = jnp.broadcast_to(b3[:, j:j + 1, :], b3.shape)
        kj = jnp.broadcast_to(k3[:, j:j + 1, :], k3.shape)
        e = jnp.exp(jnp.minimum(b3 - bj, 0.0))
        m = jnp.where(sub >= j, q3 * e * kj, 0.0)
        col = jnp.sum(m, axis=-1, keepdims=True).reshape(n, 1)
        attn = attn + jnp.where(si == (ti // 8) * 8 + j, col, 0.0)
    return attn


def _cross_attention(q, k, b, n):
    ti = lax.broadcasted_iota(I32, (n, n), 0)
    si = lax.broadcasted_iota(I32, (n, n), 1)
    row = lax.broadcasted_iota(I32, b.shape, 0)
    attn = jnp.zeros((n, n), F32)
    m = n // 2
    while m >= 8:
        nblk = n // m
        refq = jnp.concatenate(
            [jnp.broadcast_to(b[i * m - 1:i * m], (m, b.shape[1])) if i % 2 else b[i * m:(i + 1) * m]
             for i in range(nblk)], axis=0)
        refk = jnp.concatenate(
            [b[i * m:(i + 1) * m] if i % 2 else jnp.broadcast_to(b[(i + 1) * m - 1:(i + 1) * m], (m, b.shape[1]))
             for i in range(nblk)], axis=0)
        odd = ((row // m) % 2) == 1
        qt = jnp.where(odd, q * jnp.exp(jnp.minimum(b - refq, 0.0)), 0.0).astype(BF16)
        kt = jnp.where(odd, 0.0, k * jnp.exp(jnp.minimum(refk - b, 0.0))).astype(BF16)
        a = lax.dot_general(qt, kt, NT_DIMS, preferred_element_type=F32)
        keep = (((ti // m) % 2) == 1) & ((si // m) == (ti // m) - 1)
        attn = attn + jnp.where(keep, a, 0.0)
        m //= 2
    return attn


def _gla_epilogue(o, r, ng):
    ms = jnp.mean(o * o, axis=-1, keepdims=True)
    return (o * lax.rsqrt(ms + NORM_EPS) * ng * _silu(r)).astype(BF16)


def _gla_prompt_kernel(*refs):
    nb = GLA_PROMPT_BATCH
    ins, (ng_ref, o_ref, so_ref, st_ref) = refs[:5 * nb], refs[5 * nb:]
    c = pl.program_id(1)
    n = ins[0].shape[0]

    @pl.when(c == 0)
    def _():
        st_ref[...] = jnp.zeros(st_ref.shape, F32)

    ti = lax.broadcasted_iota(I32, (n, n), 0)
    si = lax.broadcasted_iota(I32, (n, n), 1)
    tri = jnp.where(ti >= si, 1.0, 0.0).astype(BF16)
    for i in range(nb):
        q_ref, k_ref, g_ref, v_ref, r_ref = ins[5 * i:5 * (i + 1)]
        b_all = _cumsum_rows(tri, g_ref[...])
        for h in range(GLA_HEADS):
            ks = slice(GLA_DK * h, GLA_DK * (h + 1))
            vs = slice(GLA_DV * h, GLA_DV * (h + 1))
            q = q_ref[:, ks]
            k = k_ref[:, ks]
            v = v_ref[:, vs]
            b = b_all[:, ks]
            s_t = st_ref[i, h]
            o = lax.dot_general((q * jnp.exp(b)).astype(BF16), s_t.astype(BF16), NT_DIMS,
                                preferred_element_type=F32)
            attn = _cross_attention(q, k, b, n) + _diag_attention(q, k, b, n)
            o = o + jnp.dot(attn.astype(BF16), v, preferred_element_type=F32)
            bl = b[n - 1:n, :]
            kd = (k * jnp.exp(bl - b)).astype(BF16)
            s_new = s_t * jnp.exp(bl) + lax.dot_general(v, kd, TN_DIMS, preferred_element_type=F32)
            st_ref[i, h] = s_new
            o_ref[i, :, vs] = _gla_epilogue(o, r_ref[:, vs], ng_ref[...])

    @pl.when(c == pl.num_programs(1) - 1)
    def _():
        for i in range(nb):
            for h in range(GLA_HEADS):
                so_ref[i, h] = st_ref[i, h].T


def gla_prompt(q, k, g, v, r, norm_g, n_batch, seq_len):
    n = GLA_CHUNK
    nb = GLA_PROMPT_BATCH
    nc = seq_len // n
    kd, vd = GLA_KEY_DIM, GLA_VAL_DIM
    in_specs, operands = [], []
    for i in range(nb):
        row = lambda b, c, i=i: ((nb * b + i) * nc + c, 0)
        in_specs += [pl.BlockSpec((n, kd), row), pl.BlockSpec((n, kd), row), pl.BlockSpec((n, kd), row),
                     pl.BlockSpec((n, vd), row), pl.BlockSpec((n, vd), row)]
        operands += [q, k, g, v, r]
    o, state = pl.pallas_call(
        _gla_prompt_kernel,
        out_shape=(jax.ShapeDtypeStruct((n_batch, seq_len, vd), BF16),
                   jax.ShapeDtypeStruct((n_batch, GLA_HEADS, GLA_DK, GLA_DV), F32)),
        grid=(n_batch // nb, nc),
        in_specs=in_specs + [pl.BlockSpec((1, GLA_DV), lambda b, c: (0, 0))],
        out_specs=(pl.BlockSpec((nb, n, vd), lambda b, c: (b, c, 0)),
                   pl.BlockSpec((nb, GLA_HEADS, GLA_DK, GLA_DV), lambda b, c: (b, 0, 0, 0))),
        scratch_shapes=[pltpu.VMEM((nb, GLA_HEADS, GLA_DV, GLA_DK), F32)],
        compiler_params=_params(("parallel", "arbitrary")),
        name="gla_prompt",
    )(*operands, norm_g.reshape(1, -1))
    return o.reshape(n_batch * seq_len, vd), state


def _gla_sample_kernel(n_new, q_ref, k_ref, g_ref, v_ref, r_ref, ng_ref, si_ref, o_ref, so_ref):
    n = q_ref.shape[0]
    ti = lax.broadcasted_iota(I32, (n, n), 0)
    si = lax.broadcasted_iota(I32, (n, n), 1)
    tri = jnp.where((ti >= si) & (ti // n_new == si // n_new), 1.0, 0.0).astype(BF16)
    b_all = _cumsum_rows(tri, g_ref[...])
    for h in range(GLA_HEADS):
        ks = slice(GLA_DK * h, GLA_DK * (h + 1))
        vs = slice(GLA_DV * h, GLA_DV * (h + 1))
        q = q_ref[:, ks]
        k = k_ref[:, ks]
        v = v_ref[:, vs]
        b = b_all[:, ks]
        attn = _diag_attention(q, k, b, n)
        o_intra = jnp.dot(attn.astype(BF16), v, preferred_element_type=F32)
        qe = (q * jnp.exp(b)).astype(BF16)
        n_sb = n // n_new
        last = [b[n_new * (sb + 1) - 1:n_new * (sb + 1), :] for sb in range(n_sb)]
        bl_rows = jnp.concatenate([jnp.broadcast_to(bl, (n_new, GLA_DK)) for bl in last], axis=0)
        kd = (k * jnp.exp(bl_rows - b)).astype(BF16)
        seq_of_vrow = lax.broadcasted_iota(I32, (n, GLA_DV), 0) // n_new
        seq_of_krow = lax.broadcasted_iota(I32, (n, GLA_DK), 0) // n_new
        o = o_intra
        for sb in range(n_sb):
            s0 = si_ref[sb, h]
            o_sb = jnp.dot(qe, s0.astype(BF16), preferred_element_type=F32)
            o = o + jnp.where(seq_of_vrow == sb, o_sb, 0.0)
            kd_sb = jnp.where(seq_of_krow == sb, kd, jnp.zeros_like(kd))
            upd = lax.dot_general(kd_sb, v, TN_DIMS, preferred_element_type=F32)
            decay_col = jnp.transpose(jnp.broadcast_to(jnp.exp(last[sb]), (8, GLA_DK)))[:, 0:1]
            so_ref[sb, h] = s0 * decay_col + upd
        o_ref[:, vs] = _gla_epilogue(o, r_ref[:, vs], ng_ref[...])


def gla_sample(q, k, g, v, r, norm_g, state, n_prompt_rows, n_new):
    n_seq = state.shape[0]
    sb = SAMPLE_SEQS
    rows = sb * n_new
    kd, vd = GLA_KEY_DIM, GLA_VAL_DIM
    base = n_prompt_rows // rows
    tok = lambda i: (base + i, 0)
    seq = lambda i: (i, 0, 0, 0)
    sblock = (sb, GLA_HEADS, GLA_DK, GLA_DV)
    return pl.pallas_call(
        functools.partial(_gla_sample_kernel, n_new),
        out_shape=(jax.ShapeDtypeStruct((n_seq * n_new, vd), BF16), jax.ShapeDtypeStruct(state.shape, F32)),
        grid=(n_seq // sb,),
        in_specs=[pl.BlockSpec((rows, kd), tok), pl.BlockSpec((rows, kd), tok), pl.BlockSpec((rows, kd), tok),
                  pl.BlockSpec((rows, vd), tok), pl.BlockSpec((rows, vd), tok),
                  pl.BlockSpec((1, GLA_DV), lambda i: (0, 0)),
                  pl.BlockSpec(sblock, seq)],
        out_specs=(pl.BlockSpec((rows, vd), lambda i: (i, 0)), pl.BlockSpec(sblock, seq)),
        compiler_params=_params(("parallel",)),
        name="gla_sample",
    )(q, k, g, v, r, norm_g.reshape(1, -1), state)


def _post_mixer_kernel(npt, ap_ref, as_ref, wo_ref, xp_ref, xs_ref, g1s_ref, g1t_ref, gf_ref, scs_ref, sct_ref, shs_ref, sht_ref,
                       rw_ref, rb_ref, tri_ref,
                       x1_ref, h_ref, eidx_ref, w_ref, rank_ref, cnt_ref, carry_ref):
    i = pl.program_id(0)
    is_s = i >= npt

    @pl.when(i == 0)
    def _():
        carry_ref[...] = jnp.zeros(carry_ref.shape, F32)

    a = jnp.where(is_s, as_ref[...], ap_ref[...])
    x1 = _pick(is_s, xp_ref, xs_ref) + _pick(is_s, g1s_ref, g1t_ref) * jnp.dot(a, wo_ref[...],
                                                                               preferred_element_type=F32)
    x1_ref[...] = x1
    h = _norm_mod(x1, gf_ref[...], _pick(is_s, scs_ref, sct_ref), _pick(is_s, shs_ref, sht_ref))
    h_ref[...] = _pack_bf16_pairs(h)

    h1, h2, _ = _split3(h)
    r1, r2, _ = _split3(rw_ref[...])
    logits = (lax.dot_general(r1, h1, NT_DIMS, preferred_element_type=F32)
              + lax.dot_general(r1, h2, NT_DIMS, preferred_element_type=F32)
              + lax.dot_general(r2, h1, NT_DIMS, preferred_element_type=F32))
    scores = jax.nn.sigmoid(logits)
    sel = scores + rb_ref[...]
    tm = sel.shape[1]
    gsz = N_EXPERTS // N_GROUPS

    sub = lax.broadcasted_iota(I32, (gsz, tm), 0)
    blocks, gscore = [], []
    for g in range(N_GROUPS):
        blk = sel[gsz * g:gsz * (g + 1)]
        m1 = jnp.max(blk, axis=0, keepdims=True)
        first = jnp.min(jnp.where(blk == m1, sub, gsz), axis=0, keepdims=True)
        m2 = jnp.max(jnp.where(sub == first, -jnp.inf, blk), axis=0, keepdims=True)
        blocks.append(blk)
        gscore.append(m1 + m2)
    masked = []
    for g in range(N_GROUPS):
        beaten = jnp.zeros((1, tm), I32)
        for o in range(N_GROUPS):
            if o == g:
                continue
            wins = (gscore[o] > gscore[g]) | ((gscore[o] == gscore[g]) & (o < g))
            beaten = beaten + wins.astype(I32)
        masked.append(jnp.where(beaten < TOPK_GROUPS, blocks[g], -jnp.inf))
    cur = jnp.concatenate(masked, axis=0)

    eid = lax.broadcasted_iota(I32, (N_EXPERTS, tm), 0)
    picked, weights = [], []
    onehot = jnp.zeros((N_EXPERTS, tm), F32)
    for _ in range(TOP_K):
        m = jnp.max(cur, axis=0, keepdims=True)
        idx = jnp.min(jnp.where(cur == m, eid, N_EXPERTS), axis=0, keepdims=True)
        hit = eid == idx
        picked.append(idx)
        weights.append(jnp.sum(jnp.where(hit, scores, 0.0), axis=0, keepdims=True))
        onehot = jnp.where(hit, 1.0, onehot)
        cur = jnp.where(hit, -jnp.inf, cur)
    wsum = weights[0]
    for wk in weights[1:]:
        wsum = wsum + wk
    scale = ROUTED_SCALE / wsum

    before = jnp.dot(onehot.astype(BF16), tri_ref[...], preferred_element_type=F32) + carry_ref[...]
    carry = carry_ref[...] + jnp.sum(onehot, axis=1, keepdims=True)
    carry_ref[...] = carry
    cnt_ref[...] = jnp.broadcast_to(carry, cnt_ref.shape)
    for kk in range(TOP_K):
        eidx_ref[kk:kk + 1, :] = picked[kk]
        w_ref[kk:kk + 1, :] = weights[kk] * scale
        rank_ref[kk:kk + 1, :] = jnp.sum(jnp.where(eid == picked[kk], before, 0.0), axis=0, keepdims=True).astype(I32)


def post_mixer(a_prompt, a_sample, wo_bf, x, mod, gffn, router_t, router_b, tri):
    t = x[0].shape[0] + x[1].shape[0]
    tm = mod.tile
    npt = mod.npt
    row = lambda i: (i, 0)
    col = lambda i: (0, i)
    const = lambda i: (0, 0)
    return pl.pallas_call(
        functools.partial(_post_mixer_kernel, mod.npt),
        out_shape=(jax.ShapeDtypeStruct((t, D_MODEL), F32), jax.ShapeDtypeStruct((t, D_MODEL // 2), U32),
                   jax.ShapeDtypeStruct((TOP_K, t), I32), jax.ShapeDtypeStruct((TOP_K, t), F32),
                   jax.ShapeDtypeStruct((TOP_K, t), I32), jax.ShapeDtypeStruct((N_EXPERTS, 128), F32)),
        grid=(t // tm,),
        in_specs=[pl.BlockSpec((tm, D_MODEL), lambda i: (jnp.minimum(i, npt - 1), 0)),
                  pl.BlockSpec((tm, D_MODEL), lambda i: (jnp.maximum(i - npt, 0), 0)),
                  pl.BlockSpec((D_MODEL, D_MODEL), const)] + _stream_specs(x, tm, npt)
        + mod.specs(2) + [pl.BlockSpec((1, D_MODEL), const)] + mod.specs(4) + mod.specs(3)
        + [pl.BlockSpec((N_EXPERTS, D_MODEL), const), pl.BlockSpec((N_EXPERTS, 1), const),
           pl.BlockSpec((tm, tm), const)],
        out_specs=(pl.BlockSpec((tm, D_MODEL), row), pl.BlockSpec((tm, D_MODEL // 2), row),
                   pl.BlockSpec((TOP_K, tm), col), pl.BlockSpec((TOP_K, tm), col), pl.BlockSpec((TOP_K, tm), col),
                   pl.BlockSpec((N_EXPERTS, 128), const)),
        scratch_shapes=[pltpu.VMEM((N_EXPERTS, 1), F32)],
        compiler_params=_params(("arbitrary",)),
        name="post_mixer",
    )(a_prompt, a_sample, wo_bf, *x, *mod.operands(2), gffn.reshape(1, -1), *mod.operands(4), *mod.operands(3),
      router_t, router_b.reshape(-1, 1), tri)


def _row_copy(src, src_row, dst, dst_row, sem):
    return pltpu.make_async_copy(src.at[src_row], dst.at[dst_row], sem)


def _by_parity(i, fn):
    @pl.when(i % 2 == 0)
    def _():
        fn(0)

    @pl.when(i % 2 == 1)
    def _():
        fn(1)


def _dispatch_kernel(zb_ref, h_ref, dest_hbm, xs_hbm, idx_a, idx_b, zero_ref, slab_ref, sem_idx, sem_zero, sem_rows):
    i = pl.program_id(0)
    te = h_ref.shape[0]
    _store_row_slabs(slab_ref, slice(None), h_ref[...])
    idx_bufs = (idx_a, idx_b)

    def idx_copy(tile, p):
        return pltpu.make_async_copy(dest_hbm.at[tile], idx_bufs[p], sem_idx.at[p])

    def zero_copy(e):
        return pltpu.make_async_copy(zero_ref, xs_hbm.at[pl.ds(zb_ref[e] * EXPERT_BLOCK, EXPERT_BLOCK)], sem_zero)

    @pl.when(i == 0)
    def _():
        idx_copy(0, 0).start()
        zero_ref[...] = jnp.zeros(zero_ref.shape, U32)

        def start(e, carry):
            @pl.when(zb_ref[e] >= 0)
            def _():
                zero_copy(e).start()
            return carry

        def wait(e, carry):
            @pl.when(zb_ref[e] >= 0)
            def _():
                zero_copy(e).wait()
            return carry

        lax.fori_loop(0, zb_ref.shape[0], start, 0)
        lax.fori_loop(0, zb_ref.shape[0], wait, 0)

    def step(p):
        @pl.when(i + 1 < pl.num_programs(0))
        def _():
            idx_copy(i + 1, 1 - p).start()

        idx_copy(i, p).wait()
        idx = idx_bufs[p]

        def issue(t, carry):
            for kk in range(TOP_K):
                _row_copy(slab_ref, t, xs_hbm, idx[kk * te + t], sem_rows).start(priority=kk % 2)
            return carry

        def drain(t, carry):
            for kk in range(TOP_K):
                _row_copy(slab_ref, t, xs_hbm, idx[kk * te + t], sem_rows).wait()
            return carry

        lax.fori_loop(0, te, issue, 0, unroll=ROW_COPY_UNROLL)
        lax.fori_loop(0, te, drain, 0, unroll=ROW_COPY_UNROLL)

    _by_parity(i, step)


def moe_dispatch(h, dest_tiles, zero_blocks, n_slots):
    t = h.shape[0]
    te = ROUTE_TILE
    return pl.pallas_call(
        _dispatch_kernel,
        out_shape=jax.ShapeDtypeStruct((n_slots, ROW_CHUNKS, 128), U32),
        grid_spec=pltpu.PrefetchScalarGridSpec(
            num_scalar_prefetch=1,
            grid=(t // te,),
            in_specs=[pl.BlockSpec((te, h.shape[1]), lambda i, zb: (i, 0)), pl.BlockSpec(memory_space=pl.ANY)],
            out_specs=pl.BlockSpec(memory_space=pl.ANY),
            scratch_shapes=[pltpu.SMEM((te * TOP_K,), I32), pltpu.SMEM((te * TOP_K,), I32),
                            pltpu.VMEM((EXPERT_BLOCK, ROW_CHUNKS, 128), U32), pltpu.VMEM((te, ROW_CHUNKS, 128), U32),
                            pltpu.SemaphoreType.DMA((2,)), pltpu.SemaphoreType.DMA, pltpu.SemaphoreType.DMA],
        ),
        compiler_params=_params(("arbitrary",)),
        name="moe_dispatch",
    )(zero_blocks, h, dest_tiles)


def _expert_kernel(be_ref, nu_ref, xs_ref, wg0_ref, wu0_ref, wd0_ref, wg1_ref, wu1_ref, wd1_ref, ys_ref,
                   wg_bf, wu_bf, wd_bf):
    b = pl.program_id(0)
    rb = EXPERT_BLOCK
    used = 2 * b < nu_ref[0]
    for s, (wg, wu, wd) in enumerate(((wg0_ref, wu0_ref, wd0_ref), (wg1_ref, wu1_ref, wd1_ref))):
        j = 2 * b + s
        fresh = (b == 0) | (be_ref[j] != be_ref[jnp.maximum(j - 2, 0)])

        @pl.when(used & fresh)
        def _():
            wg_bf[s] = wg[...].astype(BF16)
            wu_bf[s] = wu[...].astype(BF16)
            wd_bf[s] = wd[...].astype(BF16)

    @pl.when(used)
    def _():
        for s in range(2):
            rows = slice(rb * s, rb * (s + 1))
            x = _unpack_bf16_pairs(_load_row_slabs(xs_ref, rows))
            hg = jnp.dot(x, wg_bf[s], preferred_element_type=F32)
            hu = jnp.dot(x, wu_bf[s], preferred_element_type=F32)
            y = jnp.dot((_silu(hg) * hu).astype(BF16), wd_bf[s], preferred_element_type=F32)
            ys_ref[rows, :] = _pack_bf16_pairs(y)


def moe_experts(xs, block_expert, n_used, wg, wu, wd, layer):
    n_slots = xs.shape[0]
    rb = EXPERT_BLOCK
    ff = wg.shape[3]
    rows_in = lambda b, be, nu: (jnp.minimum(b, (nu[0] - 1) // 2), 0, 0)
    rows_out = lambda b, be, nu: (jnp.minimum(b, (nu[0] - 1) // 2), 0)
    w_in = lambda s: (lambda b, be, nu: (layer, be[2 * b + s], 0, 0))
    return pl.pallas_call(
        _expert_kernel,
        out_shape=jax.ShapeDtypeStruct((n_slots, D_MODEL // 2), U32),
        grid_spec=pltpu.PrefetchScalarGridSpec(
            num_scalar_prefetch=2,
            grid=(n_slots // (2 * rb),),
            in_specs=[pl.BlockSpec((2 * rb,) + xs.shape[1:], rows_in)]
            + [pl.BlockSpec((None, None, D_MODEL, ff), w_in(0)), pl.BlockSpec((None, None, D_MODEL, ff), w_in(0)),
               pl.BlockSpec((None, None, ff, D_MODEL), w_in(0)),
               pl.BlockSpec((None, None, D_MODEL, ff), w_in(1)), pl.BlockSpec((None, None, D_MODEL, ff), w_in(1)),
               pl.BlockSpec((None, None, ff, D_MODEL), w_in(1))],
            out_specs=pl.BlockSpec((2 * rb, D_MODEL // 2), rows_out),
            scratch_shapes=[pltpu.VMEM((2, D_MODEL, ff), BF16), pltpu.VMEM((2, D_MODEL, ff), BF16),
                            pltpu.VMEM((2, ff, D_MODEL), BF16)],
        ),
        compiler_params=_params(("arbitrary",)),
        name="moe_experts",
    )(block_expert, n_used, xs, wg, wu, wd, wg, wu, wd)


def _combine_kernel(npt, final, h_ref, swg_ref, swu_ref, swd_ref, x1_ref, g2s_ref, g2t_ref, w_ref, fg_ref,
                    dest_hbm, ys_hbm, *rest):
    out_refs, (idx_a, idx_b, ybuf_ref, sem_idx, sem_rows) = rest[:-5], rest[-5:]
    i = pl.program_id(0)
    n = pl.num_programs(0)
    tg = h_ref.shape[0]
    idx_bufs = (idx_a, idx_b)

    def idx_copy(tile, p):
        return pltpu.make_async_copy(dest_hbm.at[tile], idx_bufs[p], sem_idx.at[p])

    def gather_rows(p, wait):
        idx = idx_bufs[p]

        def body(t, carry):
            for kk in range(TOP_K):
                src = 0 if wait else idx[kk * tg + t]
                dst = ybuf_ref.at[p, kk]
                cp = pltpu.make_async_copy(ys_hbm.at[pl.ds(src, 1)], dst.at[pl.ds(t, 1)], sem_rows.at[p])
                cp.wait() if wait else cp.start(priority=kk % 2)
            return carry

        lax.fori_loop(0, tg, body, 0, unroll=ROW_COPY_UNROLL)

    @pl.when(i == 0)
    def _():
        idx_copy(0, 0).start()
        idx_copy(0, 0).wait()
        gather_rows(0, wait=False)

        @pl.when(n > 1)
        def _():
            idx_copy(1, 1).start()

    def prefetch(p):
        @pl.when(i + 1 < n)
        def _():
            idx_copy(i + 1, 1 - p).wait()
            gather_rows(1 - p, wait=False)

        @pl.when(i + 2 < n)
        def _():
            idx_copy(i + 2, p).start()

    _by_parity(i, prefetch)

    hb = _unpack_bf16_pairs(h_ref[...])
    hid = _silu(jnp.dot(hb, swg_ref[...], preferred_element_type=F32)) * jnp.dot(hb, swu_ref[...],
                                                                                 preferred_element_type=F32)
    acc = jnp.dot(hid.astype(BF16), swd_ref[...], preferred_element_type=F32)
    w = w_ref[...]
    gate = _pick(i >= npt, g2s_ref, g2t_ref)

    def finish(p):
        gather_rows(p, wait=True)
        half = D_MODEL // 2
        lo = jnp.zeros((tg, half), F32)
        hi = jnp.zeros((tg, half), F32)
        for kk in range(TOP_K):
            u = ybuf_ref[p, kk]
            wk = w[:, kk:kk + 1]
            lo = lo + lax.bitcast_convert_type(u << 16, F32) * wk
            hi = hi + lax.bitcast_convert_type(u & jnp.uint32(0xFFFF0000), F32) * wk
        routed = jnp.concatenate([lo, hi], axis=1)
        x2 = x1_ref[...] + gate * (routed + acc)
        if final:
            ms = jnp.mean(x2 * x2, axis=-1, keepdims=True)
            x2 = x2 * lax.rsqrt(ms + NORM_EPS) * fg_ref[...]

        @pl.when(i < npt)
        def _():
            out_refs[0][...] = x2

        @pl.when(i >= npt)
        def _():
            out_refs[1][...] = x2

    _by_parity(i, finish)


def moe_combine(h, swg_bf, swu_bf, swd_bf, x1, mod, w_tok, final_g, dest_tiles, ys, final):
    t = h.shape[0]
    tg = mod.tile
    npt = mod.npt
    row = lambda i: (i, 0)
    const = lambda i: (0, 0)
    out_shape = (jax.ShapeDtypeStruct((npt * tg, D_MODEL), F32), jax.ShapeDtypeStruct((t - npt * tg, D_MODEL), F32))
    out_specs = tuple(_stream_specs(None, tg, npt))
    return pl.pallas_call(
        functools.partial(_combine_kernel, mod.npt, final),
        out_shape=out_shape,
        grid=(t // tg,),
        in_specs=[pl.BlockSpec((tg, h.shape[1]), row), pl.BlockSpec(swg_bf.shape, const),
                  pl.BlockSpec(swu_bf.shape, const),
                  pl.BlockSpec(swd_bf.shape, const), pl.BlockSpec((tg, D_MODEL), row)]
        + mod.specs(5)
        + [pl.BlockSpec((tg, TOP_K), row), pl.BlockSpec((1, D_MODEL), const),
           pl.BlockSpec(memory_space=pl.ANY), pl.BlockSpec(memory_space=pl.ANY)],
        out_specs=out_specs,
        scratch_shapes=[pltpu.SMEM((tg * TOP_K,), I32), pltpu.SMEM((tg * TOP_K,), I32),
                        pltpu.VMEM((2, TOP_K, tg, D_MODEL // 2), U32),
                        pltpu.SemaphoreType.DMA((2,)), pltpu.SemaphoreType.DMA((2,))],
        compiler_params=_params(("arbitrary",)),
        name="moe_combine",
    )(h, swg_bf, swu_bf, swd_bf, x1, *mod.operands(5), w_tok, final_g.reshape(1, -1), dest_tiles, ys)


def _slot_kernel(eidx_ref, rank_ref, start_ref, o_ref):
    tm = eidx_ref.shape[1]
    tr = o_ref.shape[2]
    eid = lax.broadcasted_iota(I32, (N_EXPERTS, tm), 0)
    start = start_ref[...]
    for kk in range(TOP_K):
        base = jnp.sum(jnp.where(eid == eidx_ref[kk:kk + 1, :], start, 0.0), axis=0, keepdims=True)
        slot = base.astype(I32) + rank_ref[kk:kk + 1, :]
        for j in range(tm // tr):
            o_ref[j, kk:kk + 1, :] = slot[:, tr * j:tr * (j + 1)]


def assignment_slots(eidx_t, rank_t, pad_start):
    t = eidx_t.shape[1]
    tm = TOKEN_TILE
    tr = ROUTE_TILE
    col = lambda i: (0, i)
    out = pl.pallas_call(
        _slot_kernel,
        out_shape=jax.ShapeDtypeStruct((t // tr, TOP_K, tr), I32),
        grid=(t // tm,),
        in_specs=[pl.BlockSpec((TOP_K, tm), col), pl.BlockSpec((TOP_K, tm), col),
                  pl.BlockSpec((N_EXPERTS, 1), lambda i: (0, 0))],
        out_specs=pl.BlockSpec((tm // tr, TOP_K, tr), lambda i: (i, 0, 0)),
        compiler_params=_params(("parallel",)),
        name="assignment_slots",
    )(eidx_t, rank_t, pad_start.astype(F32).reshape(-1, 1))
    return out.reshape(t // tr, TOP_K * tr)


def _routing_tables(counts, n_blocks):
    rb = EXPERT_BLOCK
    counts = counts.astype(I32)
    padded = (counts + rb - 1) // rb * rb
    pad_end = jnp.cumsum(padded)
    pad_start = pad_end - padded
    n_used = pad_end[-1] // rb
    blocks = jnp.arange(n_blocks, dtype=I32)
    block_expert = jnp.sum((pad_end[None, :] <= (blocks * rb)[:, None]).astype(I32), axis=1)
    last_used = jnp.sum((pad_end <= (n_used - 1) * rb).astype(I32))
    block_expert = jnp.minimum(jnp.where(blocks < n_used, block_expert, last_used), N_EXPERTS - 1)
    zero_blocks = jnp.where(counts % rb != 0, pad_end // rb - 1, -1)
    zero_blocks = jnp.concatenate([zero_blocks, jnp.where(n_used % 2 == 1, n_used, -1).reshape(1)]).astype(I32)
    return pad_start, block_expert, n_used.reshape(1).astype(I32), zero_blocks


def moe_layer(h, x1, eidx_t, w_t, rank_t, counts, mod_route, wg, wu, wd, layer, swg_bf, swu_bf, swd_bf,
              final_g, final):
    t = h.shape[0]
    rb = EXPERT_BLOCK
    tr = ROUTE_TILE
    n_blocks = -(-(t * TOP_K) // rb) + N_EXPERTS
    n_blocks += n_blocks % 2
    pad_start, block_expert, n_used, zero_blocks = _routing_tables(counts, n_blocks)
    dest_tiles = assignment_slots(eidx_t, rank_t, pad_start)
    xs = moe_dispatch(h, dest_tiles, zero_blocks, n_blocks * rb)
    ys = moe_experts(xs, block_expert, n_used, wg, wu, wd, layer)
    return moe_combine(h, swg_bf, swu_bf, swd_bf, x1, mod_route, w_t.T, final_g, dest_tiles, ys, final)


def _rope_tables(n_batch, seq_len, n_seq, n_new):
    half = SWA_HEAD_DIM // 2
    inv = ROPE_THETA ** (-jnp.arange(half, dtype=F32) / half)
    pos = jnp.concatenate([jnp.tile(jnp.arange(seq_len, dtype=F32), n_batch),
                           jnp.tile(PAST_LEN + jnp.arange(n_new, dtype=F32), n_seq)])
    ang = pos[:, None] * inv[None, :]
    cos = jnp.tile(jnp.cos(ang), (1, 128 // half))
    sin = jnp.sin(ang)
    sin = jnp.tile(jnp.concatenate([-sin, sin], axis=1), (1, 128 // SWA_HEAD_DIM))
    return cos, sin


def kernel(x_prompt, x_sample, c_prompt, c_sample, cache_swa_k, cache_swa_v, state_gla, norm_mix_g, norm_ffn_g,
           final_g, ada_w, ada_b, swa_wqkv, swa_sinks, swa_wo, gla_win, gla_wa1, gla_wa2, gla_ba, gla_norm_g,
           gla_wo, moe_router, moe_bias, moe_wg, moe_wu, moe_wd, shared_wg, shared_wu, shared_wd):
    n_batch, seq_len, d = x_prompt.shape
    n_seq, n_new, _ = x_sample.shape
    depth = ada_w.shape[0]
    tp = n_batch * seq_len
    ts = n_seq * n_new
    t = tp + ts
    tm = TOKEN_TILE
    tr = ROUTE_TILE

    x = (x_prompt.reshape(tp, d), x_sample.reshape(ts, d))
    c_all = jnp.concatenate([jnp.repeat(c_sample, n_new, axis=0), c_prompt], axis=0)
    mod = ada_modulation(c_all, ada_w, ada_b)
    cos_tab, sin_tab = _rope_tables(n_batch, seq_len, n_seq, n_new)
    tri = jnp.triu(jnp.ones((tm, tm), BF16), k=1)

    new_k, new_v, new_s = [], [], []
    new_k_s, new_v_s, new_s_s = [], [], []
    for layer in range(depth):
        mod_tok = _Mod(mod, layer, n_batch, seq_len, tm)
        mod_route = _Mod(mod, layer, n_batch, seq_len, tr)
        m = layer // 2
        if layer % 2 == 0:
            q, k, v, k_dup, v_dup = swa_qkv(x, norm_mix_g[layer], mod_tok, swa_wqkv[m].astype(BF16), cos_tab, sin_tab)
            a_p = swa_prompt_attention(q, k_dup, v_dup, swa_sinks[m], n_batch, seq_len)
            nk = SWA_KV_HEADS * SWA_HEAD_DIM
            a_s, ck, cv = swa_sample_attention(q, k, v, cache_swa_k[m].reshape(n_seq, WINDOW, nk),
                                             cache_swa_v[m].reshape(n_seq, WINDOW, nk), swa_sinks[m], tp, n_new)
            kv_shape = (n_batch, WINDOW, SWA_KV_HEADS, SWA_HEAD_DIM)
            tails = [slice((b + 1) * seq_len - WINDOW, (b + 1) * seq_len) for b in range(n_batch)]
            new_k.append(jnp.stack([k[rows] for rows in tails]).reshape(kv_shape))
            new_v.append(jnp.stack([v[rows] for rows in tails]).reshape(kv_shape))
            new_k_s.append(ck.reshape(n_seq, WINDOW, SWA_KV_HEADS, SWA_HEAD_DIM))
            new_v_s.append(cv.reshape(n_seq, WINDOW, SWA_KV_HEADS, SWA_HEAD_DIM))
            wo = swa_wo[m]
        else:
            pad = jnp.zeros((d, 128 - GLA_GATE_RANK), F32)
            win_ext = jnp.concatenate([gla_win[m], gla_wa1[m], pad], axis=1).astype(BF16)
            wa2_pad = jnp.concatenate([gla_wa2[m], jnp.zeros((128 - GLA_GATE_RANK, GLA_KEY_DIM), F32)],
                                      axis=0).astype(BF16)
            q, k, v, r, gate = gla_project(x, norm_mix_g[layer], mod_tok, win_ext, wa2_pad, gla_ba[m])
            a_p, s_prompt = gla_prompt(q, k, gate, v, r, gla_norm_g[m], n_batch, seq_len)
            a_s, s_sample = gla_sample(q, k, gate, v, r, gla_norm_g[m], state_gla[m], tp, n_new)
            new_s.append(s_prompt)
            new_s_s.append(s_sample)
            wo = gla_wo[m]
        x1, h, eidx_t, w_t, rank_t, cnt = post_mixer(a_p, a_s, wo.astype(BF16), x, mod_tok, norm_ffn_g[layer],
                                                     moe_router[layer].T, moe_bias[layer], tri)
        x = moe_layer(h, x1, eidx_t, w_t, rank_t, cnt[:, 0], mod_route,
                      moe_wg, moe_wu, moe_wd, layer,
                      shared_wg[layer].astype(BF16), shared_wu[layer].astype(BF16), shared_wd[layer].astype(BF16),
                      final_g, layer == depth - 1)

    y_prompt = x[0].reshape(n_batch, seq_len, d)
    y_sample = x[1].reshape(n_seq, n_new, d)
    return (y_prompt, y_sample, jnp.stack(new_k), jnp.stack(new_v), jnp.stack(new_k_s), jnp.stack(new_v_s),
            jnp.stack(new_s), jnp.stack(new_s_s))
```

```python
import functools

import jax
import jax.numpy as jnp
from jax import lax
from jax.experimental import pallas as pl
from jax.experimental.pallas import tpu as pltpu

F32 = jnp.float32
BF16 = jnp.bfloat16
I32 = jnp.int32
U32 = jnp.uint32

D_MODEL = 1024
PAST_LEN = 8192
SWA_HEAD_DIM = 64
SWA_HEADS = 16
SWA_KV_HEADS = 4
SWA_GROUP = 4
WINDOW = 128
ROPE_THETA = 10000.0
GLA_HEADS = 4
GLA_DK = 128
GLA_DV = 256
GLA_KEY_DIM = 512
GLA_VAL_DIM = 1024
GLA_GATE_RANK = 16
GLA_GATE_NORMALIZER = 16.0
GLA_CHUNK = 64
N_EXPERTS = 64
TOP_K = 8
N_GROUPS = 8
TOPK_GROUPS = 4
EXPERT_FF = 256
ROUTED_SCALE = 2.5
NORM_EPS = 1e-6

TOKEN_TILE = 512
ROUTE_TILE = 512
EXPERT_BLOCK = 256
ADA_TILE = 512
SAMPLE_SEQS = 8
GLA_PROMPT_BATCH = 4
ROW_COPY_UNROLL = 16
SWA_PROMPT_BATCH = 4
VMEM_LIMIT = 48 * 1024 * 1024

NT_DIMS = (((1,), (1,)), ((), ()))
TN_DIMS = (((0,), (0,)), ((), ()))


def _params(semantics):
    return pltpu.CompilerParams(dimension_semantics=semantics, vmem_limit_bytes=VMEM_LIMIT)


def _silu(x):
    return x * jax.nn.sigmoid(x)


def _norm_mod(x, g, sc, sh):
    ms = jnp.mean(x * x, axis=-1, keepdims=True)
    return (x * lax.rsqrt(ms + NORM_EPS) * g) * (1.0 + sc) + sh


def _pack_bf16_pairs(x):
    half = x.shape[1] // 2
    xb = x.astype(BF16).astype(F32)
    lo = lax.bitcast_convert_type(xb[:, :half], U32) >> 16
    hi = lax.bitcast_convert_type(xb[:, half:], U32) & jnp.uint32(0xFFFF0000)
    return lo | hi


def _unpack_bf16_pairs(u):
    lo = lax.bitcast_convert_type(u << 16, F32)
    hi = lax.bitcast_convert_type(u & jnp.uint32(0xFFFF0000), F32)
    return jnp.concatenate([lo, hi], axis=1).astype(BF16)


ROW_CHUNKS = D_MODEL // 2 // 128


def _store_row_slabs(ref, rows, words):
    for c in range(ROW_CHUNKS):
        ref[rows, c, :] = words[:, 128 * c:128 * (c + 1)]


def _load_row_slabs(ref, rows):
    return jnp.concatenate([ref[rows, c, :] for c in range(ROW_CHUNKS)], axis=1)


def _split3(x):
    x1 = x.astype(BF16)
    r1 = x - x1.astype(F32)
    x2 = r1.astype(BF16)
    x3 = (r1 - x2.astype(F32)).astype(BF16)
    return x1, x2, x3


def _ada_kernel(c_ref, w_ref, b_ref, o_ref):
    s = _silu(c_ref[...]).astype(BF16)
    o_ref[...] = jnp.dot(s, w_ref[...].astype(BF16), preferred_element_type=F32) + b_ref[...]


def ada_modulation(c_all, ada_w, ada_b):
    depth, d, n = ada_w.shape
    rows = c_all.shape[0]
    return pl.pallas_call(
        _ada_kernel,
        out_shape=jax.ShapeDtypeStruct((depth, rows, n), F32),
        grid=(depth, n // ADA_TILE),
        in_specs=[
            pl.BlockSpec((rows, d), lambda l, j: (0, 0)),
            pl.BlockSpec((None, d, ADA_TILE), lambda l, j: (l, 0, j)),
            pl.BlockSpec((None, 1, ADA_TILE), lambda l, j: (l, 0, j)),
        ],
        out_specs=pl.BlockSpec((None, rows, ADA_TILE), lambda l, j: (l, 0, j)),
        compiler_params=_params(("parallel", "parallel")),
        name="ada_modulation",
    )(c_all, ada_w, ada_b.reshape(depth, 1, n))


class _Mod:
    def __init__(self, mod, layer, n_batch, seq_len, tile):
        depth, rows, n = mod.shape
        self.tile = tile
        self.layer = layer
        self.npt = n_batch * seq_len // tile
        self.mod_tok = mod
        self.mod_seq = mod[:, rows - n_batch:].reshape(depth, n_batch, 1, n)
        self.tiles_per_seq = seq_len // tile

    def operands(self, chunk):
        del chunk
        return [self.mod_seq, self.mod_tok]

    def specs(self, chunk):
        l, npt, tps = self.layer, self.npt, self.tiles_per_seq
        n_seq = self.mod_seq.shape[1]
        seq_spec = pl.BlockSpec((None, None, 1, D_MODEL),
                                lambda i, *_: (l, jnp.minimum(i // tps, n_seq - 1), 0, chunk))
        tok_spec = pl.BlockSpec((None, self.tile, D_MODEL),
                                lambda i, *_: (l, jnp.maximum(i - npt, 0), chunk))
        return [seq_spec, tok_spec]


def _pick(is_sample, seq_ref, tok_ref):
    return jnp.where(is_sample, tok_ref[...], seq_ref[...])


def _stream_specs(x_pair, tile, npt):
    del x_pair
    return [pl.BlockSpec((tile, D_MODEL), lambda i, *_: (jnp.minimum(i, npt - 1), 0)),
            pl.BlockSpec((tile, D_MODEL), lambda i, *_: (jnp.maximum(i - npt, 0), 0))]


def _swa_qkv_kernel(npt, xp_ref, xs_ref, g_ref, scs_ref, sct_ref, shs_ref, sht_ref, w_ref, cos_ref, sin_ref,
                    q_ref, k_ref, v_ref, kd_ref, vd_ref):
    is_s = pl.program_id(0) >= npt
    h = _norm_mod(_pick(is_s, xp_ref, xs_ref), g_ref[...], _pick(is_s, scs_ref, sct_ref), _pick(is_s, shs_ref, sht_ref))
    qkv = jnp.dot(h.astype(BF16), w_ref[...], preferred_element_type=F32)
    cos = cos_ref[...]
    sin = sin_ref[...]
    lane = lax.broadcasted_iota(I32, cos.shape, 1)
    first_half = (lane % SWA_HEAD_DIM) < (SWA_HEAD_DIM // 2)

    def rope(xc):
        rot = jnp.where(first_half, pltpu.roll(xc, 128 - SWA_HEAD_DIM // 2, 1), pltpu.roll(xc, SWA_HEAD_DIM // 2, 1))
        return xc * cos + rot * sin

    nq = SWA_HEADS * SWA_HEAD_DIM
    nk = SWA_KV_HEADS * SWA_HEAD_DIM
    for c in range(nq // 128):
        q_ref[:, 128 * c:128 * (c + 1)] = (rope(qkv[:, 128 * c:128 * (c + 1)]) * (SWA_HEAD_DIM ** -0.5)).astype(BF16)
    low = lane < SWA_HEAD_DIM

    def spread(chunk):
        rolled = pltpu.roll(chunk, SWA_HEAD_DIM, 1)
        return jnp.where(low, chunk, rolled).astype(BF16), jnp.where(low, rolled, chunk).astype(BF16)

    for c in range(nk // 128):
        kc = rope(qkv[:, nq + 128 * c:nq + 128 * (c + 1)])
        vc = qkv[:, nq + nk + 128 * c:nq + nk + 128 * (c + 1)]
        k_ref[:, 128 * c:128 * (c + 1)] = kc
        v_ref[:, 128 * c:128 * (c + 1)] = vc
        kd_ref[:, 256 * c:256 * c + 128], kd_ref[:, 256 * c + 128:256 * (c + 1)] = spread(kc)
        vd_ref[:, 256 * c:256 * c + 128], vd_ref[:, 256 * c + 128:256 * (c + 1)] = spread(vc)


def swa_qkv(x, g, mod, w_bf, cos_tab, sin_tab):
    t = x[0].shape[0] + x[1].shape[0]
    tm = mod.tile
    nq = SWA_HEADS * SWA_HEAD_DIM
    nk = SWA_KV_HEADS * SWA_HEAD_DIM
    row = lambda i: (i, 0)
    return pl.pallas_call(
        functools.partial(_swa_qkv_kernel, mod.npt),
        out_shape=(jax.ShapeDtypeStruct((t, nq), BF16), jax.ShapeDtypeStruct((t, nk), F32),
                   jax.ShapeDtypeStruct((t, nk), F32), jax.ShapeDtypeStruct((t, 2 * nk), BF16),
                   jax.ShapeDtypeStruct((t, 2 * nk), BF16)),
        grid=(t // tm,),
        in_specs=_stream_specs(x, tm, mod.npt) + [pl.BlockSpec((1, D_MODEL), lambda i: (0, 0))]
        + mod.specs(1) + mod.specs(0)
        + [pl.BlockSpec(w_bf.shape, lambda i: (0, 0)), pl.BlockSpec((tm, 128), row), pl.BlockSpec((tm, 128), row)],
        out_specs=(pl.BlockSpec((tm, nq), row), pl.BlockSpec((tm, nk), row), pl.BlockSpec((tm, nk), row),
                   pl.BlockSpec((tm, 2 * nk), row), pl.BlockSpec((tm, 2 * nk), row)),
        compiler_params=_params(("parallel",)),
        name="swa_qkv",
    )(*x, g.reshape(1, -1), *mod.operands(1), *mod.operands(0), w_bf, cos_tab, sin_tab)


def _sink_softmax(s, sink_col):
    m = jnp.maximum(jnp.max(s, axis=-1, keepdims=True), sink_col)
    e = jnp.exp(s - m)
    den = jnp.sum(e, axis=-1, keepdims=True) + jnp.exp(sink_col - m)
    return e * (1.0 / den)


def _pair_attention(sink_ref, q_ref, kblks, vblks, mask, o_ref):
    rows = q_ref.shape[0]
    half = mask.shape[1]
    per_group = SWA_GROUP // 2
    scores = []
    for g in range(SWA_KV_HEADS):
        q2 = jnp.concatenate([q_ref[:, 128 * c:128 * (c + 1)] for c in range(per_group * g, per_group * (g + 1))],
                             axis=0)
        s = lax.dot_general(q2, kblks[g], NT_DIMS, preferred_element_type=F32)
        for j in range(per_group):
            scores += [s[rows * j:rows * (j + 1), :half], s[rows * j:rows * (j + 1), half:]]
    s_all = jnp.where(mask, jnp.stack(scores), -jnp.inf)
    sinks = jnp.stack([jnp.full((1, 1), sink_ref[h], F32) for h in range(SWA_HEADS)])
    p_all = _sink_softmax(s_all, sinks).astype(BF16)
    for g in range(SWA_KV_HEADS):
        chunks = range(per_group * g, per_group * (g + 1))
        p2 = jnp.concatenate([jnp.concatenate([p_all[2 * c], p_all[2 * c + 1]], axis=1) for c in chunks], axis=0)
        o = jnp.dot(p2, vblks[g], preferred_element_type=F32).astype(BF16)
        for j, c in enumerate(chunks):
            o_ref[:, 128 * c:128 * (c + 1)] = o[rows * j:rows * (j + 1)]


def _swa_prompt_kernel(sink_ref, *refs):
    o_ref = refs[-1]
    j = pl.program_id(1)
    blk = refs[0].shape[0]
    qi = lax.broadcasted_iota(I32, (blk, 2 * blk), 0)
    sj = lax.broadcasted_iota(I32, (blk, 2 * blk), 1)
    rel = qi + blk - sj
    mask = (rel >= 0) & (rel <= WINDOW) & ((sj >= blk) | (j > 0))
    low = lax.broadcasted_iota(I32, (2 * blk, 128), 1) < SWA_HEAD_DIM
    zero = jnp.zeros((2 * blk, 128), BF16)
    for i in range(SWA_PROMPT_BATCH):
        q_ref, kc_ref, kp_ref, vc_ref, vp_ref = refs[5 * i:5 * (i + 1)]
        kblks, vblks = [], []
        for g in range(SWA_KV_HEADS):
            cs = slice(128 * g, 128 * (g + 1))
            kcat = jnp.concatenate([kp_ref[:, cs], kc_ref[:, cs]], axis=0)
            vcat = jnp.concatenate([vp_ref[:, cs], vc_ref[:, cs]], axis=0)
            kblks.append(jnp.concatenate([jnp.where(low, kcat, zero), jnp.where(low, zero, kcat)], axis=0))
            vblks.append(jnp.concatenate([jnp.where(low, vcat, zero), jnp.where(low, zero, vcat)], axis=0))
        _pair_attention(sink_ref, q_ref, kblks, vblks, mask, o_ref.at[i])


def swa_prompt_attention(q, k, v, sinks, n_batch, seq_len):
    blk = WINDOW
    nsb = SWA_PROMPT_BATCH
    nb = seq_len // blk
    nq = q.shape[1]
    nk = k.shape[1]
    in_specs, operands = [pl.BlockSpec(memory_space=pltpu.SMEM)], [sinks]
    for i in range(nsb):
        cur = lambda b, j, i=i: ((nsb * b + i) * nb + j, 0)
        prev = lambda b, j, i=i: ((nsb * b + i) * nb + jnp.maximum(j - 1, 0), 0)
        in_specs += [pl.BlockSpec((blk, nq), cur), pl.BlockSpec((blk, nk), cur), pl.BlockSpec((blk, nk), prev),
                     pl.BlockSpec((blk, nk), cur), pl.BlockSpec((blk, nk), prev)]
        operands += [q, k, k, v, v]
    out = pl.pallas_call(
        _swa_prompt_kernel,
        out_shape=jax.ShapeDtypeStruct((n_batch, seq_len, nq), BF16),
        grid=(n_batch // nsb, nb),
        in_specs=in_specs,
        out_specs=pl.BlockSpec((nsb, blk, nq), lambda b, j: (b, j, 0)),
        compiler_params=_params(("parallel", "parallel")),
        name="swa_prompt_attention",
    )(*operands)
    return out.reshape(n_batch * seq_len, nq)


def _swa_sample_kernel(n_new, sink_ref, q_ref, kn_ref, vn_ref, ck_ref, cv_ref, o_ref, nk_ref, nv_ref):
    n_sb, win, _ = ck_ref.shape
    per_seq = win + n_new
    rows = n_sb * n_new
    cols = n_sb * per_seq
    keys, vals = [], []
    for sb in range(n_sb):
        r0 = sb * n_new
        kc = ck_ref[sb]
        vc = cv_ref[sb]
        kn = kn_ref[r0:r0 + n_new, :]
        vn = vn_ref[r0:r0 + n_new, :]
        nk_ref[sb, 0:win - n_new, :] = kc[n_new:]
        nk_ref[sb, win - n_new:win, :] = kn
        nv_ref[sb, 0:win - n_new, :] = vc[n_new:]
        nv_ref[sb, win - n_new:win, :] = vn
        keys += [kc, kn]
        vals += [vc, vn]
    keys = jnp.concatenate(keys, axis=0)
    vals = jnp.concatenate(vals, axis=0)
    ri = lax.broadcasted_iota(I32, (rows, cols), 0)
    ci = lax.broadcasted_iota(I32, (rows, cols), 1)
    ti = ri % n_new
    si = ci % per_seq
    mask = (ri // n_new == ci // per_seq) & (si >= ti) & (si <= ti + WINDOW)
    low = lax.broadcasted_iota(I32, (cols, 128), 1) < SWA_HEAD_DIM
    zero = jnp.zeros((cols, 128), BF16)

    def block_diag(chunk, first):
        rolled = pltpu.roll(chunk, SWA_HEAD_DIM, 1)
        both = (jnp.where(low, chunk, rolled) if first else jnp.where(low, rolled, chunk)).astype(BF16)
        return jnp.concatenate([jnp.where(low, both, zero), jnp.where(low, zero, both)], axis=0)

    chunks = [slice(128 * (g // 2), 128 * (g // 2 + 1)) for g in range(SWA_KV_HEADS)]
    kblks = [block_diag(keys[:, cs], g % 2 == 0) for g, cs in enumerate(chunks)]
    vblks = [block_diag(vals[:, cs], g % 2 == 0) for g, cs in enumerate(chunks)]
    _pair_attention(sink_ref, q_ref, kblks, vblks, mask, o_ref)


def swa_sample_attention(q, k, v, cache_k, cache_v, sinks, n_prompt_rows, n_new):
    n_seq, win, nk = cache_k.shape
    sb = SAMPLE_SEQS
    rows = sb * n_new
    nq = q.shape[1]
    base = n_prompt_rows // rows
    tok = lambda i: (base + i, 0)
    seq = lambda i: (i, 0, 0)
    return pl.pallas_call(
        functools.partial(_swa_sample_kernel, n_new),
        out_shape=(jax.ShapeDtypeStruct((n_seq * n_new, nq), BF16),
                   jax.ShapeDtypeStruct(cache_k.shape, F32), jax.ShapeDtypeStruct(cache_v.shape, F32)),
        grid=(n_seq // sb,),
        in_specs=[pl.BlockSpec(memory_space=pltpu.SMEM),
                  pl.BlockSpec((rows, nq), tok), pl.BlockSpec((rows, nk), tok), pl.BlockSpec((rows, nk), tok),
                  pl.BlockSpec((sb, win, nk), seq), pl.BlockSpec((sb, win, nk), seq)],
        out_specs=(pl.BlockSpec((rows, nq), lambda i: (i, 0)), pl.BlockSpec((sb, win, nk), seq),
                   pl.BlockSpec((sb, win, nk), seq)),
        compiler_params=_params(("parallel",)),
        name="swa_sample_attention",
    )(sinks, q, k, v, cache_k, cache_v)


def _gla_proj_kernel(npt, xp_ref, xs_ref, g_ref, scs_ref, sct_ref, shs_ref, sht_ref, w_ref, wa2_ref, ba_ref,
                     q_ref, k_ref, v_ref, r_ref, gate_ref):
    is_s = pl.program_id(0) >= npt
    h = _norm_mod(_pick(is_s, xp_ref, xs_ref), g_ref[...], _pick(is_s, scs_ref, sct_ref), _pick(is_s, shs_ref, sht_ref))
    proj = jnp.dot(h.astype(BF16), w_ref[...], preferred_element_type=F32)
    kd = GLA_KEY_DIM
    vd = GLA_VAL_DIM
    q_ref[...] = proj[:, :kd] * (GLA_DK ** -0.5)
    k_ref[...] = proj[:, kd:2 * kd]
    v_ref[...] = proj[:, 2 * kd:2 * kd + vd].astype(BF16)
    r_ref[...] = proj[:, 2 * kd + vd:2 * kd + 2 * vd]
    low = proj[:, 2 * kd + 2 * vd:].astype(BF16)
    z = jnp.dot(low, wa2_ref[...], preferred_element_type=F32) + ba_ref[...]
    log_sig = jnp.minimum(z, 0.0) - jnp.log1p(jnp.exp(-jnp.abs(z)))
    gate_ref[...] = log_sig / GLA_GATE_NORMALIZER


def gla_project(x, g, mod, win_ext, wa2_pad, ba):
    t = x[0].shape[0] + x[1].shape[0]
    tm = mod.tile
    kd, vd = GLA_KEY_DIM, GLA_VAL_DIM
    row = lambda i: (i, 0)
    const = lambda i: (0, 0)
    return pl.pallas_call(
        functools.partial(_gla_proj_kernel, mod.npt),
        out_shape=(jax.ShapeDtypeStruct((t, kd), F32), jax.ShapeDtypeStruct((t, kd), F32),
                   jax.ShapeDtypeStruct((t, vd), BF16), jax.ShapeDtypeStruct((t, vd), F32),
                   jax.ShapeDtypeStruct((t, kd), F32)),
        grid=(t // tm,),
        in_specs=_stream_specs(x, tm, mod.npt) + [pl.BlockSpec((1, D_MODEL), const)]
        + mod.specs(1) + mod.specs(0)
        + [pl.BlockSpec(win_ext.shape, const), pl.BlockSpec(wa2_pad.shape, const), pl.BlockSpec((1, kd), const)],
        out_specs=(pl.BlockSpec((tm, kd), row), pl.BlockSpec((tm, kd), row), pl.BlockSpec((tm, vd), row),
                   pl.BlockSpec((tm, vd), row), pl.BlockSpec((tm, kd), row)),
        compiler_params=_params(("parallel",)),
        name="gla_project",
    )(*x, g.reshape(1, -1), *mod.operands(1), *mod.operands(0), win_ext, wa2_pad, ba.reshape(1, -1))


def _cumsum_rows(tri, g):
    n = g.shape[1]
    s = jnp.dot(tri, jnp.concatenate(_split3(g), axis=1), preferred_element_type=F32)
    return s[:, :n] + s[:, n:2 * n] + s[:, 2 * n:]


def _diag_attention(q, k, b, n):
    ng = n // 8
    dk = q.shape[1]
    q3 = q.reshape(ng, 8, dk)
    k3 = k.reshape(ng, 8, dk)
    b3 = b.reshape(ng, 8, dk)
    sub = lax.broadcasted_iota(I32, (ng, 8, dk), 1)
    ti = lax.broadcasted_iota(I32, (n, n), 0)
    si = lax.broadcasted_iota(I32, (n, n), 1)
    attn = jnp.zeros((n, n), F32)
    for j in range(8):
        bj = jnp.broadcast_to(b3[:, j:j + 1, :], b3.shape)
        kj = jnp.broadcast_to(k3[:, j:j + 1, :], k3.shape)
        e = jnp.exp(jnp.minimum(b3 - bj, 0.0))
        m = jnp.where(sub >= j, q3 * e * kj, 0.0)
        col = jnp.sum(m, axis=-1, keepdims=True).reshape(n, 1)
        attn = attn + jnp.where(si == (ti // 8) * 8 + j, col, 0.0)
    return attn


def _cross_attention(q, k, b, n):
    ti = lax.broadcasted_iota(I32, (n, n), 0)
    si = lax.broadcasted_iota(I32, (n, n), 1)
    row = lax.broadcasted_iota(I32, b.shape, 0)
    attn = jnp.zeros((n, n), F32)
    m = n // 2
    while m >= 8:
        nblk = n // m
        refq = jnp.concatenate(
            [jnp.broadcast_to(b[i * m - 1:i * m], (m, b.shape[1])) if i % 2 else b[i * m:(i + 1) * m]
             for i in range(nblk)], axis=0)
        refk = jnp.concatenate(
            [b[i * m:(i + 1) * m] if i % 2 else jnp.broadcast_to(b[(i + 1) * m - 1:(i + 1) * m], (m, b.shape[1]))
             for i in range(nblk)], axis=0)
        odd = ((row // m) % 2) == 1
        qt = jnp.where(odd, q * jnp.exp(jnp.minimum(b - refq, 0.0)), 0.0).astype(BF16)
        kt = jnp.where(odd, 0.0, k * jnp.exp(jnp.minimum(refk - b, 0.0))).astype(BF16)
        a = lax.dot_general(qt, kt, NT_DIMS, preferred_element_type=F32)
        keep = (((ti // m) % 2) == 1) & ((si // m) == (ti // m) - 1)
        attn = attn + jnp.where(keep, a, 0.0)
        m //= 2
    return attn


def _gla_epilogue(o, r, ng):
    ms = jnp.mean(o * o, axis=-1, keepdims=True)
    return (o * lax.rsqrt(ms + NORM_EPS) * ng * _silu(r)).astype(BF16)


def _gla_prompt_kernel(*refs):
    nb = GLA_PROMPT_BATCH
    ins, (ng_ref, o_ref, so_ref, st_ref) = refs[:5 * nb], refs[5 * nb:]
    c = pl.program_id(1)
    n = ins[0].shape[0]

    @pl.when(c == 0)
    def _():
        st_ref[...] = jnp.zeros(st_ref.shape, F32)

    ti = lax.broadcasted_iota(I32, (n, n), 0)
    si = lax.broadcasted_iota(I32, (n, n), 1)
    tri = jnp.where(ti >= si, 1.0, 0.0).astype(BF16)
    for i in range(nb):
        q_ref, k_ref, g_ref, v_ref, r_ref = ins[5 * i:5 * (i + 1)]
        b_all = _cumsum_rows(tri, g_ref[...])
        for h in range(GLA_HEADS):
            ks = slice(GLA_DK * h, GLA_DK * (h + 1))
            vs = slice(GLA_DV * h, GLA_DV * (h + 1))
            q = q_ref[:, ks]
            k = k_ref[:, ks]
            v = v_ref[:, vs]
            b = b_all[:, ks]
            s_t = st_ref[i, h]
            o = lax.dot_general((q * jnp.exp(b)).astype(BF16), s_t.astype(BF16), NT_DIMS,
                                preferred_element_type=F32)
            attn = _cross_attention(q, k, b, n) + _diag_attention(q, k, b, n)
            o = o + jnp.dot(attn.astype(BF16), v, preferred_element_type=F32)
            bl = b[n - 1:n, :]
            kd = (k * jnp.exp(bl - b)).astype(BF16)
            s_new = s_t * jnp.exp(bl) + lax.dot_general(v, kd, TN_DIMS, preferred_element_type=F32)
            st_ref[i, h] = s_new
            o_ref[i, :, vs] = _gla_epilogue(o, r_ref[:, vs], ng_ref[...])

    @pl.when(c == pl.num_programs(1) - 1)
    def _():
        for i in range(nb):
            for h in range(GLA_HEADS):
                so_ref[i, h] = st_ref[i, h].T


def gla_prompt(q, k, g, v, r, norm_g, n_batch, seq_len):
    n = GLA_CHUNK
    nb = GLA_PROMPT_BATCH
    nc = seq_len // n
    kd, vd = GLA_KEY_DIM, GLA_VAL_DIM
    in_specs, operands = [], []
    for i in range(nb):
        row = lambda b, c, i=i: ((nb * b + i) * nc + c, 0)
        in_specs += [pl.BlockSpec((n, kd), row), pl.BlockSpec((n, kd), row), pl.BlockSpec((n, kd), row),
                     pl.BlockSpec((n, vd), row), pl.BlockSpec((n, vd), row)]
        operands += [q, k, g, v, r]
    o, state = pl.pallas_call(
        _gla_prompt_kernel,
        out_shape=(jax.ShapeDtypeStruct((n_batch, seq_len, vd), BF16),
                   jax.ShapeDtypeStruct((n_batch, GLA_HEADS, GLA_DK, GLA_DV), F32)),
        grid=(n_batch // nb, nc),
        in_specs=in_specs + [pl.BlockSpec((1, GLA_DV), lambda b, c: (0, 0))],
        out_specs=(pl.BlockSpec((nb, n, vd), lambda b, c: (b, c, 0)),
                   pl.BlockSpec((nb, GLA_HEADS, GLA_DK, GLA_DV), lambda b, c: (b, 0, 0, 0))),
        scratch_shapes=[pltpu.VMEM((nb, GLA_HEADS, GLA_DV, GLA_DK), F32)],
        compiler_params=_params(("parallel", "arbitrary")),
        name="gla_prompt",
    )(*operands, norm_g.reshape(1, -1))
    return o.reshape(n_batch * seq_len, vd), state


def _gla_sample_kernel(n_new, q_ref, k_ref, g_ref, v_ref, r_ref, ng_ref, si_ref, o_ref, so_ref):
    n = q_ref.shape[0]
    ti = lax.broadcasted_iota(I32, (n, n), 0)
    si = lax.broadcasted_iota(I32, (n, n), 1)
    tri = jnp.where((ti >= si) & (ti // n_new == si // n_new), 1.0, 0.0).astype(BF16)
    b_all = _cumsum_rows(tri, g_ref[...])
    for h in range(GLA_HEADS):
        ks = slice(GLA_DK * h, GLA_DK * (h + 1))
        vs = slice(GLA_DV * h, GLA_DV * (h + 1))
        q = q_ref[:, ks]
        k = k_ref[:, ks]
        v = v_ref[:, vs]
        b = b_all[:, ks]
        attn = _diag_attention(q, k, b, n)
        o_intra = jnp.dot(attn.astype(BF16), v, preferred_element_type=F32)
        qe = (q * jnp.exp(b)).astype(BF16)
        n_sb = n // n_new
        last = [b[n_new * (sb + 1) - 1:n_new * (sb + 1), :] for sb in range(n_sb)]
        bl_rows = jnp.concatenate([jnp.broadcast_to(bl, (n_new, GLA_DK)) for bl in last], axis=0)
        kd = (k * jnp.exp(bl_rows - b)).astype(BF16)
        seq_of_vrow = lax.broadcasted_iota(I32, (n, GLA_DV), 0) // n_new
        seq_of_krow = lax.broadcasted_iota(I32, (n, GLA_DK), 0) // n_new
        o = o_intra
        for sb in range(n_sb):
            s0 = si_ref[sb, h]
            o_sb = jnp.dot(qe, s0.astype(BF16), preferred_element_type=F32)
            o = o + jnp.where(seq_of_vrow == sb, o_sb, 0.0)
            kd_sb = jnp.where(seq_of_krow == sb, kd, jnp.zeros_like(kd))
            upd = lax.dot_general(kd_sb, v, TN_DIMS, preferred_element_type=F32)
            decay_col = jnp.transpose(jnp.broadcast_to(jnp.exp(last[sb]), (8, GLA_DK)))[:, 0:1]
            so_ref[sb, h] = s0 * decay_col + upd
        o_ref[:, vs] = _gla_epilogue(o, r_ref[:, vs], ng_ref[...])


def gla_sample(q, k, g, v, r, norm_g, state, n_prompt_rows, n_new):
    n_seq = state.shape[0]
    sb = SAMPLE_SEQS
    rows = sb * n_new
    kd, vd = GLA_KEY_DIM, GLA_VAL_DIM
    base = n_prompt_rows // rows
    tok = lambda i: (base + i, 0)
    seq = lambda i: (i, 0, 0, 0)
    sblock = (sb, GLA_HEADS, GLA_DK, GLA_DV)
    return pl.pallas_call(
        functools.partial(_gla_sample_kernel, n_new),
        out_shape=(jax.ShapeDtypeStruct((n_seq * n_new, vd), BF16), jax.ShapeDtypeStruct(state.shape, F32)),
        grid=(n_seq // sb,),
        in_specs=[pl.BlockSpec((rows, kd), tok), pl.BlockSpec((rows, kd), tok), pl.BlockSpec((rows, kd), tok),
                  pl.BlockSpec((rows, vd), tok), pl.BlockSpec((rows, vd), tok),
                  pl.BlockSpec((1, GLA_DV), lambda i: (0, 0)),
                  pl.BlockSpec(sblock, seq)],
        out_specs=(pl.BlockSpec((rows, vd), lambda i: (i, 0)), pl.BlockSpec(sblock, seq)),
        compiler_params=_params(("parallel",)),
        name="gla_sample",
    )(q, k, g, v, r, norm_g.reshape(1, -1), state)


def _post_mixer_kernel(npt, ap_ref, as_ref, wo_ref, xp_ref, xs_ref, g1s_ref, g1t_ref, gf_ref, scs_ref, sct_ref, shs_ref, sht_ref,
                       rw_ref, rb_ref, tri_ref,
                       x1_ref, h_ref, eidx_ref, w_ref, rank_ref, cnt_ref, carry_ref):
    i = pl.program_id(0)
    is_s = i >= npt

    @pl.when(i == 0)
    def _():
        carry_ref[...] = jnp.zeros(carry_ref.shape, F32)

    a = jnp.where(is_s, as_ref[...], ap_ref[...])
    x1 = _pick(is_s, xp_ref, xs_ref) + _pick(is_s, g1s_ref, g1t_ref) * jnp.dot(a, wo_ref[...],
                                                                               preferred_element_type=F32)
    x1_ref[...] = x1
    h = _norm_mod(x1, gf_ref[...], _pick(is_s, scs_ref, sct_ref), _pick(is_s, shs_ref, sht_ref))
    h_ref[...] = _pack_bf16_pairs(h)

    h1, h2, _ = _split3(h)
    r1, r2, _ = _split3(rw_ref[...])
    logits = (lax.dot_general(r1, h1, NT_DIMS, preferred_element_type=F32)
              + lax.dot_general(r1, h2, NT_DIMS, preferred_element_type=F32)
              + lax.dot_general(r2, h1, NT_DIMS, preferred_element_type=F32))
    scores = jax.nn.sigmoid(logits)
    sel = scores + rb_ref[...]
    tm = sel.shape[1]
    gsz = N_EXPERTS // N_GROUPS

    sub = lax.broadcasted_iota(I32, (gsz, tm), 0)
    blocks, gscore = [], []
    for g in range(N_GROUPS):
        blk = sel[gsz * g:gsz * (g + 1)]
        m1 = jnp.max(blk, axis=0, keepdims=True)
        first = jnp.min(jnp.where(blk == m1, sub, gsz), axis=0, keepdims=True)
        m2 = jnp.max(jnp.where(sub == first, -jnp.inf, blk), axis=0, keepdims=True)
        blocks.append(blk)
        gscore.append(m1 + m2)
    masked = []
    for g in range(N_GROUPS):
        beaten = jnp.zeros((1, tm), I32)
        for o in range(N_GROUPS):
            if o == g:
                continue
            wins = (gscore[o] > gscore[g]) | ((gscore[o] == gscore[g]) & (o < g))
            beaten = beaten + wins.astype(I32)
        masked.append(jnp.where(beaten < TOPK_GROUPS, blocks[g], -jnp.inf))
    cur = jnp.concatenate(masked, axis=0)

    eid = lax.broadcasted_iota(I32, (N_EXPERTS, tm), 0)
    picked, weights = [], []
    onehot = jnp.zeros((N_EXPERTS, tm), F32)
    for _ in range(TOP_K):
        m = jnp.max(cur, axis=0, keepdims=True)
        idx = jnp.min(jnp.where(cur == m, eid, N_EXPERTS), axis=0, keepdims=True)
        hit = eid == idx
        picked.append(idx)
        weights.append(jnp.sum(jnp.where(hit, scores, 0.0), axis=0, keepdims=True))
        onehot = jnp.where(hit, 1.0, onehot)
        cur = jnp.where(hit, -jnp.inf, cur)
    wsum = weights[0]
    for wk in weights[1:]:
        wsum = wsum + wk
    scale = ROUTED_SCALE / wsum

    before = jnp.dot(onehot.astype(BF16), tri_ref[...], preferred_element_type=F32) + carry_ref[...]
    carry = carry_ref[...] + jnp.sum(onehot, axis=1, keepdims=True)
    carry_ref[...] = carry
    cnt_ref[...] = jnp.broadcast_to(carry, cnt_ref.shape)
    for kk in range(TOP_K):
        eidx_ref[kk:kk + 1, :] = picked[kk]
        w_ref[kk:kk + 1, :] = weights[kk] * scale
        rank_ref[kk:kk + 1, :] = jnp.sum(jnp.where(eid == picked[kk], before, 0.0), axis=0, keepdims=True).astype(I32)


def post_mixer(a_prompt, a_sample, wo_bf, x, mod, gffn, router_t, router_b, tri):
    t = x[0].shape[0] + x[1].shape[0]
    tm = mod.tile
    npt = mod.npt
    row = lambda i: (i, 0)
    col = lambda i: (0, i)
    const = lambda i: (0, 0)
    return pl.pallas_call(
        functools.partial(_post_mixer_kernel, mod.npt),
        out_shape=(jax.ShapeDtypeStruct((t, D_MODEL), F32), jax.ShapeDtypeStruct((t, D_MODEL // 2), U32),
                   jax.ShapeDtypeStruct((TOP_K, t), I32), jax.ShapeDtypeStruct((TOP_K, t), F32),
                   jax.ShapeDtypeStruct((TOP_K, t), I32), jax.ShapeDtypeStruct((N_EXPERTS, 128), F32)),
        grid=(t // tm,),
        in_specs=[pl.BlockSpec((tm, D_MODEL), lambda i: (jnp.minimum(i, npt - 1), 0)),
                  pl.BlockSpec((tm, D_MODEL), lambda i: (jnp.maximum(i - npt, 0), 0)),
                  pl.BlockSpec((D_MODEL, D_MODEL), const)] + _stream_specs(x, tm, npt)
        + mod.specs(2) + [pl.BlockSpec((1, D_MODEL), const)] + mod.specs(4) + mod.specs(3)
        + [pl.BlockSpec((N_EXPERTS, D_MODEL), const), pl.BlockSpec((N_EXPERTS, 1), const),
           pl.BlockSpec((tm, tm), const)],
        out_specs=(pl.BlockSpec((tm, D_MODEL), row), pl.BlockSpec((tm, D_MODEL // 2), row),
                   pl.BlockSpec((TOP_K, tm), col), pl.BlockSpec((TOP_K, tm), col), pl.BlockSpec((TOP_K, tm), col),
                   pl.BlockSpec((N_EXPERTS, 128), const)),
        scratch_shapes=[pltpu.VMEM((N_EXPERTS, 1), F32)],
        compiler_params=_params(("arbitrary",)),
        name="post_mixer",
    )(a_prompt, a_sample, wo_bf, *x, *mod.operands(2), gffn.reshape(1, -1), *mod.operands(4), *mod.operands(3),
      router_t, router_b.reshape(-1, 1), tri)


def _row_copy(src, src_row, dst, dst_row, sem):
    return pltpu.make_async_copy(src.at[src_row], dst.at[dst_row], sem)


def _by_parity(i, fn):
    @pl.when(i % 2 == 0)
    def _():
        fn(0)

    @pl.when(i % 2 == 1)
    def _():
        fn(1)


def _dispatch_kernel(zb_ref, h_ref, dest_hbm, xs_hbm, idx_a, idx_b, zero_ref, slab_ref, sem_idx, sem_zero, sem_rows):
    i = pl.program_id(0)
    te = h_ref.shape[0]
    _store_row_slabs(slab_ref, slice(None), h_ref[...])
    idx_bufs = (idx_a, idx_b)

    def idx_copy(tile, p):
        return pltpu.make_async_copy(dest_hbm.at[tile], idx_bufs[p], sem_idx.at[p])

    def zero_copy(e):
        return pltpu.make_async_copy(zero_ref, xs_hbm.at[pl.ds(zb_ref[e] * EXPERT_BLOCK, EXPERT_BLOCK)], sem_zero)

    @pl.when(i == 0)
    def _():
        idx_copy(0, 0).start()
        zero_ref[...] = jnp.zeros(zero_ref.shape, U32)

        def start(e, carry):
            @pl.when(zb_ref[e] >= 0)
            def _():
                zero_copy(e).start()
            return carry

        def wait(e, carry):
            @pl.when(zb_ref[e] >= 0)
            def _():
                zero_copy(e).wait()
            return carry

        lax.fori_loop(0, zb_ref.shape[0], start, 0)
        lax.fori_loop(0, zb_ref.shape[0], wait, 0)

    def step(p):
        @pl.when(i + 1 < pl.num_programs(0))
        def _():
            idx_copy(i + 1, 1 - p).start()

        idx_copy(i, p).wait()
        idx = idx_bufs[p]

        def issue(t, carry):
            for kk in range(TOP_K):
                _row_copy(slab_ref, t, xs_hbm, idx[kk * te + t], sem_rows).start(priority=kk % 2)
            return carry

        def drain(t, carry):
            for kk in range(TOP_K):
                _row_copy(slab_ref, t, xs_hbm, idx[kk * te + t], sem_rows).wait()
            return carry

        lax.fori_loop(0, te, issue, 0, unroll=ROW_COPY_UNROLL)
        lax.fori_loop(0, te, drain, 0, unroll=ROW_COPY_UNROLL)

    _by_parity(i, step)


def moe_dispatch(h, dest_tiles, zero_blocks, n_slots):
    t = h.shape[0]
    te = ROUTE_TILE
    return pl.pallas_call(
        _dispatch_kernel,
        out_shape=jax.ShapeDtypeStruct((n_slots, ROW_CHUNKS, 128), U32),
        grid_spec=pltpu.PrefetchScalarGridSpec(
            num_scalar_prefetch=1,
            grid=(t // te,),
            in_specs=[pl.BlockSpec((te, h.shape[1]), lambda i, zb: (i, 0)), pl.BlockSpec(memory_space=pl.ANY)],
            out_specs=pl.BlockSpec(memory_space=pl.ANY),
            scratch_shapes=[pltpu.SMEM((te * TOP_K,), I32), pltpu.SMEM((te * TOP_K,), I32),
                            pltpu.VMEM((EXPERT_BLOCK, ROW_CHUNKS, 128), U32), pltpu.VMEM((te, ROW_CHUNKS, 128), U32),
                            pltpu.SemaphoreType.DMA((2,)), pltpu.SemaphoreType.DMA, pltpu.SemaphoreType.DMA],
        ),
        compiler_params=_params(("arbitrary",)),
        name="moe_dispatch",
    )(zero_blocks, h, dest_tiles)


def _expert_kernel(be_ref, nu_ref, xs_ref, wg0_ref, wu0_ref, wd0_ref, wg1_ref, wu1_ref, wd1_ref, ys_ref,
                   wg_bf, wu_bf, wd_bf):
    b = pl.program_id(0)
    rb = EXPERT_BLOCK
    used = 2 * b < nu_ref[0]
    for s, (wg, wu, wd) in enumerate(((wg0_ref, wu0_ref, wd0_ref), (wg1_ref, wu1_ref, wd1_ref))):
        j = 2 * b + s
        fresh = (b == 0) | (be_ref[j] != be_ref[jnp.maximum(j - 2, 0)])

        @pl.when(used & fresh)
        def _():
            wg_bf[s] = wg[...].astype(BF16)
            wu_bf[s] = wu[...].astype(BF16)
            wd_bf[s] = wd[...].astype(BF16)

    @pl.when(used)
    def _():
        for s in range(2):
            rows = slice(rb * s, rb * (s + 1))
            x = _unpack_bf16_pairs(_load_row_slabs(xs_ref, rows))
            hg = jnp.dot(x, wg_bf[s], preferred_element_type=F32)
            hu = jnp.dot(x, wu_bf[s], preferred_element_type=F32)
            y = jnp.dot((_silu(hg) * hu).astype(BF16), wd_bf[s], preferred_element_type=F32)
            ys_ref[rows, :] = _pack_bf16_pairs(y)


def moe_experts(xs, block_expert, n_used, wg, wu, wd, layer):
    n_slots = xs.shape[0]
    rb = EXPERT_BLOCK
    ff = wg.shape[3]
    rows_in = lambda b, be, nu: (jnp.minimum(b, (nu[0] - 1) // 2), 0, 0)
    rows_out = lambda b, be, nu: (jnp.minimum(b, (nu[0] - 1) // 2), 0)
    w_in = lambda s: (lambda b, be, nu: (layer, be[2 * b + s], 0, 0))
    return pl.pallas_call(
        _expert_kernel,
        out_shape=jax.ShapeDtypeStruct((n_slots, D_MODEL // 2), U32),
        grid_spec=pltpu.PrefetchScalarGridSpec(
            num_scalar_prefetch=2,
            grid=(n_slots // (2 * rb),),
            in_specs=[pl.BlockSpec((2 * rb,) + xs.shape[1:], rows_in)]
            + [pl.BlockSpec((None, None, D_MODEL, ff), w_in(0)), pl.BlockSpec((None, None, D_MODEL, ff), w_in(0)),
               pl.BlockSpec((None, None, ff, D_MODEL), w_in(0)),
               pl.BlockSpec((None, None, D_MODEL, ff), w_in(1)), pl.BlockSpec((None, None, D_MODEL, ff), w_in(1)),
               pl.BlockSpec((None, None, ff, D_MODEL), w_in(1))],
            out_specs=pl.BlockSpec((2 * rb, D_MODEL // 2), rows_out),
            scratch_shapes=[pltpu.VMEM((2, D_MODEL, ff), BF16), pltpu.VMEM((2, D_MODEL, ff), BF16),
                            pltpu.VMEM((2, ff, D_MODEL), BF16)],
        ),
        compiler_params=_params(("arbitrary",)),
        name="moe_experts",
    )(block_expert, n_used, xs, wg, wu, wd, wg, wu, wd)


def _combine_kernel(npt, final, h_ref, swg_ref, swu_ref, swd_ref, x1_ref, g2s_ref, g2t_ref, w_ref, fg_ref,
                    dest_hbm, ys_hbm, *rest):
    out_refs, (idx_a, idx_b, ybuf_ref, sem_idx, sem_rows) = rest[:-5], rest[-5:]
    i = pl.program_id(0)
    n = pl.num_programs(0)
    tg = h_ref.shape[0]
    idx_bufs = (idx_a, idx_b)

    def idx_copy(tile, p):
        return pltpu.make_async_copy(dest_hbm.at[tile], idx_bufs[p], sem_idx.at[p])

    def gather_rows(p, wait):
        idx = idx_bufs[p]

        def body(t, carry):
            for kk in range(TOP_K):
                src = 0 if wait else idx[kk * tg + t]
                dst = ybuf_ref.at[p, kk]
                cp = pltpu.make_async_copy(ys_hbm.at[pl.ds(src, 1)], dst.at[pl.ds(t, 1)], sem_rows.at[p])
                cp.wait() if wait else cp.start(priority=kk % 2)
            return carry

        lax.fori_loop(0, tg, body, 0, unroll=ROW_COPY_UNROLL)

    @pl.when(i == 0)
    def _():
        idx_copy(0, 0).start()
        idx_copy(0, 0).wait()
        gather_rows(0, wait=False)

        @pl.when(n > 1)
        def _():
            idx_copy(1, 1).start()

    def prefetch(p):
        @pl.when(i + 1 < n)
        def _():
            idx_copy(i + 1, 1 - p).wait()
            gather_rows(1 - p, wait=False)

        @pl.when(i + 2 < n)
        def _():
            idx_copy(i + 2, p).start()

    _by_parity(i, prefetch)

    hb = _unpack_bf16_pairs(h_ref[...])
    hid = _silu(jnp.dot(hb, swg_ref[...], preferred_element_type=F32)) * jnp.dot(hb, swu_ref[...],
                                                                                 preferred_element_type=F32)
    acc = jnp.dot(hid.astype(BF16), swd_ref[...], preferred_element_type=F32)
    w = w_ref[...]
    gate = _pick(i >= npt, g2s_ref, g2t_ref)

    def finish(p):
        gather_rows(p, wait=True)
        half = D_MODEL // 2
        lo = jnp.zeros((tg, half), F32)
        hi = jnp.zeros((tg, half), F32)
        for kk in range(TOP_K):
            u = ybuf_ref[p, kk]
            wk = w[:, kk:kk + 1]
            lo = lo + lax.bitcast_convert_type(u << 16, F32) * wk
            hi = hi + lax.bitcast_convert_type(u & jnp.uint32(0xFFFF0000), F32) * wk
        routed = jnp.concatenate([lo, hi], axis=1)
        x2 = x1_ref[...] + gate * (routed + acc)
        if final:
            ms = jnp.mean(x2 * x2, axis=-1, keepdims=True)
            x2 = x2 * lax.rsqrt(ms + NORM_EPS) * fg_ref[...]

        @pl.when(i < npt)
        def _():
            out_refs[0][...] = x2

        @pl.when(i >= npt)
        def _():
            out_refs[1][...] = x2

    _by_parity(i, finish)


def moe_combine(h, swg_bf, swu_bf, swd_bf, x1, mod, w_tok, final_g, dest_tiles, ys, final):
    t = h.shape[0]
    tg = mod.tile
    npt = mod.npt
    row = lambda i: (i, 0)
    const = lambda i: (0, 0)
    out_shape = (jax.ShapeDtypeStruct((npt * tg, D_MODEL), F32), jax.ShapeDtypeStruct((t - npt * tg, D_MODEL), F32))
    out_specs = tuple(_stream_specs(None, tg, npt))
    return pl.pallas_call(
        functools.partial(_combine_kernel, mod.npt, final),
        out_shape=out_shape,
        grid=(t // tg,),
        in_specs=[pl.BlockSpec((tg, h.shape[1]), row), pl.BlockSpec(swg_bf.shape, const),
                  pl.BlockSpec(swu_bf.shape, const),
                  pl.BlockSpec(swd_bf.shape, const), pl.BlockSpec((tg, D_MODEL), row)]
        + mod.specs(5)
        + [pl.BlockSpec((tg, TOP_K), row), pl.BlockSpec((1, D_MODEL), const),
           pl.BlockSpec(memory_space=pl.ANY), pl.BlockSpec(memory_space=pl.ANY)],
        out_specs=out_specs,
        scratch_shapes=[pltpu.SMEM((tg * TOP_K,), I32), pltpu.SMEM((tg * TOP_K,), I32),
                        pltpu.VMEM((2, TOP_K, tg, D_MODEL // 2), U32),
                        pltpu.SemaphoreType.DMA((2,)), pltpu.SemaphoreType.DMA((2,))],
        compiler_params=_params(("arbitrary",)),
        name="moe_combine",
    )(h, swg_bf, swu_bf, swd_bf, x1, *mod.operands(5), w_tok, final_g.reshape(1, -1), dest_tiles, ys)


def _slot_kernel(eidx_ref, rank_ref, start_ref, o_ref):
    tm = eidx_ref.shape[1]
    tr = o_ref.shape[2]
    eid = lax.broadcasted_iota(I32, (N_EXPERTS, tm), 0)
    start = start_ref[...]
    for kk in range(TOP_K):
        base = jnp.sum(jnp.where(eid == eidx_ref[kk:kk + 1, :], start, 0.0), axis=0, keepdims=True)
        slot = base.astype(I32) + rank_ref[kk:kk + 1, :]
        for j in range(tm // tr):
            o_ref[j, kk:kk + 1, :] = slot[:, tr * j:tr * (j + 1)]


def assignment_slots(eidx_t, rank_t, pad_start):
    t = eidx_t.shape[1]
    tm = TOKEN_TILE
    tr = ROUTE_TILE
    col = lambda i: (0, i)
    out = pl.pallas_call(
        _slot_kernel,
        out_shape=jax.ShapeDtypeStruct((t // tr, TOP_K, tr), I32),
        grid=(t // tm,),
        in_specs=[pl.BlockSpec((TOP_K, tm), col), pl.BlockSpec((TOP_K, tm), col),
                  pl.BlockSpec((N_EXPERTS, 1), lambda i: (0, 0))],
        out_specs=pl.BlockSpec((tm // tr, TOP_K, tr), lambda i: (i, 0, 0)),
        compiler_params=_params(("parallel",)),
        name="assignment_slots",
    )(eidx_t, rank_t, pad_start.astype(F32).reshape(-1, 1))
    return out.reshape(t // tr, TOP_K * tr)


def _routing_tables(counts, n_blocks):
    rb = EXPERT_BLOCK
    counts = counts.astype(I32)
    padded = (counts + rb - 1) // rb * rb
    pad_end = jnp.cumsum(padded)
    pad_start = pad_end - padded
    n_used = pad_end[-1] // rb
    blocks = jnp.arange(n_blocks, dtype=I32)
    block_expert = jnp.sum((pad_end[None, :] <= (blocks * rb)[:, None]).astype(I32), axis=1)
    last_used = jnp.sum((pad_end <= (n_used - 1) * rb).astype(I32))
    block_expert = jnp.minimum(jnp.where(blocks < n_used, block_expert, last_used), N_EXPERTS - 1)
    zero_blocks = jnp.where(counts % rb != 0, pad_end // rb - 1, -1)
    zero_blocks = jnp.concatenate([zero_blocks, jnp.where(n_used % 2 == 1, n_used, -1).reshape(1)]).astype(I32)
    return pad_start, block_expert, n_used.reshape(1).astype(I32), zero_blocks


def moe_layer(h, x1, eidx_t, w_t, rank_t, counts, mod_route, wg, wu, wd, layer, swg_bf, swu_bf, swd_bf,
              final_g, final):
    t = h.shape[0]
    rb = EXPERT_BLOCK
    tr = ROUTE_TILE
    n_blocks = -(-(t * TOP_K) // rb) + N_EXPERTS
    n_blocks += n_blocks % 2
    pad_start, block_expert, n_used, zero_blocks = _routing_tables(counts, n_blocks)
    dest_tiles = assignment_slots(eidx_t, rank_t, pad_start)
    xs = moe_dispatch(h, dest_tiles, zero_blocks, n_blocks * rb)
    ys = moe_experts(xs, block_expert, n_used, wg, wu, wd, layer)
    return moe_combine(h, swg_bf, swu_bf, swd_bf, x1, mod_route, w_t.T, final_g, dest_tiles, ys, final)


def _rope_tables(n_batch, seq_len, n_seq, n_new):
    half = SWA_HEAD_DIM // 2
    inv = ROPE_THETA ** (-jnp.arange(half, dtype=F32) / half)
    pos = jnp.concatenate([jnp.tile(jnp.arange(seq_len, dtype=F32), n_batch),
                           jnp.tile(PAST_LEN + jnp.arange(n_new, dtype=F32), n_seq)])
    ang = pos[:, None] * inv[None, :]
    cos = jnp.tile(jnp.cos(ang), (1, 128 // half))
    sin = jnp.sin(ang)
    sin = jnp.tile(jnp.concatenate([-sin, sin], axis=1), (1, 128 // SWA_HEAD_DIM))
    return cos, sin


def kernel(x_prompt, x_sample, c_prompt, c_sample, cache_swa_k, cache_swa_v, state_gla, norm_mix_g, norm_ffn_g,
           final_g, ada_w, ada_b, swa_wqkv, swa_sinks, swa_wo, gla_win, gla_wa1, gla_wa2, gla_ba, gla_norm_g,
           gla_wo, moe_router, moe_bias, moe_wg, moe_wu, moe_wd, shared_wg, shared_wu, shared_wd):
    n_batch, seq_len, d = x_prompt.shape
    n_seq, n_new, _ = x_sample.shape
    depth = ada_w.shape[0]
    tp = n_batch * seq_len
    ts = n_seq * n_new
    t = tp + ts
    tm = TOKEN_TILE
    tr = ROUTE_TILE

    x = (x_prompt.reshape(tp, d), x_sample.reshape(ts, d))
    c_all = jnp.concatenate([jnp.repeat(c_sample, n_new, axis=0), c_prompt], axis=0)
    mod = ada_modulation(c_all, ada_w, ada_b)
    cos_tab, sin_tab = _rope_tables(n_batch, seq_len, n_seq, n_new)
    tri = jnp.triu(jnp.ones((tm, tm), BF16), k=1)

    new_k, new_v, new_s = [], [], []
    new_k_s, new_v_s, new_s_s = [], [], []
    for layer in range(depth):
        mod_tok = _Mod(mod, layer, n_batch, seq_len, tm)
        mod_route = _Mod(mod, layer, n_batch, seq_len, tr)
        m = layer // 2
        if layer % 2 == 0:
            q, k, v, k_dup, v_dup = swa_qkv(x, norm_mix_g[layer], mod_tok, swa_wqkv[m].astype(BF16), cos_tab, sin_tab)
            a_p = swa_prompt_attention(q, k_dup, v_dup, swa_sinks[m], n_batch, seq_len)
            nk = SWA_KV_HEADS * SWA_HEAD_DIM
            a_s, ck, cv = swa_sample_attention(q, k, v, cache_swa_k[m].reshape(n_seq, WINDOW, nk),
                                             cache_swa_v[m].reshape(n_seq, WINDOW, nk), swa_sinks[m], tp, n_new)
            kv_shape = (n_batch, WINDOW, SWA_KV_HEADS, SWA_HEAD_DIM)
            tails = [slice((b + 1) * seq_len - WINDOW, (b + 1) * seq_len) for b in range(n_batch)]
            new_k.append(jnp.stack([k[rows] for rows in tails]).reshape(kv_shape))
            new_v.append(jnp.stack([v[rows] for rows in tails]).reshape(kv_shape))
            new_k_s.append(ck.reshape(n_seq, WINDOW, SWA_KV_HEADS, SWA_HEAD_DIM))
            new_v_s.append(cv.reshape(n_seq, WINDOW, SWA_KV_HEADS, SWA_HEAD_DIM))
            wo = swa_wo[m]
        else:
            pad = jnp.zeros((d, 128 - GLA_GATE_RANK), F32)
            win_ext = jnp.concatenate([gla_win[m], gla_wa1[m], pad], axis=1).astype(BF16)
            wa2_pad = jnp.concatenate([gla_wa2[m], jnp.zeros((128 - GLA_GATE_RANK, GLA_KEY_DIM), F32)],
                                      axis=0).astype(BF16)
            q, k, v, r, gate = gla_project(x, norm_mix_g[layer], mod_tok, win_ext, wa2_pad, gla_ba[m])
            a_p, s_prompt = gla_prompt(q, k, gate, v, r, gla_norm_g[m], n_batch, seq_len)
            a_s, s_sample = gla_sample(q, k, gate, v, r, gla_norm_g[m], state_gla[m], tp, n_new)
            new_s.append(s_prompt)
            new_s_s.append(s_sample)
            wo = gla_wo[m]
        x1, h, eidx_t, w_t, rank_t, cnt = post_mixer(a_p, a_s, wo.astype(BF16), x, mod_tok, norm_ffn_g[layer],
                                                     moe_router[layer].T, moe_bias[layer], tri)
        x = moe_layer(h, x1, eidx_t, w_t, rank_t, cnt[:, 0], mod_route,
                      moe_wg, moe_wu, moe_wd, layer,
                      shared_wg[layer].astype(BF16), shared_wu[layer].astype(BF16), shared_wd[layer].astype(BF16),
                      final_g, layer == depth - 1)

    y_prompt = x[0].reshape(n_batch, seq_len, d)
    y_sample = x[1].reshape(n_seq, n_new, d)
    return (y_prompt, y_sample, jnp.stack(new_k), jnp.stack(new_v), jnp.stack(new_k_s), jnp.stack(new_v_s),
            jnp.stack(new_s), jnp.stack(new_s_s))
```

```python
import functools

import jax
import jax.numpy as jnp
from jax import lax
from jax.experimental import pallas as pl
from jax.experimental.pallas import tpu as pltpu

F32 = jnp.float32
BF16 = jnp.bfloat16
I32 = jnp.int32
U32 = jnp.uint32

D_MODEL = 1024
PAST_LEN = 8192
SWA_HEAD_DIM = 64
SWA_HEADS = 16
SWA_KV_HEADS = 4
SWA_GROUP = 4
WINDOW = 128
ROPE_THETA = 10000.0
GLA_HEADS = 4
GLA_DK = 128
GLA_DV = 256
GLA_KEY_DIM = 512
GLA_VAL_DIM = 1024
GLA_GATE_RANK = 16
GLA_GATE_NORMALIZER = 16.0
GLA_CHUNK = 64
N_EXPERTS = 64
TOP_K = 8
N_GROUPS = 8
TOPK_GROUPS = 4
EXPERT_FF = 256
ROUTED_SCALE = 2.5
NORM_EPS = 1e-6

TOKEN_TILE = 512
ROUTE_TILE = 512
EXPERT_BLOCK = 256
ADA_TILE = 512
SAMPLE_SEQS = 8
GLA_PROMPT_BATCH = 8
ROW_COPY_UNROLL = 32
SWA_PROMPT_BATCH = 4
VMEM_LIMIT = 48 * 1024 * 1024

NT_DIMS = (((1,), (1,)), ((), ()))
TN_DIMS = (((0,), (0,)), ((), ()))


def _params(semantics):
    return pltpu.CompilerParams(dimension_semantics=semantics, vmem_limit_bytes=VMEM_LIMIT)


def _silu(x):
    return x * jax.nn.sigmoid(x)


def _norm_mod(x, g, sc, sh):
    ms = jnp.mean(x * x, axis=-1, keepdims=True)
    return (x * lax.rsqrt(ms + NORM_EPS) * g) * (1.0 + sc) + sh


def _pack_bf16_pairs(x):
    half = x.shape[1] // 2
    xb = x.astype(BF16).astype(F32)
    lo = lax.bitcast_convert_type(xb[:, :half], U32) >> 16
    hi = lax.bitcast_convert_type(xb[:, half:], U32) & jnp.uint32(0xFFFF0000)
    return lo | hi


def _unpack_bf16_pairs(u):
    lo = lax.bitcast_convert_type(u << 16, F32)
    hi = lax.bitcast_convert_type(u & jnp.uint32(0xFFFF0000), F32)
    return jnp.concatenate([lo, hi], axis=1).astype(BF16)


ROW_CHUNKS = D_MODEL // 2 // 128


def _store_row_slabs(ref, rows, words):
    for c in range(ROW_CHUNKS):
        ref[rows, c, :] = words[:, 128 * c:128 * (c + 1)]


def _load_row_slabs(ref, rows):
    return jnp.concatenate([ref[rows, c, :] for c in range(ROW_CHUNKS)], axis=1)


def _split3(x):
    x1 = x.astype(BF16)
    r1 = x - x1.astype(F32)
    x2 = r1.astype(BF16)
    x3 = (r1 - x2.astype(F32)).astype(BF16)
    return x1, x2, x3


def _ada_kernel(c_ref, w_ref, b_ref, o_ref):
    s = _silu(c_ref[...]).astype(BF16)
    o_ref[...] = jnp.dot(s, w_ref[...].astype(BF16), preferred_element_type=F32) + b_ref[...]


def ada_modulation(c_all, ada_w, ada_b):
    depth, d, n = ada_w.shape
    rows = c_all.shape[0]
    return pl.pallas_call(
        _ada_kernel,
        out_shape=jax.ShapeDtypeStruct((depth, rows, n), F32),
        grid=(depth, n // ADA_TILE),
        in_specs=[
            pl.BlockSpec((rows, d), lambda l, j: (0, 0)),
            pl.BlockSpec((None, d, ADA_TILE), lambda l, j: (l, 0, j)),
            pl.BlockSpec((None, 1, ADA_TILE), lambda l, j: (l, 0, j)),
        ],
        out_specs=pl.BlockSpec((None, rows, ADA_TILE), lambda l, j: (l, 0, j)),
        compiler_params=_params(("parallel", "parallel")),
        name="ada_modulation",
    )(c_all, ada_w, ada_b.reshape(depth, 1, n))


class _Mod:
    def __init__(self, mod, layer, n_batch, seq_len, tile):
        depth, rows, n = mod.shape
        self.tile = tile
        self.layer = layer
        self.npt = n_batch * seq_len // tile
        self.mod_tok = mod
        self.mod_seq = mod[:, rows - n_batch:].reshape(depth, n_batch, 1, n)
        self.tiles_per_seq = seq_len // tile

    def operands(self, chunk):
        del chunk
        return [self.mod_seq, self.mod_tok]

    def specs(self, chunk):
        l, npt, tps = self.layer, self.npt, self.tiles_per_seq
        n_seq = self.mod_seq.shape[1]
        seq_spec = pl.BlockSpec((None, None, 1, D_MODEL),
                                lambda i, *_: (l, jnp.minimum(i // tps, n_seq - 1), 0, chunk))
        tok_spec = pl.BlockSpec((None, self.tile, D_MODEL),
                                lambda i, *_: (l, jnp.maximum(i - npt, 0), chunk))
        return [seq_spec, tok_spec]


def _pick(is_sample, seq_ref, tok_ref):
    return jnp.where(is_sample, tok_ref[...], seq_ref[...])


def _stream_specs(x_pair, tile, npt):
    del x_pair
    return [pl.BlockSpec((tile, D_MODEL), lambda i, *_: (jnp.minimum(i, npt - 1), 0)),
            pl.BlockSpec((tile, D_MODEL), lambda i, *_: (jnp.maximum(i - npt, 0), 0))]


def _swa_qkv_kernel(npt, xp_ref, xs_ref, g_ref, scs_ref, sct_ref, shs_ref, sht_ref, w_ref, cos_ref, sin_ref,
                    q_ref, k_ref, v_ref, kd_ref, vd_ref):
    is_s = pl.program_id(0) >= npt
    h = _norm_mod(_pick(is_s, xp_ref, xs_ref), g_ref[...], _pick(is_s, scs_ref, sct_ref), _pick(is_s, shs_ref, sht_ref))
    qkv = jnp.dot(h.astype(BF16), w_ref[...], preferred_element_type=F32)
    cos = cos_ref[...]
    sin = sin_ref[...]
    lane = lax.broadcasted_iota(I32, cos.shape, 1)
    first_half = (lane % SWA_HEAD_DIM) < (SWA_HEAD_DIM // 2)

    def rope(xc):
        rot = jnp.where(first_half, pltpu.roll(xc, 128 - SWA_HEAD_DIM // 2, 1), pltpu.roll(xc, SWA_HEAD_DIM // 2, 1))
        return xc * cos + rot * sin

    nq = SWA_HEADS * SWA_HEAD_DIM
    nk = SWA_KV_HEADS * SWA_HEAD_DIM
    for c in range(nq // 128):
        q_ref[:, 128 * c:128 * (c + 1)] = (rope(qkv[:, 128 * c:128 * (c + 1)]) * (SWA_HEAD_DIM ** -0.5)).astype(BF16)
    low = lane < SWA_HEAD_DIM

    def spread(chunk):
        rolled = pltpu.roll(chunk, SWA_HEAD_DIM, 1)
        return jnp.where(low, chunk, rolled).astype(BF16), jnp.where(low, rolled, chunk).astype(BF16)

    for c in range(nk // 128):
        kc = rope(qkv[:, nq + 128 * c:nq + 128 * (c + 1)])
        vc = qkv[:, nq + nk + 128 * c:nq + nk + 128 * (c + 1)]
        k_ref[:, 128 * c:128 * (c + 1)] = kc
        v_ref[:, 128 * c:128 * (c + 1)] = vc
        kd_ref[:, 256 * c:256 * c + 128], kd_ref[:, 256 * c + 128:256 * (c + 1)] = spread(kc)
        vd_ref[:, 256 * c:256 * c + 128], vd_ref[:, 256 * c + 128:256 * (c + 1)] = spread(vc)


def swa_qkv(x, g, mod, w_bf, cos_tab, sin_tab):
    t = x[0].shape[0] + x[1].shape[0]
    tm = mod.tile
    nq = SWA_HEADS * SWA_HEAD_DIM
    nk = SWA_KV_HEADS * SWA_HEAD_DIM
    row = lambda i: (i, 0)
    return pl.pallas_call(
        functools.partial(_swa_qkv_kernel, mod.npt),
        out_shape=(jax.ShapeDtypeStruct((t, nq), BF16), jax.ShapeDtypeStruct((t, nk), F32),
                   jax.ShapeDtypeStruct((t, nk), F32), jax.ShapeDtypeStruct((t, 2 * nk), BF16),
                   jax.ShapeDtypeStruct((t, 2 * nk), BF16)),
        grid=(t // tm,),
        in_specs=_stream_specs(x, tm, mod.npt) + [pl.BlockSpec((1, D_MODEL), lambda i: (0, 0))]
        + mod.specs(1) + mod.specs(0)
        + [pl.BlockSpec(w_bf.shape, lambda i: (0, 0)), pl.BlockSpec((tm, 128), row), pl.BlockSpec((tm, 128), row)],
        out_specs=(pl.BlockSpec((tm, nq), row), pl.BlockSpec((tm, nk), row), pl.BlockSpec((tm, nk), row),
                   pl.BlockSpec((tm, 2 * nk), row), pl.BlockSpec((tm, 2 * nk), row)),
        compiler_params=_params(("parallel",)),
        name="swa_qkv",
    )(*x, g.reshape(1, -1), *mod.operands(1), *mod.operands(0), w_bf, cos_tab, sin_tab)


def _sink_softmax(s, sink_col):
    m = jnp.maximum(jnp.max(s, axis=-1, keepdims=True), sink_col)
    e = jnp.exp(s - m)
    den = jnp.sum(e, axis=-1, keepdims=True) + jnp.exp(sink_col - m)
    return e * (1.0 / den)


def _pair_attention(sink_ref, q_ref, kblks, vblks, mask, o_ref):
    rows = q_ref.shape[0]
    half = mask.shape[1]
    per_group = SWA_GROUP // 2
    scores = []
    for g in range(SWA_KV_HEADS):
        q2 = jnp.concatenate([q_ref[:, 128 * c:128 * (c + 1)] for c in range(per_group * g, per_group * (g + 1))],
                             axis=0)
        s = lax.dot_general(q2, kblks[g], NT_DIMS, preferred_element_type=F32)
        for j in range(per_group):
            scores += [s[rows * j:rows * (j + 1), :half], s[rows * j:rows * (j + 1), half:]]
    s_all = jnp.where(mask, jnp.stack(scores), -jnp.inf)
    sinks = jnp.stack([jnp.full((1, 1), sink_ref[h], F32) for h in range(SWA_HEADS)])
    p_all = _sink_softmax(s_all, sinks).astype(BF16)
    for g in range(SWA_KV_HEADS):
        chunks = range(per_group * g, per_group * (g + 1))
        p2 = jnp.concatenate([jnp.concatenate([p_all[2 * c], p_all[2 * c + 1]], axis=1) for c in chunks], axis=0)
        o = jnp.dot(p2, vblks[g], preferred_element_type=F32).astype(BF16)
        for j, c in enumerate(chunks):
            o_ref[:, 128 * c:128 * (c + 1)] = o[rows * j:rows * (j + 1)]


def _swa_prompt_kernel(sink_ref, *refs):
    o_ref = refs[-1]
    j = pl.program_id(1)
    blk = refs[0].shape[0]
    qi = lax.broadcasted_iota(I32, (blk, 2 * blk), 0)
    sj = lax.broadcasted_iota(I32, (blk, 2 * blk), 1)
    rel = qi + blk - sj
    mask = (rel >= 0) & (rel <= WINDOW) & ((sj >= blk) | (j > 0))
    low = lax.broadcasted_iota(I32, (2 * blk, 128), 1) < SWA_HEAD_DIM
    zero = jnp.zeros((2 * blk, 128), BF16)
    for i in range(SWA_PROMPT_BATCH):
        q_ref, kc_ref, kp_ref, vc_ref, vp_ref = refs[5 * i:5 * (i + 1)]
        kblks, vblks = [], []
        for g in range(SWA_KV_HEADS):
            cs = slice(128 * g, 128 * (g + 1))
            kcat = jnp.concatenate([kp_ref[:, cs], kc_ref[:, cs]], axis=0)
            vcat = jnp.concatenate([vp_ref[:, cs], vc_ref[:, cs]], axis=0)
            kblks.append(jnp.concatenate([jnp.where(low, kcat, zero), jnp.where(low, zero, kcat)], axis=0))
            vblks.append(jnp.concatenate([jnp.where(low, vcat, zero), jnp.where(low, zero, vcat)], axis=0))
        _pair_attention(sink_ref, q_ref, kblks, vblks, mask, o_ref.at[i])


def swa_prompt_attention(q, k, v, sinks, n_batch, seq_len):
    blk = WINDOW
    nsb = SWA_PROMPT_BATCH
    nb = seq_len // blk
    nq = q.shape[1]
    nk = k.shape[1]
    in_specs, operands = [pl.BlockSpec(memory_space=pltpu.SMEM)], [sinks]
    for i in range(nsb):
        cur = lambda b, j, i=i: ((nsb * b + i) * nb + j, 0)
        prev = lambda b, j, i=i: ((nsb * b + i) * nb + jnp.maximum(j - 1, 0), 0)
        in_specs += [pl.BlockSpec((blk, nq), cur), pl.BlockSpec((blk, nk), cur), pl.BlockSpec((blk, nk), prev),
                     pl.BlockSpec((blk, nk), cur), pl.BlockSpec((blk, nk), prev)]
        operands += [q, k, k, v, v]
    out = pl.pallas_call(
        _swa_prompt_kernel,
        out_shape=jax.ShapeDtypeStruct((n_batch, seq_len, nq), BF16),
        grid=(n_batch // nsb, nb),
        in_specs=in_specs,
        out_specs=pl.BlockSpec((nsb, blk, nq), lambda b, j: (b, j, 0)),
        compiler_params=_params(("parallel", "parallel")),
        name="swa_prompt_attention",
    )(*operands)
    return out.reshape(n_batch * seq_len, nq)


def _swa_sample_kernel(n_new, sink_ref, q_ref, kn_ref, vn_ref, ck_ref, cv_ref, o_ref, nk_ref, nv_ref):
    n_sb, win, _ = ck_ref.shape
    per_seq = win + n_new
    rows = n_sb * n_new
    cols = n_sb * per_seq
    keys, vals = [], []
    for sb in range(n_sb):
        r0 = sb * n_new
        kc = ck_ref[sb]
        vc = cv_ref[sb]
        kn = kn_ref[r0:r0 + n_new, :]
        vn = vn_ref[r0:r0 + n_new, :]
        nk_ref[sb, 0:win - n_new, :] = kc[n_new:]
        nk_ref[sb, win - n_new:win, :] = kn
        nv_ref[sb, 0:win - n_new, :] = vc[n_new:]
        nv_ref[sb, win - n_new:win, :] = vn
        keys += [kc, kn]
        vals += [vc, vn]
    keys = jnp.concatenate(keys, axis=0)
    vals = jnp.concatenate(vals, axis=0)
    ri = lax.broadcasted_iota(I32, (rows, cols), 0)
    ci = lax.broadcasted_iota(I32, (rows, cols), 1)
    ti = ri % n_new
    si = ci % per_seq
    mask = (ri // n_new == ci // per_seq) & (si >= ti) & (si <= ti + WINDOW)
    low = lax.broadcasted_iota(I32, (cols, 128), 1) < SWA_HEAD_DIM
    zero = jnp.zeros((cols, 128), BF16)

    def block_diag(chunk, first):
        rolled = pltpu.roll(chunk, SWA_HEAD_DIM, 1)
        both = (jnp.where(low, chunk, rolled) if first else jnp.where(low, rolled, chunk)).astype(BF16)
        return jnp.concatenate([jnp.where(low, both, zero), jnp.where(low, zero, both)], axis=0)

    chunks = [slice(128 * (g // 2), 128 * (g // 2 + 1)) for g in range(SWA_KV_HEADS)]
    kblks = [block_diag(keys[:, cs], g % 2 == 0) for g, cs in enumerate(chunks)]
    vblks = [block_diag(vals[:, cs], g % 2 == 0) for g, cs in enumerate(chunks)]
    _pair_attention(sink_ref, q_ref, kblks, vblks, mask, o_ref)


def swa_sample_attention(q, k, v, cache_k, cache_v, sinks, n_prompt_rows, n_new):
    n_seq, win, nk = cache_k.shape
    sb = SAMPLE_SEQS
    rows = sb * n_new
    nq = q.shape[1]
    base = n_prompt_rows // rows
    tok = lambda i: (base + i, 0)
    seq = lambda i: (i, 0, 0)
    return pl.pallas_call(
        functools.partial(_swa_sample_kernel, n_new),
        out_shape=(jax.ShapeDtypeStruct((n_seq * n_new, nq), BF16),
                   jax.ShapeDtypeStruct(cache_k.shape, F32), jax.ShapeDtypeStruct(cache_v.shape, F32)),
        grid=(n_seq // sb,),
        in_specs=[pl.BlockSpec(memory_space=pltpu.SMEM),
                  pl.BlockSpec((rows, nq), tok), pl.BlockSpec((rows, nk), tok), pl.BlockSpec((rows, nk), tok),
                  pl.BlockSpec((sb, win, nk), seq), pl.BlockSpec((sb, win, nk), seq)],
        out_specs=(pl.BlockSpec((rows, nq), lambda i: (i, 0)), pl.BlockSpec((sb, win, nk), seq),
                   pl.BlockSpec((sb, win, nk), seq)),
        compiler_params=_params(("parallel",)),
        name="swa_sample_attention",
    )(sinks, q, k, v, cache_k, cache_v)


def _gla_proj_kernel(npt, xp_ref, xs_ref, g_ref, scs_ref, sct_ref, shs_ref, sht_ref, w_ref, wa2_ref, ba_ref,
                     q_ref, k_ref, v_ref, r_ref, gate_ref):
    is_s = pl.program_id(0) >= npt
    h = _norm_mod(_pick(is_s, xp_ref, xs_ref), g_ref[...], _pick(is_s, scs_ref, sct_ref), _pick(is_s, shs_ref, sht_ref))
    proj = jnp.dot(h.astype(BF16), w_ref[...], preferred_element_type=F32)
    kd = GLA_KEY_DIM
    vd = GLA_VAL_DIM
    q_ref[...] = proj[:, :kd] * (GLA_DK ** -0.5)
    k_ref[...] = proj[:, kd:2 * kd]
    v_ref[...] = proj[:, 2 * kd:2 * kd + vd].astype(BF16)
    r_ref[...] = proj[:, 2 * kd + vd:2 * kd + 2 * vd]
    low = proj[:, 2 * kd + 2 * vd:].astype(BF16)
    z = jnp.dot(low, wa2_ref[...], preferred_element_type=F32) + ba_ref[...]
    log_sig = jnp.minimum(z, 0.0) - jnp.log1p(jnp.exp(-jnp.abs(z)))
    gate_ref[...] = log_sig / GLA_GATE_NORMALIZER


def gla_project(x, g, mod, win_ext, wa2_pad, ba):
    t = x[0].shape[0] + x[1].shape[0]
    tm = mod.tile
    kd, vd = GLA_KEY_DIM, GLA_VAL_DIM
    row = lambda i: (i, 0)
    const = lambda i: (0, 0)
    return pl.pallas_call(
        functools.partial(_gla_proj_kernel, mod.npt),
        out_shape=(jax.ShapeDtypeStruct((t, kd), F32), jax.ShapeDtypeStruct((t, kd), F32),
                   jax.ShapeDtypeStruct((t, vd), BF16), jax.ShapeDtypeStruct((t, vd), F32),
                   jax.ShapeDtypeStruct((t, kd), F32)),
        grid=(t // tm,),
        in_specs=_stream_specs(x, tm, mod.npt) + [pl.BlockSpec((1, D_MODEL), const)]
        + mod.specs(1) + mod.specs(0)
        + [pl.BlockSpec(win_ext.shape, const), pl.BlockSpec(wa2_pad.shape, const), pl.BlockSpec((1, kd), const)],
        out_specs=(pl.BlockSpec((tm, kd), row), pl.BlockSpec((tm, kd), row), pl.BlockSpec((tm, vd), row),
                   pl.BlockSpec((tm, vd), row), pl.BlockSpec((tm, kd), row)),
        compiler_params=_params(("parallel",)),
        name="gla_project",
    )(*x, g.reshape(1, -1), *mod.operands(1), *mod.operands(0), win_ext, wa2_pad, ba.reshape(1, -1))


def _cumsum_rows(tri, g):
    n = g.shape[1]
    s = jnp.dot(tri, jnp.concatenate(_split3(g), axis=1), preferred_element_type=F32)
    return s[:, :n] + s[:, n:2 * n] + s[:, 2 * n:]


def _diag_attention(q, k, b, n):
    ng = n // 8
    dk = q.shape[1]
    q3 = q.reshape(ng, 8, dk)
    k3 = k.reshape(ng, 8, dk)
    b3 = b.reshape(ng, 8, dk)
    sub = lax.broadcasted_iota(I32, (ng, 8, dk), 1)
    ti = lax.broadcasted_iota(I32, (n, n), 0)
    si = lax.broadcasted_iota(I32, (n, n), 1)
    attn = jnp.zeros((n, n), F32)
    for j in range(8):
        bj = jnp.broadcast_to(b3[:, j:j + 1, :], b3.shape)
        kj = jnp.broadcast_to(k3[:, j:j + 1, :], k3.shape)
        e = jnp.exp(jnp.minimum(b3 - bj, 0.0))
        m = jnp.where(sub >= j, q3 * e * kj, 0.0)
        col = jnp.sum(m, axis=-1, keepdims=True).reshape(n, 1)
        attn = attn + jnp.where(si == (ti // 8) * 8 + j, col, 0.0)
    return attn


def _cross_attention(q, k, b, n):
    ti = lax.broadcasted_iota(I32, (n, n), 0)
    si = lax.broadcasted_iota(I32, (n, n), 1)
    row = lax.broadcasted_iota(I32, b.shape, 0)
    attn = jnp.zeros((n, n), F32)
    m = n // 2
    while m >= 8:
        nblk = n // m
        refq = jnp.concatenate(
            [jnp.broadcast_to(b[i * m - 1:i * m], (m, b.shape[1])) if i % 2 else b[i * m:(i + 1) * m]
             for i in range(nblk)], axis=0)
        refk = jnp.concatenate(
            [b[i * m:(i + 1) * m] if i % 2 else jnp.broadcast_to(b[(i + 1) * m - 1:(i + 1) * m], (m, b.shape[1]))
             for i in range(nblk)], axis=0)
        odd = ((row // m) % 2) == 1
        qt = jnp.where(odd, q * jnp.exp(jnp.minimum(b - refq, 0.0)), 0.0).astype(BF16)
        kt = jnp.where(odd, 0.0, k * jnp.exp(jnp.minimum(refk - b, 0.0))).astype(BF16)
        a = lax.dot_general(qt, kt, NT_DIMS, preferred_element_type=F32)
        keep = (((ti // m) % 2) == 1) & ((si // m) == (ti // m) - 1)
        attn = attn + jnp.where(keep, a, 0.0)
        m //= 2
    return attn


def _gla_epilogue(o, r, ng):
    ms = jnp.mean(o * o, axis=-1, keepdims=True)
    return (o * lax.rsqrt(ms + NORM_EPS) * ng * _silu(r)).astype(BF16)


def _gla_prompt_kernel(*refs):
    nb = GLA_PROMPT_BATCH
    ins, (ng_ref, o_ref, so_ref, st_ref) = refs[:5 * nb], refs[5 * nb:]
    c = pl.program_id(1)
    n = ins[0].shape[0]

    @pl.when(c == 0)
    def _():
        st_ref[...] = jnp.zeros(st_ref.shape, F32)

    ti = lax.broadcasted_iota(I32, (n, n), 0)
    si = lax.broadcasted_iota(I32, (n, n), 1)
    tri = jnp.where(ti >= si, 1.0, 0.0).astype(BF16)
    for i in range(nb):
        q_ref, k_ref, g_ref, v_ref, r_ref = ins[5 * i:5 * (i + 1)]
        b_all = _cumsum_rows(tri, g_ref[...])
        for h in range(GLA_HEADS):
            ks = slice(GLA_DK * h, GLA_DK * (h + 1))
            vs = slice(GLA_DV * h, GLA_DV * (h + 1))
            q = q_ref[:, ks]
            k = k_ref[:, ks]
            v = v_ref[:, vs]
            b = b_all[:, ks]
            s_t = st_ref[i, h]
            o = lax.dot_general((q * jnp.exp(b)).astype(BF16), s_t.astype(BF16), NT_DIMS,
                                preferred_element_type=F32)
            attn = _cross_attention(q, k, b, n) + _diag_attention(q, k, b, n)
            o = o + jnp.dot(attn.astype(BF16), v, preferred_element_type=F32)
            bl = b[n - 1:n, :]
            kd = (k * jnp.exp(bl - b)).astype(BF16)
            s_new = s_t * jnp.exp(bl) + lax.dot_general(v, kd, TN_DIMS, preferred_element_type=F32)
            st_ref[i, h] = s_new
            o_ref[i, :, vs] = _gla_epilogue(o, r_ref[:, vs], ng_ref[...])

    @pl.when(c == pl.num_programs(1) - 1)
    def _():
        for i in range(nb):
            for h in range(GLA_HEADS):
                so_ref[i, h] = st_ref[i, h].T


def gla_prompt(q, k, g, v, r, norm_g, n_batch, seq_len):
    n = GLA_CHUNK
    nb = GLA_PROMPT_BATCH
    nc = seq_len // n
    kd, vd = GLA_KEY_DIM, GLA_VAL_DIM
    in_specs, operands = [], []
    for i in range(nb):
        row = lambda b, c, i=i: ((nb * b + i) * nc + c, 0)
        in_specs += [pl.BlockSpec((n, kd), row), pl.BlockSpec((n, kd), row), pl.BlockSpec((n, kd), row),
                     pl.BlockSpec((n, vd), row), pl.BlockSpec((n, vd), row)]
        operands += [q, k, g, v, r]
    o, state = pl.pallas_call(
        _gla_prompt_kernel,
        out_shape=(jax.ShapeDtypeStruct((n_batch, seq_len, vd), BF16),
                   jax.ShapeDtypeStruct((n_batch, GLA_HEADS, GLA_DK, GLA_DV), F32)),
        grid=(n_batch // nb, nc),
        in_specs=in_specs + [pl.BlockSpec((1, GLA_DV), lambda b, c: (0, 0))],
        out_specs=(pl.BlockSpec((nb, n, vd), lambda b, c: (b, c, 0)),
                   pl.BlockSpec((nb, GLA_HEADS, GLA_DK, GLA_DV), lambda b, c: (b, 0, 0, 0))),
        scratch_shapes=[pltpu.VMEM((nb, GLA_HEADS, GLA_DV, GLA_DK), F32)],
        compiler_params=_params(("parallel", "arbitrary")),
        name="gla_prompt",
    )(*operands, norm_g.reshape(1, -1))
    return o.reshape(n_batch * seq_len, vd), state


def _gla_sample_kernel(n_new, q_ref, k_ref, g_ref, v_ref, r_ref, ng_ref, si_ref, o_ref, so_ref):
    n = q_ref.shape[0]
    ti = lax.broadcasted_iota(I32, (n, n), 0)
    si = lax.broadcasted_iota(I32, (n, n), 1)
    tri = jnp.where((ti >= si) & (ti // n_new == si // n_new), 1.0, 0.0).astype(BF16)
    b_all = _cumsum_rows(tri, g_ref[...])
    for h in range(GLA_HEADS):
        ks = slice(GLA_DK * h, GLA_DK * (h + 1))
        vs = slice(GLA_DV * h, GLA_DV * (h + 1))
        q = q_ref[:, ks]
        k = k_ref[:, ks]
        v = v_ref[:, vs]
        b = b_all[:, ks]
        attn = _diag_attention(q, k, b, n)
        o_intra = jnp.dot(attn.astype(BF16), v, preferred_element_type=F32)
        qe = (q * jnp.exp(b)).astype(BF16)
        n_sb = n // n_new
        last = [b[n_new * (sb + 1) - 1:n_new * (sb + 1), :] for sb in range(n_sb)]
        bl_rows = jnp.concatenate([jnp.broadcast_to(bl, (n_new, GLA_DK)) for bl in last], axis=0)
        kd = (k * jnp.exp(bl_rows - b)).astype(BF16)
        seq_of_vrow = lax.broadcasted_iota(I32, (n, GLA_DV), 0) // n_new
        seq_of_krow = lax.broadcasted_iota(I32, (n, GLA_DK), 0) // n_new
        o = o_intra
        for sb in range(n_sb):
            s0 = si_ref[sb, h]
            o_sb = jnp.dot(qe, s0.astype(BF16), preferred_element_type=F32)
            o = o + jnp.where(seq_of_vrow == sb, o_sb, 0.0)
            kd_sb = jnp.where(seq_of_krow == sb, kd, jnp.zeros_like(kd))
            upd = lax.dot_general(kd_sb, v, TN_DIMS, preferred_element_type=F32)
            decay_col = jnp.transpose(jnp.broadcast_to(jnp.exp(last[sb]), (8, GLA_DK)))[:, 0:1]
            so_ref[sb, h] = s0 * decay_col + upd
        o_ref[:, vs] = _gla_epilogue(o, r_ref[:, vs], ng_ref[...])


def gla_sample(q, k, g, v, r, norm_g, state, n_prompt_rows, n_new):
    n_seq = state.shape[0]
    sb = SAMPLE_SEQS
    rows = sb * n_new
    kd, vd = GLA_KEY_DIM, GLA_VAL_DIM
    base = n_prompt_rows // rows
    tok = lambda i: (base + i, 0)
    seq = lambda i: (i, 0, 0, 0)
    sblock = (sb, GLA_HEADS, GLA_DK, GLA_DV)
    return pl.pallas_call(
        functools.partial(_gla_sample_kernel, n_new),
        out_shape=(jax.ShapeDtypeStruct((n_seq * n_new, vd), BF16), jax.ShapeDtypeStruct(state.shape, F32)),
        grid=(n_seq // sb,),
        in_specs=[pl.BlockSpec((rows, kd), tok), pl.BlockSpec((rows, kd), tok), pl.BlockSpec((rows, kd), tok),
                  pl.BlockSpec((rows, vd), tok), pl.BlockSpec((rows, vd), tok),
                  pl.BlockSpec((1, GLA_DV), lambda i: (0, 0)),
                  pl.BlockSpec(sblock, seq)],
        out_specs=(pl.BlockSpec((rows, vd), lambda i: (i, 0)), pl.BlockSpec(sblock, seq)),
        compiler_params=_params(("parallel",)),
        name="gla_sample",
    )(q, k, g, v, r, norm_g.reshape(1, -1), state)


def _post_mixer_kernel(npt, ap_ref, as_ref, wo_ref, xp_ref, xs_ref, g1s_ref, g1t_ref, gf_ref, scs_ref, sct_ref, shs_ref, sht_ref,
                       rw_ref, rb_ref, tri_ref,
                       x1_ref, h_ref, eidx_ref, w_ref, rank_ref, cnt_ref, carry_ref):
    i = pl.program_id(0)
    is_s = i >= npt

    @pl.when(i == 0)
    def _():
        carry_ref[...] = jnp.zeros(carry_ref.shape, F32)

    a = jnp.where(is_s, as_ref[...], ap_ref[...])
    x1 = _pick(is_s, xp_ref, xs_ref) + _pick(is_s, g1s_ref, g1t_ref) * jnp.dot(a, wo_ref[...],
                                                                               preferred_element_type=F32)
    x1_ref[...] = x1
    h = _norm_mod(x1, gf_ref[...], _pick(is_s, scs_ref, sct_ref), _pick(is_s, shs_ref, sht_ref))
    h_ref[...] = _pack_bf16_pairs(h)

    h1, h2, _ = _split3(h)
    r1, r2, _ = _split3(rw_ref[...])
    logits = (lax.dot_general(r1, h1, NT_DIMS, preferred_element_type=F32)
              + lax.dot_general(r1, h2, NT_DIMS, preferred_element_type=F32)
              + lax.dot_general(r2, h1, NT_DIMS, preferred_element_type=F32))
    scores = jax.nn.sigmoid(logits)
    sel = scores + rb_ref[...]
    tm = sel.shape[1]
    gsz = N_EXPERTS // N_GROUPS

    sub = lax.broadcasted_iota(I32, (gsz, tm), 0)
    blocks, gscore = [], []
    for g in range(N_GROUPS):
        blk = sel[gsz * g:gsz * (g + 1)]
        m1 = jnp.max(blk, axis=0, keepdims=True)
        first = jnp.min(jnp.where(blk == m1, sub, gsz), axis=0, keepdims=True)
        m2 = jnp.max(jnp.where(sub == first, -jnp.inf, blk), axis=0, keepdims=True)
        blocks.append(blk)
        gscore.append(m1 + m2)
    masked = []
    for g in range(N_GROUPS):
        beaten = jnp.zeros((1, tm), I32)
        for o in range(N_GROUPS):
            if o == g:
                continue
            wins = (gscore[o] > gscore[g]) | ((gscore[o] == gscore[g]) & (o < g))
            beaten = beaten + wins.astype(I32)
        masked.append(jnp.where(beaten < TOPK_GROUPS, blocks[g], -jnp.inf))
    cur = jnp.concatenate(masked, axis=0)

    eid = lax.broadcasted_iota(I32, (N_EXPERTS, tm), 0)
    picked, weights = [], []
    onehot = jnp.zeros((N_EXPERTS, tm), F32)
    for _ in range(TOP_K):
        m = jnp.max(cur, axis=0, keepdims=True)
        idx = jnp.min(jnp.where(cur == m, eid, N_EXPERTS), axis=0, keepdims=True)
        hit = eid == idx
        picked.append(idx)
        weights.append(jnp.sum(jnp.where(hit, scores, 0.0), axis=0, keepdims=True))
        onehot = jnp.where(hit, 1.0, onehot)
        cur = jnp.where(hit, -jnp.inf, cur)
    wsum = weights[0]
    for wk in weights[1:]:
        wsum = wsum + wk
    scale = ROUTED_SCALE / wsum

    before = jnp.dot(onehot.astype(BF16), tri_ref[...], preferred_element_type=F32) + carry_ref[...]
    carry = carry_ref[...] + jnp.sum(onehot, axis=1, keepdims=True)
    carry_ref[...] = carry
    cnt_ref[...] = jnp.broadcast_to(carry, cnt_ref.shape)
    for kk in range(TOP_K):
        eidx_ref[kk:kk + 1, :] = picked[kk]
        w_ref[kk:kk + 1, :] = weights[kk] * scale
        rank_ref[kk:kk + 1, :] = jnp.sum(jnp.where(eid == picked[kk], before, 0.0), axis=0, keepdims=True).astype(I32)


def post_mixer(a_prompt, a_sample, wo_bf, x, mod, gffn, router_t, router_b, tri):
    t = x[0].shape[0] + x[1].shape[0]
    tm = mod.tile
    npt = mod.npt
    row = lambda i: (i, 0)
    col = lambda i: (0, i)
    const = lambda i: (0, 0)
    return pl.pallas_call(
        functools.partial(_post_mixer_kernel, mod.npt),
        out_shape=(jax.ShapeDtypeStruct((t, D_MODEL), F32), jax.ShapeDtypeStruct((t, D_MODEL // 2), U32),
                   jax.ShapeDtypeStruct((TOP_K, t), I32), jax.ShapeDtypeStruct((TOP_K, t), F32),
                   jax.ShapeDtypeStruct((TOP_K, t), I32), jax.ShapeDtypeStruct((N_EXPERTS, 128), F32)),
        grid=(t // tm,),
        in_specs=[pl.BlockSpec((tm, D_MODEL), lambda i: (jnp.minimum(i, npt - 1), 0)),
                  pl.BlockSpec((tm, D_MODEL), lambda i: (jnp.maximum(i - npt, 0), 0)),
                  pl.BlockSpec((D_MODEL, D_MODEL), const)] + _stream_specs(x, tm, npt)
        + mod.specs(2) + [pl.BlockSpec((1, D_MODEL), const)] + mod.specs(4) + mod.specs(3)
        + [pl.BlockSpec((N_EXPERTS, D_MODEL), const), pl.BlockSpec((N_EXPERTS, 1), const),
           pl.BlockSpec((tm, tm), const)],
        out_specs=(pl.BlockSpec((tm, D_MODEL), row), pl.BlockSpec((tm, D_MODEL // 2), row),
                   pl.BlockSpec((TOP_K, tm), col), pl.BlockSpec((TOP_K, tm), col), pl.BlockSpec((TOP_K, tm), col),
                   pl.BlockSpec((N_EXPERTS, 128), const)),
        scratch_shapes=[pltpu.VMEM((N_EXPERTS, 1), F32)],
        compiler_params=_params(("arbitrary",)),
        name="post_mixer",
    )(a_prompt, a_sample, wo_bf, *x, *mod.operands(2), gffn.reshape(1, -1), *mod.operands(4), *mod.operands(3),
      router_t, router_b.reshape(-1, 1), tri)


def _row_copy(src, src_row, dst, dst_row, sem):
    return pltpu.make_async_copy(src.at[src_row], dst.at[dst_row], sem)


def _by_parity(i, fn):
    @pl.when(i % 2 == 0)
    def _():
        fn(0)

    @pl.when(i % 2 == 1)
    def _():
        fn(1)


def _dispatch_kernel(zb_ref, h_ref, dest_hbm, xs_hbm, idx_a, idx_b, zero_ref, slab_ref, sem_idx, sem_zero, sem_rows):
    i = pl.program_id(0)
    te = h_ref.shape[0]
    _store_row_slabs(slab_ref, slice(None), h_ref[...])
    idx_bufs = (idx_a, idx_b)

    def idx_copy(tile, p):
        return pltpu.make_async_copy(dest_hbm.at[tile], idx_bufs[p], sem_idx.at[p])

    def zero_copy(e):
        return pltpu.make_async_copy(zero_ref, xs_hbm.at[pl.ds(zb_ref[e] * EXPERT_BLOCK, EXPERT_BLOCK)], sem_zero)

    @pl.when(i == 0)
    def _():
        idx_copy(0, 0).start()
        zero_ref[...] = jnp.zeros(zero_ref.shape, U32)

        def start(e, carry):
            @pl.when(zb_ref[e] >= 0)
            def _():
                zero_copy(e).start()
            return carry

        def wait(e, carry):
            @pl.when(zb_ref[e] >= 0)
            def _():
                zero_copy(e).wait()
            return carry

        lax.fori_loop(0, zb_ref.shape[0], start, 0)
        lax.fori_loop(0, zb_ref.shape[0], wait, 0)

    def step(p):
        @pl.when(i + 1 < pl.num_programs(0))
        def _():
            idx_copy(i + 1, 1 - p).start()

        idx_copy(i, p).wait()
        idx = idx_bufs[p]

        def issue(t, carry):
            for kk in range(TOP_K):
                _row_copy(slab_ref, t, xs_hbm, idx[kk * te + t], sem_rows).start(priority=kk % 2)
            return carry

        def drain(t, carry):
            for kk in range(TOP_K):
                _row_copy(slab_ref, t, xs_hbm, idx[kk * te + t], sem_rows).wait()
            return carry

        lax.fori_loop(0, te, issue, 0, unroll=ROW_COPY_UNROLL)
        lax.fori_loop(0, te, drain, 0, unroll=ROW_COPY_UNROLL)

    _by_parity(i, step)


def moe_dispatch(h, dest_tiles, zero_blocks, n_slots):
    t = h.shape[0]
    te = ROUTE_TILE
    return pl.pallas_call(
        _dispatch_kernel,
        out_shape=jax.ShapeDtypeStruct((n_slots, ROW_CHUNKS, 128), U32),
        grid_spec=pltpu.PrefetchScalarGridSpec(
            num_scalar_prefetch=1,
            grid=(t // te,),
            in_specs=[pl.BlockSpec((te, h.shape[1]), lambda i, zb: (i, 0)), pl.BlockSpec(memory_space=pl.ANY)],
            out_specs=pl.BlockSpec(memory_space=pl.ANY),
            scratch_shapes=[pltpu.SMEM((te * TOP_K,), I32), pltpu.SMEM((te * TOP_K,), I32),
                            pltpu.VMEM((EXPERT_BLOCK, ROW_CHUNKS, 128), U32), pltpu.VMEM((te, ROW_CHUNKS, 128), U32),
                            pltpu.SemaphoreType.DMA((2,)), pltpu.SemaphoreType.DMA, pltpu.SemaphoreType.DMA],
        ),
        compiler_params=_params(("arbitrary",)),
        name="moe_dispatch",
    )(zero_blocks, h, dest_tiles)


def _expert_kernel(be_ref, nu_ref, xs_ref, wg0_ref, wu0_ref, wd0_ref, wg1_ref, wu1_ref, wd1_ref, ys_ref,
                   wg_bf, wu_bf, wd_bf):
    b = pl.program_id(0)
    rb = EXPERT_BLOCK
    used = 2 * b < nu_ref[0]
    for s, (wg, wu, wd) in enumerate(((wg0_ref, wu0_ref, wd0_ref), (wg1_ref, wu1_ref, wd1_ref))):
        j = 2 * b + s
        fresh = (b == 0) | (be_ref[j] != be_ref[jnp.maximum(j - 2, 0)])

        @pl.when(used & fresh)
        def _():
            wg_bf[s] = wg[...].astype(BF16)
            wu_bf[s] = wu[...].astype(BF16)
            wd_bf[s] = wd[...].astype(BF16)

    @pl.when(used)
    def _():
        for s in range(2):
            rows = slice(rb * s, rb * (s + 1))
            x = _unpack_bf16_pairs(_load_row_slabs(xs_ref, rows))
            hg = jnp.dot(x, wg_bf[s], preferred_element_type=F32)
            hu = jnp.dot(x, wu_bf[s], preferred_element_type=F32)
            y = jnp.dot((_silu(hg) * hu).astype(BF16), wd_bf[s], preferred_element_type=F32)
            ys_ref[rows, :] = _pack_bf16_pairs(y)


def moe_experts(xs, block_expert, n_used, wg, wu, wd, layer):
    n_slots = xs.shape[0]
    rb = EXPERT_BLOCK
    ff = wg.shape[3]
    rows_in = lambda b, be, nu: (jnp.minimum(b, (nu[0] - 1) // 2), 0, 0)
    rows_out = lambda b, be, nu: (jnp.minimum(b, (nu[0] - 1) // 2), 0)
    w_in = lambda s: (lambda b, be, nu: (layer, be[2 * b + s], 0, 0))
    return pl.pallas_call(
        _expert_kernel,
        out_shape=jax.ShapeDtypeStruct((n_slots, D_MODEL // 2), U32),
        grid_spec=pltpu.PrefetchScalarGridSpec(
            num_scalar_prefetch=2,
            grid=(n_slots // (2 * rb),),
            in_specs=[pl.BlockSpec((2 * rb,) + xs.shape[1:], rows_in)]
            + [pl.BlockSpec((None, None, D_MODEL, ff), w_in(0)), pl.BlockSpec((None, None, D_MODEL, ff), w_in(0)),
               pl.BlockSpec((None, None, ff, D_MODEL), w_in(0)),
               pl.BlockSpec((None, None, D_MODEL, ff), w_in(1)), pl.BlockSpec((None, None, D_MODEL, ff), w_in(1)),
               pl.BlockSpec((None, None, ff, D_MODEL), w_in(1))],
            out_specs=pl.BlockSpec((2 * rb, D_MODEL // 2), rows_out),
            scratch_shapes=[pltpu.VMEM((2, D_MODEL, ff), BF16), pltpu.VMEM((2, D_MODEL, ff), BF16),
                            pltpu.VMEM((2, ff, D_MODEL), BF16)],
        ),
        compiler_params=_params(("arbitrary",)),
        name="moe_experts",
    )(block_expert, n_used, xs, wg, wu, wd, wg, wu, wd)


def _combine_kernel(npt, final, h_ref, swg_ref, swu_ref, swd_ref, x1_ref, g2s_ref, g2t_ref, w_ref, fg_ref,
                    dest_hbm, ys_hbm, *rest):
    out_refs, (idx_a, idx_b, ybuf_ref, sem_idx, sem_rows) = rest[:-5], rest[-5:]
    i = pl.program_id(0)
    n = pl.num_programs(0)
    tg = h_ref.shape[0]
    idx_bufs = (idx_a, idx_b)

    def idx_copy(tile, p):
        return pltpu.make_async_copy(dest_hbm.at[tile], idx_bufs[p], sem_idx.at[p])

    def gather_rows(p, wait):
        idx = idx_bufs[p]

        def body(t, carry):
            for kk in range(TOP_K):
                src = 0 if wait else idx[kk * tg + t]
                dst = ybuf_ref.at[p, kk]
                cp = pltpu.make_async_copy(ys_hbm.at[pl.ds(src, 1)], dst.at[pl.ds(t, 1)], sem_rows.at[p])
                cp.wait() if wait else cp.start(priority=kk % 2)
            return carry

        lax.fori_loop(0, tg, body, 0, unroll=ROW_COPY_UNROLL)

    @pl.when(i == 0)
    def _():
        idx_copy(0, 0).start()
        idx_copy(0, 0).wait()
        gather_rows(0, wait=False)

        @pl.when(n > 1)
        def _():
            idx_copy(1, 1).start()

    def prefetch(p):
        @pl.when(i + 1 < n)
        def _():
            idx_copy(i + 1, 1 - p).wait()
            gather_rows(1 - p, wait=False)

        @pl.when(i + 2 < n)
        def _():
            idx_copy(i + 2, p).start()

    _by_parity(i, prefetch)

    hb = _unpack_bf16_pairs(h_ref[...])
    hid = _silu(jnp.dot(hb, swg_ref[...], preferred_element_type=F32)) * jnp.dot(hb, swu_ref[...],
                                                                                 preferred_element_type=F32)
    acc = jnp.dot(hid.astype(BF16), swd_ref[...], preferred_element_type=F32)
    w = w_ref[...]
    gate = _pick(i >= npt, g2s_ref, g2t_ref)

    def finish(p):
        gather_rows(p, wait=True)
        half = D_MODEL // 2
        lo = jnp.zeros((tg, half), F32)
        hi = jnp.zeros((tg, half), F32)
        for kk in range(TOP_K):
            u = ybuf_ref[p, kk]
            wk = w[:, kk:kk + 1]
            lo = lo + lax.bitcast_convert_type(u << 16, F32) * wk
            hi = hi + lax.bitcast_convert_type(u & jnp.uint32(0xFFFF0000), F32) * wk
        routed = jnp.concatenate([lo, hi], axis=1)
        x2 = x1_ref[...] + gate * (routed + acc)
        if final:
            ms = jnp.mean(x2 * x2, axis=-1, keepdims=True)
            x2 = x2 * lax.rsqrt(ms + NORM_EPS) * fg_ref[...]

        @pl.when(i < npt)
        def _():
            out_refs[0][...] = x2

        @pl.when(i >= npt)
        def _():
            out_refs[1][...] = x2

    _by_parity(i, finish)


def moe_combine(h, swg_bf, swu_bf, swd_bf, x1, mod, w_tok, final_g, dest_tiles, ys, final):
    t = h.shape[0]
    tg = mod.tile
    npt = mod.npt
    row = lambda i: (i, 0)
    const = lambda i: (0, 0)
    out_shape = (jax.ShapeDtypeStruct((npt * tg, D_MODEL), F32), jax.ShapeDtypeStruct((t - npt * tg, D_MODEL), F32))
    out_specs = tuple(_stream_specs(None, tg, npt))
    return pl.pallas_call(
        functools.partial(_combine_kernel, mod.npt, final),
        out_shape=out_shape,
        grid=(t // tg,),
        in_specs=[pl.BlockSpec((tg, h.shape[1]), row), pl.BlockSpec(swg_bf.shape, const),
                  pl.BlockSpec(swu_bf.shape, const),
                  pl.BlockSpec(swd_bf.shape, const), pl.BlockSpec((tg, D_MODEL), row)]
        + mod.specs(5)
        + [pl.BlockSpec((tg, TOP_K), row), pl.BlockSpec((1, D_MODEL), const),
           pl.BlockSpec(memory_space=pl.ANY), pl.BlockSpec(memory_space=pl.ANY)],
        out_specs=out_specs,
        scratch_shapes=[pltpu.SMEM((tg * TOP_K,), I32), pltpu.SMEM((tg * TOP_K,), I32),
                        pltpu.VMEM((2, TOP_K, tg, D_MODEL // 2), U32),
                        pltpu.SemaphoreType.DMA((2,)), pltpu.SemaphoreType.DMA((2,))],
        compiler_params=_params(("arbitrary",)),
        name="moe_combine",
    )(h, swg_bf, swu_bf, swd_bf, x1, *mod.operands(5), w_tok, final_g.reshape(1, -1), dest_tiles, ys)


def _slot_kernel(eidx_ref, rank_ref, start_ref, o_ref):
    tm = eidx_ref.shape[1]
    tr = o_ref.shape[2]
    eid = lax.broadcasted_iota(I32, (N_EXPERTS, tm), 0)
    start = start_ref[...]
    for kk in range(TOP_K):
        base = jnp.sum(jnp.where(eid == eidx_ref[kk:kk + 1, :], start, 0.0), axis=0, keepdims=True)
        slot = base.astype(I32) + rank_ref[kk:kk + 1, :]
        for j in range(tm // tr):
            o_ref[j, kk:kk + 1, :] = slot[:, tr * j:tr * (j + 1)]


def assignment_slots(eidx_t, rank_t, pad_start):
    t = eidx_t.shape[1]
    tm = TOKEN_TILE
    tr = ROUTE_TILE
    col = lambda i: (0, i)
    out = pl.pallas_call(
        _slot_kernel,
        out_shape=jax.ShapeDtypeStruct((t // tr, TOP_K, tr), I32),
        grid=(t // tm,),
        in_specs=[pl.BlockSpec((TOP_K, tm), col), pl.BlockSpec((TOP_K, tm), col),
                  pl.BlockSpec((N_EXPERTS, 1), lambda i: (0, 0))],
        out_specs=pl.BlockSpec((tm // tr, TOP_K, tr), lambda i: (i, 0, 0)),
        compiler_params=_params(("parallel",)),
        name="assignment_slots",
    )(eidx_t, rank_t, pad_start.astype(F32).reshape(-1, 1))
    return out.reshape(t // tr, TOP_K * tr)


def _routing_tables(counts, n_blocks):
    rb = EXPERT_BLOCK
    counts = counts.astype(I32)
    padded = (counts + rb - 1) // rb * rb
    pad_end = jnp.cumsum(padded)
    pad_start = pad_end - padded
    n_used = pad_end[-1] // rb
    blocks = jnp.arange(n_blocks, dtype=I32)
    block_expert = jnp.sum((pad_end[None, :] <= (blocks * rb)[:, None]).astype(I32), axis=1)
    last_used = jnp.sum((pad_end <= (n_used - 1) * rb).astype(I32))
    block_expert = jnp.minimum(jnp.where(blocks < n_used, block_expert, last_used), N_EXPERTS - 1)
    zero_blocks = jnp.where(counts % rb != 0, pad_end // rb - 1, -1)
    zero_blocks = jnp.concatenate([zero_blocks, jnp.where(n_used % 2 == 1, n_used, -1).reshape(1)]).astype(I32)
    return pad_start, block_expert, n_used.reshape(1).astype(I32), zero_blocks


def moe_layer(h, x1, eidx_t, w_t, rank_t, counts, mod_route, wg, wu, wd, layer, swg_bf, swu_bf, swd_bf,
              final_g, final):
    t = h.shape[0]
    rb = EXPERT_BLOCK
    tr = ROUTE_TILE
    n_blocks = -(-(t * TOP_K) // rb) + N_EXPERTS
    n_blocks += n_blocks % 2
    pad_start, block_expert, n_used, zero_blocks = _routing_tables(counts, n_blocks)
    dest_tiles = assignment_slots(eidx_t, rank_t, pad_start)
    xs = moe_dispatch(h, dest_tiles, zero_blocks, n_blocks * rb)
    ys = moe_experts(xs, block_expert, n_used, wg, wu, wd, layer)
    return moe_combine(h, swg_bf, swu_bf, swd_bf, x1, mod_route, w_t.T, final_g, dest_tiles, ys, final)


def _rope_tables(n_batch, seq_len, n_seq, n_new):
    half = SWA_HEAD_DIM // 2
    inv = ROPE_THETA ** (-jnp.arange(half, dtype=F32) / half)
    pos = jnp.concatenate([jnp.tile(jnp.arange(seq_len, dtype=F32), n_batch),
                           jnp.tile(PAST_LEN + jnp.arange(n_new, dtype=F32), n_seq)])
    ang = pos[:, None] * inv[None, :]
    cos = jnp.tile(jnp.cos(ang), (1, 128 // half))
    sin = jnp.sin(ang)
    sin = jnp.tile(jnp.concatenate([-sin, sin], axis=1), (1, 128 // SWA_HEAD_DIM))
    return cos, sin


def kernel(x_prompt, x_sample, c_prompt, c_sample, cache_swa_k, cache_swa_v, state_gla, norm_mix_g, norm_ffn_g,
           final_g, ada_w, ada_b, swa_wqkv, swa_sinks, swa_wo, gla_win, gla_wa1, gla_wa2, gla_ba, gla_norm_g,
           gla_wo, moe_router, moe_bias, moe_wg, moe_wu, moe_wd, shared_wg, shared_wu, shared_wd):
    n_batch, seq_len, d = x_prompt.shape
    n_seq, n_new, _ = x_sample.shape
    depth = ada_w.shape[0]
    tp = n_batch * seq_len
    ts = n_seq * n_new
    t = tp + ts
    tm = TOKEN_TILE
    tr = ROUTE_TILE

    x = (x_prompt.reshape(tp, d), x_sample.reshape(ts, d))
    c_all = jnp.concatenate([jnp.repeat(c_sample, n_new, axis=0), c_prompt], axis=0)
    mod = ada_modulation(c_all, ada_w, ada_b)
    cos_tab, sin_tab = _rope_tables(n_batch, seq_len, n_seq, n_new)
    tri = jnp.triu(jnp.ones((tm, tm), BF16), k=1)

    new_k, new_v, new_s = [], [], []
    new_k_s, new_v_s, new_s_s = [], [], []
    for layer in range(depth):
        mod_tok = _Mod(mod, layer, n_batch, seq_len, tm)
        mod_route = _Mod(mod, layer, n_batch, seq_len, tr)
        m = layer // 2
        if layer % 2 == 0:
            q, k, v, k_dup, v_dup = swa_qkv(x, norm_mix_g[layer], mod_tok, swa_wqkv[m].astype(BF16), cos_tab, sin_tab)
            a_p = swa_prompt_attention(q, k_dup, v_dup, swa_sinks[m], n_batch, seq_len)
            nk = SWA_KV_HEADS * SWA_HEAD_DIM
            a_s, ck, cv = swa_sample_attention(q, k, v, cache_swa_k[m].reshape(n_seq, WINDOW, nk),
                                             cache_swa_v[m].reshape(n_seq, WINDOW, nk), swa_sinks[m], tp, n_new)
            kv_shape = (n_batch, WINDOW, SWA_KV_HEADS, SWA_HEAD_DIM)
            tails = [slice((b + 1) * seq_len - WINDOW, (b + 1) * seq_len) for b in range(n_batch)]
            new_k.append(jnp.stack([k[rows] for rows in tails]).reshape(kv_shape))
            new_v.append(jnp.stack([v[rows] for rows in tails]).reshape(kv_shape))
            new_k_s.append(ck.reshape(n_seq, WINDOW, SWA_KV_HEADS, SWA_HEAD_DIM))
            new_v_s.append(cv.reshape(n_seq, WINDOW, SWA_KV_HEADS, SWA_HEAD_DIM))
            wo = swa_wo[m]
        else:
            pad = jnp.zeros((d, 128 - GLA_GATE_RANK), F32)
            win_ext = jnp.concatenate([gla_win[m], gla_wa1[m], pad], axis=1).astype(BF16)
            wa2_pad = jnp.concatenate([gla_wa2[m], jnp.zeros((128 - GLA_GATE_RANK, GLA_KEY_DIM), F32)],
                                      axis=0).astype(BF16)
            q, k, v, r, gate = gla_project(x, norm_mix_g[layer], mod_tok, win_ext, wa2_pad, gla_ba[m])
            a_p, s_prompt = gla_prompt(q, k, gate, v, r, gla_norm_g[m], n_batch, seq_len)
            a_s, s_sample = gla_sample(q, k, gate, v, r, gla_norm_g[m], state_gla[m], tp, n_new)
            new_s.append(s_prompt)
            new_s_s.append(s_sample)
            wo = gla_wo[m]
        x1, h, eidx_t, w_t, rank_t, cnt = post_mixer(a_p, a_s, wo.astype(BF16), x, mod_tok, norm_ffn_g[layer],
                                                     moe_router[layer].T, moe_bias[layer], tri)
        x = moe_layer(h, x1, eidx_t, w_t, rank_t, cnt[:, 0], mod_route,
                      moe_wg, moe_wu, moe_wd, layer,
                      shared_wg[layer].astype(BF16), shared_wu[layer].astype(BF16), shared_wd[layer].astype(BF16),
                      final_g, layer == depth - 1)

    y_prompt = x[0].reshape(n_batch, seq_len, d)
    y_sample = x[1].reshape(n_seq, n_new, d)
    return (y_prompt, y_sample, jnp.stack(new_k), jnp.stack(new_v), jnp.stack(new_k_s), jnp.stack(new_v_s),
            jnp.stack(new_s), jnp.stack(new_s_s))
```
